```python
import math
import jax, jax.numpy as jnp
from jax import lax
import numpy as np

D_MODEL = 1024
BATCH = 1
SEQ = 16384
DEPTH = 1
DEC_BATCH = 8
DEC_SEQ = 32
PAST_LEN = 1024

CHUNK = 64
S5_WIDTH = D_MODEL // 2
S5_GROUP = 16
S5_GROUPS = S5_WIDTH // S5_GROUP
S5_STATE = 64
GM_WIDTH = D_MODEL // 2
GM_CHUNK = 128
GM_HEADS = 8
GM_HEAD_DIM = GM_WIDTH // GM_HEADS
MIX_WIDTH = S5_WIDTH + GM_WIDTH
IN_WIDTH = S5_WIDTH + 2 * GM_WIDTH
D_FF = -(-8 * D_MODEL // (3 * 256)) * 256
N_MOD = 6
EPS = 1e-6

kernel_name = 'hymba_s5_gmlp_adaln_stream_step'


def rmsnorm(x, g):
    xf = x.astype(jnp.float32)
    y = xf * lax.rsqrt(jnp.mean(xf * xf, axis=-1, keepdims=True) + EPS)
    return (y * g.astype(jnp.float32)).astype(x.dtype)


def layernorm(x, g, b):
    xf = x.astype(jnp.float32)
    mu = jnp.mean(xf, axis=-1, keepdims=True)
    var = jnp.mean(jnp.square(xf - mu), axis=-1, keepdims=True)
    y = (xf - mu) * lax.rsqrt(var + EPS)
    return (y * g.astype(jnp.float32) + b.astype(jnp.float32)).astype(x.dtype)


def _scan_combine(e1, e2):
    a1r, a1i, b1r, b1i = e1
    a2r, a2i, b2r, b2i = e2
    return (a2r * a1r - a2i * a1i,
            a2r * a1i + a2i * a1r,
            a2r * b1r - a2i * b1i + b2r,
            a2r * b1i + a2i * b1r + b2i)


def s5_mixer(u, x0_re, x0_im, lam_re, lam_im, log_step, b_re, b_im, c_re, c_im, d):
    f32 = jnp.float32
    bsz, seq, _ = u.shape
    uf = u.astype(f32).reshape(bsz, seq, S5_GROUPS, S5_GROUP)
    lr = lam_re.astype(f32)
    li = lam_im.astype(f32)
    step = jnp.exp(log_step.astype(f32))[:, None]
    mag = jnp.exp(lr * step)
    ab_re = mag * jnp.cos(li * step)
    ab_im = mag * jnp.sin(li * step)
    den = lr * lr + li * li
    f_re = ((ab_re - 1.0) * lr + ab_im * li) / den
    f_im = (ab_im * lr - (ab_re - 1.0) * li) / den
    br = b_re.astype(f32)
    bi = b_im.astype(f32)
    bb_re = f_re[..., None] * br - f_im[..., None] * bi
    bb_im = f_re[..., None] * bi + f_im[..., None] * br
    bu_re = jnp.einsum('blgp,gnp->blgn', uf, bb_re)
    bu_im = jnp.einsum('blgp,gnp->blgn', uf, bb_im)
    x0r = x0_re.astype(f32)
    x0i = x0_im.astype(f32)
    bu_re = bu_re.at[:, 0].add(ab_re * x0r - ab_im * x0i)
    bu_im = bu_im.at[:, 0].add(ab_re * x0i + ab_im * x0r)
    a_re = jnp.broadcast_to(ab_re, bu_re.shape)
    a_im = jnp.broadcast_to(ab_im, bu_im.shape)
    _, _, s_re, s_im = lax.associative_scan(_scan_combine, (a_re, a_im, bu_re, bu_im), axis=1)
    y = (jnp.einsum('blgn,gpn->blgp', s_re, c_re.astype(f32))
         - jnp.einsum('blgn,gpn->blgp', s_im, c_im.astype(f32))
         + d.astype(f32) * uf)
    return (y.reshape(bsz, seq, S5_WIDTH).astype(u.dtype),
            s_re[:, -1].astype(u.dtype), s_im[:, -1].astype(u.dtype))


def gmlp_mixer(u, v, w_s, b_s):
    bsz, seq, _ = v.shape
    n_chunks = -(-seq // GM_CHUNK)
    pad = n_chunks * GM_CHUNK - seq
    vp = jnp.pad(v, ((0, 0), (0, pad), (0, 0))).reshape(bsz, n_chunks, GM_CHUNK, GM_HEADS, GM_HEAD_DIM)
    blk = jnp.arange(GM_CHUNK) // CHUNK
    mask = blk[None, :] <= blk[:, None]
    w = jnp.where(mask[None], w_s, jnp.zeros_like(w_s))
    mixed = jnp.einsum('hij,bkjhc->bkihc', w, vp) + jnp.transpose(b_s)[None, None, :, :, None]
    mixed = mixed.reshape(bsz, n_chunks * GM_CHUNK, GM_WIDTH)[:, :seq]
    return u * mixed


def trunk_layer(x, c, x0_re, x0_im, norm1_g, norm2_g, w_ada, b_ada, w_in,
                lam_re, lam_im, log_step, b_re, b_im, c_re, c_im, d, w_glu, b_glu,
                gm_ln_g, gm_ln_b, gm_w_s, gm_b_s, w_out, ffn_w_gu, ffn_w_down):
    mod = (jax.nn.silu(c) @ w_ada + b_ada)[:, None, :]
    shift1, scale1, gate1, shift2, scale2, gate2 = jnp.split(mod, N_MOD, axis=-1)
    h = rmsnorm(x, norm1_g) * (1.0 + scale1) + shift1
    z = h @ w_in
    z_s5, z_u, z_v = jnp.split(z, [S5_WIDTH, S5_WIDTH + GM_WIDTH], axis=-1)
    y_s5, s_re, s_im = s5_mixer(z_s5, x0_re, x0_im, lam_re, lam_im, log_step, b_re, b_im, c_re, c_im, d)
    y_s5 = jax.nn.gelu(y_s5)
    y_s5 = y_s5 * jax.nn.sigmoid(y_s5 @ w_glu + b_glu)
    u = jax.nn.gelu(z_u)
    v = layernorm(jax.nn.gelu(z_v), gm_ln_g, gm_ln_b)
    y_gm = gmlp_mixer(u, v, gm_w_s, gm_b_s)
    x = x + gate1 * (jnp.concatenate([y_s5, y_gm], axis=-1) @ w_out)
    h = rmsnorm(x, norm2_g) * (1.0 + scale2) + shift2
    g, up = jnp.split(h @ ffn_w_gu, 2, axis=-1)
    x = x + gate2 * ((jax.nn.silu(g) * up) @ ffn_w_down)
    return x, s_re, s_im, v


def setup_inputs(seed: int = 0) -> dict:
    key = jax.random.key(seed)
    ks = jax.random.split(key, 32)
    f32 = jnp.float32
    nrm = lambda k, shape, s: jax.random.normal(k, shape, f32) * s
    lam_im0 = math.pi * jnp.arange(S5_STATE, dtype=f32)
    return {
        'x_prompt': nrm(ks[0], (BATCH, SEQ, D_MODEL), 1.0),
        'x_sample': nrm(ks[1], (DEC_BATCH, DEC_SEQ, D_MODEL), 1.0),
        'state_s5_re': nrm(ks[2], (DEPTH, DEC_BATCH, S5_GROUPS, S5_STATE), 0.5),
        'state_s5_im': nrm(ks[3], (DEPTH, DEC_BATCH, S5_GROUPS, S5_STATE), 0.5),
        'c_prompt': nrm(ks[4], (BATCH, D_MODEL), 1.0),
        'c_sample': nrm(ks[5], (DEC_BATCH, D_MODEL), 1.0),
        'norm1_g': 1.0 + nrm(ks[6], (DEPTH, D_MODEL), 0.01),
        'norm2_g': 1.0 + nrm(ks[7], (DEPTH, D_MODEL), 0.01),
        'w_ada': nrm(ks[8], (DEPTH, D_MODEL, N_MOD * D_MODEL), D_MODEL ** -0.5),
        'b_ada': nrm(ks[9], (DEPTH, N_MOD * D_MODEL), 0.01),
        'w_in': nrm(ks[10], (DEPTH, D_MODEL, IN_WIDTH), D_MODEL ** -0.5),
        's5_lambda_re': -0.5 + nrm(ks[11], (DEPTH, S5_GROUPS, S5_STATE), 0.01),
        's5_lambda_im': lam_im0[None, None, :] + nrm(ks[12], (DEPTH, S5_GROUPS, S5_STATE), 0.01),
        's5_log_step': jax.random.uniform(ks[13], (DEPTH, S5_GROUPS), f32, math.log(1e-3), math.log(1e-1)),
        's5_b_re': nrm(ks[14], (DEPTH, S5_GROUPS, S5_STATE, S5_GROUP), (2 * S5_GROUP) ** -0.5),
        's5_b_im': nrm(ks[15], (DEPTH, S5_GROUPS, S5_STATE, S5_GROUP), (2 * S5_GROUP) ** -0.5),
        's5_c_re': nrm(ks[16], (DEPTH, S5_GROUPS, S5_GROUP, S5_STATE), (2 * S5_STATE) ** -0.5),
        's5_c_im': nrm(ks[17], (DEPTH, S5_GROUPS, S5_GROUP, S5_STATE), (2 * S5_STATE) ** -0.5),
        's5_d': nrm(ks[18], (DEPTH, S5_GROUPS, S5_GROUP), 1.0),
        's5_w_glu': nrm(ks[19], (DEPTH, S5_WIDTH, S5_WIDTH), S5_WIDTH ** -0.5),
        's5_b_glu': nrm(ks[20], (DEPTH, S5_WIDTH), 0.01),
        'gm_ln_g': 1.0 + nrm(ks[21], (DEPTH, GM_WIDTH), 0.01),
        'gm_ln_b': nrm(ks[22], (DEPTH, GM_WIDTH), 0.01),
        'gm_w_s': nrm(ks[23], (DEPTH, GM_HEADS, GM_CHUNK, GM_CHUNK), GM_CHUNK ** -0.5),
        'gm_b_s': 1.0 + nrm(ks[24], (DEPTH, GM_HEADS, GM_CHUNK), 0.01),
        'w_out': nrm(ks[25], (DEPTH, MIX_WIDTH, D_MODEL), MIX_WIDTH ** -0.5),
        'ffn_w_gu': nrm(ks[26], (DEPTH, D_MODEL, 2 * D_FF), D_MODEL ** -0.5),
        'ffn_w_down': nrm(ks[27], (DEPTH, D_FF, D_MODEL), D_FF ** -0.5),
        'final_g': 1.0 + nrm(ks[28], (D_MODEL,), 0.01),
    }


def reference(x_prompt, x_sample, state_s5_re, state_s5_im, c_prompt, c_sample,
              norm1_g, norm2_g, w_ada, b_ada, w_in,
              s5_lambda_re, s5_lambda_im, s5_log_step, s5_b_re, s5_b_im, s5_c_re, s5_c_im, s5_d,
              s5_w_glu, s5_b_glu, gm_ln_g, gm_ln_b, gm_w_s, gm_b_s, w_out, ffn_w_gu, ffn_w_down,
              final_g):
    xp = x_prompt
    xs = x_sample
    p_re, p_im, s_re_l, s_im_l, v_l = [], [], [], [], []
    for l in range(DEPTH):
        weights = (norm1_g[l], norm2_g[l], w_ada[l], b_ada[l], w_in[l],
                   s5_lambda_re[l], s5_lambda_im[l], s5_log_step[l], s5_b_re[l], s5_b_im[l],
                   s5_c_re[l], s5_c_im[l], s5_d[l], s5_w_glu[l], s5_b_glu[l],
                   gm_ln_g[l], gm_ln_b[l], gm_w_s[l], gm_b_s[l], w_out[l], ffn_w_gu[l], ffn_w_down[l])
        zero_state = jnp.zeros((xp.shape[0], S5_GROUPS, S5_STATE), xp.dtype)
        xp, pr, pi_, _ = trunk_layer(xp, c_prompt, zero_state, zero_state, *weights)
        xs, sr, si, vs = trunk_layer(xs, c_sample, state_s5_re[l], state_s5_im[l], *weights)
        p_re.append(pr)
        p_im.append(pi_)
        s_re_l.append(sr)
        s_im_l.append(si)
        v_l.append(vs)
    y_prompt = rmsnorm(xp, final_g)
    y_sample = rmsnorm(xs, final_g)
    new_s5_re_prompt = jnp.stack(p_re, axis=0)
    new_s5_im_prompt = jnp.stack(p_im, axis=0)
    new_s5_re_sample = jnp.stack(s_re_l, axis=0)
    new_s5_im_sample = jnp.stack(s_im_l, axis=0)
    new_gm_v_sample = jnp.stack(v_l, axis=0)
    return (y_prompt, y_sample, new_s5_re_prompt, new_s5_im_prompt, new_s5_re_sample, new_s5_im_sample, new_gm_v_sample)
```

```python
import functools
import math

import jax
import jax.numpy as jnp
from jax import lax
from jax.experimental import pallas as pl
from jax.experimental.pallas import tpu as pltpu

D_MODEL = 1024
S5_WIDTH = 512
S5_GROUP = 16
S5_GROUPS = 32
S5_STATE = 64
GM_WIDTH = 512
GM_CHUNK = 128
GM_HEADS = 8
GM_HEAD_DIM = 64
CHUNK = 64
IN_WIDTH = S5_WIDTH + 2 * GM_WIDTH
D_FF = 2816
N_MOD = 6
EPS = 1e-6

LANES = 128
MXU_DIM = 256
VMEM_LIMIT_BYTES = 56 * 1024 * 1024

SUB = MXU_DIM // S5_GROUP
N_SUBPOW = 4
N_DBLPOW = 8

F32 = jnp.float32
BF16 = jnp.bfloat16
HIGHEST = lax.Precision.HIGHEST


def _cparams(n_grid_axes):
    return pltpu.CompilerParams(
        dimension_semantics=("arbitrary",) * n_grid_axes,
        vmem_limit_bytes=VMEM_LIMIT_BYTES,
    )


def _const_spec(shape):
    nd = len(shape)
    return pl.BlockSpec(shape, lambda *_: (0,) * nd, pipeline_mode=pl.Buffered(1))


def _rms_scale(x):
    return lax.rsqrt(jnp.mean(x * x, axis=-1, keepdims=True) + EPS)


def _cmul(ar, ai, xr, xi):
    return ar * xr - ai * xi, ar * xi + ai * xr


def _div_pow2(idx, divisor):
    shift = divisor.bit_length() - 1
    assert divisor == 1 << shift
    return lax.shift_right_logical(idx, shift)


def _mod_pow2(idx, divisor):
    assert divisor & (divisor - 1) == 0
    return lax.bitwise_and(idx, divisor - 1)


def _ada_kernel(c_ref, w_ref, b_ref, o_ref):
    c = c_ref[...]
    s = c * jax.nn.sigmoid(c)
    o_ref[...] = jnp.dot(s, w_ref[...], preferred_element_type=F32, precision=HIGHEST) + b_ref[...]


def _ada_call(c_pad, w_ada, b_ada):
    rows = c_pad.shape[0]
    n_out = w_ada.shape[1]
    bn = D_MODEL
    return pl.pallas_call(
        _ada_kernel,
        out_shape=jax.ShapeDtypeStruct((rows, n_out), F32),
        grid=(n_out // bn,),
        in_specs=[
            pl.BlockSpec((rows, D_MODEL), lambda j: (0, 0)),
            pl.BlockSpec((D_MODEL, bn), lambda j: (0, j)),
            pl.BlockSpec((1, bn), lambda j: (0, j)),
        ],
        out_specs=pl.BlockSpec((rows, bn), lambda j: (0, j)),
        compiler_params=_cparams(1),
        name="ada",
    )(c_pad, w_ada, b_ada)


def _discretise(lr, li, ls):
    step = jnp.exp(ls)
    mag = jnp.exp(lr * step)
    ar = mag * jnp.cos(li * step)
    ai = mag * jnp.sin(li * step)
    den = lr * lr + li * li
    fr = ((ar - 1.0) * lr + ai * li) / den
    fi = (ai * lr - (ar - 1.0) * li) / den
    return ar, ai, fr, fi


def _s5_prep_kernel(lr2_ref, li2_ref, ls2_ref, lrc_ref, lic_ref, lsc_ref,
                    btr_ref, bti_ref, c2r_ref, c2i_ref, dt_ref,
                    wk_ref, qm_ref, tabr_ref, tabi_ref):
    n, sub, p = S5_STATE, SUB, S5_GROUP
    width = sub * p

    a2r, a2i, _, _ = _discretise(lr2_ref[0], li2_ref[0], ls2_ref[0])
    lane2 = lax.broadcasted_iota(jnp.int32, (p, 2 * n), 1)
    first = lane2 < n
    c2r = c2r_ref[0]
    c2i = c2i_ref[0]
    pr = jnp.ones_like(a2r)
    pi = jnp.zeros_like(a2r)
    ccat = []
    for _ in range(sub + 1):
        ccat.append(c2r * jnp.where(first, pr, -pi) + c2i * jnp.where(first, -pi, -pr))
        pr, pi = _cmul(a2r, a2i, pr, pi)
    qm_ref[0] = jnp.concatenate(ccat[1:], axis=0).astype(BF16)
    rcat = jnp.concatenate(ccat[:sub], axis=0)

    acr, aci, fr, fi = _discretise(lrc_ref[0], lic_ref[0], lsc_ref[0])
    btr = btr_ref[0]
    bti = bti_ref[0]
    bbr, bbi = _cmul(fr, fi, btr, bti)
    pows = [(jnp.ones_like(acr), jnp.zeros_like(acr))]
    for _ in range(sub):
        pows.append(_cmul(acr, aci, *pows[-1]))
    lane_blk = _div_pow2(lax.broadcasted_iota(jnp.int32, (n, width), 1), p)
    apr = jnp.zeros((n, width), F32)
    api = jnp.zeros((n, width), F32)
    for k in range(sub):
        sel = lane_blk == k
        apr = jnp.where(sel, pows[sub - 1 - k][0], apr)
        api = jnp.where(sel, pows[sub - 1 - k][1], api)
    pmr, pmi = _cmul(apr, api, bbr, bbi)
    wk_ref[0, width:width + n, :] = pmr.astype(BF16)
    wk_ref[0, width + n:, :] = pmi.astype(BF16)

    bbcat = jnp.concatenate([bbr, bbi], axis=0)
    kt = jnp.dot(rcat, bbcat, preferred_element_type=F32, precision=HIGHEST)
    row = lax.broadcasted_iota(jnp.int32, (width, width), 0)
    lane = lax.broadcasted_iota(jnp.int32, (width, width), 1)
    kt = kt + jnp.where(row == _mod_pow2(lane, p), dt_ref[0], 0.0)
    col_blk = _div_pow2(lane, p)
    m16 = jnp.zeros((width, width), F32)
    for k in range(sub):
        if k == 0:
            shifted = kt
        else:
            shifted = jnp.concatenate(
                [jnp.zeros((k * p, width), F32), kt[:width - k * p]], axis=0)
        m16 = jnp.where(col_blk == k, shifted, m16)
    wk_ref[0, :width, :] = m16.astype(BF16)

    tab_lane = lax.broadcasted_iota(jnp.int32, (n, LANES), 1)
    tabr = jnp.zeros((n, LANES), F32)
    tabi = jnp.zeros((n, LANES), F32)
    cols = []
    cur = pows[sub]
    for _ in range(N_SUBPOW):
        cols.append(cur)
        cur = _cmul(pows[sub][0], pows[sub][1], *cur)
    cur = cols[N_SUBPOW - 1]
    for _ in range(N_DBLPOW):
        cols.append(cur)
        cur = _cmul(cur[0], cur[1], *cur)
    for idx, (vr, vi) in enumerate(cols):
        tabr = jnp.where(tab_lane == idx, vr, tabr)
        tabi = jnp.where(tab_lane == idx, vi, tabi)
    tabr_ref[0] = tabr
    tabi_ref[0] = tabi


def _s5_prep_call(lam_re, lam_im, log_step, b_re, b_im, c_re, c_im, d):
    g, n, p, sub = S5_GROUPS, S5_STATE, S5_GROUP, SUB
    width = sub * p
    ls = jnp.broadcast_to(log_step[:, None], (g, n))
    row2 = lambda a: jnp.concatenate([a, a], axis=-1)[:, None, :]
    col = lambda a: a[:, :, None]
    btr = jnp.tile(b_re, (1, 1, sub))
    bti = jnp.tile(b_im, (1, 1, sub))
    c2r = jnp.concatenate([c_re, c_re], axis=-1)
    c2i = jnp.concatenate([c_im, c_im], axis=-1)
    dt = jnp.concatenate([d, jnp.zeros((g, width - p), F32)], axis=1)[:, :, None]
    grp = lambda shape: pl.BlockSpec((1,) + shape, lambda i: (i, 0, 0))
    return pl.pallas_call(
        _s5_prep_kernel,
        out_shape=(
            jax.ShapeDtypeStruct((g, width + 2 * n, width), BF16),
            jax.ShapeDtypeStruct((g, width, 2 * n), BF16),
            jax.ShapeDtypeStruct((g, n, LANES), F32),
            jax.ShapeDtypeStruct((g, n, LANES), F32),
        ),
        grid=(g,),
        in_specs=[grp((1, 2 * n))] * 3 + [grp((n, 1))] * 3
        + [grp((n, width))] * 2 + [grp((p, 2 * n))] * 2 + [grp((width, 1))],
        out_specs=(grp((width + 2 * n, width)), grp((width, 2 * n)),
                   grp((n, LANES)), grp((n, LANES))),
        compiler_params=_cparams(1),
        name="s5_prep",
    )(row2(lam_re), row2(lam_im), row2(ls), col(lam_re), col(lam_im), col(ls),
      btr, bti, c2r, c2i, dt)


def _inproj_kernel(x_ref, sh_ref, sc_ref, g1_ref, w_ref, lng_ref, lnb_ref,
                   zt_ref, u_ref, v_ref, *maybe_vf_ref, n_ph, rows, lane_rows):
    gain = g1_ref[...] * (1.0 + sc_ref[...])
    shift = sh_ref[...]
    hs = []
    for ph in range(n_ph):
        xs = x_ref[:, ph * D_MODEL:(ph + 1) * D_MODEL]
        hs.append(xs * _rms_scale(xs) * gain + shift)
    hb = jnp.concatenate(hs, axis=0).astype(BF16)

    z5 = jnp.dot(hb, w_ref[:, :S5_WIDTH], preferred_element_type=F32)
    for ph in range(n_ph):
        zz = z5[ph * rows:(ph + 1) * rows]
        if lane_rows > rows:
            zz = jnp.concatenate([zz, jnp.zeros((lane_rows - rows, S5_WIDTH), F32)], axis=0)
        zt = zz.T.reshape(S5_GROUPS, S5_GROUP, lane_rows)
        zt_ref[:, ph, :, :] = zt.astype(BF16)

    zu = jnp.dot(hb, w_ref[:, S5_WIDTH:S5_WIDTH + GM_WIDTH], preferred_element_type=F32)
    u = jax.nn.gelu(zu).astype(BF16)
    for ph in range(n_ph):
        u_ref[:, ph * GM_WIDTH:(ph + 1) * GM_WIDTH] = u[ph * rows:(ph + 1) * rows]

    zv = jnp.dot(hb, w_ref[:, S5_WIDTH + GM_WIDTH:], preferred_element_type=F32)
    gv = jax.nn.gelu(zv)
    mu = jnp.mean(gv, axis=-1, keepdims=True)
    cen = gv - mu
    var = jnp.mean(cen * cen, axis=-1, keepdims=True)
    v = cen * lax.rsqrt(var + EPS) * lng_ref[...] + lnb_ref[...]
    vb = v.astype(BF16)
    for ph in range(n_ph):
        v_ref[:, ph * GM_WIDTH:(ph + 1) * GM_WIDTH] = vb[ph * rows:(ph + 1) * rows]
        for vf_ref in maybe_vf_ref:
            vf_ref[:, ph * GM_WIDTH:(ph + 1) * GM_WIDTH] = v[ph * rows:(ph + 1) * rows]


def _inproj_call(x_ph, shift1, scale1, norm1_g, w_in_b, ln_g, ln_b, *, t, n_ph, lane_rows, want_vf):
    rows = x_ph.shape[0]
    mrows = shift1.shape[0]
    kern = functools.partial(_inproj_kernel, n_ph=n_ph, rows=rows, lane_rows=lane_rows)
    tok_spec = pl.BlockSpec((rows, n_ph * GM_WIDTH), lambda i: (0, i))
    out_shape = [
        jax.ShapeDtypeStruct((S5_GROUPS, t, S5_GROUP, lane_rows), BF16),
        jax.ShapeDtypeStruct((rows, t * GM_WIDTH), BF16),
        jax.ShapeDtypeStruct((rows, t * GM_WIDTH), BF16),
    ]
    out_specs = [
        pl.BlockSpec((S5_GROUPS, n_ph, S5_GROUP, lane_rows), lambda i: (0, i, 0, 0)),
        tok_spec, tok_spec,
    ]
    if want_vf:
        out_shape.append(jax.ShapeDtypeStruct((rows, t * GM_WIDTH), F32))
        out_specs.append(tok_spec)
    return pl.pallas_call(
        kern,
        out_shape=tuple(out_shape),
        grid=(t // n_ph,),
        in_specs=[
            pl.BlockSpec((rows, n_ph * D_MODEL), lambda i: (0, i)),
            _const_spec((mrows, D_MODEL)),
            _const_spec((mrows, D_MODEL)),
            _const_spec((1, D_MODEL)),
            _const_spec((D_MODEL, IN_WIDTH)),
            _const_spec((1, GM_WIDTH)),
            _const_spec((1, GM_WIDTH)),
        ],
        out_specs=tuple(out_specs),
        compiler_params=_cparams(1),
        name="inproj",
    )(x_ph, shift1, scale1, norm1_g, w_in_b, ln_g, ln_b)


def _s5_kernel(*refs, n_sub, lanes, scan):
    if scan:
        zt_ref, wk_ref, qm_ref, tabr_ref, tabi_ref, yt_ref, sf_ref, ybuf, lbuf = refs
        s0_ref = None
    else:
        zt_ref, wk_ref, qm_ref, tabr_ref, tabi_ref, s0_ref, yt_ref, sf_ref, ybuf, lbuf = refs
    n = S5_STATE
    width = SUB * S5_GROUP
    wk = wk_ref[0]
    qm = qm_ref[0]
    tabr = tabr_ref[0]
    tabi = tabi_ref[0]
    col = lambda i: (tabr[:, i:i + 1], tabi[:, i:i + 1])

    a1r, a1i = col(0)
    lr = li = None
    for j in range(n_sub):
        u = zt_ref[0, j * width:(j + 1) * width, :]
        r = jnp.dot(wk, u, preferred_element_type=F32)
        ybuf[j * width:(j + 1) * width, :] = r[:width]
        wr = r[width:width + n]
        wi = r[width + n:]
        if j == 0:
            lr, li = wr, wi
        else:
            tr, ti = _cmul(a1r, a1i, lr, li)
            lr, li = tr + wr, ti + wi
        lbuf[j, :n, :] = lr
        lbuf[j, n:, :] = li

    if scan:
        lane = lax.broadcasted_iota(jnp.int32, (n, lanes), 1)
        xr, xi = lr, li
        for i in range(int(math.log2(lanes))):
            sh = 1 << i
            mr, mi = col(N_SUBPOW + i)
            rr = jnp.where(lane >= sh, pltpu.roll(xr, sh, 1), 0.0)
            ri = jnp.where(lane >= sh, pltpu.roll(xi, sh, 1), 0.0)
            tr, ti = _cmul(mr, mi, rr, ri)
            xr, xi = xr + tr, xi + ti
        sr = jnp.where(lane >= 1, pltpu.roll(xr, 1, 1), 0.0)
        si = jnp.where(lane >= 1, pltpu.roll(xi, 1, 1), 0.0)
    else:
        sr = s0_ref[0, :n, :]
        si = s0_ref[0, n:, :]

    for j in range(n_sub):
        if j == 0:
            pr, pi = sr, si
        else:
            ar, ai = col(j - 1)
            tr, ti = _cmul(ar, ai, sr, si)
            pr, pi = lbuf[j - 1, :n, :] + tr, lbuf[j - 1, n:, :] + ti
        sp = jnp.concatenate([pr, pi], axis=0).astype(BF16)
        y = ybuf[j * width:(j + 1) * width, :] + jnp.dot(qm, sp, preferred_element_type=F32)
        yt_ref[0, j * width:(j + 1) * width, :] = y.astype(BF16)
    ar, ai = col(n_sub - 1)
    tr, ti = _cmul(ar, ai, sr, si)
    sf_ref[0, :n, :] = lbuf[n_sub - 1, :n, :] + tr
    sf_ref[0, n:, :] = lbuf[n_sub - 1, n:, :] + ti


def _s5_call(zt, wk, qm, tabr, tabi, s0, *, n_sub, scan):
    g, rows, lanes = zt.shape
    n = S5_STATE
    width = SUB * S5_GROUP
    if scan:
        assert n_sub == N_SUBPOW and lanes <= (1 << N_DBLPOW)
    grp = lambda shape: pl.BlockSpec((1,) + shape, lambda i: (i, 0, 0))
    in_specs = [grp((rows, lanes)), grp((width + 2 * n, width)), grp((width, 2 * n)),
                grp((n, LANES)), grp((n, LANES))]
    args = [zt, wk, qm, tabr, tabi]
    if not scan:
        in_specs.append(grp((2 * n, lanes)))
        args.append(s0)
    kern = functools.partial(_s5_kernel, n_sub=n_sub, lanes=lanes, scan=scan)
    return pl.pallas_call(
        kern,
        out_shape=(jax.ShapeDtypeStruct((g, rows, lanes), BF16),
                   jax.ShapeDtypeStruct((g, 2 * n, lanes), F32)),
        grid=(g,),
        in_specs=in_specs,
        out_specs=(grp((rows, lanes)), grp((2 * n, lanes))),
        scratch_shapes=[pltpu.VMEM((rows, lanes), F32),
                        pltpu.VMEM((n_sub, 2 * n, lanes), F32)],
        compiler_params=_cparams(1),
        name="s5",
    )(*args)


def _glu_kernel(yt_ref, w_ref, b_ref, m_ref, *, n_ph, rows, lane_rows):
    gs = []
    for ph in range(n_ph):
        yt = yt_ref[:, ph, :, :].astype(F32).reshape(S5_WIDTH, lane_rows)
        gs.append(jax.nn.gelu(yt.T[:rows]))
    gy = jnp.concatenate(gs, axis=0)
    gate = jnp.dot(gy.astype(BF16), w_ref[...], preferred_element_type=F32) + b_ref[...]
    m = (gy * jax.nn.sigmoid(gate)).astype(BF16)
    for ph in range(n_ph):
        m_ref[:, ph * S5_WIDTH:(ph + 1) * S5_WIDTH] = m[ph * rows:(ph + 1) * rows]


def _glu_call(yt4, w_glu_b, b_glu, *, rows, n_ph):
    g, t, p, lane_rows = yt4.shape
    kern = functools.partial(_glu_kernel, n_ph=n_ph, rows=rows, lane_rows=lane_rows)
    return pl.pallas_call(
        kern,
        out_shape=jax.ShapeDtypeStruct((rows, t * S5_WIDTH), BF16),
        grid=(t // n_ph,),
        in_specs=[
            pl.BlockSpec((g, n_ph, p, lane_rows), lambda i: (0, i, 0, 0)),
            _const_spec((S5_WIDTH, S5_WIDTH)),
            _const_spec((1, S5_WIDTH)),
        ],
        out_specs=pl.BlockSpec((rows, n_ph * S5_WIDTH), lambda i: (0, i)),
        compiler_params=_cparams(1),
        name="glu",
    )(yt4, w_glu_b, b_glu)


def _main_kernel(x_ref, m_ref, u_ref, v_ref, mod_ref, g2_ref, gf_ref, gw_ref, gbt_ref,
                 wo_ref, wgu_ref, wd_ref, y_ref, ygm_ref, *, tm, cl, n_fc):
    hd = GM_HEAD_DIM
    blk_i = _div_pow2(lax.broadcasted_iota(jnp.int32, (cl, cl), 0), CHUNK)
    blk_j = _div_pow2(lax.broadcasted_iota(jnp.int32, (cl, cl), 1), CHUNK)
    causal = blk_j <= blk_i
    for h in range(GM_HEADS):
        wm = jnp.where(causal, gw_ref[h, :cl, :cl], 0.0).astype(BF16)
        bcol = gbt_ref[:cl, h:h + 1]
        for ci in range(tm // cl):
            rs = slice(ci * cl, (ci + 1) * cl)
            cs = slice(h * hd, (h + 1) * hd)
            mixed = jnp.dot(wm, v_ref[rs, cs], preferred_element_type=F32) + bcol
            ygm_ref[rs, cs] = (u_ref[rs, cs].astype(F32) * mixed).astype(BF16)

    attn = (jnp.dot(m_ref[...], wo_ref[:S5_WIDTH, :], preferred_element_type=F32)
            + jnp.dot(ygm_ref[...], wo_ref[S5_WIDTH:, :], preferred_element_type=F32))
    mod = mod_ref[0]
    gate1, shift2, scale2, gate2 = mod[2:3], mod[3:4], mod[4:5], mod[5:6]
    x1 = x_ref[...] + gate1 * attn
    h2 = (x1 * _rms_scale(x1) * (g2_ref[...] * (1.0 + scale2)) + shift2).astype(BF16)

    fc = D_FF // n_fc
    acc = None
    for c in range(n_fc):
        gg = jnp.dot(h2, wgu_ref[:, c * fc:(c + 1) * fc], preferred_element_type=F32)
        up = jnp.dot(h2, wgu_ref[:, D_FF + c * fc:D_FF + (c + 1) * fc], preferred_element_type=F32)
        act = (gg * jax.nn.sigmoid(gg) * up).astype(BF16)
        part = jnp.dot(act, wd_ref[c * fc:(c + 1) * fc, :], preferred_element_type=F32)
        acc = part if acc is None else acc + part
    x2 = x1 + gate2 * acc
    y_ref[...] = x2 * _rms_scale(x2) * gf_ref[...]


def _main_call(x2d, m, u, v, mod, norm2_g, final_g, gm_w, gm_bt, w_out_b, w_gu_b, w_down_b,
               *, tm, cl, tiles_per_stream, n_fc):
    n_tok = x2d.shape[0]
    kern = functools.partial(_main_kernel, tm=tm, cl=cl, n_fc=n_fc)
    tok = lambda width: pl.BlockSpec((tm, width), lambda i: (i, 0))
    return pl.pallas_call(
        kern,
        out_shape=jax.ShapeDtypeStruct((n_tok, D_MODEL), F32),
        grid=(n_tok // tm,),
        in_specs=[
            tok(D_MODEL), tok(S5_WIDTH), tok(GM_WIDTH), tok(GM_WIDTH),
            pl.BlockSpec((1, N_MOD, D_MODEL), lambda i: (i // tiles_per_stream, 0, 0)),
            _const_spec((1, D_MODEL)),
            _const_spec((1, D_MODEL)),
            _const_spec((GM_HEADS, GM_CHUNK, GM_CHUNK)),
            _const_spec((GM_CHUNK, GM_HEADS)),
            _const_spec((D_MODEL, D_MODEL)),
            _const_spec((D_MODEL, 2 * D_FF)),
            _const_spec((D_FF, D_MODEL)),
        ],
        out_specs=tok(D_MODEL),
        scratch_shapes=[pltpu.VMEM((tm, GM_WIDTH), BF16)],
        compiler_params=_cparams(1),
        name="main",
    )(x2d, m, u, v, mod, norm2_g, final_g, gm_w, gm_bt, w_out_b, w_gu_b, w_down_b)


def _trunk(x, mod, s0, prm, *, t, n_ph, tm, n_fc, want_vf):
    b, seq, _ = x.shape
    n_chunks = seq // t
    rows = b * n_chunks
    scan = s0 is None
    assert (b == 1) if scan else (n_chunks == 1)
    lane_rows = -(-rows // LANES) * LANES
    n_sub = t // SUB

    x_ph = x.reshape(rows, t * D_MODEL)
    zt4, u, v, *maybe_vf = _inproj_call(
        x_ph, mod[:, 0], mod[:, 1], prm["norm1_g"], prm["w_in"], prm["ln_g"], prm["ln_b"],
        t=t, n_ph=n_ph, lane_rows=lane_rows, want_vf=want_vf)

    if scan:
        s0_l = None
    else:
        s0_l = jnp.transpose(s0, (1, 2, 0))
        s0_l = jnp.pad(s0_l, ((0, 0), (0, 0), (0, lane_rows - rows)))
    yt, sfin = _s5_call(zt4.reshape(S5_GROUPS, t * S5_GROUP, lane_rows),
                        prm["wk"], prm["qm"], prm["tabr"], prm["tabi"], s0_l,
                        n_sub=n_sub, scan=scan)
    m = _glu_call(yt.reshape(S5_GROUPS, t, S5_GROUP, lane_rows), prm["w_glu"], prm["b_glu"],
                  rows=rows, n_ph=n_ph)

    n_tok = b * seq
    y = _main_call(
        x.reshape(n_tok, D_MODEL), m.reshape(n_tok, S5_WIDTH), u.reshape(n_tok, GM_WIDTH),
        v.reshape(n_tok, GM_WIDTH), mod, prm["norm2_g"], prm["final_g"], prm["gm_w"], prm["gm_bt"],
        prm["w_out"], prm["w_gu"], prm["w_down"],
        tm=tm, cl=min(GM_CHUNK, seq), tiles_per_stream=seq // tm, n_fc=n_fc)

    if scan:
        fin = sfin[:, :, rows - 1][None]
    else:
        fin = jnp.transpose(sfin[:, :, :rows], (2, 0, 1))
    vf = maybe_vf[0].reshape(b, seq, GM_WIDTH) if want_vf else None
    return y.reshape(b, seq, D_MODEL), fin[..., :S5_STATE], fin[..., S5_STATE:], vf


def kernel(x_prompt, x_sample, state_s5_re, state_s5_im, c_prompt, c_sample, norm1_g, norm2_g, w_ada, b_ada, w_in, s5_lambda_re, s5_lambda_im, s5_log_step, s5_b_re, s5_b_im, s5_c_re, s5_c_im, s5_d, s5_w_glu, s5_b_glu, gm_ln_g, gm_ln_b, gm_w_s, gm_b_s, w_out, ffn_w_gu, ffn_w_down, final_g):
    depth = w_in.shape[0]
    assert depth == 1
    l = 0
    n_p = c_prompt.shape[0]
    n_s = c_sample.shape[0]

    c_all = jnp.concatenate([c_prompt, c_sample], axis=0)
    c_pad = jnp.pad(c_all, ((0, -c_all.shape[0] % 8), (0, 0)))
    mod_all = _ada_call(c_pad, w_ada[l], b_ada[l][None, :])
    mod_p = mod_all[:n_p].reshape(n_p, N_MOD, D_MODEL)
    mod_s = mod_all[n_p:n_p + n_s].reshape(n_s, N_MOD, D_MODEL)

    wk, qm, tabr, tabi = _s5_prep_call(
        s5_lambda_re[l], s5_lambda_im[l], s5_log_step[l], s5_b_re[l], s5_b_im[l],
        s5_c_re[l], s5_c_im[l], s5_d[l])

    prm = dict(
        norm1_g=norm1_g[l][None, :], norm2_g=norm2_g[l][None, :], final_g=final_g[None, :],
        w_in=w_in[l].astype(BF16), ln_g=gm_ln_g[l][None, :], ln_b=gm_ln_b[l][None, :],
        wk=wk, qm=qm, tabr=tabr, tabi=tabi,
        w_glu=s5_w_glu[l].astype(BF16), b_glu=s5_b_glu[l][None, :],
        gm_w=gm_w_s[l], gm_bt=jnp.transpose(gm_b_s[l]),
        w_out=w_out[l].astype(BF16), w_gu=ffn_w_gu[l].astype(BF16),
        w_down=ffn_w_down[l].astype(BF16),
    )

    yp, pre, pim, _ = _trunk(x_prompt, mod_p, None, prm,
                             t=N_SUBPOW * SUB, n_ph=4, tm=512, n_fc=2, want_vf=False)
    s0 = jnp.concatenate([state_s5_re[l], state_s5_im[l]], axis=-1)
    seq_s = x_sample.shape[1]
    ys, sre, sim, vs = _trunk(x_sample, mod_s, s0, prm,
                              t=seq_s, n_ph=seq_s, tm=seq_s, n_fc=2, want_vf=True)

    return (yp, ys, pre[None], pim[None], sre[None], sim[None], vs[None])
```

```python
import functools
import math

import jax
import jax.numpy as jnp
from jax import lax
from jax.experimental import pallas as pl
from jax.experimental.pallas import tpu as pltpu

D_MODEL = 1024
S5_WIDTH = 512
S5_GROUP = 16
S5_GROUPS = 32
S5_STATE = 64
GM_WIDTH = 512
GM_CHUNK = 128
GM_HEADS = 8
GM_HEAD_DIM = 64
CHUNK = 64
IN_WIDTH = S5_WIDTH + 2 * GM_WIDTH
D_FF = 2816
N_MOD = 6
EPS = 1e-6

LANES = 128
MXU_DIM = 256
VMEM_LIMIT_BYTES = 56 * 1024 * 1024

SUB = MXU_DIM // S5_GROUP
N_SUBPOW = 4
N_DBLPOW = 8

F32 = jnp.float32
BF16 = jnp.bfloat16
HIGHEST = lax.Precision.HIGHEST


def _cparams(n_grid_axes):
    return pltpu.CompilerParams(
        dimension_semantics=("arbitrary",) * n_grid_axes,
        vmem_limit_bytes=VMEM_LIMIT_BYTES,
    )


def _const_spec(shape):
    nd = len(shape)
    return pl.BlockSpec(shape, lambda *_: (0,) * nd, pipeline_mode=pl.Buffered(1))


def _rms_scale(x):
    return lax.rsqrt(jnp.mean(x * x, axis=-1, keepdims=True) + EPS)


def _cmul(ar, ai, xr, xi):
    return ar * xr - ai * xi, ar * xi + ai * xr


OCT = 8


def _oct_shape(n_rows, t, width):
    assert n_rows % OCT == 0 and width % LANES == 0
    return (n_rows // OCT, width // LANES, t, OCT, LANES)


def _oct_store(ref, ph, val):
    n_oct, n_lb = ref.shape[0], ref.shape[1]
    for lb in range(n_lb):
        ref[:, lb, ph, :, :] = val[:, lb * LANES:(lb + 1) * LANES].reshape(n_oct, OCT, LANES)


def _oct_load(ref, t):
    n_lb = ref.shape[0] // (t * OCT)
    chunks = []
    for cc in range(OCT):
        chunks.append(jnp.concatenate(
            [ref[pl.ds(lb * t * OCT + cc, t, stride=OCT), :] for lb in range(n_lb)], axis=1))
    return jnp.concatenate(chunks, axis=0)


def _div_pow2(idx, divisor):
    shift = divisor.bit_length() - 1
    assert divisor == 1 << shift
    return lax.shift_right_logical(idx, shift)


def _mod_pow2(idx, divisor):
    assert divisor & (divisor - 1) == 0
    return lax.bitwise_and(idx, divisor - 1)


def _ada_kernel(c_ref, w_ref, b_ref, o_ref):
    c = c_ref[...]
    s = c * jax.nn.sigmoid(c)
    o_ref[...] = jnp.dot(s, w_ref[...], preferred_element_type=F32, precision=HIGHEST) + b_ref[...]


def _ada_call(c_pad, w_ada, b_ada):
    rows = c_pad.shape[0]
    n_out = w_ada.shape[1]
    bn = D_MODEL
    return pl.pallas_call(
        _ada_kernel,
        out_shape=jax.ShapeDtypeStruct((rows, n_out), F32),
        grid=(n_out // bn,),
        in_specs=[
            pl.BlockSpec((rows, D_MODEL), lambda j: (0, 0)),
            pl.BlockSpec((D_MODEL, bn), lambda j: (0, j)),
            pl.BlockSpec((1, bn), lambda j: (0, j)),
        ],
        out_specs=pl.BlockSpec((rows, bn), lambda j: (0, j)),
        compiler_params=_cparams(1),
        name="ada",
    )(c_pad, w_ada, b_ada)


def _discretise(lr, li, ls):
    step = jnp.exp(ls)
    mag = jnp.exp(lr * step)
    ar = mag * jnp.cos(li * step)
    ai = mag * jnp.sin(li * step)
    den = lr * lr + li * li
    fr = ((ar - 1.0) * lr + ai * li) / den
    fi = (ai * lr - (ar - 1.0) * li) / den
    return ar, ai, fr, fi


def _s5_prep_kernel(lr2_ref, li2_ref, ls2_ref, lrc_ref, lic_ref, lsc_ref,
                    btr_ref, bti_ref, c2r_ref, c2i_ref, dt_ref,
                    wk_ref, qm_ref, tabr_ref, tabi_ref):
    n, sub, p = S5_STATE, SUB, S5_GROUP
    width = sub * p

    a2r, a2i, _, _ = _discretise(lr2_ref[0], li2_ref[0], ls2_ref[0])
    lane2 = lax.broadcasted_iota(jnp.int32, (p, 2 * n), 1)
    first = lane2 < n
    c2r = c2r_ref[0]
    c2i = c2i_ref[0]
    pr = jnp.ones_like(a2r)
    pi = jnp.zeros_like(a2r)
    ccat = []
    for _ in range(sub + 1):
        ccat.append(c2r * jnp.where(first, pr, -pi) + c2i * jnp.where(first, -pi, -pr))
        pr, pi = _cmul(a2r, a2i, pr, pi)
    qm_ref[0] = jnp.concatenate(ccat[1:], axis=0).astype(BF16)
    rcat = jnp.concatenate(ccat[:sub], axis=0)

    acr, aci, fr, fi = _discretise(lrc_ref[0], lic_ref[0], lsc_ref[0])
    btr = btr_ref[0]
    bti = bti_ref[0]
    bbr, bbi = _cmul(fr, fi, btr, bti)
    pows = [(jnp.ones_like(acr), jnp.zeros_like(acr))]
    for _ in range(sub):
        pows.append(_cmul(acr, aci, *pows[-1]))
    lane_blk = _div_pow2(lax.broadcasted_iota(jnp.int32, (n, width), 1), p)
    apr = jnp.zeros((n, width), F32)
    api = jnp.zeros((n, width), F32)
    for k in range(sub):
        sel = lane_blk == k
        apr = jnp.where(sel, pows[sub - 1 - k][0], apr)
        api = jnp.where(sel, pows[sub - 1 - k][1], api)
    pmr, pmi = _cmul(apr, api, bbr, bbi)
    wk_ref[0, width:width + n, :] = pmr.astype(BF16)
    wk_ref[0, width + n:, :] = pmi.astype(BF16)

    bbcat = jnp.concatenate([bbr, bbi], axis=0)
    kt = jnp.dot(rcat, bbcat, preferred_element_type=F32, precision=HIGHEST)
    row = lax.broadcasted_iota(jnp.int32, (width, width), 0)
    lane = lax.broadcasted_iota(jnp.int32, (width, width), 1)
    kt = kt + jnp.where(row == _mod_pow2(lane, p), dt_ref[0], 0.0)
    col_blk = _div_pow2(lane, p)
    m16 = jnp.zeros((width, width), F32)
    for k in range(sub):
        if k == 0:
            shifted = kt
        else:
            shifted = jnp.concatenate(
                [jnp.zeros((k * p, width), F32), kt[:width - k * p]], axis=0)
        m16 = jnp.where(col_blk == k, shifted, m16)
    wk_ref[0, :width, :] = m16.astype(BF16)

    tab_lane = lax.broadcasted_iota(jnp.int32, (n, LANES), 1)
    tabr = jnp.zeros((n, LANES), F32)
    tabi = jnp.zeros((n, LANES), F32)
    cols = []
    cur = pows[sub]
    for _ in range(N_SUBPOW):
        cols.append(cur)
        cur = _cmul(pows[sub][0], pows[sub][1], *cur)
    cur = cols[N_SUBPOW - 1]
    for _ in range(N_DBLPOW):
        cols.append(cur)
        cur = _cmul(cur[0], cur[1], *cur)
    for idx, (vr, vi) in enumerate(cols):
        tabr = jnp.where(tab_lane == idx, vr, tabr)
        tabi = jnp.where(tab_lane == idx, vi, tabi)
    tabr_ref[0] = tabr
    tabi_ref[0] = tabi


def _s5_prep_call(lam_re, lam_im, log_step, b_re, b_im, c_re, c_im, d):
    g, n, p, sub = S5_GROUPS, S5_STATE, S5_GROUP, SUB
    width = sub * p
    ls = jnp.broadcast_to(log_step[:, None], (g, n))
    row2 = lambda a: jnp.concatenate([a, a], axis=-1)[:, None, :]
    col = lambda a: a[:, :, None]
    btr = jnp.tile(b_re, (1, 1, sub))
    bti = jnp.tile(b_im, (1, 1, sub))
    c2r = jnp.concatenate([c_re, c_re], axis=-1)
    c2i = jnp.concatenate([c_im, c_im], axis=-1)
    dt = jnp.concatenate([d, jnp.zeros((g, width - p), F32)], axis=1)[:, :, None]
    grp = lambda shape: pl.BlockSpec((1,) + shape, lambda i: (i, 0, 0))
    return pl.pallas_call(
        _s5_prep_kernel,
        out_shape=(
            jax.ShapeDtypeStruct((g, width + 2 * n, width), BF16),
            jax.ShapeDtypeStruct((g, width, 2 * n), BF16),
            jax.ShapeDtypeStruct((g, n, LANES), F32),
            jax.ShapeDtypeStruct((g, n, LANES), F32),
        ),
        grid=(g,),
        in_specs=[grp((1, 2 * n))] * 3 + [grp((n, 1))] * 3
        + [grp((n, width))] * 2 + [grp((p, 2 * n))] * 2 + [grp((width, 1))],
        out_specs=(grp((width + 2 * n, width)), grp((width, 2 * n)),
                   grp((n, LANES)), grp((n, LANES))),
        compiler_params=_cparams(1),
        name="s5_prep",
    )(row2(lam_re), row2(lam_im), row2(ls), col(lam_re), col(lam_im), col(ls),
      btr, bti, c2r, c2i, dt)


def _inproj_kernel(x_ref, sh_ref, sc_ref, g1_ref, w_ref, lng_ref, lnb_ref,
                   zt_ref, u_ref, v_ref, *, n_ph, rows, lane_rows):
    gain = g1_ref[...] * (1.0 + sc_ref[...])
    shift = sh_ref[...]
    hs = []
    for ph in range(n_ph):
        xs = x_ref[:, ph, :]
        hs.append(xs * _rms_scale(xs) * gain + shift)
    hb = jnp.concatenate(hs, axis=0).astype(BF16)

    z5 = jnp.dot(hb, w_ref[:, :S5_WIDTH], preferred_element_type=F32)
    for ph in range(n_ph):
        zz = z5[ph * rows:(ph + 1) * rows]
        if lane_rows > rows:
            zz = jnp.concatenate([zz, jnp.zeros((lane_rows - rows, S5_WIDTH), F32)], axis=0)
        zt = zz.T.reshape(S5_GROUPS, S5_GROUP, lane_rows)
        zt_ref[:, ph, :, :] = zt.astype(BF16)

    zu = jnp.dot(hb, w_ref[:, S5_WIDTH:S5_WIDTH + GM_WIDTH], preferred_element_type=F32)
    u = jax.nn.gelu(zu)
    for ph in range(n_ph):
        _oct_store(u_ref, ph, u[ph * rows:(ph + 1) * rows])

    zv = jnp.dot(hb, w_ref[:, S5_WIDTH + GM_WIDTH:], preferred_element_type=F32)
    gv = jax.nn.gelu(zv)
    mu = jnp.mean(gv, axis=-1, keepdims=True)
    cen = gv - mu
    var = jnp.mean(cen * cen, axis=-1, keepdims=True)
    v = cen * lax.rsqrt(var + EPS) * lng_ref[...] + lnb_ref[...]
    for ph in range(n_ph):
        _oct_store(v_ref, ph, v[ph * rows:(ph + 1) * rows])


def _phase_blocks(n_rows, row_blk):
    if row_blk % LANES == 0:
        assert n_rows % row_blk == 0
        return n_rows // row_blk, row_blk, n_rows
    assert row_blk == n_rows
    lane_rows = -(-n_rows // LANES) * LANES
    return 1, lane_rows, lane_rows


def _inproj_call(x3, shift1, scale1, norm1_g, w_in_b, ln_g, ln_b, *, n_ph, row_blk):
    rows, t, _ = x3.shape
    mrows = shift1.shape[0]
    assert mrows in (1, row_blk)
    n_rb, lane_blk, lane_rows = _phase_blocks(rows, row_blk)
    kern = functools.partial(_inproj_kernel, n_ph=n_ph, rows=row_blk, lane_rows=lane_blk)
    oct_shape = _oct_shape(rows, t, GM_WIDTH)
    tok_spec = pl.BlockSpec((row_blk // OCT, oct_shape[1], n_ph, OCT, LANES),
                            lambda i, j: (i, 0, j, 0, 0))
    return pl.pallas_call(
        kern,
        out_shape=(
            jax.ShapeDtypeStruct((S5_GROUPS, t, S5_GROUP, lane_rows), BF16),
            jax.ShapeDtypeStruct(oct_shape, F32),
            jax.ShapeDtypeStruct(oct_shape, F32),
        ),
        grid=(n_rb, t // n_ph),
        in_specs=[
            pl.BlockSpec((row_blk, n_ph, D_MODEL), lambda i, j: (i, j, 0)),
            _const_spec((mrows, D_MODEL)),
            _const_spec((mrows, D_MODEL)),
            _const_spec((1, D_MODEL)),
            _const_spec((D_MODEL, IN_WIDTH)),
            _const_spec((1, GM_WIDTH)),
            _const_spec((1, GM_WIDTH)),
        ],
        out_specs=(
            pl.BlockSpec((S5_GROUPS, n_ph, S5_GROUP, lane_blk), lambda i, j: (0, j, 0, i)),
            tok_spec, tok_spec,
        ),
        compiler_params=_cparams(2),
        name="inproj",
    )(x3, shift1, scale1, norm1_g, w_in_b, ln_g, ln_b)


def _s5_kernel(*refs, n_sub, lanes, scan):
    if scan:
        zt_ref, wk_ref, qm_ref, tabr_ref, tabi_ref, yt_ref, sf_ref, ybuf, lbuf = refs
        s0_ref = None
    else:
        zt_ref, wk_ref, qm_ref, tabr_ref, tabi_ref, s0_ref, yt_ref, sf_ref, ybuf, lbuf = refs
    n = S5_STATE
    width = SUB * S5_GROUP
    wk = wk_ref[0]
    qm = qm_ref[0]
    tabr = tabr_ref[0]
    tabi = tabi_ref[0]
    col = lambda i: (tabr[:, i:i + 1], tabi[:, i:i + 1])

    a1r, a1i = col(0)
    lr = li = None
    for j in range(n_sub):
        u = zt_ref[0, j * width:(j + 1) * width, :]
        r = jnp.dot(wk, u, preferred_element_type=F32)
        ybuf[j * width:(j + 1) * width, :] = r[:width]
        wr = r[width:width + n]
        wi = r[width + n:]
        if j == 0:
            lr, li = wr, wi
        else:
            tr, ti = _cmul(a1r, a1i, lr, li)
            lr, li = tr + wr, ti + wi
        lbuf[j, :n, :] = lr
        lbuf[j, n:, :] = li

    if scan:
        lane = lax.broadcasted_iota(jnp.int32, (n, lanes), 1)
        xr, xi = lr, li
        for i in range(int(math.log2(lanes))):
            sh = 1 << i
            mr, mi = col(N_SUBPOW + i)
            rr = jnp.where(lane >= sh, pltpu.roll(xr, sh, 1), 0.0)
            ri = jnp.where(lane >= sh, pltpu.roll(xi, sh, 1), 0.0)
            tr, ti = _cmul(mr, mi, rr, ri)
            xr, xi = xr + tr, xi + ti
        sr = jnp.where(lane >= 1, pltpu.roll(xr, 1, 1), 0.0)
        si = jnp.where(lane >= 1, pltpu.roll(xi, 1, 1), 0.0)
    else:
        sr = s0_ref[0, :n, :]
        si = s0_ref[0, n:, :]

    for j in range(n_sub):
        if j == 0:
            pr, pi = sr, si
        else:
            ar, ai = col(j - 1)
            tr, ti = _cmul(ar, ai, sr, si)
            pr, pi = lbuf[j - 1, :n, :] + tr, lbuf[j - 1, n:, :] + ti
        sp = jnp.concatenate([pr, pi], axis=0).astype(BF16)
        y = ybuf[j * width:(j + 1) * width, :] + jnp.dot(qm, sp, preferred_element_type=F32)
        yt_ref[0, j * width:(j + 1) * width, :] = y.astype(BF16)
    ar, ai = col(n_sub - 1)
    tr, ti = _cmul(ar, ai, sr, si)
    sf_ref[0, :n, :] = lbuf[n_sub - 1, :n, :] + tr
    sf_ref[0, n:, :] = lbuf[n_sub - 1, n:, :] + ti


def _s5_call(zt, wk, qm, tabr, tabi, s0, *, n_sub, scan):
    g, rows, lanes = zt.shape
    n = S5_STATE
    width = SUB * S5_GROUP
    if scan:
        assert n_sub == N_SUBPOW and lanes <= (1 << N_DBLPOW)
    grp = lambda shape: pl.BlockSpec((1,) + shape, lambda i: (i, 0, 0))
    in_specs = [grp((rows, lanes)), grp((width + 2 * n, width)), grp((width, 2 * n)),
                grp((n, LANES)), grp((n, LANES))]
    args = [zt, wk, qm, tabr, tabi]
    if not scan:
        in_specs.append(grp((2 * n, lanes)))
        args.append(s0)
    kern = functools.partial(_s5_kernel, n_sub=n_sub, lanes=lanes, scan=scan)
    return pl.pallas_call(
        kern,
        out_shape=(jax.ShapeDtypeStruct((g, rows, lanes), BF16),
                   jax.ShapeDtypeStruct((g, 2 * n, lanes), F32)),
        grid=(g,),
        in_specs=in_specs,
        out_specs=(grp((rows, lanes)), grp((2 * n, lanes))),
        scratch_shapes=[pltpu.VMEM((rows, lanes), F32),
                        pltpu.VMEM((n_sub, 2 * n, lanes), F32)],
        compiler_params=_cparams(1),
        name="s5",
    )(*args)


def _glu_kernel(yt_ref, w_ref, b_ref, m_ref, *, n_ph, rows, lane_rows):
    gs = []
    for ph in range(n_ph):
        yt = yt_ref[:, ph, :, :].astype(F32).reshape(S5_WIDTH, lane_rows)
        gs.append(jax.nn.gelu(yt.T[:rows]))
    gy = jnp.concatenate(gs, axis=0)
    gate = jnp.dot(gy.astype(BF16), w_ref[...], preferred_element_type=F32) + b_ref[...]
    m = gy * jax.nn.sigmoid(gate)
    for ph in range(n_ph):
        _oct_store(m_ref, ph, m[ph * rows:(ph + 1) * rows])


def _glu_call(yt4, w_glu_b, b_glu, *, rows, n_ph, row_blk):
    g, t, p, lane_rows = yt4.shape
    n_rb, lane_blk, lane_rows_expected = _phase_blocks(rows, row_blk)
    assert lane_rows == lane_rows_expected
    kern = functools.partial(_glu_kernel, n_ph=n_ph, rows=row_blk, lane_rows=lane_blk)
    oct_shape = _oct_shape(rows, t, S5_WIDTH)
    return pl.pallas_call(
        kern,
        out_shape=jax.ShapeDtypeStruct(oct_shape, F32),
        grid=(n_rb, t // n_ph),
        in_specs=[
            pl.BlockSpec((g, n_ph, p, lane_blk), lambda i, j: (0, j, 0, i)),
            _const_spec((S5_WIDTH, S5_WIDTH)),
            _const_spec((1, S5_WIDTH)),
        ],
        out_specs=pl.BlockSpec((row_blk // OCT, oct_shape[1], n_ph, OCT, LANES),
                               lambda i, j: (i, 0, j, 0, 0)),
        compiler_params=_cparams(2),
        name="glu",
    )(yt4, w_glu_b, b_glu)


def _main_kernel(x_ref, m_ref, u_ref, v_ref, mod_ref, g2_ref, gf_ref, gw_ref, gbt_ref,
                 wo_ref, wgu_ref, wd_ref, y_ref, us_ref, vs_ref, ygm_ref, *, tm, cl, seq, n_fc):
    t = tm // OCT
    us_ref[...] = _oct_load(u_ref, t)
    vs_ref[...] = _oct_load(v_ref, t).astype(BF16)
    hd = GM_HEAD_DIM
    blk_i = _div_pow2(lax.broadcasted_iota(jnp.int32, (cl, cl), 0), CHUNK)
    blk_j = _div_pow2(lax.broadcasted_iota(jnp.int32, (cl, cl), 1), CHUNK)
    causal = blk_j <= blk_i
    for h in range(GM_HEADS):
        wm = jnp.where(causal, gw_ref[h, :cl, :cl], 0.0).astype(BF16)
        bcol = gbt_ref[:cl, h:h + 1]
        for ci in range(tm // cl):
            rs = slice(ci * cl, (ci + 1) * cl)
            cs = slice(h * hd, (h + 1) * hd)
            mixed = jnp.dot(wm, vs_ref[rs, cs], preferred_element_type=F32) + bcol
            ygm_ref[rs, cs] = (us_ref[rs, cs] * mixed).astype(BF16)

    attn = (jnp.dot(_oct_load(m_ref, t).astype(BF16), wo_ref[:S5_WIDTH, :], preferred_element_type=F32)
            + jnp.dot(ygm_ref[...], wo_ref[S5_WIDTH:, :], preferred_element_type=F32))

    n_streams = mod_ref.shape[0]

    def mod_rows(idx):
        if n_streams == 1:
            return mod_ref[0, idx:idx + 1, :]
        return jnp.concatenate(
            [jnp.broadcast_to(mod_ref[s, idx:idx + 1, :], (seq, D_MODEL)) for s in range(n_streams)],
            axis=0)

    gate1, shift2, scale2, gate2 = mod_rows(2), mod_rows(3), mod_rows(4), mod_rows(5)
    x1 = x_ref[...] + gate1 * attn
    h2 = (x1 * _rms_scale(x1) * (g2_ref[...] * (1.0 + scale2)) + shift2).astype(BF16)

    fc = D_FF // n_fc
    acc = None
    for c in range(n_fc):
        gg = jnp.dot(h2, wgu_ref[:, c * fc:(c + 1) * fc], preferred_element_type=F32)
        up = jnp.dot(h2, wgu_ref[:, D_FF + c * fc:D_FF + (c + 1) * fc], preferred_element_type=F32)
        act = (gg * jax.nn.sigmoid(gg) * up).astype(BF16)
        part = jnp.dot(act, wd_ref[c * fc:(c + 1) * fc, :], preferred_element_type=F32)
        acc = part if acc is None else acc + part
    x2 = x1 + gate2 * acc
    y_ref[...] = x2 * _rms_scale(x2) * gf_ref[...]


def _main_call(x2d, m, u, v, mod, norm2_g, final_g, gm_w, gm_bt, w_out_b, w_gu_b, w_down_b,
               *, seq, n_fc):
    n_tok = x2d.shape[0]
    n_oct, n_lb, t, _, _ = m.shape
    tm = OCT * t
    assert n_oct * tm == n_tok
    streams_per_tile = max(1, tm // seq)
    tiles_per_stream = max(1, seq // tm)
    assert streams_per_tile * seq == tm or tiles_per_stream * tm == seq
    cl = min(GM_CHUNK, seq)
    kern = functools.partial(_main_kernel, tm=tm, cl=cl, seq=seq, n_fc=n_fc)
    tok = lambda width: pl.BlockSpec((tm, width), lambda i: (i, 0))
    oct_rows = n_lb * t * OCT
    octs = pl.BlockSpec((oct_rows, LANES), lambda i: (i, 0))
    m, u, v = (a.reshape(n_oct * oct_rows, LANES) for a in (m, u, v))
    return pl.pallas_call(
        kern,
        out_shape=jax.ShapeDtypeStruct((n_tok, D_MODEL), F32),
        grid=(n_oct,),
        in_specs=[
            tok(D_MODEL), octs, octs, octs,
            pl.BlockSpec((streams_per_tile, N_MOD, D_MODEL), lambda i: (i // tiles_per_stream, 0, 0)),
            _const_spec((1, D_MODEL)),
            _const_spec((1, D_MODEL)),
            _const_spec((GM_HEADS, GM_CHUNK, GM_CHUNK)),
            _const_spec((GM_CHUNK, GM_HEADS)),
            _const_spec((D_MODEL, D_MODEL)),
            _const_spec((D_MODEL, 2 * D_FF)),
            _const_spec((D_FF, D_MODEL)),
        ],
        out_specs=tok(D_MODEL),
        scratch_shapes=[pltpu.VMEM((tm, GM_WIDTH), F32), pltpu.VMEM((tm, GM_WIDTH), BF16),
                        pltpu.VMEM((tm, GM_WIDTH), BF16)],
        compiler_params=_cparams(1),
        name="main",
    )(x2d, m, u, v, mod, norm2_g, final_g, gm_w, gm_bt, w_out_b, w_gu_b, w_down_b)


def _trunk(x, mod, s0, prm, *, t, n_ph, row_blk, n_fc):
    b, seq, _ = x.shape
    n_chunks = seq // t
    rows = b * n_chunks
    scan = s0 is None
    assert (b == 1) if scan else (n_chunks == 1)
    n_sub = t // SUB

    zt4, u, v = _inproj_call(
        x.reshape(rows, t, D_MODEL), mod[:, 0], mod[:, 1], prm["norm1_g"], prm["w_in"],
        prm["ln_g"], prm["ln_b"], n_ph=n_ph, row_blk=row_blk)
    lane_rows = zt4.shape[-1]

    if scan:
        s0_l = None
    else:
        s0_l = jnp.transpose(s0, (1, 2, 0))
        s0_l = jnp.pad(s0_l, ((0, 0), (0, 0), (0, lane_rows - rows)))
    yt, sfin = _s5_call(zt4.reshape(S5_GROUPS, t * S5_GROUP, lane_rows),
                        prm["wk"], prm["qm"], prm["tabr"], prm["tabi"], s0_l,
                        n_sub=n_sub, scan=scan)
    m = _glu_call(yt.reshape(S5_GROUPS, t, S5_GROUP, lane_rows), prm["w_glu"], prm["b_glu"],
                  rows=rows, n_ph=n_ph, row_blk=row_blk)

    y = _main_call(
        x.reshape(b * seq, D_MODEL), m, u, v, mod, prm["norm2_g"], prm["final_g"],
        prm["gm_w"], prm["gm_bt"], prm["w_out"], prm["w_gu"], prm["w_down"], seq=seq, n_fc=n_fc)

    if scan:
        fin = sfin[:, :, rows - 1][None]
    else:
        fin = jnp.transpose(sfin[:, :, :rows], (2, 0, 1))
    return y.reshape(b, seq, D_MODEL), fin[..., :S5_STATE], fin[..., S5_STATE:], v


def _oct_to_tokens(a, b, seq):
    n_oct, n_lb, t, _, _ = a.shape
    return jnp.transpose(a, (0, 3, 2, 1, 4)).reshape(b, seq, n_lb * LANES)


def kernel(x_prompt, x_sample, state_s5_re, state_s5_im, c_prompt, c_sample, norm1_g, norm2_g, w_ada, b_ada, w_in, s5_lambda_re, s5_lambda_im, s5_log_step, s5_b_re, s5_b_im, s5_c_re, s5_c_im, s5_d, s5_w_glu, s5_b_glu, gm_ln_g, gm_ln_b, gm_w_s, gm_b_s, w_out, ffn_w_gu, ffn_w_down, final_g):
    depth = w_in.shape[0]
    assert depth == 1
    l = 0
    n_p = c_prompt.shape[0]
    n_s = c_sample.shape[0]

    c_all = jnp.concatenate([c_prompt, c_sample], axis=0)
    c_pad = jnp.pad(c_all, ((0, -c_all.shape[0] % 8), (0, 0)))
    mod_all = _ada_call(c_pad, w_ada[l], b_ada[l][None, :])
    mod_p = mod_all[:n_p].reshape(n_p, N_MOD, D_MODEL)
    mod_s = mod_all[n_p:n_p + n_s].reshape(n_s, N_MOD, D_MODEL)

    wk, qm, tabr, tabi = _s5_prep_call(
        s5_lambda_re[l], s5_lambda_im[l], s5_log_step[l], s5_b_re[l], s5_b_im[l],
        s5_c_re[l], s5_c_im[l], s5_d[l])

    prm = dict(
        norm1_g=norm1_g[l][None, :], norm2_g=norm2_g[l][None, :], final_g=final_g[None, :],
        w_in=w_in[l].astype(BF16), ln_g=gm_ln_g[l][None, :], ln_b=gm_ln_b[l][None, :],
        wk=wk, qm=qm, tabr=tabr, tabi=tabi,
        w_glu=s5_w_glu[l].astype(BF16), b_glu=s5_b_glu[l][None, :],
        gm_w=gm_w_s[l], gm_bt=jnp.transpose(gm_b_s[l]),
        w_out=w_out[l].astype(BF16), w_gu=ffn_w_gu[l].astype(BF16),
        w_down=ffn_w_down[l].astype(BF16),
    )

    yp, pre, pim, _ = _trunk(x_prompt, mod_p, None, prm,
                             t=N_SUBPOW * SUB, n_ph=8, row_blk=LANES, n_fc=2)
    s0 = jnp.concatenate([state_s5_re[l], state_s5_im[l]], axis=-1)
    n_b, seq_s, _ = x_sample.shape
    ys, sre, sim, vs = _trunk(x_sample, mod_s, s0, prm,
                              t=seq_s, n_ph=seq_s, row_blk=n_b, n_fc=2)
    vs = _oct_to_tokens(vs, n_b, seq_s)

    return (yp, ys, pre[None], pim[None], sre[None], sim[None], vs[None])
```

```python
import functools
import math

import jax
import jax.numpy as jnp
from jax import lax
from jax.experimental import pallas as pl
from jax.experimental.pallas import tpu as pltpu

D_MODEL = 1024
S5_WIDTH = 512
S5_GROUP = 16
S5_GROUPS = 32
S5_STATE = 64
GM_WIDTH = 512
GM_CHUNK = 128
GM_HEADS = 8
GM_HEAD_DIM = 64
CHUNK = 64
IN_WIDTH = S5_WIDTH + 2 * GM_WIDTH
D_FF = 2816
N_MOD = 6
EPS = 1e-6

LANES = 128
MXU_DIM = 256
VMEM_LIMIT_BYTES = 56 * 1024 * 1024

SUB = MXU_DIM // S5_GROUP
N_SUBPOW = 4
N_DBLPOW = 8
PREP_GROUPS_PER_STEP = 4
S5_GROUPS_PER_STEP_SHORT = 4

F32 = jnp.float32
BF16 = jnp.bfloat16
HIGHEST = lax.Precision.HIGHEST


def _cparams(n_grid_axes):
    return pltpu.CompilerParams(
        dimension_semantics=("arbitrary",) * n_grid_axes,
        vmem_limit_bytes=VMEM_LIMIT_BYTES,
    )


def _const_spec(shape):
    nd = len(shape)
    return pl.BlockSpec(shape, lambda *_: (0,) * nd, pipeline_mode=pl.Buffered(1))


def _rms_scale(x):
    return lax.rsqrt(jnp.mean(x * x, axis=-1, keepdims=True) + EPS)


def _cmul(ar, ai, xr, xi):
    return ar * xr - ai * xi, ar * xi + ai * xr


OCT = 8


def _oct_shape(n_rows, t, width):
    assert n_rows % OCT == 0 and width % LANES == 0
    return (n_rows // OCT, width // LANES, t, OCT, LANES)


def _oct_store(ref, ph, val):
    n_oct, n_lb = ref.shape[0], ref.shape[1]
    for lb in range(n_lb):
        ref[:, lb, ph, :, :] = val[:, lb * LANES:(lb + 1) * LANES].reshape(n_oct, OCT, LANES)


def _oct_load(ref, t):
    n_lb = ref.shape[0] // (t * OCT)
    chunks = []
    for cc in range(OCT):
        chunks.append(jnp.concatenate(
            [ref[pl.ds(lb * t * OCT + cc, t, stride=OCT), :] for lb in range(n_lb)], axis=1))
    return jnp.concatenate(chunks, axis=0)


def _div_pow2(idx, divisor):
    shift = divisor.bit_length() - 1
    assert divisor == 1 << shift
    return lax.shift_right_logical(idx, shift)


def _mod_pow2(idx, divisor):
    assert divisor & (divisor - 1) == 0
    return lax.bitwise_and(idx, divisor - 1)


def _ada_kernel(c_ref, w_ref, b_ref, o_ref):
    c = c_ref[...]
    s = c * jax.nn.sigmoid(c)
    w = w_ref[...]
    s_hi, w_hi = s.astype(BF16), w.astype(BF16)
    s_lo = (s - s_hi.astype(F32)).astype(BF16)
    w_lo = (w - w_hi.astype(F32)).astype(BF16)
    dot = functools.partial(jnp.dot, preferred_element_type=F32)
    o_ref[...] = dot(s_hi, w_hi) + dot(s_lo, w_hi) + dot(s_hi, w_lo) + b_ref[...]


def _ada_call(c_pad, w_ada, b_ada):
    rows = c_pad.shape[0]
    n_out = w_ada.shape[1]
    bn = D_MODEL
    return pl.pallas_call(
        _ada_kernel,
        out_shape=jax.ShapeDtypeStruct((rows, n_out), F32),
        grid=(n_out // bn,),
        in_specs=[
            pl.BlockSpec((rows, D_MODEL), lambda j: (0, 0)),
            pl.BlockSpec((D_MODEL, bn), lambda j: (0, j)),
            pl.BlockSpec((1, bn), lambda j: (0, j)),
        ],
        out_specs=pl.BlockSpec((rows, bn), lambda j: (0, j)),
        compiler_params=_cparams(1),
        name="ada",
    )(c_pad, w_ada, b_ada)


def _discretise(lr, li, ls):
    step = jnp.exp(ls)
    mag = jnp.exp(lr * step)
    ar = mag * jnp.cos(li * step)
    ai = mag * jnp.sin(li * step)
    den = lr * lr + li * li
    fr = ((ar - 1.0) * lr + ai * li) / den
    fi = (ai * lr - (ar - 1.0) * li) / den
    return ar, ai, fr, fi


def _s5_prep_kernel(*refs):
    for gi in range(refs[0].shape[0]):
        _s5_prep_group(*[r.at[gi] for r in refs])


def _s5_prep_group(lr2_ref, li2_ref, ls2_ref, lrc_ref, lic_ref, lsc_ref,
                   btr_ref, bti_ref, c2r_ref, c2i_ref, dt_ref,
                   wk_ref, qm_ref, tabr_ref, tabi_ref):
    n, sub, p = S5_STATE, SUB, S5_GROUP
    width = sub * p

    a2r, a2i, _, _ = _discretise(lr2_ref[...], li2_ref[...], ls2_ref[...])
    lane2 = lax.broadcasted_iota(jnp.int32, (p, 2 * n), 1)
    first = lane2 < n
    c2r = c2r_ref[...]
    c2i = c2i_ref[...]
    pr = jnp.ones_like(a2r)
    pi = jnp.zeros_like(a2r)
    ccat = []
    for _ in range(sub + 1):
        ccat.append(c2r * jnp.where(first, pr, -pi) + c2i * jnp.where(first, -pi, -pr))
        pr, pi = _cmul(a2r, a2i, pr, pi)
    qm_ref[...] = jnp.concatenate(ccat[1:], axis=0).astype(BF16)
    rcat = jnp.concatenate(ccat[:sub], axis=0)

    acr, aci, fr, fi = _discretise(lrc_ref[...], lic_ref[...], lsc_ref[...])
    btr = btr_ref[...]
    bti = bti_ref[...]
    bbr, bbi = _cmul(fr, fi, btr, bti)
    pows = [(jnp.ones_like(acr), jnp.zeros_like(acr))]
    for _ in range(sub):
        pows.append(_cmul(acr, aci, *pows[-1]))
    lane_blk = _div_pow2(lax.broadcasted_iota(jnp.int32, (n, width), 1), p)
    apr = jnp.zeros((n, width), F32)
    api = jnp.zeros((n, width), F32)
    for k in range(sub):
        sel = lane_blk == k
        apr = jnp.where(sel, pows[sub - 1 - k][0], apr)
        api = jnp.where(sel, pows[sub - 1 - k][1], api)
    pmr, pmi = _cmul(apr, api, bbr, bbi)
    wk_ref[width:width + n, :] = pmr.astype(BF16)
    wk_ref[width + n:, :] = pmi.astype(BF16)

    bbcat = jnp.concatenate([bbr, bbi], axis=0)
    kt = jnp.dot(rcat, bbcat, preferred_element_type=F32, precision=HIGHEST)
    row = lax.broadcasted_iota(jnp.int32, (width, width), 0)
    lane = lax.broadcasted_iota(jnp.int32, (width, width), 1)
    kt = kt + jnp.where(row == _mod_pow2(lane, p), dt_ref[...], 0.0)
    col_blk = _div_pow2(lane, p)
    m16 = jnp.zeros((width, width), F32)
    for k in range(sub):
        if k == 0:
            shifted = kt
        else:
            shifted = jnp.concatenate(
                [jnp.zeros((k * p, width), F32), kt[:width - k * p]], axis=0)
        m16 = jnp.where(col_blk == k, shifted, m16)
    wk_ref[:width, :] = m16.astype(BF16)

    tab_lane = lax.broadcasted_iota(jnp.int32, (n, LANES), 1)
    tabr = jnp.zeros((n, LANES), F32)
    tabi = jnp.zeros((n, LANES), F32)
    cols = []
    cur = pows[sub]
    for _ in range(N_SUBPOW):
        cols.append(cur)
        cur = _cmul(pows[sub][0], pows[sub][1], *cur)
    cur = cols[N_SUBPOW - 1]
    for _ in range(N_DBLPOW):
        cols.append(cur)
        cur = _cmul(cur[0], cur[1], *cur)
    for idx, (vr, vi) in enumerate(cols):
        tabr = jnp.where(tab_lane == idx, vr, tabr)
        tabi = jnp.where(tab_lane == idx, vi, tabi)
    tabr_ref[...] = tabr
    tabi_ref[...] = tabi


def _s5_prep_call(lam_re, lam_im, log_step, b_re, b_im, c_re, c_im, d):
    g, n, p, sub = S5_GROUPS, S5_STATE, S5_GROUP, SUB
    width = sub * p
    ls = jnp.broadcast_to(log_step[:, None], (g, n))
    row2 = lambda a: jnp.concatenate([a, a], axis=-1)[:, None, :]
    col = lambda a: a[:, :, None]
    btr = jnp.tile(b_re, (1, 1, sub))
    bti = jnp.tile(b_im, (1, 1, sub))
    c2r = jnp.concatenate([c_re, c_re], axis=-1)
    c2i = jnp.concatenate([c_im, c_im], axis=-1)
    dt = jnp.concatenate([d, jnp.zeros((g, width - p), F32)], axis=1)[:, :, None]
    grp = lambda shape: pl.BlockSpec((PREP_GROUPS_PER_STEP,) + shape, lambda i: (i, 0, 0))
    return pl.pallas_call(
        _s5_prep_kernel,
        out_shape=(
            jax.ShapeDtypeStruct((g, width + 2 * n, width), BF16),
            jax.ShapeDtypeStruct((g, width, 2 * n), BF16),
            jax.ShapeDtypeStruct((g, n, LANES), F32),
            jax.ShapeDtypeStruct((g, n, LANES), F32),
        ),
        grid=(g // PREP_GROUPS_PER_STEP,),
        in_specs=[grp((1, 2 * n))] * 3 + [grp((n, 1))] * 3
        + [grp((n, width))] * 2 + [grp((p, 2 * n))] * 2 + [grp((width, 1))],
        out_specs=(grp((width + 2 * n, width)), grp((width, 2 * n)),
                   grp((n, LANES)), grp((n, LANES))),
        compiler_params=_cparams(1),
        name="s5_prep",
    )(row2(lam_re), row2(lam_im), row2(ls), col(lam_re), col(lam_im), col(ls),
      btr, bti, c2r, c2i, dt)


def _inproj_kernel(x_ref, sh_ref, sc_ref, g1_ref, w_ref, lng_ref, lnb_ref,
                   zt_ref, u_ref, v_ref, hs_ref, *, n_ph, rows, lane_rows):
    m = rows * n_ph
    gain = (g1_ref[...] * (1.0 + sc_ref[...]))[:, None, :]
    shift = sh_ref[...][:, None, :]
    x3 = x_ref[...]
    h = (x3 * _rms_scale(x3) * gain + shift).reshape(m, D_MODEL)
    hb = h.astype(BF16)

    zu = jnp.dot(hb, w_ref[:, S5_WIDTH:S5_WIDTH + GM_WIDTH], preferred_element_type=F32)
    u_ref[...] = jax.nn.gelu(zu).reshape(rows, n_ph, GM_WIDTH)

    zv = jnp.dot(hb, w_ref[:, S5_WIDTH + GM_WIDTH:], preferred_element_type=F32)
    gv = jax.nn.gelu(zv)
    mu = jnp.mean(gv, axis=-1, keepdims=True)
    cen = gv - mu
    var = jnp.mean(cen * cen, axis=-1, keepdims=True)
    v = cen * lax.rsqrt(var + EPS) * lng_ref[...] + lnb_ref[...]
    v_ref[...] = v.reshape(rows, n_ph, GM_WIDTH)

    n_lb = D_MODEL // LANES
    for lb in range(n_lb):
        hs_ref[lb] = h[:, lb * LANES:(lb + 1) * LANES]
    hp = jnp.concatenate(
        [jnp.concatenate([hs_ref[lb, pl.ds(ph, rows, stride=n_ph), :] for lb in range(n_lb)], axis=1)
         for ph in range(n_ph)], axis=0).astype(BF16)
    z5 = jnp.dot(hp, w_ref[:, :S5_WIDTH], preferred_element_type=F32)
    for ph in range(n_ph):
        zz = z5[ph * rows:(ph + 1) * rows]
        if lane_rows > rows:
            zz = jnp.concatenate([zz, jnp.zeros((lane_rows - rows, S5_WIDTH), F32)], axis=0)
        zt = zz.T.reshape(S5_GROUPS, S5_GROUP, lane_rows)
        zt_ref[:, ph, :, :] = zt.astype(BF16)


def _phase_blocks(n_rows, row_blk):
    if row_blk % LANES == 0:
        assert n_rows % row_blk == 0
        return n_rows // row_blk, row_blk, n_rows
    assert row_blk == n_rows
    lane_rows = -(-n_rows // LANES) * LANES
    return 1, lane_rows, lane_rows


def _inproj_call(x3, shift1, scale1, norm1_g, w_in_b, ln_g, ln_b, *, n_ph, row_blk):
    rows, t, _ = x3.shape
    mrows = shift1.shape[0]
    assert mrows in (1, row_blk)
    n_rb, lane_blk, lane_rows = _phase_blocks(rows, row_blk)
    kern = functools.partial(_inproj_kernel, n_ph=n_ph, rows=row_blk, lane_rows=lane_blk)
    tok_spec = pl.BlockSpec((row_blk, n_ph, GM_WIDTH), lambda i, j: (i, j, 0))
    return pl.pallas_call(
        kern,
        out_shape=(
            jax.ShapeDtypeStruct((S5_GROUPS, t, S5_GROUP, lane_rows), BF16),
            jax.ShapeDtypeStruct((rows, t, GM_WIDTH), F32),
            jax.ShapeDtypeStruct((rows, t, GM_WIDTH), F32),
        ),
        grid=(n_rb, t // n_ph),
        in_specs=[
            pl.BlockSpec((row_blk, n_ph, D_MODEL), lambda i, j: (i, j, 0)),
            _const_spec((mrows, D_MODEL)),
            _const_spec((mrows, D_MODEL)),
            _const_spec((1, D_MODEL)),
            _const_spec((D_MODEL, IN_WIDTH)),
            _const_spec((1, GM_WIDTH)),
            _const_spec((1, GM_WIDTH)),
        ],
        out_specs=(
            pl.BlockSpec((S5_GROUPS, n_ph, S5_GROUP, lane_blk), lambda i, j: (0, j, 0, i)),
            tok_spec, tok_spec,
        ),
        scratch_shapes=[pltpu.VMEM((D_MODEL // LANES, row_blk * n_ph, LANES), F32)],
        compiler_params=_cparams(2),
        name="inproj",
    )(x3, shift1, scale1, norm1_g, w_in_b, ln_g, ln_b)


def _s5_kernel(*refs, n_sub, lanes, scan):
    *grouped, ybuf, lbuf = refs
    for gi in range(grouped[0].shape[0]):
        _s5_group(*[r.at[gi] for r in grouped], ybuf, lbuf, n_sub=n_sub, lanes=lanes, scan=scan)


def _s5_group(*refs, n_sub, lanes, scan):
    if scan:
        zt_ref, wk_ref, qm_ref, tabr_ref, tabi_ref, yt_ref, sf_ref, ybuf, lbuf = refs
        s0_ref = None
    else:
        zt_ref, wk_ref, qm_ref, tabr_ref, tabi_ref, s0_ref, yt_ref, sf_ref, ybuf, lbuf = refs
    n = S5_STATE
    width = SUB * S5_GROUP
    wk = wk_ref[...]
    qm = qm_ref[...]
    tabr = tabr_ref[...]
    tabi = tabi_ref[...]
    col = lambda i: (tabr[:, i:i + 1], tabi[:, i:i + 1])

    a1r, a1i = col(0)
    lr = li = None
    for j in range(n_sub):
        u = zt_ref[j * width:(j + 1) * width, :]
        r = jnp.dot(wk, u, preferred_element_type=F32)
        ybuf[j * width:(j + 1) * width, :] = r[:width]
        wr = r[width:width + n]
        wi = r[width + n:]
        if j == 0:
            lr, li = wr, wi
        else:
            tr, ti = _cmul(a1r, a1i, lr, li)
            lr, li = tr + wr, ti + wi
        lbuf[j, :n, :] = lr
        lbuf[j, n:, :] = li

    if scan:
        lane = lax.broadcasted_iota(jnp.int32, (n, lanes), 1)
        xr, xi = lr, li
        for i in range(int(math.log2(lanes))):
            sh = 1 << i
            mr, mi = col(N_SUBPOW + i)
            rr = jnp.where(lane >= sh, pltpu.roll(xr, sh, 1), 0.0)
            ri = jnp.where(lane >= sh, pltpu.roll(xi, sh, 1), 0.0)
            tr, ti = _cmul(mr, mi, rr, ri)
            xr, xi = xr + tr, xi + ti
        sr = jnp.where(lane >= 1, pltpu.roll(xr, 1, 1), 0.0)
        si = jnp.where(lane >= 1, pltpu.roll(xi, 1, 1), 0.0)
    else:
        sr = s0_ref[:n, :]
        si = s0_ref[n:, :]

    for j in range(n_sub):
        if j == 0:
            pr, pi = sr, si
        else:
            ar, ai = col(j - 1)
            tr, ti = _cmul(ar, ai, sr, si)
            pr, pi = lbuf[j - 1, :n, :] + tr, lbuf[j - 1, n:, :] + ti
        sp = jnp.concatenate([pr, pi], axis=0).astype(BF16)
        y = ybuf[j * width:(j + 1) * width, :] + jnp.dot(qm, sp, preferred_element_type=F32)
        yt_ref[j * width:(j + 1) * width, :] = y.astype(BF16)
    ar, ai = col(n_sub - 1)
    tr, ti = _cmul(ar, ai, sr, si)
    sf_ref[:n, :] = lbuf[n_sub - 1, :n, :] + tr
    sf_ref[n:, :] = lbuf[n_sub - 1, n:, :] + ti


def _s5_call(zt, wk, qm, tabr, tabi, s0, *, n_sub, scan):
    g, rows, lanes = zt.shape
    n = S5_STATE
    width = SUB * S5_GROUP
    if scan:
        assert n_sub == N_SUBPOW and lanes <= (1 << N_DBLPOW)
    gps = 1 if scan else S5_GROUPS_PER_STEP_SHORT
    grp = lambda shape: pl.BlockSpec((gps,) + shape, lambda i: (i, 0, 0))
    in_specs = [grp((rows, lanes)), grp((width + 2 * n, width)), grp((width, 2 * n)),
                grp((n, LANES)), grp((n, LANES))]
    args = [zt, wk, qm, tabr, tabi]
    if not scan:
        in_specs.append(grp((2 * n, lanes)))
        args.append(s0)
    kern = functools.partial(_s5_kernel, n_sub=n_sub, lanes=lanes, scan=scan)
    return pl.pallas_call(
        kern,
        out_shape=(jax.ShapeDtypeStruct((g, rows, lanes), BF16),
                   jax.ShapeDtypeStruct((g, 2 * n, lanes), F32)),
        grid=(g // gps,),
        in_specs=in_specs,
        out_specs=(grp((rows, lanes)), grp((2 * n, lanes))),
        scratch_shapes=[pltpu.VMEM((rows, lanes), F32),
                        pltpu.VMEM((n_sub, 2 * n, lanes), F32)],
        compiler_params=_cparams(1),
        name="s5",
    )(*args)


def _glu_kernel(yt_ref, w_ref, b_ref, m_ref, *, n_ph, rows, lane_rows):
    gs = []
    for ph in range(n_ph):
        yt = yt_ref[:, ph, :, :].astype(F32).reshape(S5_WIDTH, lane_rows)
        gs.append(jax.nn.gelu(yt.T[:rows]))
    gy = jnp.concatenate(gs, axis=0)
    gate = jnp.dot(gy.astype(BF16), w_ref[...], preferred_element_type=F32) + b_ref[...]
    m = gy * jax.nn.sigmoid(gate)
    for ph in range(n_ph):
        _oct_store(m_ref, ph, m[ph * rows:(ph + 1) * rows])


def _glu_call(yt4, w_glu_b, b_glu, *, rows, n_ph, row_blk):
    g, t, p, lane_rows = yt4.shape
    n_rb, lane_blk, lane_rows_expected = _phase_blocks(rows, row_blk)
    assert lane_rows == lane_rows_expected
    kern = functools.partial(_glu_kernel, n_ph=n_ph, rows=row_blk, lane_rows=lane_blk)
    oct_shape = _oct_shape(rows, t, S5_WIDTH)
    return pl.pallas_call(
        kern,
        out_shape=jax.ShapeDtypeStruct(oct_shape, F32),
        grid=(n_rb, t // n_ph),
        in_specs=[
            pl.BlockSpec((g, n_ph, p, lane_blk), lambda i, j: (0, j, 0, i)),
            _const_spec((S5_WIDTH, S5_WIDTH)),
            _const_spec((1, S5_WIDTH)),
        ],
        out_specs=pl.BlockSpec((row_blk // OCT, oct_shape[1], n_ph, OCT, LANES),
                               lambda i, j: (i, 0, j, 0, 0)),
        compiler_params=_cparams(2),
        name="glu",
    )(yt4, w_glu_b, b_glu)


def _main_kernel(x_ref, m_ref, u_ref, v_ref, mod_ref, g2_ref, gf_ref, gw_ref, gbt_ref,
                 wo_ref, wgu_ref, wd_ref, y_ref, ygm_ref, *, tm, cl, seq, n_fc):
    t = tm // OCT
    hd = GM_HEAD_DIM
    blk_i = _div_pow2(lax.broadcasted_iota(jnp.int32, (cl, cl), 0), CHUNK)
    blk_j = _div_pow2(lax.broadcasted_iota(jnp.int32, (cl, cl), 1), CHUNK)
    causal = blk_j <= blk_i
    for h in range(GM_HEADS):
        wm = jnp.where(causal, gw_ref[h, :cl, :cl], 0.0).astype(BF16)
        bcol = gbt_ref[:cl, h:h + 1]
        for ci in range(tm // cl):
            rs = slice(ci * cl, (ci + 1) * cl)
            cs = slice(h * hd, (h + 1) * hd)
            mixed = jnp.dot(wm, v_ref[rs, cs].astype(BF16), preferred_element_type=F32) + bcol
            ygm_ref[rs, cs] = (u_ref[rs, cs] * mixed).astype(BF16)

    attn = (jnp.dot(_oct_load(m_ref, t).astype(BF16), wo_ref[:S5_WIDTH, :], preferred_element_type=F32)
            + jnp.dot(ygm_ref[...], wo_ref[S5_WIDTH:, :], preferred_element_type=F32))

    n_streams = mod_ref.shape[0]

    def mod_rows(idx):
        if n_streams == 1:
            return mod_ref[0, idx:idx + 1, :]
        return jnp.concatenate(
            [jnp.broadcast_to(mod_ref[s, idx:idx + 1, :], (seq, D_MODEL)) for s in range(n_streams)],
            axis=0)

    gate1, shift2, scale2, gate2 = mod_rows(2), mod_rows(3), mod_rows(4), mod_rows(5)
    x1 = x_ref[...] + gate1 * attn
    h2 = (x1 * _rms_scale(x1) * (g2_ref[...] * (1.0 + scale2)) + shift2).astype(BF16)

    fc = D_FF // n_fc
    acc = None
    for c in range(n_fc):
        gg = jnp.dot(h2, wgu_ref[:, c * fc:(c + 1) * fc], preferred_element_type=F32)
        up = jnp.dot(h2, wgu_ref[:, D_FF + c * fc:D_FF + (c + 1) * fc], preferred_element_type=F32)
        act = (gg * jax.nn.sigmoid(gg) * up).astype(BF16)
        part = jnp.dot(act, wd_ref[c * fc:(c + 1) * fc, :], preferred_element_type=F32)
        acc = part if acc is None else acc + part
    x2 = x1 + gate2 * acc
    y_ref[...] = x2 * _rms_scale(x2) * gf_ref[...]


def _main_call(x2d, m, u, v, mod, norm2_g, final_g, gm_w, gm_bt, w_out_b, w_gu_b, w_down_b,
               *, seq, n_fc):
    n_tok = x2d.shape[0]
    n_oct, n_lb, t, _, _ = m.shape
    tm = OCT * t
    assert n_oct * tm == n_tok
    streams_per_tile = max(1, tm // seq)
    tiles_per_stream = max(1, seq // tm)
    assert streams_per_tile * seq == tm or tiles_per_stream * tm == seq
    cl = min(GM_CHUNK, seq)
    kern = functools.partial(_main_kernel, tm=tm, cl=cl, seq=seq, n_fc=n_fc)
    tok = lambda width: pl.BlockSpec((tm, width), lambda i: (i, 0))
    oct_rows = n_lb * t * OCT
    octs = pl.BlockSpec((oct_rows, LANES), lambda i: (i, 0))
    m = m.reshape(n_oct * oct_rows, LANES)
    return pl.pallas_call(
        kern,
        out_shape=jax.ShapeDtypeStruct((n_tok, D_MODEL), F32),
        grid=(n_oct,),
        in_specs=[
            tok(D_MODEL), octs, tok(GM_WIDTH), tok(GM_WIDTH),
            pl.BlockSpec((streams_per_tile, N_MOD, D_MODEL), lambda i: (i // tiles_per_stream, 0, 0)),
            _const_spec((1, D_MODEL)),
            _const_spec((1, D_MODEL)),
            _const_spec((GM_HEADS, GM_CHUNK, GM_CHUNK)),
            _const_spec((GM_CHUNK, GM_HEADS)),
            _const_spec((D_MODEL, D_MODEL)),
            _const_spec((D_MODEL, 2 * D_FF)),
            _const_spec((D_FF, D_MODEL)),
        ],
        out_specs=tok(D_MODEL),
        scratch_shapes=[pltpu.VMEM((tm, GM_WIDTH), BF16)],
        compiler_params=_cparams(1),
        name="main",
    )(x2d, m, u, v, mod, norm2_g, final_g, gm_w, gm_bt, w_out_b, w_gu_b, w_down_b)


def _trunk(x, mod, s0, prm, *, t, n_ph, row_blk, n_fc):
    b, seq, _ = x.shape
    n_chunks = seq // t
    rows = b * n_chunks
    scan = s0 is None
    assert (b == 1) if scan else (n_chunks == 1)
    n_sub = t // SUB

    zt4, u, v = _inproj_call(
        x.reshape(rows, t, D_MODEL), mod[:, 0], mod[:, 1], prm["norm1_g"], prm["w_in"],
        prm["ln_g"], prm["ln_b"], n_ph=n_ph, row_blk=row_blk)
    lane_rows = zt4.shape[-1]

    if scan:
        s0_l = None
    else:
        s0_l = jnp.transpose(s0, (1, 2, 0))
        s0_l = jnp.pad(s0_l, ((0, 0), (0, 0), (0, lane_rows - rows)))
    yt, sfin = _s5_call(zt4.reshape(S5_GROUPS, t * S5_GROUP, lane_rows),
                        prm["wk"], prm["qm"], prm["tabr"], prm["tabi"], s0_l,
                        n_sub=n_sub, scan=scan)
    m = _glu_call(yt.reshape(S5_GROUPS, t, S5_GROUP, lane_rows), prm["w_glu"], prm["b_glu"],
                  rows=rows, n_ph=n_ph, row_blk=row_blk)

    n_tok = b * seq
    y = _main_call(
        x.reshape(n_tok, D_MODEL), m, u.reshape(n_tok, GM_WIDTH), v.reshape(n_tok, GM_WIDTH),
        mod, prm["norm2_g"], prm["final_g"],
        prm["gm_w"], prm["gm_bt"], prm["w_out"], prm["w_gu"], prm["w_down"], seq=seq, n_fc=n_fc)

    if scan:
        fin = sfin[:, :, rows - 1][None]
    else:
        fin = jnp.transpose(sfin[:, :, :rows], (2, 0, 1))
    return (y.reshape(b, seq, D_MODEL), fin[..., :S5_STATE], fin[..., S5_STATE:],
            v.reshape(b, seq, GM_WIDTH))


def kernel(x_prompt, x_sample, state_s5_re, state_s5_im, c_prompt, c_sample, norm1_g, norm2_g, w_ada, b_ada, w_in, s5_lambda_re, s5_lambda_im, s5_log_step, s5_b_re, s5_b_im, s5_c_re, s5_c_im, s5_d, s5_w_glu, s5_b_glu, gm_ln_g, gm_ln_b, gm_w_s, gm_b_s, w_out, ffn_w_gu, ffn_w_down, final_g):
    depth = w_in.shape[0]
    assert depth == 1
    l = 0
    n_p = c_prompt.shape[0]
    n_s = c_sample.shape[0]

    c_all = jnp.concatenate([c_prompt, c_sample], axis=0)
    c_pad = jnp.pad(c_all, ((0, -c_all.shape[0] % 8), (0, 0)))
    mod_all = _ada_call(c_pad, w_ada[l], b_ada[l][None, :])
    mod_p = mod_all[:n_p].reshape(n_p, N_MOD, D_MODEL)
    mod_s = mod_all[n_p:n_p + n_s].reshape(n_s, N_MOD, D_MODEL)

    wk, qm, tabr, tabi = _s5_prep_call(
        s5_lambda_re[l], s5_lambda_im[l], s5_log_step[l], s5_b_re[l], s5_b_im[l],
        s5_c_re[l], s5_c_im[l], s5_d[l])

    prm = dict(
        norm1_g=norm1_g[l][None, :], norm2_g=norm2_g[l][None, :], final_g=final_g[None, :],
        w_in=w_in[l].astype(BF16), ln_g=gm_ln_g[l][None, :], ln_b=gm_ln_b[l][None, :],
        wk=wk, qm=qm, tabr=tabr, tabi=tabi,
        w_glu=s5_w_glu[l].astype(BF16), b_glu=s5_b_glu[l][None, :],
        gm_w=gm_w_s[l], gm_bt=jnp.transpose(gm_b_s[l]),
        w_out=w_out[l].astype(BF16), w_gu=ffn_w_gu[l].astype(BF16),
        w_down=ffn_w_down[l].astype(BF16),
    )

    yp, pre, pim, _ = _trunk(x_prompt, mod_p, None, prm,
                             t=N_SUBPOW * SUB, n_ph=8, row_blk=LANES, n_fc=2)
    s0 = jnp.concatenate([state_s5_re[l], state_s5_im[l]], axis=-1)
    n_b, seq_s, _ = x_sample.shape
    ys, sre, sim, vs = _trunk(x_sample, mod_s, s0, prm,
                              t=seq_s, n_ph=seq_s, row_blk=n_b, n_fc=2)
    return (yp, ys, pre[None], pim[None], sre[None], sim[None], vs[None])
```

```python
import functools
import math

import jax
import jax.numpy as jnp
from jax import lax
from jax.experimental import pallas as pl
from jax.experimental.pallas import tpu as pltpu

D_MODEL = 1024
S5_WIDTH = 512
S5_GROUP = 16
S5_GROUPS = 32
S5_STATE = 64
GM_WIDTH = 512
GM_CHUNK = 128
GM_HEADS = 8
GM_HEAD_DIM = 64
CHUNK = 64
IN_WIDTH = S5_WIDTH + 2 * GM_WIDTH
D_FF = 2816
N_MOD = 6
EPS = 1e-6

LANES = 128
MXU_DIM = 256
VMEM_LIMIT_BYTES = 56 * 1024 * 1024

SUB = MXU_DIM // S5_GROUP
N_SUBPOW = 4
N_DBLPOW = 8
PREP_GROUPS_PER_STEP = 4
S5_GROUPS_PER_STEP_SHORT = 4

F32 = jnp.float32
BF16 = jnp.bfloat16
HIGHEST = lax.Precision.HIGHEST


def _cparams(n_grid_axes):
    return pltpu.CompilerParams(
        dimension_semantics=("arbitrary",) * n_grid_axes,
        vmem_limit_bytes=VMEM_LIMIT_BYTES,
    )


def _const_spec(shape):
    nd = len(shape)
    return pl.BlockSpec(shape, lambda *_: (0,) * nd, pipeline_mode=pl.Buffered(1))


def _rms_scale(x):
    return lax.rsqrt(jnp.mean(x * x, axis=-1, keepdims=True) + EPS)


def _cmul(ar, ai, xr, xi):
    return ar * xr - ai * xi, ar * xi + ai * xr


OCT = 8


def _oct_shape(n_rows, t, width):
    assert n_rows % OCT == 0 and width % LANES == 0
    return (n_rows // OCT, width // LANES, t, OCT, LANES)


def _oct_store(ref, ph, val):
    n_oct, n_lb = ref.shape[0], ref.shape[1]
    for lb in range(n_lb):
        ref[:, lb, ph, :, :] = val[:, lb * LANES:(lb + 1) * LANES].reshape(n_oct, OCT, LANES)


def _oct_load(ref, t):
    n_lb = ref.shape[0] // (t * OCT)
    chunks = []
    for cc in range(OCT):
        chunks.append(jnp.concatenate(
            [ref[pl.ds(lb * t * OCT + cc, t, stride=OCT), :] for lb in range(n_lb)], axis=1))
    return jnp.concatenate(chunks, axis=0)


def _div_pow2(idx, divisor):
    shift = divisor.bit_length() - 1
    assert divisor == 1 << shift
    return lax.shift_right_logical(idx, shift)


def _mod_pow2(idx, divisor):
    assert divisor & (divisor - 1) == 0
    return lax.bitwise_and(idx, divisor - 1)


def _ada_kernel(c_ref, w_ref, b_ref, o_ref):
    c = c_ref[...]
    s = c * jax.nn.sigmoid(c)
    w = w_ref[...]
    s_hi, w_hi = s.astype(BF16), w.astype(BF16)
    s_lo = (s - s_hi.astype(F32)).astype(BF16)
    w_lo = (w - w_hi.astype(F32)).astype(BF16)
    dot = functools.partial(jnp.dot, preferred_element_type=F32)
    o_ref[...] = dot(s_hi, w_hi) + dot(s_lo, w_hi) + dot(s_hi, w_lo) + b_ref[...]


def _ada_call(c_pad, w_ada, b_ada):
    rows = c_pad.shape[0]
    n_out = w_ada.shape[1]
    bn = D_MODEL
    return pl.pallas_call(
        _ada_kernel,
        out_shape=jax.ShapeDtypeStruct((rows, n_out), F32),
        grid=(n_out // bn,),
        in_specs=[
            pl.BlockSpec((rows, D_MODEL), lambda j: (0, 0)),
            pl.BlockSpec((D_MODEL, bn), lambda j: (0, j)),
            pl.BlockSpec((1, bn), lambda j: (0, j)),
        ],
        out_specs=pl.BlockSpec((rows, bn), lambda j: (0, j)),
        compiler_params=_cparams(1),
        name="ada",
    )(c_pad, w_ada, b_ada)


def _discretise(lr, li, ls):
    step = jnp.exp(ls)
    mag = jnp.exp(lr * step)
    ar = mag * jnp.cos(li * step)
    ai = mag * jnp.sin(li * step)
    den = lr * lr + li * li
    fr = ((ar - 1.0) * lr + ai * li) / den
    fi = (ai * lr - (ar - 1.0) * li) / den
    return ar, ai, fr, fi


def _s5_prep_kernel(*refs):
    for gi in range(refs[0].shape[0]):
        _s5_prep_group(*[r.at[gi] for r in refs])


def _s5_prep_group(lr2_ref, li2_ref, ls2_ref, lrc_ref, lic_ref, lsc_ref,
                   btr_ref, bti_ref, c2r_ref, c2i_ref, dt_ref,
                   wk_ref, qm_ref, tabr_ref, tabi_ref):
    n, sub, p = S5_STATE, SUB, S5_GROUP
    width = sub * p

    a2r, a2i, _, _ = _discretise(lr2_ref[...], li2_ref[...], ls2_ref[...])
    lane2 = lax.broadcasted_iota(jnp.int32, (p, 2 * n), 1)
    first = lane2 < n
    c2r = c2r_ref[...]
    c2i = c2i_ref[...]
    pr = jnp.ones_like(a2r)
    pi = jnp.zeros_like(a2r)
    ccat = []
    for _ in range(sub + 1):
        ccat.append(c2r * jnp.where(first, pr, -pi) + c2i * jnp.where(first, -pi, -pr))
        pr, pi = _cmul(a2r, a2i, pr, pi)
    qm_ref[...] = jnp.concatenate(ccat[1:], axis=0).astype(BF16)
    rcat = jnp.concatenate(ccat[:sub], axis=0)

    acr, aci, fr, fi = _discretise(lrc_ref[...], lic_ref[...], lsc_ref[...])
    btr = btr_ref[...]
    bti = bti_ref[...]
    bbr, bbi = _cmul(fr, fi, btr, bti)
    pows = [(jnp.ones_like(acr), jnp.zeros_like(acr))]
    for _ in range(sub):
        pows.append(_cmul(acr, aci, *pows[-1]))
    lane_blk = _div_pow2(lax.broadcasted_iota(jnp.int32, (n, width), 1), p)
    apr = jnp.zeros((n, width), F32)
    api = jnp.zeros((n, width), F32)
    for k in range(sub):
        sel = lane_blk == k
        apr = jnp.where(sel, pows[sub - 1 - k][0], apr)
        api = jnp.where(sel, pows[sub - 1 - k][1], api)
    pmr, pmi = _cmul(apr, api, bbr, bbi)
    wk_ref[width:width + n, :] = pmr.astype(BF16)
    wk_ref[width + n:, :] = pmi.astype(BF16)

    bbcat = jnp.concatenate([bbr, bbi], axis=0)
    kt = jnp.dot(rcat, bbcat, preferred_element_type=F32, precision=HIGHEST)
    row = lax.broadcasted_iota(jnp.int32, (width, width), 0)
    lane = lax.broadcasted_iota(jnp.int32, (width, width), 1)
    kt = kt + jnp.where(row == _mod_pow2(lane, p), dt_ref[...], 0.0)
    col_blk = _div_pow2(lane, p)
    m16 = jnp.zeros((width, width), F32)
    for k in range(sub):
        if k == 0:
            shifted = kt
        else:
            shifted = jnp.concatenate(
                [jnp.zeros((k * p, width), F32), kt[:width - k * p]], axis=0)
        m16 = jnp.where(col_blk == k, shifted, m16)
    wk_ref[:width, :] = m16.astype(BF16)

    tab_lane = lax.broadcasted_iota(jnp.int32, (n, LANES), 1)
    tabr = jnp.zeros((n, LANES), F32)
    tabi = jnp.zeros((n, LANES), F32)
    cols = []
    cur = pows[sub]
    for _ in range(N_SUBPOW):
        cols.append(cur)
        cur = _cmul(pows[sub][0], pows[sub][1], *cur)
    cur = cols[N_SUBPOW - 1]
    for _ in range(N_DBLPOW):
        cols.append(cur)
        cur = _cmul(cur[0], cur[1], *cur)
    for idx, (vr, vi) in enumerate(cols):
        tabr = jnp.where(tab_lane == idx, vr, tabr)
        tabi = jnp.where(tab_lane == idx, vi, tabi)
    tabr_ref[...] = tabr
    tabi_ref[...] = tabi


def _s5_prep_call(lam_re, lam_im, log_step, b_re, b_im, c_re, c_im, d):
    g, n, p, sub = S5_GROUPS, S5_STATE, S5_GROUP, SUB
    width = sub * p
    ls = jnp.broadcast_to(log_step[:, None], (g, n))
    row2 = lambda a: jnp.concatenate([a, a], axis=-1)[:, None, :]
    col = lambda a: a[:, :, None]
    btr = jnp.tile(b_re, (1, 1, sub))
    bti = jnp.tile(b_im, (1, 1, sub))
    c2r = jnp.concatenate([c_re, c_re], axis=-1)
    c2i = jnp.concatenate([c_im, c_im], axis=-1)
    dt = jnp.concatenate([d, jnp.zeros((g, width - p), F32)], axis=1)[:, :, None]
    grp = lambda shape: pl.BlockSpec((PREP_GROUPS_PER_STEP,) + shape, lambda i: (i, 0, 0))
    return pl.pallas_call(
        _s5_prep_kernel,
        out_shape=(
            jax.ShapeDtypeStruct((g, width + 2 * n, width), BF16),
            jax.ShapeDtypeStruct((g, width, 2 * n), BF16),
            jax.ShapeDtypeStruct((g, n, LANES), F32),
            jax.ShapeDtypeStruct((g, n, LANES), F32),
        ),
        grid=(g // PREP_GROUPS_PER_STEP,),
        in_specs=[grp((1, 2 * n))] * 3 + [grp((n, 1))] * 3
        + [grp((n, width))] * 2 + [grp((p, 2 * n))] * 2 + [grp((width, 1))],
        out_specs=(grp((width + 2 * n, width)), grp((width, 2 * n)),
                   grp((n, LANES)), grp((n, LANES))),
        compiler_params=_cparams(1),
        name="s5_prep",
    )(row2(lam_re), row2(lam_im), row2(ls), col(lam_re), col(lam_im), col(ls),
      btr, bti, c2r, c2i, dt)


def _inproj_kernel(x_ref, sh_ref, sc_ref, g1_ref, w_ref, zt_ref, zuv_ref, hs_ref,
                   *, n_ph, rows, lane_rows):
    m = rows * n_ph
    gain = (g1_ref[...] * (1.0 + sc_ref[...]))[:, None, :]
    shift = sh_ref[...][:, None, :]
    x3 = x_ref[...]
    h = (x3 * _rms_scale(x3) * gain + shift).reshape(m, D_MODEL)
    zuv = jnp.dot(h.astype(BF16), w_ref[:, S5_WIDTH:], preferred_element_type=F32)
    zuv_ref[...] = zuv.reshape(rows, n_ph, 2 * GM_WIDTH)

    n_lb = D_MODEL // LANES
    for lb in range(n_lb):
        hs_ref[lb] = h[:, lb * LANES:(lb + 1) * LANES]
    hp = jnp.concatenate(
        [jnp.concatenate([hs_ref[lb, pl.ds(ph, rows, stride=n_ph), :] for lb in range(n_lb)], axis=1)
         for ph in range(n_ph)], axis=0).astype(BF16)
    z5 = jnp.dot(hp, w_ref[:, :S5_WIDTH], preferred_element_type=F32)
    for ph in range(n_ph):
        zz = z5[ph * rows:(ph + 1) * rows]
        if lane_rows > rows:
            zz = jnp.concatenate([zz, jnp.zeros((lane_rows - rows, S5_WIDTH), F32)], axis=0)
        zt = zz.T.reshape(S5_GROUPS, S5_GROUP, lane_rows)
        zt_ref[:, ph, :, :] = zt.astype(BF16)


def _phase_blocks(n_rows, row_blk):
    if row_blk % LANES == 0:
        assert n_rows % row_blk == 0
        return n_rows // row_blk, row_blk, n_rows
    assert row_blk == n_rows
    lane_rows = -(-n_rows // LANES) * LANES
    return 1, lane_rows, lane_rows


def _inproj_call(x3, shift1, scale1, norm1_g, w_in_b, *, n_ph, row_blk):
    rows, t, _ = x3.shape
    mrows = shift1.shape[0]
    assert mrows in (1, row_blk)
    n_rb, lane_blk, lane_rows = _phase_blocks(rows, row_blk)
    kern = functools.partial(_inproj_kernel, n_ph=n_ph, rows=row_blk, lane_rows=lane_blk)
    return pl.pallas_call(
        kern,
        out_shape=(
            jax.ShapeDtypeStruct((S5_GROUPS, t, S5_GROUP, lane_rows), BF16),
            jax.ShapeDtypeStruct((rows, t, 2 * GM_WIDTH), F32),
        ),
        grid=(n_rb, t // n_ph),
        in_specs=[
            pl.BlockSpec((row_blk, n_ph, D_MODEL), lambda i, j: (i, j, 0)),
            _const_spec((mrows, D_MODEL)),
            _const_spec((mrows, D_MODEL)),
            _const_spec((1, D_MODEL)),
            _const_spec((D_MODEL, IN_WIDTH)),
        ],
        out_specs=(
            pl.BlockSpec((S5_GROUPS, n_ph, S5_GROUP, lane_blk), lambda i, j: (0, j, 0, i)),
            pl.BlockSpec((row_blk, n_ph, 2 * GM_WIDTH), lambda i, j: (i, j, 0)),
        ),
        scratch_shapes=[pltpu.VMEM((D_MODEL // LANES, row_blk * n_ph, LANES), F32)],
        compiler_params=_cparams(2),
        name="inproj",
    )(x3, shift1, scale1, norm1_g, w_in_b)


def _s5_kernel(*refs, n_sub, lanes, scan):
    *grouped, ybuf, lbuf = refs
    for gi in range(grouped[0].shape[0]):
        _s5_group(*[r.at[gi] for r in grouped], ybuf, lbuf, n_sub=n_sub, lanes=lanes, scan=scan)


def _s5_group(*refs, n_sub, lanes, scan):
    if scan:
        zt_ref, wk_ref, qm_ref, tabr_ref, tabi_ref, yt_ref, sf_ref, ybuf, lbuf = refs
        s0_ref = None
    else:
        zt_ref, wk_ref, qm_ref, tabr_ref, tabi_ref, s0_ref, yt_ref, sf_ref, ybuf, lbuf = refs
    n = S5_STATE
    width = SUB * S5_GROUP
    wk = wk_ref[...]
    qm = qm_ref[...]
    tabr = tabr_ref[...]
    tabi = tabi_ref[...]
    col = lambda i: (tabr[:, i:i + 1], tabi[:, i:i + 1])

    a1r, a1i = col(0)
    lr = li = None
    for j in range(n_sub):
        u = zt_ref[j * width:(j + 1) * width, :]
        r = jnp.dot(wk, u, preferred_element_type=F32)
        ybuf[j * width:(j + 1) * width, :] = r[:width]
        wr = r[width:width + n]
        wi = r[width + n:]
        if j == 0:
            lr, li = wr, wi
        else:
            tr, ti = _cmul(a1r, a1i, lr, li)
            lr, li = tr + wr, ti + wi
        lbuf[j, :n, :] = lr
        lbuf[j, n:, :] = li

    if scan:
        lane = lax.broadcasted_iota(jnp.int32, (n, lanes), 1)
        xr, xi = lr, li
        for i in range(int(math.log2(lanes))):
            sh = 1 << i
            mr, mi = col(N_SUBPOW + i)
            rr = jnp.where(lane >= sh, pltpu.roll(xr, sh, 1), 0.0)
            ri = jnp.where(lane >= sh, pltpu.roll(xi, sh, 1), 0.0)
            tr, ti = _cmul(mr, mi, rr, ri)
            xr, xi = xr + tr, xi + ti
        sr = jnp.where(lane >= 1, pltpu.roll(xr, 1, 1), 0.0)
        si = jnp.where(lane >= 1, pltpu.roll(xi, 1, 1), 0.0)
    else:
        sr = s0_ref[:n, :]
        si = s0_ref[n:, :]

    for j in range(n_sub):
        if j == 0:
            pr, pi = sr, si
        else:
            ar, ai = col(j - 1)
            tr, ti = _cmul(ar, ai, sr, si)
            pr, pi = lbuf[j - 1, :n, :] + tr, lbuf[j - 1, n:, :] + ti
        sp = jnp.concatenate([pr, pi], axis=0).astype(BF16)
        y = ybuf[j * width:(j + 1) * width, :] + jnp.dot(qm, sp, preferred_element_type=F32)
        yt_ref[j * width:(j + 1) * width, :] = y.astype(BF16)
    ar, ai = col(n_sub - 1)
    tr, ti = _cmul(ar, ai, sr, si)
    sf_ref[:n, :] = lbuf[n_sub - 1, :n, :] + tr
    sf_ref[n:, :] = lbuf[n_sub - 1, n:, :] + ti


def _s5_call(zt, wk, qm, tabr, tabi, s0, *, n_sub, scan):
    g, rows, lanes = zt.shape
    n = S5_STATE
    width = SUB * S5_GROUP
    if scan:
        assert n_sub == N_SUBPOW and lanes <= (1 << N_DBLPOW)
    gps = 1 if scan else S5_GROUPS_PER_STEP_SHORT
    grp = lambda shape: pl.BlockSpec((gps,) + shape, lambda i: (i, 0, 0))
    in_specs = [grp((rows, lanes)), grp((width + 2 * n, width)), grp((width, 2 * n)),
                grp((n, LANES)), grp((n, LANES))]
    args = [zt, wk, qm, tabr, tabi]
    if not scan:
        in_specs.append(grp((2 * n, lanes)))
        args.append(s0)
    kern = functools.partial(_s5_kernel, n_sub=n_sub, lanes=lanes, scan=scan)
    return pl.pallas_call(
        kern,
        out_shape=(jax.ShapeDtypeStruct((g, rows, lanes), BF16),
                   jax.ShapeDtypeStruct((g, 2 * n, lanes), F32)),
        grid=(g // gps,),
        in_specs=in_specs,
        out_specs=(grp((rows, lanes)), grp((2 * n, lanes))),
        scratch_shapes=[pltpu.VMEM((rows, lanes), F32),
                        pltpu.VMEM((n_sub, 2 * n, lanes), F32)],
        compiler_params=_cparams(1),
        name="s5",
    )(*args)


def _glu_kernel(yt_ref, w_ref, b_ref, m_ref, *, n_ph, rows, lane_rows):
    gs = []
    for ph in range(n_ph):
        yt = yt_ref[:, ph, :, :].astype(F32).reshape(S5_WIDTH, lane_rows)
        gs.append(jax.nn.gelu(yt.T[:rows]))
    gy = jnp.concatenate(gs, axis=0)
    gate = jnp.dot(gy.astype(BF16), w_ref[...], preferred_element_type=F32) + b_ref[...]
    m = gy * jax.nn.sigmoid(gate)
    for ph in range(n_ph):
        _oct_store(m_ref, ph, m[ph * rows:(ph + 1) * rows])


def _glu_call(yt4, w_glu_b, b_glu, *, rows, n_ph, row_blk):
    g, t, p, lane_rows = yt4.shape
    n_rb, lane_blk, lane_rows_expected = _phase_blocks(rows, row_blk)
    assert lane_rows == lane_rows_expected
    kern = functools.partial(_glu_kernel, n_ph=n_ph, rows=row_blk, lane_rows=lane_blk)
    oct_shape = _oct_shape(rows, t, S5_WIDTH)
    return pl.pallas_call(
        kern,
        out_shape=jax.ShapeDtypeStruct(oct_shape, F32),
        grid=(n_rb, t // n_ph),
        in_specs=[
            pl.BlockSpec((g, n_ph, p, lane_blk), lambda i, j: (0, j, 0, i)),
            _const_spec((S5_WIDTH, S5_WIDTH)),
            _const_spec((1, S5_WIDTH)),
        ],
        out_specs=pl.BlockSpec((row_blk // OCT, oct_shape[1], n_ph, OCT, LANES),
                               lambda i, j: (i, 0, j, 0, 0)),
        compiler_params=_cparams(2),
        name="glu",
    )(yt4, w_glu_b, b_glu)


def _ff_chunks(n_fc):
    tiles = D_FF // MXU_DIM
    assert tiles * MXU_DIM == D_FF
    bounds = [MXU_DIM * ((tiles * c + n_fc - 1) // n_fc) for c in range(n_fc + 1)]
    return list(zip(bounds[:-1], bounds[1:]))


def _main_kernel(x_ref, m_ref, zuv_ref, mod_ref, g2_ref, gf_ref, lng_ref, lnb_ref, gw_ref, gbt_ref,
                 wo_ref, wgu_ref, wd_ref, y_ref, *rest, tm, cl, seq, n_fc):
    *maybe_v_out_ref, v_ref, ygm_ref, attn_ref = rest
    t = tm // OCT
    hd = GM_HEAD_DIM

    attn_ref[...] = jnp.dot(_oct_load(m_ref, t).astype(BF16), wo_ref[:S5_WIDTH, :],
                            preferred_element_type=F32)

    gv = jax.nn.gelu(zuv_ref[:, GM_WIDTH:])
    cen = gv - jnp.mean(gv, axis=-1, keepdims=True)
    var = jnp.mean(cen * cen, axis=-1, keepdims=True)
    v = cen * lax.rsqrt(var + EPS) * lng_ref[...] + lnb_ref[...]
    for v_out_ref in maybe_v_out_ref:
        v_out_ref[...] = v
    v_ref[...] = v.astype(BF16)

    blk_i = _div_pow2(lax.broadcasted_iota(jnp.int32, (cl, cl), 0), CHUNK)
    blk_j = _div_pow2(lax.broadcasted_iota(jnp.int32, (cl, cl), 1), CHUNK)
    causal = blk_j <= blk_i
    first_head = lax.broadcasted_iota(jnp.int32, (cl, 2 * hd), 1) < hd
    for pr in range(GM_HEADS // 2):
        h0, h1 = 2 * pr, 2 * pr + 1
        wm = jnp.concatenate(
            [jnp.where(causal, gw_ref[h, :cl, :cl], 0.0) for h in (h0, h1)], axis=1).astype(BF16)
        bias = jnp.where(first_head, gbt_ref[:cl, h0:h0 + 1], gbt_ref[:cl, h1:h1 + 1])
        cs = slice(h0 * hd, (h1 + 1) * hd)
        for ci in range(tm // cl):
            rs = slice(ci * cl, (ci + 1) * cl)
            vv = v_ref[rs, cs]
            zero = jnp.zeros_like(vv)
            rhs = jnp.concatenate([jnp.where(first_head, vv, zero), jnp.where(first_head, zero, vv)],
                                  axis=0)
            mixed = jnp.dot(wm, rhs, preferred_element_type=F32) + bias
            ygm_ref[rs, cs] = (jax.nn.gelu(zuv_ref[rs, cs]) * mixed).astype(BF16)

    attn = attn_ref[...] + jnp.dot(ygm_ref[...], wo_ref[S5_WIDTH:, :], preferred_element_type=F32)

    n_streams = mod_ref.shape[0]

    def mod_rows(idx):
        if n_streams == 1:
            return mod_ref[0, idx:idx + 1, :]
        return jnp.concatenate(
            [jnp.broadcast_to(mod_ref[s, idx:idx + 1, :], (seq, D_MODEL)) for s in range(n_streams)],
            axis=0)

    gate1, shift2, scale2, gate2 = mod_rows(2), mod_rows(3), mod_rows(4), mod_rows(5)
    x1 = x_ref[...] + gate1 * attn
    h2 = (x1 * _rms_scale(x1) * (g2_ref[...] * (1.0 + scale2)) + shift2).astype(BF16)

    acc = None
    for lo, hi in _ff_chunks(n_fc):
        gg = jnp.dot(h2, wgu_ref[:, lo:hi], preferred_element_type=F32)
        up = jnp.dot(h2, wgu_ref[:, D_FF + lo:D_FF + hi], preferred_element_type=F32)
        act = (gg * jax.nn.sigmoid(gg) * up).astype(BF16)
        part = jnp.dot(act, wd_ref[lo:hi, :], preferred_element_type=F32)
        acc = part if acc is None else acc + part
    x2 = x1 + gate2 * acc
    y_ref[...] = x2 * _rms_scale(x2) * gf_ref[...]


def _main_call(x2d, m, zuv, mod, norm2_g, final_g, ln_g, ln_b, gm_w, gm_bt, w_out_b, w_gu_b, w_down_b,
               *, seq, n_fc, want_v):
    n_tok = x2d.shape[0]
    n_oct, n_lb, t, _, _ = m.shape
    tm = OCT * t
    assert n_oct * tm == n_tok
    streams_per_tile = max(1, tm // seq)
    tiles_per_stream = max(1, seq // tm)
    assert streams_per_tile * seq == tm or tiles_per_stream * tm == seq
    cl = min(GM_CHUNK, seq)
    kern = functools.partial(_main_kernel, tm=tm, cl=cl, seq=seq, n_fc=n_fc)
    tok = lambda width: pl.BlockSpec((tm, width), lambda i: (i, 0))
    oct_rows = n_lb * t * OCT
    octs = pl.BlockSpec((oct_rows, LANES), lambda i: (i, 0))
    m = m.reshape(n_oct * oct_rows, LANES)
    out_shape = [jax.ShapeDtypeStruct((n_tok, D_MODEL), F32)]
    out_specs = [tok(D_MODEL)]
    if want_v:
        out_shape.append(jax.ShapeDtypeStruct((n_tok, GM_WIDTH), F32))
        out_specs.append(tok(GM_WIDTH))
    return pl.pallas_call(
        kern,
        out_shape=tuple(out_shape),
        grid=(n_oct,),
        in_specs=[
            tok(D_MODEL), octs, tok(2 * GM_WIDTH),
            pl.BlockSpec((streams_per_tile, N_MOD, D_MODEL), lambda i: (i // tiles_per_stream, 0, 0)),
            _const_spec((1, D_MODEL)),
            _const_spec((1, D_MODEL)),
            _const_spec((1, GM_WIDTH)),
            _const_spec((1, GM_WIDTH)),
            _const_spec((GM_HEADS, GM_CHUNK, GM_CHUNK)),
            _const_spec((GM_CHUNK, GM_HEADS)),
            _const_spec((D_MODEL, D_MODEL)),
            _const_spec((D_MODEL, 2 * D_FF)),
            _const_spec((D_FF, D_MODEL)),
        ],
        out_specs=tuple(out_specs),
        scratch_shapes=[pltpu.VMEM((tm, GM_WIDTH), BF16), pltpu.VMEM((tm, GM_WIDTH), BF16),
                        pltpu.VMEM((tm, D_MODEL), F32)],
        compiler_params=_cparams(1),
        name="main",
    )(x2d, m, zuv, mod, norm2_g, final_g, ln_g, ln_b, gm_w, gm_bt, w_out_b, w_gu_b, w_down_b)


def _trunk(x, mod, s0, prm, *, t, n_ph, row_blk, n_fc, want_v):
    b, seq, _ = x.shape
    n_chunks = seq // t
    rows = b * n_chunks
    scan = s0 is None
    assert (b == 1) if scan else (n_chunks == 1)
    n_sub = t // SUB

    zt4, zuv = _inproj_call(
        x.reshape(rows, t, D_MODEL), mod[:, 0], mod[:, 1], prm["norm1_g"], prm["w_in"],
        n_ph=n_ph, row_blk=row_blk)
    lane_rows = zt4.shape[-1]

    if scan:
        s0_l = None
    else:
        s0_l = jnp.transpose(s0, (1, 2, 0))
        s0_l = jnp.pad(s0_l, ((0, 0), (0, 0), (0, lane_rows - rows)))
    yt, sfin = _s5_call(zt4.reshape(S5_GROUPS, t * S5_GROUP, lane_rows),
                        prm["wk"], prm["qm"], prm["tabr"], prm["tabi"], s0_l,
                        n_sub=n_sub, scan=scan)
    m = _glu_call(yt.reshape(S5_GROUPS, t, S5_GROUP, lane_rows), prm["w_glu"], prm["b_glu"],
                  rows=rows, n_ph=n_ph, row_blk=row_blk)

    n_tok = b * seq
    y, *maybe_v = _main_call(
        x.reshape(n_tok, D_MODEL), m, zuv.reshape(n_tok, 2 * GM_WIDTH),
        mod, prm["norm2_g"], prm["final_g"], prm["ln_g"], prm["ln_b"],
        prm["gm_w"], prm["gm_bt"], prm["w_out"], prm["w_gu"], prm["w_down"],
        seq=seq, n_fc=n_fc, want_v=want_v)

    if scan:
        fin = sfin[:, :, rows - 1][None]
    else:
        fin = jnp.transpose(sfin[:, :, :rows], (2, 0, 1))
    v = maybe_v[0].reshape(b, seq, GM_WIDTH) if want_v else None
    return y.reshape(b, seq, D_MODEL), fin[..., :S5_STATE], fin[..., S5_STATE:], v


def kernel(x_prompt, x_sample, state_s5_re, state_s5_im, c_prompt, c_sample, norm1_g, norm2_g, w_ada, b_ada, w_in, s5_lambda_re, s5_lambda_im, s5_log_step, s5_b_re, s5_b_im, s5_c_re, s5_c_im, s5_d, s5_w_glu, s5_b_glu, gm_ln_g, gm_ln_b, gm_w_s, gm_b_s, w_out, ffn_w_gu, ffn_w_down, final_g):
    depth = w_in.shape[0]
    assert depth == 1
    l = 0
    n_p = c_prompt.shape[0]
    n_s = c_sample.shape[0]

    c_all = jnp.concatenate([c_prompt, c_sample], axis=0)
    c_pad = jnp.pad(c_all, ((0, -c_all.shape[0] % 8), (0, 0)))
    mod_all = _ada_call(c_pad, w_ada[l], b_ada[l][None, :])
    mod_p = mod_all[:n_p].reshape(n_p, N_MOD, D_MODEL)
    mod_s = mod_all[n_p:n_p + n_s].reshape(n_s, N_MOD, D_MODEL)

    wk, qm, tabr, tabi = _s5_prep_call(
        s5_lambda_re[l], s5_lambda_im[l], s5_log_step[l], s5_b_re[l], s5_b_im[l],
        s5_c_re[l], s5_c_im[l], s5_d[l])

    prm = dict(
        norm1_g=norm1_g[l][None, :], norm2_g=norm2_g[l][None, :], final_g=final_g[None, :],
        w_in=w_in[l].astype(BF16), ln_g=gm_ln_g[l][None, :], ln_b=gm_ln_b[l][None, :],
        wk=wk, qm=qm, tabr=tabr, tabi=tabi,
        w_glu=s5_w_glu[l].astype(BF16), b_glu=s5_b_glu[l][None, :],
        gm_w=gm_w_s[l], gm_bt=jnp.transpose(gm_b_s[l]),
        w_out=w_out[l].astype(BF16), w_gu=ffn_w_gu[l].astype(BF16),
        w_down=ffn_w_down[l].astype(BF16),
    )

    yp, pre, pim, _ = _trunk(x_prompt, mod_p, None, prm,
                             t=N_SUBPOW * SUB, n_ph=8, row_blk=LANES, n_fc=2, want_v=False)
    s0 = jnp.concatenate([state_s5_re[l], state_s5_im[l]], axis=-1)
    n_b, seq_s, _ = x_sample.shape
    ys, sre, sim, vs = _trunk(x_sample, mod_s, s0, prm,
                              t=seq_s, n_ph=seq_s, row_blk=n_b, n_fc=2, want_v=True)
    return (yp, ys, pre[None], pim[None], sre[None], sim[None], vs[None])
```

```python
import functools
import math

import jax
import jax.numpy as jnp
from jax import lax
from jax.experimental import pallas as pl
from jax.experimental.pallas import tpu as pltpu

D_MODEL = 1024
S5_WIDTH = 512
S5_GROUP = 16
S5_GROUPS = 32
S5_STATE = 64
GM_WIDTH = 512
GM_CHUNK = 128
GM_HEADS = 8
GM_HEAD_DIM = 64
CHUNK = 64
IN_WIDTH = S5_WIDTH + 2 * GM_WIDTH
D_FF = 2816
N_MOD = 6
EPS = 1e-6

LANES = 128
BF16_SUBLANES = 16
MXU_DIM = 256
VMEM_LIMIT_BYTES = 56 * 1024 * 1024

SUB = MXU_DIM // S5_GROUP
S5_LONG_SUBS = 4
PREP_GROUPS_PER_STEP = 4
S5_GROUPS_PER_STEP_SHORT = 4
S5_GROUPS_PER_STEP_LONG = 2

F32 = jnp.float32
BF16 = jnp.bfloat16
HIGHEST = lax.Precision.HIGHEST


def _cparams(n_grid_axes):
    return pltpu.CompilerParams(
        dimension_semantics=("arbitrary",) * n_grid_axes,
        vmem_limit_bytes=VMEM_LIMIT_BYTES,
    )


def _const_spec(shape):
    nd = len(shape)
    return pl.BlockSpec(shape, lambda *_: (0,) * nd, pipeline_mode=pl.Buffered(1))


def _rms_scale(x):
    return lax.rsqrt(jnp.mean(x * x, axis=-1, keepdims=True) + EPS)


_GELU_C0 = math.sqrt(2.0 / math.pi)
_GELU_C1 = 0.044715 * _GELU_C0


def _gelu(x):
    hx = 0.5 * x
    return hx + hx * jnp.tanh(x * (_GELU_C0 + _GELU_C1 * (x * x)))


def _sigmoid(x):
    return 0.5 * jnp.tanh(0.5 * x) + 0.5


def _cmul(ar, ai, xr, xi):
    return ar * xr - ai * xi, ar * xi + ai * xr


OCT = 8


def _oct_shape(n_rows, t, width):
    assert n_rows % OCT == 0 and width % LANES == 0
    return (n_rows // OCT, width // LANES, t, OCT, LANES)


def _oct_store(ref, ph, val):
    n_oct, n_lb = ref.shape[0], ref.shape[1]
    for lb in range(n_lb):
        ref[:, lb, ph, :, :] = val[:, lb * LANES:(lb + 1) * LANES].reshape(n_oct, OCT, LANES)


def _oct_load(ref, t):
    n_lb = ref.shape[0] // (t * OCT)
    chunks = []
    for cc in range(OCT):
        chunks.append(jnp.concatenate(
            [ref[pl.ds(lb * t * OCT + cc, t, stride=OCT), :] for lb in range(n_lb)], axis=1))
    return jnp.concatenate(chunks, axis=0)


def _cast_rider_specs(arrays, n_steps):
    specs, shapes = [], []
    for a in arrays:
        rows, cols = a.shape
        blk = rows // n_steps
        assert blk * n_steps == rows and blk % BF16_SUBLANES == 0
        specs.append(pl.BlockSpec((blk, cols), lambda i: (i, 0)))
        shapes.append(jax.ShapeDtypeStruct((rows, cols), BF16))
    return specs, shapes


def _split_riders(refs, n_in, n_out, n_cast):
    ins, refs = refs[:n_in], refs[n_in:]
    cast_in, refs = refs[:n_cast], refs[n_cast:]
    outs, refs = refs[:n_out], refs[n_out:]
    cast_out, scratch = refs[:n_cast], refs[n_cast:]
    return (*ins, *outs, *scratch), list(zip(cast_in, cast_out))


def _run_riders(pairs):
    for src, dst in pairs:
        dst[...] = src[...].astype(BF16)


def _div_pow2(idx, divisor):
    shift = divisor.bit_length() - 1
    assert divisor == 1 << shift
    return lax.shift_right_logical(idx, shift)


def _mod_pow2(idx, divisor):
    assert divisor & (divisor - 1) == 0
    return lax.bitwise_and(idx, divisor - 1)


def _ada_kernel(c_ref, w_ref, b_ref, o_ref):
    c = c_ref[...]
    s = c * jax.nn.sigmoid(c)
    w = w_ref[...]
    s_hi, w_hi = s.astype(BF16), w.astype(BF16)
    s_lo = (s - s_hi.astype(F32)).astype(BF16)
    w_lo = (w - w_hi.astype(F32)).astype(BF16)
    dot = functools.partial(jnp.dot, preferred_element_type=F32)
    o_ref[...] = dot(s_hi, w_hi) + dot(s_lo, w_hi) + dot(s_hi, w_lo) + b_ref[...]


def _ada_call(c_pad, w_ada, b_ada):
    rows = c_pad.shape[0]
    n_out = w_ada.shape[1]
    bn = D_MODEL
    return pl.pallas_call(
        _ada_kernel,
        out_shape=jax.ShapeDtypeStruct((rows, n_out), F32),
        grid=(n_out // bn,),
        in_specs=[
            pl.BlockSpec((rows, D_MODEL), lambda j: (0, 0)),
            pl.BlockSpec((D_MODEL, bn), lambda j: (0, j)),
            pl.BlockSpec((1, bn), lambda j: (0, j)),
        ],
        out_specs=pl.BlockSpec((rows, bn), lambda j: (0, j)),
        compiler_params=_cparams(1),
        name="ada",
    )(c_pad, w_ada, b_ada)


def _discretise(lr, li, ls):
    step = jnp.exp(ls)
    mag = jnp.exp(lr * step)
    ar = mag * jnp.cos(li * step)
    ai = mag * jnp.sin(li * step)
    den = lr * lr + li * li
    fr = ((ar - 1.0) * lr + ai * li) / den
    fi = (ai * lr - (ar - 1.0) * li) / den
    return ar, ai, fr, fi


def _s5_prep_kernel(*refs, n_cast):
    refs, riders = _split_riders(refs, n_in=11, n_out=4, n_cast=n_cast)
    for gi in range(refs[0].shape[0]):
        _s5_prep_group(*[r.at[gi] for r in refs])
    _run_riders(riders)


def _s5_prep_group(lr2_ref, li2_ref, ls2_ref, lrc_ref, lic_ref, lsc_ref,
                   btr_ref, bti_ref, c2r_ref, c2i_ref, dt_ref,
                   wk_ref, qm_ref, tabr_ref, tabi_ref):
    n, sub, p = S5_STATE, SUB, S5_GROUP
    width = sub * p

    a2r, a2i, _, _ = _discretise(lr2_ref[...], li2_ref[...], ls2_ref[...])
    lane2 = lax.broadcasted_iota(jnp.int32, (p, 2 * n), 1)
    first = lane2 < n
    c2r = c2r_ref[...]
    c2i = c2i_ref[...]
    pr = jnp.ones_like(a2r)
    pi = jnp.zeros_like(a2r)
    ccat = []
    for _ in range(sub + 1):
        ccat.append(c2r * jnp.where(first, pr, -pi) + c2i * jnp.where(first, -pi, -pr))
        pr, pi = _cmul(a2r, a2i, pr, pi)
    qm_ref[...] = jnp.concatenate(ccat[1:], axis=0).astype(BF16)
    rcat = jnp.concatenate(ccat[:sub], axis=0)

    acr, aci, fr, fi = _discretise(lrc_ref[...], lic_ref[...], lsc_ref[...])
    btr = btr_ref[...]
    bti = bti_ref[...]
    bbr, bbi = _cmul(fr, fi, btr, bti)
    pows = [(jnp.ones_like(acr), jnp.zeros_like(acr))]
    for _ in range(sub):
        pows.append(_cmul(acr, aci, *pows[-1]))
    lane_blk = _div_pow2(lax.broadcasted_iota(jnp.int32, (n, width), 1), p)
    apr = jnp.zeros((n, width), F32)
    api = jnp.zeros((n, width), F32)
    for k in range(sub):
        sel = lane_blk == k
        apr = jnp.where(sel, pows[sub - 1 - k][0], apr)
        api = jnp.where(sel, pows[sub - 1 - k][1], api)
    pmr, pmi = _cmul(apr, api, bbr, bbi)
    wk_ref[width:width + n, :] = pmr.astype(BF16)
    wk_ref[width + n:, :] = pmi.astype(BF16)

    bbcat = jnp.concatenate([bbr, bbi], axis=0)
    kt = jnp.dot(rcat, bbcat, preferred_element_type=F32, precision=HIGHEST)
    row = lax.broadcasted_iota(jnp.int32, (width, width), 0)
    lane = lax.broadcasted_iota(jnp.int32, (width, width), 1)
    kt = kt + jnp.where(row == _mod_pow2(lane, p), dt_ref[...], 0.0)
    col_blk = _div_pow2(lane, p)
    m16 = jnp.zeros((width, width), F32)
    for k in range(sub):
        if k == 0:
            shifted = kt
        else:
            shifted = jnp.concatenate(
                [jnp.zeros((k * p, width), F32), kt[:width - k * p]], axis=0)
        m16 = jnp.where(col_blk == k, shifted, m16)
    wk_ref[:width, :] = m16.astype(BF16)

    tabr_ref[...] = jnp.broadcast_to(pows[sub][0], (n, LANES))
    tabi_ref[...] = jnp.broadcast_to(pows[sub][1], (n, LANES))


def _s5_prep_call(lam_re, lam_im, log_step, b_re, b_im, c_re, c_im, d, cast_riders=()):
    g, n, p, sub = S5_GROUPS, S5_STATE, S5_GROUP, SUB
    width = sub * p
    ls = jnp.broadcast_to(log_step[:, None], (g, n))
    row2 = lambda a: jnp.concatenate([a, a], axis=-1)[:, None, :]
    col = lambda a: a[:, :, None]
    btr = jnp.tile(b_re, (1, 1, sub))
    bti = jnp.tile(b_im, (1, 1, sub))
    c2r = jnp.concatenate([c_re, c_re], axis=-1)
    c2i = jnp.concatenate([c_im, c_im], axis=-1)
    dt = jnp.concatenate([d, jnp.zeros((g, width - p), F32)], axis=1)[:, :, None]
    grp = lambda shape: pl.BlockSpec((PREP_GROUPS_PER_STEP,) + shape, lambda i: (i, 0, 0))
    n_steps = g // PREP_GROUPS_PER_STEP
    rider_specs, rider_shapes = _cast_rider_specs(cast_riders, n_steps)
    return pl.pallas_call(
        functools.partial(_s5_prep_kernel, n_cast=len(cast_riders)),
        out_shape=(
            jax.ShapeDtypeStruct((g, width + 2 * n, width), BF16),
            jax.ShapeDtypeStruct((g, width, 2 * n), BF16),
            jax.ShapeDtypeStruct((g, n, LANES), F32),
            jax.ShapeDtypeStruct((g, n, LANES), F32),
            *rider_shapes,
        ),
        grid=(n_steps,),
        in_specs=[grp((1, 2 * n))] * 3 + [grp((n, 1))] * 3
        + [grp((n, width))] * 2 + [grp((p, 2 * n))] * 2 + [grp((width, 1))] + rider_specs,
        out_specs=(grp((width + 2 * n, width)), grp((width, 2 * n)),
                   grp((n, LANES)), grp((n, LANES)), *rider_specs),
        compiler_params=_cparams(1),
        name="s5_prep",
    )(row2(lam_re), row2(lam_im), row2(ls), col(lam_re), col(lam_im), col(ls),
      btr, bti, c2r, c2i, dt, *cast_riders)


def _inproj_kernel(x_ref, sh_ref, sc_ref, g1_ref, w_ref, zt_ref, zuv_ref, hs_ref,
                   *, n_ph, rows, lane_rows):
    m = rows * n_ph
    gain = (g1_ref[...] * (1.0 + sc_ref[...]))[:, None, :]
    shift = sh_ref[...][:, None, :]
    x3 = x_ref[...]
    h = (x3 * _rms_scale(x3) * gain + shift).reshape(m, D_MODEL)
    zuv = jnp.dot(h.astype(BF16), w_ref[:, S5_WIDTH:], preferred_element_type=F32)
    zuv_ref[...] = zuv.reshape(rows, n_ph, 2 * GM_WIDTH)

    n_lb = D_MODEL // LANES
    for lb in range(n_lb):
        hs_ref[lb] = h[:, lb * LANES:(lb + 1) * LANES]
    hp = jnp.concatenate(
        [jnp.concatenate([hs_ref[lb, pl.ds(ph, rows, stride=n_ph), :] for lb in range(n_lb)], axis=1)
         for ph in range(n_ph)], axis=0).astype(BF16)
    z5 = jnp.dot(hp, w_ref[:, :S5_WIDTH], preferred_element_type=F32)
    for ph in range(n_ph):
        zz = z5[ph * rows:(ph + 1) * rows]
        if lane_rows > rows:
            zz = jnp.concatenate([zz, jnp.zeros((lane_rows - rows, S5_WIDTH), F32)], axis=0)
        zt = zz.T.reshape(S5_GROUPS, S5_GROUP, lane_rows)
        zt_ref[:, ph, :, :] = zt.astype(BF16)


def _phase_blocks(n_rows, row_blk):
    if row_blk % LANES == 0:
        assert n_rows % row_blk == 0
        return n_rows // row_blk, row_blk, n_rows
    assert row_blk == n_rows
    lane_rows = -(-n_rows // LANES) * LANES
    return 1, lane_rows, lane_rows


def _inproj_call(x3, shift1, scale1, norm1_g, w_in_b, *, n_ph, row_blk):
    rows, t, _ = x3.shape
    mrows = shift1.shape[0]
    assert mrows in (1, row_blk)
    n_rb, lane_blk, lane_rows = _phase_blocks(rows, row_blk)
    kern = functools.partial(_inproj_kernel, n_ph=n_ph, rows=row_blk, lane_rows=lane_blk)
    return pl.pallas_call(
        kern,
        out_shape=(
            jax.ShapeDtypeStruct((S5_GROUPS, t, S5_GROUP, lane_rows), BF16),
            jax.ShapeDtypeStruct((rows, t, 2 * GM_WIDTH), F32),
        ),
        grid=(n_rb, t // n_ph),
        in_specs=[
            pl.BlockSpec((row_blk, n_ph, D_MODEL), lambda i, j: (i, j, 0)),
            _const_spec((mrows, D_MODEL)),
            _const_spec((mrows, D_MODEL)),
            _const_spec((1, D_MODEL)),
            _const_spec((D_MODEL, IN_WIDTH)),
        ],
        out_specs=(
            pl.BlockSpec((S5_GROUPS, n_ph, S5_GROUP, lane_blk), lambda i, j: (0, j, 0, i)),
            pl.BlockSpec((row_blk, n_ph, 2 * GM_WIDTH), lambda i, j: (i, j, 0)),
        ),
        scratch_shapes=[pltpu.VMEM((D_MODEL // LANES, row_blk * n_ph, LANES), F32)],
        compiler_params=_cparams(2),
        name="inproj",
    )(x3, shift1, scale1, norm1_g, w_in_b)


def _s5_kernel(*refs, n_sub, lanes, scan, n_cast):
    refs, riders = _split_riders(refs, n_in=5 if scan else 6, n_out=2, n_cast=n_cast)
    for gi in range(refs[0].shape[0]):
        _s5_group(*[r.at[gi] for r in refs], n_sub=n_sub, lanes=lanes, scan=scan)
    _run_riders(riders)


def _s5_group(*refs, n_sub, lanes, scan):
    if scan:
        zt_ref, wk_ref, qm_ref, tabr_ref, tabi_ref, yt_ref, sf_ref, ybuf, lbuf = refs
        s0_ref = None
    else:
        zt_ref, wk_ref, qm_ref, tabr_ref, tabi_ref, s0_ref, yt_ref, sf_ref, ybuf, lbuf = refs
    n = S5_STATE
    width = SUB * S5_GROUP
    wk = wk_ref[...]
    qm = qm_ref[...]
    widen = lambda tile: jnp.concatenate([tile] * (lanes // LANES), axis=1)
    a1r, a1i = widen(tabr_ref[...]), widen(tabi_ref[...])
    sub_pows = [(a1r, a1i)]
    for _ in range(n_sub - 1):
        sub_pows.append(_cmul(a1r, a1i, *sub_pows[-1]))

    lr = li = None
    for j in range(n_sub):
        u = zt_ref[j * width:(j + 1) * width, :]
        r = jnp.dot(wk, u, preferred_element_type=F32)
        ybuf[j * width:(j + 1) * width, :] = r[:width]
        wr = r[width:width + n]
        wi = r[width + n:]
        if j == 0:
            lr, li = wr, wi
        else:
            tr, ti = _cmul(a1r, a1i, lr, li)
            lr, li = tr + wr, ti + wi
        lbuf[j, :n, :] = lr
        lbuf[j, n:, :] = li

    if scan:
        lane = lax.broadcasted_iota(jnp.int32, (n, lanes), 1)
        xr, xi = lr, li
        mr, mi = sub_pows[n_sub - 1]
        for i in range(int(math.log2(lanes))):
            sh = 1 << i
            rr = jnp.where(lane >= sh, pltpu.roll(xr, sh, 1), 0.0)
            ri = jnp.where(lane >= sh, pltpu.roll(xi, sh, 1), 0.0)
            tr, ti = _cmul(mr, mi, rr, ri)
            xr, xi = xr + tr, xi + ti
            mr, mi = _cmul(mr, mi, mr, mi)
        sr = jnp.where(lane >= 1, pltpu.roll(xr, 1, 1), 0.0)
        si = jnp.where(lane >= 1, pltpu.roll(xi, 1, 1), 0.0)
    else:
        sr = s0_ref[:n, :]
        si = s0_ref[n:, :]

    for j in range(n_sub):
        if j == 0:
            pr, pi = sr, si
        else:
            tr, ti = _cmul(*sub_pows[j - 1], sr, si)
            pr, pi = lbuf[j - 1, :n, :] + tr, lbuf[j - 1, n:, :] + ti
        sp = jnp.concatenate([pr, pi], axis=0).astype(BF16)
        y = ybuf[j * width:(j + 1) * width, :] + jnp.dot(qm, sp, preferred_element_type=F32)
        yt_ref[j * width:(j + 1) * width, :] = y.astype(BF16)
    tr, ti = _cmul(*sub_pows[n_sub - 1], sr, si)
    sf_ref[:n, :] = lbuf[n_sub - 1, :n, :] + tr
    sf_ref[n:, :] = lbuf[n_sub - 1, n:, :] + ti


def _s5_call(zt, wk, qm, tabr, tabi, s0, *, n_sub, scan, cast_riders=()):
    g, rows, lanes = zt.shape
    n = S5_STATE
    width = SUB * S5_GROUP
    assert lanes % LANES == 0 and lanes & (lanes - 1) == 0
    gps = S5_GROUPS_PER_STEP_LONG if scan else S5_GROUPS_PER_STEP_SHORT
    grp = lambda shape: pl.BlockSpec((gps,) + shape, lambda i: (i, 0, 0))
    in_specs = [grp((rows, lanes)), grp((width + 2 * n, width)), grp((width, 2 * n)),
                grp((n, LANES)), grp((n, LANES))]
    args = [zt, wk, qm, tabr, tabi]
    if not scan:
        in_specs.append(grp((2 * n, lanes)))
        args.append(s0)
    rider_specs, rider_shapes = _cast_rider_specs(cast_riders, g // gps)
    kern = functools.partial(_s5_kernel, n_sub=n_sub, lanes=lanes, scan=scan,
                             n_cast=len(cast_riders))
    return pl.pallas_call(
        kern,
        out_shape=(jax.ShapeDtypeStruct((g, rows, lanes), BF16),
                   jax.ShapeDtypeStruct((g, 2 * n, lanes), F32), *rider_shapes),
        grid=(g // gps,),
        in_specs=in_specs + rider_specs,
        out_specs=(grp((rows, lanes)), grp((2 * n, lanes)), *rider_specs),
        scratch_shapes=[pltpu.VMEM((gps, rows, lanes), F32),
                        pltpu.VMEM((gps, n_sub, 2 * n, lanes), F32)],
        compiler_params=_cparams(1),
        name="s5",
    )(*args, *cast_riders)


def _glu_kernel(yt_ref, w_ref, b_ref, m_ref, *, n_ph, rows, lane_rows):
    gs = []
    for ph in range(n_ph):
        yt = yt_ref[:, ph, :, :].astype(F32).reshape(S5_WIDTH, lane_rows)
        gs.append(_gelu(yt.T[:rows]))
    gy = jnp.concatenate(gs, axis=0)
    gate = jnp.dot(gy.astype(BF16), w_ref[...], preferred_element_type=F32) + b_ref[...]
    m = gy * _sigmoid(gate)
    for ph in range(n_ph):
        _oct_store(m_ref, ph, m[ph * rows:(ph + 1) * rows])


def _glu_call(yt4, w_glu_b, b_glu, *, rows, n_ph, row_blk):
    g, t, p, lane_rows = yt4.shape
    n_rb, lane_blk, lane_rows_expected = _phase_blocks(rows, row_blk)
    assert lane_rows == lane_rows_expected
    kern = functools.partial(_glu_kernel, n_ph=n_ph, rows=row_blk, lane_rows=lane_blk)
    oct_shape = _oct_shape(rows, t, S5_WIDTH)
    return pl.pallas_call(
        kern,
        out_shape=jax.ShapeDtypeStruct(oct_shape, F32),
        grid=(n_rb, t // n_ph),
        in_specs=[
            pl.BlockSpec((g, n_ph, p, lane_blk), lambda i, j: (0, j, 0, i)),
            _const_spec((S5_WIDTH, S5_WIDTH)),
            _const_spec((1, S5_WIDTH)),
        ],
        out_specs=pl.BlockSpec((row_blk // OCT, oct_shape[1], n_ph, OCT, LANES),
                               lambda i, j: (i, 0, j, 0, 0)),
        compiler_params=_cparams(2),
        name="glu",
    )(yt4, w_glu_b, b_glu)


def _ff_chunks(n_fc):
    tiles = D_FF // MXU_DIM
    assert tiles * MXU_DIM == D_FF
    bounds = [MXU_DIM * ((tiles * c + n_fc - 1) // n_fc) for c in range(n_fc + 1)]
    return list(zip(bounds[:-1], bounds[1:]))


def _main_kernel(x_ref, m_ref, zuv_ref, mod_ref, g2_ref, gf_ref, lng_ref, lnb_ref, gw_ref, gbt_ref,
                 wo_ref, wgu_ref, wd_ref, y_ref, *rest, tm, cl, seq, n_fc):
    *maybe_v_out_ref, v_ref, ygm_ref, attn_ref = rest
    t = tm // OCT
    hd = GM_HEAD_DIM

    attn_ref[...] = jnp.dot(_oct_load(m_ref, t).astype(BF16), wo_ref[:S5_WIDTH, :],
                            preferred_element_type=F32)

    gv = _gelu(zuv_ref[:, GM_WIDTH:])
    cen = gv - jnp.mean(gv, axis=-1, keepdims=True)
    var = jnp.mean(cen * cen, axis=-1, keepdims=True)
    v = cen * lax.rsqrt(var + EPS) * lng_ref[...] + lnb_ref[...]
    for v_out_ref in maybe_v_out_ref:
        v_out_ref[...] = v
    v_ref[...] = v.astype(BF16)

    blk_i = _div_pow2(lax.broadcasted_iota(jnp.int32, (cl, cl), 0), CHUNK)
    blk_j = _div_pow2(lax.broadcasted_iota(jnp.int32, (cl, cl), 1), CHUNK)
    causal = blk_j <= blk_i
    first_head = lax.broadcasted_iota(jnp.int32, (cl, 2 * hd), 1) < hd
    for pr in range(GM_HEADS // 2):
        h0, h1 = 2 * pr, 2 * pr + 1
        wm = jnp.concatenate(
            [jnp.where(causal, gw_ref[h, :cl, :cl], 0.0) for h in (h0, h1)], axis=1).astype(BF16)
        bias = jnp.where(first_head, gbt_ref[:cl, h0:h0 + 1], gbt_ref[:cl, h1:h1 + 1])
        cs = slice(h0 * hd, (h1 + 1) * hd)
        for ci in range(tm // cl):
            rs = slice(ci * cl, (ci + 1) * cl)
            vv = v_ref[rs, cs]
            zero = jnp.zeros_like(vv)
            rhs = jnp.concatenate([jnp.where(first_head, vv, zero), jnp.where(first_head, zero, vv)],
                                  axis=0)
            mixed = jnp.dot(wm, rhs, preferred_element_type=F32) + bias
            ygm_ref[rs, cs] = (_gelu(zuv_ref[rs, cs]) * mixed).astype(BF16)

    attn = attn_ref[...] + jnp.dot(ygm_ref[...], wo_ref[S5_WIDTH:, :], preferred_element_type=F32)

    n_streams = mod_ref.shape[0]

    def mod_rows(idx):
        if n_streams == 1:
            return mod_ref[0, idx:idx + 1, :]
        return jnp.concatenate(
            [jnp.broadcast_to(mod_ref[s, idx:idx + 1, :], (seq, D_MODEL)) for s in range(n_streams)],
            axis=0)

    gate1, shift2, scale2, gate2 = mod_rows(2), mod_rows(3), mod_rows(4), mod_rows(5)
    x1 = x_ref[...] + gate1 * attn
    h2 = (x1 * _rms_scale(x1) * (g2_ref[...] * (1.0 + scale2)) + shift2).astype(BF16)

    acc = None
    for lo, hi in _ff_chunks(n_fc):
        gg = jnp.dot(h2, wgu_ref[:, lo:hi], preferred_element_type=F32)
        up = jnp.dot(h2, wgu_ref[:, D_FF + lo:D_FF + hi], preferred_element_type=F32)
        act = (gg * jax.nn.sigmoid(gg) * up).astype(BF16)
        part = jnp.dot(act, wd_ref[lo:hi, :], preferred_element_type=F32)
        acc = part if acc is None else acc + part
    x2 = x1 + gate2 * acc
    y_ref[...] = x2 * _rms_scale(x2) * gf_ref[...]


def _main_call(x2d, m, zuv, mod, norm2_g, final_g, ln_g, ln_b, gm_w, gm_bt, w_out_b, w_gu_b, w_down_b,
               *, seq, n_fc, want_v):
    n_tok = x2d.shape[0]
    n_oct, n_lb, t, _, _ = m.shape
    tm = OCT * t
    assert n_oct * tm == n_tok
    streams_per_tile = max(1, tm // seq)
    tiles_per_stream = max(1, seq // tm)
    assert streams_per_tile * seq == tm or tiles_per_stream * tm == seq
    cl = min(GM_CHUNK, seq)
    kern = functools.partial(_main_kernel, tm=tm, cl=cl, seq=seq, n_fc=n_fc)
    tok = lambda width: pl.BlockSpec((tm, width), lambda i: (i, 0))
    oct_rows = n_lb * t * OCT
    octs = pl.BlockSpec((oct_rows, LANES), lambda i: (i, 0))
    m = m.reshape(n_oct * oct_rows, LANES)
    out_shape = [jax.ShapeDtypeStruct((n_tok, D_MODEL), F32)]
    out_specs = [tok(D_MODEL)]
    if want_v:
        out_shape.append(jax.ShapeDtypeStruct((n_tok, GM_WIDTH), F32))
        out_specs.append(tok(GM_WIDTH))
    return pl.pallas_call(
        kern,
        out_shape=tuple(out_shape),
        grid=(n_oct,),
        in_specs=[
            tok(D_MODEL), octs, tok(2 * GM_WIDTH),
            pl.BlockSpec((streams_per_tile, N_MOD, D_MODEL), lambda i: (i // tiles_per_stream, 0, 0)),
            _const_spec((1, D_MODEL)),
            _const_spec((1, D_MODEL)),
            _const_spec((1, GM_WIDTH)),
            _const_spec((1, GM_WIDTH)),
            _const_spec((GM_HEADS, GM_CHUNK, GM_CHUNK)),
            _const_spec((GM_CHUNK, GM_HEADS)),
            _const_spec((D_MODEL, D_MODEL)),
            _const_spec((D_MODEL, 2 * D_FF)),
            _const_spec((D_FF, D_MODEL)),
        ],
        out_specs=tuple(out_specs),
        scratch_shapes=[pltpu.VMEM((tm, GM_WIDTH), BF16), pltpu.VMEM((tm, GM_WIDTH), BF16),
                        pltpu.VMEM((tm, D_MODEL), F32)],
        compiler_params=_cparams(1),
        name="main",
    )(x2d, m, zuv, mod, norm2_g, final_g, ln_g, ln_b, gm_w, gm_bt, w_out_b, w_gu_b, w_down_b)


def _trunk(x, mod, s0, prm, *, t, n_ph, row_blk, n_fc, want_v, f32_weights=None):
    b, seq, _ = x.shape
    n_chunks = seq // t
    rows = b * n_chunks
    scan = s0 is None
    assert (b == 1) if scan else (n_chunks == 1)
    n_sub = t // SUB

    zt4, zuv = _inproj_call(
        x.reshape(rows, t, D_MODEL), mod[:, 0], mod[:, 1], prm["norm1_g"], prm["w_in"],
        n_ph=n_ph, row_blk=row_blk)
    lane_rows = zt4.shape[-1]

    if scan:
        s0_l = None
    else:
        s0_l = jnp.transpose(s0, (1, 2, 0))
        s0_l = jnp.pad(s0_l, ((0, 0), (0, 0), (0, lane_rows - rows)))
    pending = dict(f32_weights or {})
    yt, sfin, *cast = _s5_call(zt4.reshape(S5_GROUPS, t * S5_GROUP, lane_rows),
                               prm["wk"], prm["qm"], prm["tabr"], prm["tabi"], s0_l,
                               n_sub=n_sub, scan=scan, cast_riders=tuple(pending.values()))
    prm = {**prm, **dict(zip(pending.keys(), cast))}
    m = _glu_call(yt.reshape(S5_GROUPS, t, S5_GROUP, lane_rows), prm["w_glu"], prm["b_glu"],
                  rows=rows, n_ph=n_ph, row_blk=row_blk)

    n_tok = b * seq
    y, *maybe_v = _main_call(
        x.reshape(n_tok, D_MODEL), m, zuv.reshape(n_tok, 2 * GM_WIDTH),
        mod, prm["norm2_g"], prm["final_g"], prm["ln_g"], prm["ln_b"],
        prm["gm_w"], prm["gm_bt"], prm["w_out"], prm["w_gu"], prm["w_down"],
        seq=seq, n_fc=n_fc, want_v=want_v)

    if scan:
        fin = sfin[:, :, rows - 1][None]
    else:
        fin = jnp.transpose(sfin[:, :, :rows], (2, 0, 1))
    v = maybe_v[0].reshape(b, seq, GM_WIDTH) if want_v else None
    return y.reshape(b, seq, D_MODEL), fin[..., :S5_STATE], fin[..., S5_STATE:], v, prm


def kernel(x_prompt, x_sample, state_s5_re, state_s5_im, c_prompt, c_sample, norm1_g, norm2_g, w_ada, b_ada, w_in, s5_lambda_re, s5_lambda_im, s5_log_step, s5_b_re, s5_b_im, s5_c_re, s5_c_im, s5_d, s5_w_glu, s5_b_glu, gm_ln_g, gm_ln_b, gm_w_s, gm_b_s, w_out, ffn_w_gu, ffn_w_down, final_g):
    depth = w_in.shape[0]
    assert depth == 1
    l = 0
    n_p = c_prompt.shape[0]
    n_s = c_sample.shape[0]

    c_all = jnp.concatenate([c_prompt, c_sample], axis=0)
    c_pad = jnp.pad(c_all, ((0, -c_all.shape[0] % 8), (0, 0)))
    mod_all = _ada_call(c_pad, w_ada[l], b_ada[l][None, :])
    mod_p = mod_all[:n_p].reshape(n_p, N_MOD, D_MODEL)
    mod_s = mod_all[n_p:n_p + n_s].reshape(n_s, N_MOD, D_MODEL)

    wk, qm, tabr, tabi, w_in_b = _s5_prep_call(
        s5_lambda_re[l], s5_lambda_im[l], s5_log_step[l], s5_b_re[l], s5_b_im[l],
        s5_c_re[l], s5_c_im[l], s5_d[l], cast_riders=(w_in[l],))

    prm = dict(
        norm1_g=norm1_g[l][None, :], norm2_g=norm2_g[l][None, :], final_g=final_g[None, :],
        w_in=w_in_b, ln_g=gm_ln_g[l][None, :], ln_b=gm_ln_b[l][None, :],
        wk=wk, qm=qm, tabr=tabr, tabi=tabi, b_glu=s5_b_glu[l][None, :],
        gm_w=gm_w_s[l], gm_bt=jnp.transpose(gm_b_s[l]),
    )
    later_weights = dict(w_glu=s5_w_glu[l], w_out=w_out[l], w_gu=ffn_w_gu[l], w_down=ffn_w_down[l])

    yp, pre, pim, _, prm = _trunk(x_prompt, mod_p, None, prm, f32_weights=later_weights,
                                  t=S5_LONG_SUBS * SUB, n_ph=8, row_blk=LANES, n_fc=2, want_v=False)
    s0 = jnp.concatenate([state_s5_re[l], state_s5_im[l]], axis=-1)
    n_b, seq_s, _ = x_sample.shape
    ys, sre, sim, vs, _ = _trunk(x_sample, mod_s, s0, prm,
                                 t=seq_s, n_ph=seq_s, row_blk=n_b, n_fc=2, want_v=True)
    return (yp, ys, pre[None], pim[None], sre[None], sim[None], vs[None])
```

```python
import functools
import math

import jax
import jax.numpy as jnp
from jax import lax
from jax.experimental import pallas as pl
from jax.experimental.pallas import tpu as pltpu

D_MODEL = 1024
S5_WIDTH = 512
S5_GROUP = 16
S5_GROUPS = 32
S5_STATE = 64
GM_WIDTH = 512
GM_CHUNK = 128
GM_HEADS = 8
GM_HEAD_DIM = 64
CHUNK = 64
IN_WIDTH = S5_WIDTH + 2 * GM_WIDTH
D_FF = 2816
N_MOD = 6
EPS = 1e-6

LANES = 128
BF16_SUBLANES = 16
MXU_DIM = 256
VMEM_LIMIT_BYTES = 56 * 1024 * 1024

SUB = MXU_DIM // S5_GROUP
S5_LONG_SUBS = 4
PREP_GROUPS_PER_STEP = 4
S5_GROUPS_PER_STEP_SHORT = 8
S5_GROUPS_PER_STEP_LONG = 2

F32 = jnp.float32
BF16 = jnp.bfloat16
HIGHEST = lax.Precision.HIGHEST


def _cparams(n_grid_axes):
    return pltpu.CompilerParams(
        dimension_semantics=("arbitrary",) * n_grid_axes,
        vmem_limit_bytes=VMEM_LIMIT_BYTES,
    )


def _const_spec(shape):
    nd = len(shape)
    return pl.BlockSpec(shape, lambda *_: (0,) * nd, pipeline_mode=pl.Buffered(1))


def _rms_scale(x):
    return lax.rsqrt(jnp.mean(x * x, axis=-1, keepdims=True) + EPS)


_GELU_C0 = math.sqrt(2.0 / math.pi)
_GELU_C1 = 0.044715 * _GELU_C0


def _gelu(x):
    hx = 0.5 * x
    return hx + hx * jnp.tanh(x * (_GELU_C0 + _GELU_C1 * (x * x)))


def _sigmoid(x):
    return 0.5 * jnp.tanh(0.5 * x) + 0.5


def _cmul(ar, ai, xr, xi):
    return ar * xr - ai * xi, ar * xi + ai * xr


OCT = 8


def _oct_shape(n_rows, t, width):
    assert n_rows % OCT == 0 and width % LANES == 0
    return (n_rows // OCT, width // LANES, t, OCT, LANES)


def _oct_store(ref, ph, val):
    n_oct, n_lb = ref.shape[0], ref.shape[1]
    for lb in range(n_lb):
        ref[:, lb, ph, :, :] = val[:, lb * LANES:(lb + 1) * LANES].reshape(n_oct, OCT, LANES)


def _oct_load(ref, t, n_lb, c0, n_chunks):
    chunks = []
    for c in range(c0, c0 + n_chunks):
        base = (c // OCT) * n_lb * t * OCT + c % OCT
        chunks.append(jnp.concatenate(
            [ref[pl.ds(base + lb * t * OCT, t, stride=OCT), :] for lb in range(n_lb)], axis=1))
    return jnp.concatenate(chunks, axis=0)


def _cast_rider_specs(arrays, n_steps):
    specs, shapes = [], []
    for a in arrays:
        rows, cols = a.shape
        blk = rows // n_steps
        assert blk * n_steps == rows and blk % BF16_SUBLANES == 0
        specs.append(pl.BlockSpec((blk, cols), lambda i: (i, 0)))
        shapes.append(jax.ShapeDtypeStruct((rows, cols), BF16))
    return specs, shapes


def _split_riders(refs, n_in, n_out, n_cast):
    ins, refs = refs[:n_in], refs[n_in:]
    cast_in, refs = refs[:n_cast], refs[n_cast:]
    outs, refs = refs[:n_out], refs[n_out:]
    cast_out, scratch = refs[:n_cast], refs[n_cast:]
    return (*ins, *outs, *scratch), list(zip(cast_in, cast_out))


def _run_riders(pairs):
    for src, dst in pairs:
        dst[...] = src[...].astype(BF16)


def _div_pow2(idx, divisor):
    shift = divisor.bit_length() - 1
    assert divisor == 1 << shift
    return lax.shift_right_logical(idx, shift)


def _mod_pow2(idx, divisor):
    assert divisor & (divisor - 1) == 0
    return lax.bitwise_and(idx, divisor - 1)


def _ada_kernel(c_ref, w_ref, b_ref, o_ref):
    c = c_ref[...]
    s = c * jax.nn.sigmoid(c)
    w = w_ref[...]
    s_hi, w_hi = s.astype(BF16), w.astype(BF16)
    s_lo = (s - s_hi.astype(F32)).astype(BF16)
    w_lo = (w - w_hi.astype(F32)).astype(BF16)
    dot = functools.partial(jnp.dot, preferred_element_type=F32)
    o_ref[...] = dot(s_hi, w_hi) + dot(s_lo, w_hi) + dot(s_hi, w_lo) + b_ref[...]


def _ada_call(c_pad, w_ada, b_ada):
    rows = c_pad.shape[0]
    n_out = w_ada.shape[1]
    bn = D_MODEL
    return pl.pallas_call(
        _ada_kernel,
        out_shape=jax.ShapeDtypeStruct((rows, n_out), F32),
        grid=(n_out // bn,),
        in_specs=[
            pl.BlockSpec((rows, D_MODEL), lambda j: (0, 0)),
            pl.BlockSpec((D_MODEL, bn), lambda j: (0, j)),
            pl.BlockSpec((1, bn), lambda j: (0, j)),
        ],
        out_specs=pl.BlockSpec((rows, bn), lambda j: (0, j)),
        compiler_params=_cparams(1),
        name="ada",
    )(c_pad, w_ada, b_ada)


def _discretise(lr, li, ls):
    step = jnp.exp(ls)
    mag = jnp.exp(lr * step)
    ar = mag * jnp.cos(li * step)
    ai = mag * jnp.sin(li * step)
    den = lr * lr + li * li
    fr = ((ar - 1.0) * lr + ai * li) / den
    fi = (ai * lr - (ar - 1.0) * li) / den
    return ar, ai, fr, fi


def _s5_prep_kernel(*refs, n_cast):
    refs, riders = _split_riders(refs, n_in=8, n_out=4, n_cast=n_cast)
    lam_re_ref, lam_im_ref, ls_ref, d_ref, *grouped = refs
    gps = grouped[0].shape[0]
    for gi in range(gps):
        g = pl.program_id(0) * gps + gi
        _s5_prep_group(g, lam_re_ref, lam_im_ref, ls_ref, d_ref, *[r.at[gi] for r in grouped])
    _run_riders(riders)


def _place(x, rows, cols, row_of_col):
    r = lax.broadcasted_iota(jnp.int32, (rows, cols), 0)
    c = lax.broadcasted_iota(jnp.int32, (rows, cols), 1)
    sel = jnp.where(r == row_of_col(c), 1.0, 0.0)
    return jnp.dot(x, sel, preferred_element_type=F32, precision=HIGHEST)


def _to_column(row):
    k = row.shape[1]
    r = lax.broadcasted_iota(jnp.int32, (k, k), 0)
    c = lax.broadcasted_iota(jnp.int32, (k, k), 1)
    return jnp.sum(jnp.where(r == c, jnp.broadcast_to(row, (k, k)), 0.0), axis=1, keepdims=True)


def _s5_prep_group(g, lam_re_ref, lam_im_ref, ls_ref, d_ref, b_re_ref, b_im_ref, c_re_ref, c_im_ref,
                   wk_ref, qm_ref, tabr_ref, tabi_ref):
    n, sub, p = S5_STATE, SUB, S5_GROUP
    width = sub * p

    lr_row = lam_re_ref[pl.ds(g, 1), :]
    li_row = lam_im_ref[pl.ds(g, 1), :]
    ls_all = ls_ref[...]
    grp_lane = lax.broadcasted_iota(jnp.int32, ls_all.shape, 1)
    ls = jnp.sum(jnp.where(grp_lane == g, ls_all, 0.0), axis=1, keepdims=True)
    wrap_n = lambda c: _mod_pow2(c, n)
    wrap_p = lambda c: _mod_pow2(c, p)

    twice = lambda row: _place(jnp.broadcast_to(row, (p, n)), n, 2 * n, wrap_n)
    a2r, a2i, _, _ = _discretise(twice(lr_row), twice(li_row), ls)
    lane2 = lax.broadcasted_iota(jnp.int32, (p, 2 * n), 1)
    first = lane2 < n
    c2r = _place(c_re_ref[...], n, 2 * n, wrap_n)
    c2i = _place(c_im_ref[...], n, 2 * n, wrap_n)
    pr = jnp.ones_like(a2r)
    pi = jnp.zeros_like(a2r)
    ccat = []
    for _ in range(sub + 1):
        ccat.append(c2r * jnp.where(first, pr, -pi) + c2i * jnp.where(first, -pi, -pr))
        pr, pi = _cmul(a2r, a2i, pr, pi)
    qm_ref[...] = jnp.concatenate(ccat[1:], axis=0).astype(BF16)
    rcat = jnp.concatenate(ccat[:sub], axis=0)

    acr, aci, fr, fi = _discretise(_to_column(lr_row), _to_column(li_row), ls)
    btr = _place(b_re_ref[...], p, width, wrap_p)
    bti = _place(b_im_ref[...], p, width, wrap_p)
    bbr, bbi = _cmul(fr, fi, btr, bti)
    pows = [(jnp.ones_like(acr), jnp.zeros_like(acr))]
    for _ in range(sub):
        pows.append(_cmul(acr, aci, *pows[-1]))
    lane_blk = _div_pow2(lax.broadcasted_iota(jnp.int32, (n, width), 1), p)
    apr = jnp.zeros((n, width), F32)
    api = jnp.zeros((n, width), F32)
    for k in range(sub):
        sel = lane_blk == k
        apr = jnp.where(sel, pows[sub - 1 - k][0], apr)
        api = jnp.where(sel, pows[sub - 1 - k][1], api)
    pmr, pmi = _cmul(apr, api, bbr, bbi)
    wk_ref[width:width + n, :] = pmr.astype(BF16)
    wk_ref[width + n:, :] = pmi.astype(BF16)

    bbcat = jnp.concatenate([bbr, bbi], axis=0)
    kt = jnp.dot(rcat, bbcat, preferred_element_type=F32, precision=HIGHEST)
    row = lax.broadcasted_iota(jnp.int32, (width, width), 0)
    lane = lax.broadcasted_iota(jnp.int32, (width, width), 1)
    d_col = _to_column(d_ref[pl.ds(g, 1), :])
    row_p = lax.broadcasted_iota(jnp.int32, (p, width), 0)
    lane_p = lax.broadcasted_iota(jnp.int32, (p, width), 1)
    d_diag = jnp.where(row_p == _mod_pow2(lane_p, p), d_col, 0.0)
    kt = jnp.concatenate([kt[:p] + d_diag, kt[p:]], axis=0)
    col_blk = _div_pow2(lane, p)
    m16 = jnp.zeros((width, width), F32)
    for k in range(sub):
        if k == 0:
            shifted = kt
        else:
            shifted = jnp.concatenate(
                [jnp.zeros((k * p, width), F32), kt[:width - k * p]], axis=0)
        m16 = jnp.where(col_blk == k, shifted, m16)
    wk_ref[:width, :] = m16.astype(BF16)

    tabr_ref[...] = jnp.broadcast_to(pows[sub][0], (n, LANES))
    tabi_ref[...] = jnp.broadcast_to(pows[sub][1], (n, LANES))


def _s5_prep_call(lam_re, lam_im, log_step, b_re, b_im, c_re, c_im, d, cast_riders=()):
    g, n, p, sub = S5_GROUPS, S5_STATE, S5_GROUP, SUB
    width = sub * p
    grp = lambda shape: pl.BlockSpec((PREP_GROUPS_PER_STEP,) + shape, lambda i: (i, 0, 0))
    whole = lambda a: pl.BlockSpec(a.shape, lambda i: (0,) * a.ndim)
    n_steps = g // PREP_GROUPS_PER_STEP
    rider_specs, rider_shapes = _cast_rider_specs(cast_riders, n_steps)
    return pl.pallas_call(
        functools.partial(_s5_prep_kernel, n_cast=len(cast_riders)),
        out_shape=(
            jax.ShapeDtypeStruct((g, width + 2 * n, width), BF16),
            jax.ShapeDtypeStruct((g, width, 2 * n), BF16),
            jax.ShapeDtypeStruct((g, n, LANES), F32),
            jax.ShapeDtypeStruct((g, n, LANES), F32),
            *rider_shapes,
        ),
        grid=(n_steps,),
        in_specs=[whole(lam_re), whole(lam_im), whole(log_step), whole(d)]
        + [grp((n, p))] * 2 + [grp((p, n))] * 2 + rider_specs,
        out_specs=(grp((width + 2 * n, width)), grp((width, 2 * n)),
                   grp((n, LANES)), grp((n, LANES)), *rider_specs),
        compiler_params=_cparams(1),
        name="s5_prep",
    )(lam_re, lam_im, log_step, d, b_re, b_im, c_re, c_im, *cast_riders)


def _inproj_kernel(x_ref, sh_ref, sc_ref, g1_ref, w_ref, zt_ref, zuv_ref, hs_ref,
                   *, n_ph, rows, lane_rows):
    m = rows * n_ph
    gain = (g1_ref[...] * (1.0 + sc_ref[...]))[:, None, :]
    shift = sh_ref[...][:, None, :]
    x3 = x_ref[...]
    h = (x3 * _rms_scale(x3) * gain + shift).reshape(m, D_MODEL)
    zuv = jnp.dot(h.astype(BF16), w_ref[:, S5_WIDTH:], preferred_element_type=F32)
    zuv_ref[...] = zuv.reshape(rows, n_ph, 2 * GM_WIDTH)

    n_lb = D_MODEL // LANES
    for lb in range(n_lb):
        hs_ref[lb] = h[:, lb * LANES:(lb + 1) * LANES]
    hp = jnp.concatenate(
        [jnp.concatenate([hs_ref[lb, pl.ds(ph, rows, stride=n_ph), :] for lb in range(n_lb)], axis=1)
         for ph in range(n_ph)], axis=0).astype(BF16)
    z5 = jnp.dot(hp, w_ref[:, :S5_WIDTH], preferred_element_type=F32)
    for ph in range(n_ph):
        zz = z5[ph * rows:(ph + 1) * rows]
        if lane_rows > rows:
            zz = jnp.concatenate([zz, jnp.zeros((lane_rows - rows, S5_WIDTH), F32)], axis=0)
        zt = zz.T.reshape(S5_GROUPS, S5_GROUP, lane_rows)
        zt_ref[:, ph, :, :] = zt.astype(BF16)


def _phase_blocks(n_rows, row_blk):
    if row_blk % LANES == 0:
        assert n_rows % row_blk == 0
        return n_rows // row_blk, row_blk, n_rows
    assert row_blk == n_rows
    lane_rows = -(-n_rows // LANES) * LANES
    return 1, lane_rows, lane_rows


def _inproj_call(x3, shift1, scale1, norm1_g, w_in_b, *, n_ph, row_blk):
    rows, t, _ = x3.shape
    mrows = shift1.shape[0]
    assert mrows in (1, row_blk)
    n_rb, lane_blk, lane_rows = _phase_blocks(rows, row_blk)
    kern = functools.partial(_inproj_kernel, n_ph=n_ph, rows=row_blk, lane_rows=lane_blk)
    return pl.pallas_call(
        kern,
        out_shape=(
            jax.ShapeDtypeStruct((S5_GROUPS, t, S5_GROUP, lane_rows), BF16),
            jax.ShapeDtypeStruct((rows, t, 2 * GM_WIDTH), F32),
        ),
        grid=(n_rb, t // n_ph),
        in_specs=[
            pl.BlockSpec((row_blk, n_ph, D_MODEL), lambda i, j: (i, j, 0)),
            _const_spec((mrows, D_MODEL)),
            _const_spec((mrows, D_MODEL)),
            _const_spec((1, D_MODEL)),
            _const_spec((D_MODEL, IN_WIDTH)),
        ],
        out_specs=(
            pl.BlockSpec((S5_GROUPS, n_ph, S5_GROUP, lane_blk), lambda i, j: (0, j, 0, i)),
            pl.BlockSpec((row_blk, n_ph, 2 * GM_WIDTH), lambda i, j: (i, j, 0)),
        ),
        scratch_shapes=[pltpu.VMEM((D_MODEL // LANES, row_blk * n_ph, LANES), F32)],
        compiler_params=_cparams(2),
        name="inproj",
    )(x3, shift1, scale1, norm1_g, w_in_b)


def _s5_kernel(*refs, n_sub, lanes, scan, n_cast):
    refs, riders = _split_riders(refs, n_in=5 if scan else 6, n_out=2, n_cast=n_cast)
    for gi in range(refs[0].shape[0]):
        _s5_group(*[r.at[gi] for r in refs], n_sub=n_sub, lanes=lanes, scan=scan)
    _run_riders(riders)


def _s5_group(*refs, n_sub, lanes, scan):
    if scan:
        zt_ref, wk_ref, qm_ref, tabr_ref, tabi_ref, yt_ref, sf_ref, ybuf, lbuf = refs
        s0_ref = None
    else:
        zt_ref, wk_ref, qm_ref, tabr_ref, tabi_ref, s0_ref, yt_ref, sf_ref, ybuf, lbuf = refs
    n = S5_STATE
    width = SUB * S5_GROUP
    wk = wk_ref[...]
    qm = qm_ref[...]
    widen = lambda tile: jnp.concatenate([tile] * (lanes // LANES), axis=1)
    a1r, a1i = widen(tabr_ref[...]), widen(tabi_ref[...])
    sub_pows = [(a1r, a1i)]
    for _ in range(n_sub - 1):
        sub_pows.append(_cmul(a1r, a1i, *sub_pows[-1]))

    lr = li = None
    for j in range(n_sub):
        u = zt_ref[j * width:(j + 1) * width, :]
        r = jnp.dot(wk, u, preferred_element_type=F32)
        ybuf[j * width:(j + 1) * width, :] = r[:width]
        wr = r[width:width + n]
        wi = r[width + n:]
        if j == 0:
            lr, li = wr, wi
        else:
            tr, ti = _cmul(a1r, a1i, lr, li)
            lr, li = tr + wr, ti + wi
        lbuf[j, :n, :] = lr
        lbuf[j, n:, :] = li

    if scan:
        lane = lax.broadcasted_iota(jnp.int32, (n, lanes), 1)
        xr, xi = lr, li
        mr, mi = sub_pows[n_sub - 1]
        for i in range(int(math.log2(lanes))):
            sh = 1 << i
            rr = jnp.where(lane >= sh, pltpu.roll(xr, sh, 1), 0.0)
            ri = jnp.where(lane >= sh, pltpu.roll(xi, sh, 1), 0.0)
            tr, ti = _cmul(mr, mi, rr, ri)
            xr, xi = xr + tr, xi + ti
            mr, mi = _cmul(mr, mi, mr, mi)
        sr = jnp.where(lane >= 1, pltpu.roll(xr, 1, 1), 0.0)
        si = jnp.where(lane >= 1, pltpu.roll(xi, 1, 1), 0.0)
    else:
        sr = s0_ref[:n, :]
        si = s0_ref[n:, :]

    for j in range(n_sub):
        if j == 0:
            pr, pi = sr, si
        else:
            tr, ti = _cmul(*sub_pows[j - 1], sr, si)
            pr, pi = lbuf[j - 1, :n, :] + tr, lbuf[j - 1, n:, :] + ti
        sp = jnp.concatenate([pr, pi], axis=0).astype(BF16)
        y = ybuf[j * width:(j + 1) * width, :] + jnp.dot(qm, sp, preferred_element_type=F32)
        yt_ref[j * width:(j + 1) * width, :] = y.astype(BF16)
    tr, ti = _cmul(*sub_pows[n_sub - 1], sr, si)
    keep = slice(lanes - LANES, lanes)
    sf_ref[:n, :] = lbuf[n_sub - 1, :n, keep] + tr[:, keep]
    sf_ref[n:, :] = lbuf[n_sub - 1, n:, keep] + ti[:, keep]


def _s5_call(zt, wk, qm, tabr, tabi, s0, *, n_sub, scan, cast_riders=()):
    g, rows, lanes = zt.shape
    n = S5_STATE
    width = SUB * S5_GROUP
    assert lanes % LANES == 0 and lanes & (lanes - 1) == 0
    gps = S5_GROUPS_PER_STEP_LONG if scan else S5_GROUPS_PER_STEP_SHORT
    grp = lambda shape: pl.BlockSpec((gps,) + shape, lambda i: (i, 0, 0))
    in_specs = [grp((rows, lanes)), grp((width + 2 * n, width)), grp((width, 2 * n)),
                grp((n, LANES)), grp((n, LANES))]
    args = [zt, wk, qm, tabr, tabi]
    if not scan:
        in_specs.append(grp((2 * n, lanes)))
        args.append(s0)
    rider_specs, rider_shapes = _cast_rider_specs(cast_riders, g // gps)
    kern = functools.partial(_s5_kernel, n_sub=n_sub, lanes=lanes, scan=scan,
                             n_cast=len(cast_riders))
    return pl.pallas_call(
        kern,
        out_shape=(jax.ShapeDtypeStruct((g, rows, lanes), BF16),
                   jax.ShapeDtypeStruct((g, 2 * n, LANES), F32), *rider_shapes),
        grid=(g // gps,),
        in_specs=in_specs + rider_specs,
        out_specs=(grp((rows, lanes)), grp((2 * n, LANES)), *rider_specs),
        scratch_shapes=[pltpu.VMEM((gps, rows, lanes), F32),
                        pltpu.VMEM((gps, n_sub, 2 * n, lanes), F32)],
        compiler_params=_cparams(1),
        name="s5",
    )(*args, *cast_riders)


def _glu_kernel(yt_ref, w_ref, b_ref, m_ref, *, n_ph, rows, lane_rows):
    gs = []
    for ph in range(n_ph):
        yt = yt_ref[:, ph, :, :].astype(F32).reshape(S5_WIDTH, lane_rows)
        gs.append(_gelu(yt.T[:rows]))
    gy = jnp.concatenate(gs, axis=0)
    gate = jnp.dot(gy.astype(BF16), w_ref[...], preferred_element_type=F32) + b_ref[...]
    m = gy * _sigmoid(gate)
    for ph in range(n_ph):
        _oct_store(m_ref, ph, m[ph * rows:(ph + 1) * rows])


def _glu_call(yt4, w_glu_b, b_glu, *, rows, n_ph, row_blk):
    g, t, p, lane_rows = yt4.shape
    n_rb, lane_blk, lane_rows_expected = _phase_blocks(rows, row_blk)
    assert lane_rows == lane_rows_expected
    kern = functools.partial(_glu_kernel, n_ph=n_ph, rows=row_blk, lane_rows=lane_blk)
    oct_shape = _oct_shape(rows, t, S5_WIDTH)
    return pl.pallas_call(
        kern,
        out_shape=jax.ShapeDtypeStruct(oct_shape, F32),
        grid=(n_rb, t // n_ph),
        in_specs=[
            pl.BlockSpec((g, n_ph, p, lane_blk), lambda i, j: (0, j, 0, i)),
            _const_spec((S5_WIDTH, S5_WIDTH)),
            _const_spec((1, S5_WIDTH)),
        ],
        out_specs=pl.BlockSpec((row_blk // OCT, oct_shape[1], n_ph, OCT, LANES),
                               lambda i, j: (i, 0, j, 0, 0)),
        compiler_params=_cparams(2),
        name="glu",
    )(yt4, w_glu_b, b_glu)


def _ff_chunks(n_fc):
    tiles = D_FF // MXU_DIM
    assert tiles * MXU_DIM == D_FF
    bounds = [MXU_DIM * ((tiles * c + n_fc - 1) // n_fc) for c in range(n_fc + 1)]
    return list(zip(bounds[:-1], bounds[1:]))


def _main_kernel(*refs, tm, n_parts, **static):
    rows = tm // n_parts
    for part in range(n_parts):
        _main_rows(*refs, r0=part * rows, rows=rows, tm=tm, **static)


def _main_rows(x_ref, m_ref, zuv_ref, mod_ref, g2_ref, gf_ref, lng_ref, lnb_ref, gw_ref, gbt_ref,
               wo_ref, wgu_ref, wd_ref, y_ref, *rest, r0, rows, tm, t, cl, seq, n_fc):
    *maybe_v_out_ref, v_ref, ygm_ref, attn_ref = rest
    hd = GM_HEAD_DIM
    rr = slice(r0, r0 + rows)
    assert rows % t == 0 and rows % cl == 0

    m_rows = _oct_load(m_ref, t, S5_WIDTH // LANES, r0 // t, rows // t)
    attn_ref[rr, :] = jnp.dot(m_rows.astype(BF16), wo_ref[:S5_WIDTH, :], preferred_element_type=F32)

    gv = _gelu(zuv_ref[rr, GM_WIDTH:])
    cen = gv - jnp.mean(gv, axis=-1, keepdims=True)
    var = jnp.mean(cen * cen, axis=-1, keepdims=True)
    v = cen * lax.rsqrt(var + EPS) * lng_ref[...] + lnb_ref[...]
    for v_out_ref in maybe_v_out_ref:
        v_out_ref[rr, :] = v
    v_ref[rr, :] = v.astype(BF16)

    blk_i = _div_pow2(lax.broadcasted_iota(jnp.int32, (cl, cl), 0), CHUNK)
    blk_j = _div_pow2(lax.broadcasted_iota(jnp.int32, (cl, cl), 1), CHUNK)
    causal = blk_j <= blk_i
    first_head = lax.broadcasted_iota(jnp.int32, (cl, 2 * hd), 1) < hd
    for pr in range(GM_HEADS // 2):
        h0, h1 = 2 * pr, 2 * pr + 1
        wm = jnp.concatenate(
            [jnp.where(causal, gw_ref[h, :cl, :cl], 0.0) for h in (h0, h1)], axis=1).astype(BF16)
        bias = jnp.where(first_head, gbt_ref[:cl, h0:h0 + 1], gbt_ref[:cl, h1:h1 + 1])
        cs = slice(h0 * hd, (h1 + 1) * hd)
        for ci in range(rows // cl):
            rs = slice(r0 + ci * cl, r0 + (ci + 1) * cl)
            vv = v_ref[rs, cs]
            zero = jnp.zeros_like(vv)
            rhs = jnp.concatenate([jnp.where(first_head, vv, zero), jnp.where(first_head, zero, vv)],
                                  axis=0)
            mixed = jnp.dot(wm, rhs, preferred_element_type=F32) + bias
            ygm_ref[rs, cs] = (_gelu(zuv_ref[rs, cs]) * mixed).astype(BF16)

    attn = attn_ref[rr, :] + jnp.dot(ygm_ref[rr, :], wo_ref[S5_WIDTH:, :], preferred_element_type=F32)

    def mod_rows(idx):
        if mod_ref.shape[0] == 1:
            return mod_ref[0, idx:idx + 1, :]
        assert rows % seq == 0
        return jnp.concatenate(
            [jnp.broadcast_to(mod_ref[s, idx:idx + 1, :], (seq, D_MODEL))
             for s in range(r0 // seq, (r0 + rows) // seq)], axis=0)

    gate1, shift2, scale2, gate2 = mod_rows(2), mod_rows(3), mod_rows(4), mod_rows(5)
    x1 = x_ref[rr, :] + gate1 * attn
    h2 = (x1 * _rms_scale(x1) * (g2_ref[...] * (1.0 + scale2)) + shift2).astype(BF16)

    acc = None
    for lo, hi in _ff_chunks(n_fc):
        gg = jnp.dot(h2, wgu_ref[:, lo:hi], preferred_element_type=F32)
        up = jnp.dot(h2, wgu_ref[:, D_FF + lo:D_FF + hi], preferred_element_type=F32)
        act = (gg * jax.nn.sigmoid(gg) * up).astype(BF16)
        part = jnp.dot(act, wd_ref[lo:hi, :], preferred_element_type=F32)
        acc = part if acc is None else acc + part
    x2 = x1 + gate2 * acc
    y_ref[rr, :] = x2 * _rms_scale(x2) * gf_ref[...]


def _main_call(x2d, m, zuv, mod, norm2_g, final_g, ln_g, ln_b, gm_w, gm_bt, w_out_b, w_gu_b, w_down_b,
               *, seq, n_fc, n_parts, want_v):
    n_tok = x2d.shape[0]
    n_oct, n_lb, t, _, _ = m.shape
    assert n_oct % n_parts == 0 and n_oct * OCT * t == n_tok
    tm = n_parts * OCT * t
    streams_per_tile = max(1, tm // seq)
    tiles_per_stream = max(1, seq // tm)
    assert streams_per_tile * seq == tm or tiles_per_stream * tm == seq
    cl = min(GM_CHUNK, seq)
    kern = functools.partial(_main_kernel, tm=tm, t=t, cl=cl, seq=seq, n_fc=n_fc, n_parts=n_parts)
    tok = lambda width: pl.BlockSpec((tm, width), lambda i: (i, 0))
    oct_rows = n_lb * t * OCT
    octs = pl.BlockSpec((n_parts * oct_rows, LANES), lambda i: (i, 0))
    m = m.reshape(n_oct * oct_rows, LANES)
    out_shape = [jax.ShapeDtypeStruct((n_tok, D_MODEL), F32)]
    out_specs = [tok(D_MODEL)]
    if want_v:
        out_shape.append(jax.ShapeDtypeStruct((n_tok, GM_WIDTH), F32))
        out_specs.append(tok(GM_WIDTH))
    return pl.pallas_call(
        kern,
        out_shape=tuple(out_shape),
        grid=(n_oct // n_parts,),
        in_specs=[
            tok(D_MODEL), octs, tok(2 * GM_WIDTH),
            pl.BlockSpec((streams_per_tile, N_MOD, D_MODEL), lambda i: (i // tiles_per_stream, 0, 0)),
            _const_spec((1, D_MODEL)),
            _const_spec((1, D_MODEL)),
            _const_spec((1, GM_WIDTH)),
            _const_spec((1, GM_WIDTH)),
            _const_spec((GM_HEADS, GM_CHUNK, GM_CHUNK)),
            _const_spec((GM_CHUNK, GM_HEADS)),
            _const_spec((D_MODEL, D_MODEL)),
            _const_spec((D_MODEL, 2 * D_FF)),
            _const_spec((D_FF, D_MODEL)),
        ],
        out_specs=tuple(out_specs),
        scratch_shapes=[pltpu.VMEM((tm, GM_WIDTH), BF16), pltpu.VMEM((tm, GM_WIDTH), BF16),
                        pltpu.VMEM((tm, D_MODEL), F32)],
        compiler_params=_cparams(1),
        name="main",
    )(x2d, m, zuv, mod, norm2_g, final_g, ln_g, ln_b, gm_w, gm_bt, w_out_b, w_gu_b, w_down_b)


def _trunk(x, mod, s0, prm, *, t, n_ph, row_blk, n_fc, n_parts, want_v, f32_weights=None):
    b, seq, _ = x.shape
    n_chunks = seq // t
    rows = b * n_chunks
    scan = s0 is None
    assert (b == 1) if scan else (n_chunks == 1)
    n_sub = t // SUB

    zt4, zuv = _inproj_call(
        x.reshape(rows, t, D_MODEL), mod[:, 0], mod[:, 1], prm["norm1_g"], prm["w_in"],
        n_ph=n_ph, row_blk=row_blk)
    lane_rows = zt4.shape[-1]

    if scan:
        s0_l = None
    else:
        s0_l = jnp.transpose(s0, (1, 2, 0))
        s0_l = jnp.pad(s0_l, ((0, 0), (0, 0), (0, lane_rows - rows)))
    pending = dict(f32_weights or {})
    yt, sfin, *cast = _s5_call(zt4.reshape(S5_GROUPS, t * S5_GROUP, lane_rows),
                               prm["wk"], prm["qm"], prm["tabr"], prm["tabi"], s0_l,
                               n_sub=n_sub, scan=scan, cast_riders=tuple(pending.values()))
    prm = {**prm, **dict(zip(pending.keys(), cast))}
    m = _glu_call(yt.reshape(S5_GROUPS, t, S5_GROUP, lane_rows), prm["w_glu"], prm["b_glu"],
                  rows=rows, n_ph=n_ph, row_blk=row_blk)

    n_tok = b * seq
    y, *maybe_v = _main_call(
        x.reshape(n_tok, D_MODEL), m, zuv.reshape(n_tok, 2 * GM_WIDTH),
        mod, prm["norm2_g"], prm["final_g"], prm["ln_g"], prm["ln_b"],
        prm["gm_w"], prm["gm_bt"], prm["w_out"], prm["w_gu"], prm["w_down"],
        seq=seq, n_fc=n_fc, n_parts=n_parts, want_v=want_v)

    first_kept = lane_rows - LANES
    if scan:
        fin = sfin[:, :, rows - 1 - first_kept][None]
    else:
        assert first_kept == 0
        fin = jnp.transpose(sfin[:, :, :rows], (2, 0, 1))
    v = maybe_v[0].reshape(b, seq, GM_WIDTH) if want_v else None
    return y.reshape(b, seq, D_MODEL), fin[..., :S5_STATE], fin[..., S5_STATE:], v, prm


def kernel(x_prompt, x_sample, state_s5_re, state_s5_im, c_prompt, c_sample, norm1_g, norm2_g, w_ada, b_ada, w_in, s5_lambda_re, s5_lambda_im, s5_log_step, s5_b_re, s5_b_im, s5_c_re, s5_c_im, s5_d, s5_w_glu, s5_b_glu, gm_ln_g, gm_ln_b, gm_w_s, gm_b_s, w_out, ffn_w_gu, ffn_w_down, final_g):
    depth = w_in.shape[0]
    assert depth == 1
    l = 0
    n_p = c_prompt.shape[0]
    n_s = c_sample.shape[0]

    c_all = jnp.concatenate([c_prompt, c_sample], axis=0)
    c_pad = jnp.pad(c_all, ((0, -c_all.shape[0] % 8), (0, 0)))
    mod_all = _ada_call(c_pad, w_ada[l], b_ada[l][None, :])
    mod_p = mod_all[:n_p].reshape(n_p, N_MOD, D_MODEL)
    mod_s = mod_all[n_p:n_p + n_s].reshape(n_s, N_MOD, D_MODEL)

    wk, qm, tabr, tabi, w_in_b = _s5_prep_call(
        s5_lambda_re[l], s5_lambda_im[l], s5_log_step[l][None, :], s5_b_re[l], s5_b_im[l],
        s5_c_re[l], s5_c_im[l], s5_d[l], cast_riders=(w_in[l],))

    prm = dict(
        norm1_g=norm1_g[l][None, :], norm2_g=norm2_g[l][None, :], final_g=final_g[None, :],
        w_in=w_in_b, ln_g=gm_ln_g[l][None, :], ln_b=gm_ln_b[l][None, :],
        wk=wk, qm=qm, tabr=tabr, tabi=tabi, b_glu=s5_b_glu[l][None, :],
        gm_w=gm_w_s[l], gm_bt=jnp.transpose(gm_b_s[l]),
    )
    later_weights = dict(w_glu=s5_w_glu[l], w_out=w_out[l], w_gu=ffn_w_gu[l], w_down=ffn_w_down[l])

    yp, pre, pim, _, prm = _trunk(x_prompt, mod_p, None, prm, f32_weights=later_weights,
                                  t=S5_LONG_SUBS * SUB, n_ph=8, row_blk=LANES, n_fc=2, n_parts=1, want_v=False)
    s0 = jnp.concatenate([state_s5_re[l], state_s5_im[l]], axis=-1)
    n_b, seq_s, _ = x_sample.shape
    ys, sre, sim, vs, _ = _trunk(x_sample, mod_s, s0, prm,
                                 t=seq_s, n_ph=seq_s, row_blk=n_b, n_fc=2, n_parts=1, want_v=True)
    return (yp, ys, pre[None], pim[None], sre[None], sim[None], vs[None])
```

```python
import functools
import math

import jax
import jax.numpy as jnp
from jax import lax
from jax.experimental import pallas as pl
from jax.experimental.pallas import tpu as pltpu

D_MODEL = 1024
S5_WIDTH = 512
S5_GROUP = 16
S5_GROUPS = 32
S5_STATE = 64
GM_WIDTH = 512
GM_CHUNK = 128
GM_HEADS = 8
GM_HEAD_DIM = 64
CHUNK = 64
IN_WIDTH = S5_WIDTH + 2 * GM_WIDTH
D_FF = 2816
N_MOD = 6
EPS = 1e-6

LANES = 128
BF16_SUBLANES = 16
MXU_DIM = 256
VMEM_LIMIT_BYTES = 56 * 1024 * 1024

SUB = MXU_DIM // S5_GROUP
S5_LONG_SUBS = 4
PREP_GROUPS_PER_STEP = 4
S5_GROUPS_PER_STEP_SHORT = 8
S5_GROUPS_PER_STEP_LONG = 4

F32 = jnp.float32
BF16 = jnp.bfloat16
HIGHEST = lax.Precision.HIGHEST


def _cparams(n_grid_axes):
    return pltpu.CompilerParams(
        dimension_semantics=("arbitrary",) * n_grid_axes,
        vmem_limit_bytes=VMEM_LIMIT_BYTES,
    )


def _const_spec(shape):
    nd = len(shape)
    return pl.BlockSpec(shape, lambda *_: (0,) * nd, pipeline_mode=pl.Buffered(1))


def _rms_scale(x):
    return lax.rsqrt(jnp.mean(x * x, axis=-1, keepdims=True) + EPS)


_GELU_C0 = math.sqrt(2.0 / math.pi)
_GELU_C1 = 0.044715 * _GELU_C0


def _gelu(x):
    hx = 0.5 * x
    return hx + hx * jnp.tanh(x * (_GELU_C0 + _GELU_C1 * (x * x)))


def _sigmoid(x):
    return 0.5 * jnp.tanh(0.5 * x) + 0.5


def _cmul(ar, ai, xr, xi):
    return ar * xr - ai * xi, ar * xi + ai * xr


OCT = 8


def _oct_shape(n_rows, t, width):
    assert n_rows % OCT == 0 and width % LANES == 0
    return (n_rows // OCT, width // LANES, t, OCT, LANES)


def _oct_store(ref, ph, val):
    n_oct, n_lb = ref.shape[0], ref.shape[1]
    for lb in range(n_lb):
        ref[:, lb, ph, :, :] = val[:, lb * LANES:(lb + 1) * LANES].reshape(n_oct, OCT, LANES)


def _oct_load(ref, t, n_lb, c0, n_chunks):
    chunks = []
    for c in range(c0, c0 + n_chunks):
        base = (c // OCT) * n_lb * t * OCT + c % OCT
        chunks.append(jnp.concatenate(
            [ref[pl.ds(base + lb * t * OCT, t, stride=OCT), :] for lb in range(n_lb)], axis=1))
    return jnp.concatenate(chunks, axis=0)


def _cast_rider_specs(arrays, n_steps):
    specs, shapes = [], []
    for a in arrays:
        rows, cols = a.shape
        blk = rows // n_steps
        assert blk * n_steps == rows and blk % BF16_SUBLANES == 0
        specs.append(pl.BlockSpec((blk, cols), lambda i: (i, 0)))
        shapes.append(jax.ShapeDtypeStruct((rows, cols), BF16))
    return specs, shapes


def _split_riders(refs, n_in, n_out, n_cast):
    ins, refs = refs[:n_in], refs[n_in:]
    cast_in, refs = refs[:n_cast], refs[n_cast:]
    outs, refs = refs[:n_out], refs[n_out:]
    cast_out, scratch = refs[:n_cast], refs[n_cast:]
    return (*ins, *outs, *scratch), list(zip(cast_in, cast_out))


def _run_riders(pairs):
    for src, dst in pairs:
        dst[...] = src[...].astype(BF16)


def _div_pow2(idx, divisor):
    shift = divisor.bit_length() - 1
    assert divisor == 1 << shift
    return lax.shift_right_logical(idx, shift)


def _mod_pow2(idx, divisor):
    assert divisor & (divisor - 1) == 0
    return lax.bitwise_and(idx, divisor - 1)


def _ada_kernel(c_ref, w_ref, b_ref, o_ref):
    c = c_ref[...]
    s = c * jax.nn.sigmoid(c)
    w = w_ref[...]
    s_hi, w_hi = s.astype(BF16), w.astype(BF16)
    s_lo = (s - s_hi.astype(F32)).astype(BF16)
    w_lo = (w - w_hi.astype(F32)).astype(BF16)
    dot = functools.partial(jnp.dot, preferred_element_type=F32)
    o_ref[...] = dot(s_hi, w_hi) + dot(s_lo, w_hi) + dot(s_hi, w_lo) + b_ref[...]


def _ada_call(c_pad, w_ada, b_ada):
    rows = c_pad.shape[0]
    n_out = w_ada.shape[1]
    bn = D_MODEL
    return pl.pallas_call(
        _ada_kernel,
        out_shape=jax.ShapeDtypeStruct((rows, n_out), F32),
        grid=(n_out // bn,),
        in_specs=[
            pl.BlockSpec((rows, D_MODEL), lambda j: (0, 0)),
            pl.BlockSpec((D_MODEL, bn), lambda j: (0, j)),
            pl.BlockSpec((1, bn), lambda j: (0, j)),
        ],
        out_specs=pl.BlockSpec((rows, bn), lambda j: (0, j)),
        compiler_params=_cparams(1),
        name="ada",
    )(c_pad, w_ada, b_ada)


def _discretise(lr, li, ls):
    step = jnp.exp(ls)
    mag = jnp.exp(lr * step)
    ar = mag * jnp.cos(li * step)
    ai = mag * jnp.sin(li * step)
    den = lr * lr + li * li
    fr = ((ar - 1.0) * lr + ai * li) / den
    fi = (ai * lr - (ar - 1.0) * li) / den
    return ar, ai, fr, fi


def _s5_prep_kernel(*refs, n_cast):
    refs, riders = _split_riders(refs, n_in=8, n_out=4, n_cast=n_cast)
    lam_re_ref, lam_im_ref, ls_ref, d_ref, *grouped = refs
    gps = grouped[0].shape[0]
    for gi in range(gps):
        g = pl.program_id(0) * gps + gi
        _s5_prep_group(g, lam_re_ref, lam_im_ref, ls_ref, d_ref, *[r.at[gi] for r in grouped])
    _run_riders(riders)


def _place(x, rows, cols, row_of_col):
    r = lax.broadcasted_iota(jnp.int32, (rows, cols), 0)
    c = lax.broadcasted_iota(jnp.int32, (rows, cols), 1)
    sel = jnp.where(r == row_of_col(c), 1.0, 0.0)
    return jnp.dot(x, sel, preferred_element_type=F32, precision=HIGHEST)


def _to_column(row):
    k = row.shape[1]
    r = lax.broadcasted_iota(jnp.int32, (k, k), 0)
    c = lax.broadcasted_iota(jnp.int32, (k, k), 1)
    return jnp.sum(jnp.where(r == c, jnp.broadcast_to(row, (k, k)), 0.0), axis=1, keepdims=True)


def _s5_prep_group(g, lam_re_ref, lam_im_ref, ls_ref, d_ref, b_re_ref, b_im_ref, c_re_ref, c_im_ref,
                   wk_ref, qm_ref, tabr_ref, tabi_ref):
    n, sub, p = S5_STATE, SUB, S5_GROUP
    width = sub * p

    lr_row = lam_re_ref[pl.ds(g, 1), :]
    li_row = lam_im_ref[pl.ds(g, 1), :]
    ls_all = ls_ref[...]
    grp_lane = lax.broadcasted_iota(jnp.int32, ls_all.shape, 1)
    ls = jnp.sum(jnp.where(grp_lane == g, ls_all, 0.0), axis=1, keepdims=True)
    wrap_n = lambda c: _mod_pow2(c, n)
    wrap_p = lambda c: _mod_pow2(c, p)

    twice = lambda row: _place(jnp.broadcast_to(row, (p, n)), n, 2 * n, wrap_n)
    a2r, a2i, _, _ = _discretise(twice(lr_row), twice(li_row), ls)
    lane2 = lax.broadcasted_iota(jnp.int32, (p, 2 * n), 1)
    first = lane2 < n
    c2r = _place(c_re_ref[...], n, 2 * n, wrap_n)
    c2i = _place(c_im_ref[...], n, 2 * n, wrap_n)
    pr = jnp.ones_like(a2r)
    pi = jnp.zeros_like(a2r)
    ccat = []
    for _ in range(sub + 1):
        ccat.append(c2r * jnp.where(first, pr, -pi) + c2i * jnp.where(first, -pi, -pr))
        pr, pi = _cmul(a2r, a2i, pr, pi)
    qm_ref[...] = jnp.concatenate(ccat[1:], axis=0).astype(BF16)
    rcat = jnp.concatenate(ccat[:sub], axis=0)

    acr, aci, fr, fi = _discretise(_to_column(lr_row), _to_column(li_row), ls)
    btr = _place(b_re_ref[...], p, width, wrap_p)
    bti = _place(b_im_ref[...], p, width, wrap_p)
    bbr, bbi = _cmul(fr, fi, btr, bti)
    pows = [(jnp.ones_like(acr), jnp.zeros_like(acr))]
    for _ in range(sub):
        pows.append(_cmul(acr, aci, *pows[-1]))
    lane_blk = _div_pow2(lax.broadcasted_iota(jnp.int32, (n, width), 1), p)
    apr = jnp.zeros((n, width), F32)
    api = jnp.zeros((n, width), F32)
    for k in range(sub):
        sel = lane_blk == k
        apr = jnp.where(sel, pows[sub - 1 - k][0], apr)
        api = jnp.where(sel, pows[sub - 1 - k][1], api)
    pmr, pmi = _cmul(apr, api, bbr, bbi)
    wk_ref[width:width + n, :] = pmr.astype(BF16)
    wk_ref[width + n:, :] = pmi.astype(BF16)

    bbcat = jnp.concatenate([bbr, bbi], axis=0)
    kt = jnp.dot(rcat, bbcat, preferred_element_type=F32, precision=HIGHEST)
    row = lax.broadcasted_iota(jnp.int32, (width, width), 0)
    lane = lax.broadcasted_iota(jnp.int32, (width, width), 1)
    d_col = _to_column(d_ref[pl.ds(g, 1), :])
    row_p = lax.broadcasted_iota(jnp.int32, (p, width), 0)
    lane_p = lax.broadcasted_iota(jnp.int32, (p, width), 1)
    d_diag = jnp.where(row_p == _mod_pow2(lane_p, p), d_col, 0.0)
    kt = jnp.concatenate([kt[:p] + d_diag, kt[p:]], axis=0)
    col_blk = _div_pow2(lane, p)
    m16 = jnp.zeros((width, width), F32)
    for k in range(sub):
        if k == 0:
            shifted = kt
        else:
            shifted = jnp.concatenate(
                [jnp.zeros((k * p, width), F32), kt[:width - k * p]], axis=0)
        m16 = jnp.where(col_blk == k, shifted, m16)
    wk_ref[:width, :] = m16.astype(BF16)

    tabr_ref[...] = jnp.broadcast_to(pows[sub][0], (n, LANES))
    tabi_ref[...] = jnp.broadcast_to(pows[sub][1], (n, LANES))


def _s5_prep_call(lam_re, lam_im, log_step, b_re, b_im, c_re, c_im, d, cast_riders=()):
    g, n, p, sub = S5_GROUPS, S5_STATE, S5_GROUP, SUB
    width = sub * p
    grp = lambda shape: pl.BlockSpec((PREP_GROUPS_PER_STEP,) + shape, lambda i: (i, 0, 0))
    whole = lambda a: pl.BlockSpec(a.shape, lambda i: (0,) * a.ndim)
    n_steps = g // PREP_GROUPS_PER_STEP
    rider_specs, rider_shapes = _cast_rider_specs(cast_riders, n_steps)
    return pl.pallas_call(
        functools.partial(_s5_prep_kernel, n_cast=len(cast_riders)),
        out_shape=(
            jax.ShapeDtypeStruct((g, width + 2 * n, width), BF16),
            jax.ShapeDtypeStruct((g, width, 2 * n), BF16),
            jax.ShapeDtypeStruct((g, n, LANES), F32),
            jax.ShapeDtypeStruct((g, n, LANES), F32),
            *rider_shapes,
        ),
        grid=(n_steps,),
        in_specs=[whole(lam_re), whole(lam_im), whole(log_step), whole(d)]
        + [grp((n, p))] * 2 + [grp((p, n))] * 2 + rider_specs,
        out_specs=(grp((width + 2 * n, width)), grp((width, 2 * n)),
                   grp((n, LANES)), grp((n, LANES)), *rider_specs),
        compiler_params=_cparams(1),
        name="s5_prep",
    )(lam_re, lam_im, log_step, d, b_re, b_im, c_re, c_im, *cast_riders)


def _inproj_kernel(x_ref, sh_ref, sc_ref, g1_ref, w_ref, zt_ref, zuv_ref, hs_ref,
                   *, n_ph, rows, lane_rows):
    m = rows * n_ph
    gain = (g1_ref[...] * (1.0 + sc_ref[...]))[:, None, :]
    shift = sh_ref[...][:, None, :]
    x3 = x_ref[...]
    h = (x3 * _rms_scale(x3) * gain + shift).reshape(m, D_MODEL)
    zuv = jnp.dot(h.astype(BF16), w_ref[:, S5_WIDTH:], preferred_element_type=F32)
    zuv_ref[...] = zuv.reshape(rows, n_ph, 2 * GM_WIDTH)

    n_lb = D_MODEL // LANES
    for lb in range(n_lb):
        hs_ref[lb] = h[:, lb * LANES:(lb + 1) * LANES]
    hp = jnp.concatenate(
        [jnp.concatenate([hs_ref[lb, pl.ds(ph, rows, stride=n_ph), :] for lb in range(n_lb)], axis=1)
         for ph in range(n_ph)], axis=0).astype(BF16)
    z5 = jnp.dot(hp, w_ref[:, :S5_WIDTH], preferred_element_type=F32)
    for ph in range(n_ph):
        zz = z5[ph * rows:(ph + 1) * rows]
        if lane_rows > rows:
            zz = jnp.concatenate([zz, jnp.zeros((lane_rows - rows, S5_WIDTH), F32)], axis=0)
        zt = zz.T.reshape(S5_GROUPS, S5_GROUP, lane_rows)
        zt_ref[:, ph, :, :] = zt.astype(BF16)


def _phase_blocks(n_rows, row_blk):
    if row_blk % LANES == 0:
        assert n_rows % row_blk == 0
        return n_rows // row_blk, row_blk, n_rows
    assert row_blk == n_rows
    lane_rows = -(-n_rows // LANES) * LANES
    return 1, lane_rows, lane_rows


def _inproj_call(x3, shift1, scale1, norm1_g, w_in_b, *, n_ph, row_blk):
    rows, t, _ = x3.shape
    mrows = shift1.shape[0]
    assert mrows in (1, row_blk)
    n_rb, lane_blk, lane_rows = _phase_blocks(rows, row_blk)
    kern = functools.partial(_inproj_kernel, n_ph=n_ph, rows=row_blk, lane_rows=lane_blk)
    return pl.pallas_call(
        kern,
        out_shape=(
            jax.ShapeDtypeStruct((S5_GROUPS, t, S5_GROUP, lane_rows), BF16),
            jax.ShapeDtypeStruct((rows, t, 2 * GM_WIDTH), F32),
        ),
        grid=(n_rb, t // n_ph),
        in_specs=[
            pl.BlockSpec((row_blk, n_ph, D_MODEL), lambda i, j: (i, j, 0)),
            _const_spec((mrows, D_MODEL)),
            _const_spec((mrows, D_MODEL)),
            _const_spec((1, D_MODEL)),
            _const_spec((D_MODEL, IN_WIDTH)),
        ],
        out_specs=(
            pl.BlockSpec((S5_GROUPS, n_ph, S5_GROUP, lane_blk), lambda i, j: (0, j, 0, i)),
            pl.BlockSpec((row_blk, n_ph, 2 * GM_WIDTH), lambda i, j: (i, j, 0)),
        ),
        scratch_shapes=[pltpu.VMEM((D_MODEL // LANES, row_blk * n_ph, LANES), F32)],
        compiler_params=_cparams(2),
        name="inproj",
    )(x3, shift1, scale1, norm1_g, w_in_b)


def _s5_kernel(*refs, n_sub, lanes, scan, n_cast):
    refs, riders = _split_riders(refs, n_in=5 if scan else 6, n_out=2, n_cast=n_cast)
    if scan:
        zt_ref, wk_ref, qm_ref, tabr_ref, tabi_ref, yt_ref, sf_ref, ybuf, lbuf = refs
        s0_ref = None
    else:
        zt_ref, wk_ref, qm_ref, tabr_ref, tabi_ref, s0_ref, yt_ref, sf_ref, ybuf, lbuf = refs
    groups = range(zt_ref.shape[0])
    n = S5_STATE
    width = SUB * S5_GROUP

    widen = lambda tile: jnp.concatenate([tile] * (lanes // LANES), axis=1)
    sub_pows = []
    for g in groups:
        a1 = (widen(tabr_ref[g]), widen(tabi_ref[g]))
        pows = [a1]
        for _ in range(n_sub - 1):
            pows.append(_cmul(*a1, *pows[-1]))
        sub_pows.append(pows)

    local = []
    for g in groups:
        wk = wk_ref[g]
        lr = li = None
        for j in range(n_sub):
            u = zt_ref[g, j * width:(j + 1) * width, :]
            r = jnp.dot(wk, u, preferred_element_type=F32)
            ybuf[g, j * width:(j + 1) * width, :] = r[:width]
            wr = r[width:width + n]
            wi = r[width + n:]
            if j == 0:
                lr, li = wr, wi
            else:
                tr, ti = _cmul(*sub_pows[g][0], lr, li)
                lr, li = tr + wr, ti + wi
            lbuf[g, j, :n, :] = lr
            lbuf[g, j, n:, :] = li
        local.append((lr, li))

    if scan:
        lane = lax.broadcasted_iota(jnp.int32, (n, lanes), 1)
        xs = list(local)
        ms = [sub_pows[g][n_sub - 1] for g in groups]
        for i in range(int(math.log2(lanes))):
            sh = 1 << i
            for g in groups:
                xr, xi = xs[g]
                rr = jnp.where(lane >= sh, pltpu.roll(xr, sh, 1), 0.0)
                ri = jnp.where(lane >= sh, pltpu.roll(xi, sh, 1), 0.0)
                tr, ti = _cmul(*ms[g], rr, ri)
                xs[g] = (xr + tr, xi + ti)
                ms[g] = _cmul(*ms[g], *ms[g])
        entering = [(jnp.where(lane >= 1, pltpu.roll(xr, 1, 1), 0.0),
                     jnp.where(lane >= 1, pltpu.roll(xi, 1, 1), 0.0)) for xr, xi in xs]
    else:
        entering = [(s0_ref[g, :n, :], s0_ref[g, n:, :]) for g in groups]

    keep = slice(lanes - LANES, lanes)
    for g in groups:
        qm = qm_ref[g]
        sr, si = entering[g]
        for j in range(n_sub):
            if j == 0:
                pr, pi = sr, si
            else:
                tr, ti = _cmul(*sub_pows[g][j - 1], sr, si)
                pr, pi = lbuf[g, j - 1, :n, :] + tr, lbuf[g, j - 1, n:, :] + ti
            sp = jnp.concatenate([pr, pi], axis=0).astype(BF16)
            y = ybuf[g, j * width:(j + 1) * width, :] + jnp.dot(qm, sp, preferred_element_type=F32)
            yt_ref[g, j * width:(j + 1) * width, :] = y.astype(BF16)
        tr, ti = _cmul(*sub_pows[g][n_sub - 1], sr, si)
        sf_ref[g, :n, :] = lbuf[g, n_sub - 1, :n, keep] + tr[:, keep]
        sf_ref[g, n:, :] = lbuf[g, n_sub - 1, n:, keep] + ti[:, keep]
    _run_riders(riders)


def _s5_call(zt, wk, qm, tabr, tabi, s0, *, n_sub, scan, cast_riders=()):
    g, rows, lanes = zt.shape
    n = S5_STATE
    width = SUB * S5_GROUP
    assert lanes % LANES == 0 and lanes & (lanes - 1) == 0
    gps = S5_GROUPS_PER_STEP_LONG if scan else S5_GROUPS_PER_STEP_SHORT
    grp = lambda shape: pl.BlockSpec((gps,) + shape, lambda i: (i, 0, 0))
    in_specs = [grp((rows, lanes)), grp((width + 2 * n, width)), grp((width, 2 * n)),
                grp((n, LANES)), grp((n, LANES))]
    args = [zt, wk, qm, tabr, tabi]
    if not scan:
        in_specs.append(grp((2 * n, lanes)))
        args.append(s0)
    rider_specs, rider_shapes = _cast_rider_specs(cast_riders, g // gps)
    kern = functools.partial(_s5_kernel, n_sub=n_sub, lanes=lanes, scan=scan,
                             n_cast=len(cast_riders))
    return pl.pallas_call(
        kern,
        out_shape=(jax.ShapeDtypeStruct((g, rows, lanes), BF16),
                   jax.ShapeDtypeStruct((g, 2 * n, LANES), F32), *rider_shapes),
        grid=(g // gps,),
        in_specs=in_specs + rider_specs,
        out_specs=(grp((rows, lanes)), grp((2 * n, LANES)), *rider_specs),
        scratch_shapes=[pltpu.VMEM((gps, rows, lanes), F32),
                        pltpu.VMEM((gps, n_sub, 2 * n, lanes), F32)],
        compiler_params=_cparams(1),
        name="s5",
    )(*args, *cast_riders)


def _glu_kernel(yt_ref, w_ref, b_ref, m_ref, *, n_ph, rows, lane_rows):
    gs = []
    for ph in range(n_ph):
        yt = yt_ref[:, ph, :, :].astype(F32).reshape(S5_WIDTH, lane_rows)
        gs.append(_gelu(yt.T[:rows]))
    gy = jnp.concatenate(gs, axis=0)
    gate = jnp.dot(gy.astype(BF16), w_ref[...], preferred_element_type=F32) + b_ref[...]
    m = gy * _sigmoid(gate)
    for ph in range(n_ph):
        _oct_store(m_ref, ph, m[ph * rows:(ph + 1) * rows])


def _glu_call(yt4, w_glu_b, b_glu, *, rows, n_ph, row_blk):
    g, t, p, lane_rows = yt4.shape
    n_rb, lane_blk, lane_rows_expected = _phase_blocks(rows, row_blk)
    assert lane_rows == lane_rows_expected
    kern = functools.partial(_glu_kernel, n_ph=n_ph, rows=row_blk, lane_rows=lane_blk)
    oct_shape = _oct_shape(rows, t, S5_WIDTH)
    return pl.pallas_call(
        kern,
        out_shape=jax.ShapeDtypeStruct(oct_shape, F32),
        grid=(n_rb, t // n_ph),
        in_specs=[
            pl.BlockSpec((g, n_ph, p, lane_blk), lambda i, j: (0, j, 0, i)),
            _const_spec((S5_WIDTH, S5_WIDTH)),
            _const_spec((1, S5_WIDTH)),
        ],
        out_specs=pl.BlockSpec((row_blk // OCT, oct_shape[1], n_ph, OCT, LANES),
                               lambda i, j: (i, 0, j, 0, 0)),
        compiler_params=_cparams(2),
        name="glu",
    )(yt4, w_glu_b, b_glu)


def _ff_chunks(n_fc):
    tiles = D_FF // MXU_DIM
    assert tiles * MXU_DIM == D_FF
    bounds = [MXU_DIM * ((tiles * c + n_fc - 1) // n_fc) for c in range(n_fc + 1)]
    return list(zip(bounds[:-1], bounds[1:]))


def _main_kernel(*refs, tm, n_parts, **static):
    rows = tm // n_parts
    for part in range(n_parts):
        _main_rows(*refs, r0=part * rows, rows=rows, tm=tm, **static)


def _main_rows(x_ref, m_ref, zuv_ref, mod_ref, g2_ref, gf_ref, lng_ref, lnb_ref, gw_ref, gbt_ref,
               wo_ref, wgu_ref, wd_ref, y_ref, *rest, r0, rows, tm, t, cl, seq, n_fc):
    *maybe_v_out_ref, v_ref, ygm_ref, attn_ref = rest
    hd = GM_HEAD_DIM
    rr = slice(r0, r0 + rows)
    assert rows % t == 0 and rows % cl == 0

    m_rows = _oct_load(m_ref, t, S5_WIDTH // LANES, r0 // t, rows // t)
    attn_ref[rr, :] = jnp.dot(m_rows.astype(BF16), wo_ref[:S5_WIDTH, :], preferred_element_type=F32)

    gv = _gelu(zuv_ref[rr, GM_WIDTH:])
    cen = gv - jnp.mean(gv, axis=-1, keepdims=True)
    var = jnp.mean(cen * cen, axis=-1, keepdims=True)
    v = cen * lax.rsqrt(var + EPS) * lng_ref[...] + lnb_ref[...]
    for v_out_ref in maybe_v_out_ref:
        v_out_ref[rr, :] = v
    v_ref[rr, :] = v.astype(BF16)

    blk_i = _div_pow2(lax.broadcasted_iota(jnp.int32, (cl, cl), 0), CHUNK)
    blk_j = _div_pow2(lax.broadcasted_iota(jnp.int32, (cl, cl), 1), CHUNK)
    causal = blk_j <= blk_i
    first_head = lax.broadcasted_iota(jnp.int32, (cl, 2 * hd), 1) < hd
    for pr in range(GM_HEADS // 2):
        h0, h1 = 2 * pr, 2 * pr + 1
        wm = jnp.concatenate(
            [jnp.where(causal, gw_ref[h, :cl, :cl], 0.0) for h in (h0, h1)], axis=1).astype(BF16)
        bias = jnp.where(first_head, gbt_ref[:cl, h0:h0 + 1], gbt_ref[:cl, h1:h1 + 1])
        cs = slice(h0 * hd, (h1 + 1) * hd)
        for ci in range(rows // cl):
            rs = slice(r0 + ci * cl, r0 + (ci + 1) * cl)
            vv = v_ref[rs, cs]
            zero = jnp.zeros_like(vv)
            rhs = jnp.concatenate([jnp.where(first_head, vv, zero), jnp.where(first_head, zero, vv)],
                                  axis=0)
            mixed = jnp.dot(wm, rhs, preferred_element_type=F32) + bias
            ygm_ref[rs, cs] = (_gelu(zuv_ref[rs, cs]) * mixed).astype(BF16)

    attn = attn_ref[rr, :] + jnp.dot(ygm_ref[rr, :], wo_ref[S5_WIDTH:, :], preferred_element_type=F32)

    def mod_rows(idx):
        if mod_ref.shape[0] == 1:
            return mod_ref[0, idx:idx + 1, :]
        assert rows % seq == 0
        return jnp.concatenate(
            [jnp.broadcast_to(mod_ref[s, idx:idx + 1, :], (seq, D_MODEL))
             for s in range(r0 // seq, (r0 + rows) // seq)], axis=0)

    gate1, shift2, scale2, gate2 = mod_rows(2), mod_rows(3), mod_rows(4), mod_rows(5)
    x1 = x_ref[rr, :] + gate1 * attn
    h2 = (x1 * _rms_scale(x1) * (g2_ref[...] * (1.0 + scale2)) + shift2).astype(BF16)

    acc = None
    for lo, hi in _ff_chunks(n_fc):
        gg = jnp.dot(h2, wgu_ref[:, lo:hi], preferred_element_type=F32)
        up = jnp.dot(h2, wgu_ref[:, D_FF + lo:D_FF + hi], preferred_element_type=F32)
        act = (gg * jax.nn.sigmoid(gg) * up).astype(BF16)
        part = jnp.dot(act, wd_ref[lo:hi, :], preferred_element_type=F32)
        acc = part if acc is None else acc + part
    x2 = x1 + gate2 * acc
    y_ref[rr, :] = x2 * _rms_scale(x2) * gf_ref[...]


def _main_call(x2d, m, zuv, mod, norm2_g, final_g, ln_g, ln_b, gm_w, gm_bt, w_out_b, w_gu_b, w_down_b,
               *, seq, n_fc, n_parts, want_v):
    n_tok = x2d.shape[0]
    n_oct, n_lb, t, _, _ = m.shape
    assert n_oct % n_parts == 0 and n_oct * OCT * t == n_tok
    tm = n_parts * OCT * t
    streams_per_tile = max(1, tm // seq)
    tiles_per_stream = max(1, seq // tm)
    assert streams_per_tile * seq == tm or tiles_per_stream * tm == seq
    cl = min(GM_CHUNK, seq)
    kern = functools.partial(_main_kernel, tm=tm, t=t, cl=cl, seq=seq, n_fc=n_fc, n_parts=n_parts)
    tok = lambda width: pl.BlockSpec((tm, width), lambda i: (i, 0))
    oct_rows = n_lb * t * OCT
    octs = pl.BlockSpec((n_parts * oct_rows, LANES), lambda i: (i, 0))
    m = m.reshape(n_oct * oct_rows, LANES)
    out_shape = [jax.ShapeDtypeStruct((n_tok, D_MODEL), F32)]
    out_specs = [tok(D_MODEL)]
    if want_v:
        out_shape.append(jax.ShapeDtypeStruct((n_tok, GM_WIDTH), F32))
        out_specs.append(tok(GM_WIDTH))
    return pl.pallas_call(
        kern,
        out_shape=tuple(out_shape),
        grid=(n_oct // n_parts,),
        in_specs=[
            tok(D_MODEL), octs, tok(2 * GM_WIDTH),
            pl.BlockSpec((streams_per_tile, N_MOD, D_MODEL), lambda i: (i // tiles_per_stream, 0, 0)),
            _const_spec((1, D_MODEL)),
            _const_spec((1, D_MODEL)),
            _const_spec((1, GM_WIDTH)),
            _const_spec((1, GM_WIDTH)),
            _const_spec((GM_HEADS, GM_CHUNK, GM_CHUNK)),
            _const_spec((GM_CHUNK, GM_HEADS)),
            _const_spec((D_MODEL, D_MODEL)),
            _const_spec((D_MODEL, 2 * D_FF)),
            _const_spec((D_FF, D_MODEL)),
        ],
        out_specs=tuple(out_specs),
        scratch_shapes=[pltpu.VMEM((tm, GM_WIDTH), BF16), pltpu.VMEM((tm, GM_WIDTH), BF16),
                        pltpu.VMEM((tm, D_MODEL), F32)],
        compiler_params=_cparams(1),
        name="main",
    )(x2d, m, zuv, mod, norm2_g, final_g, ln_g, ln_b, gm_w, gm_bt, w_out_b, w_gu_b, w_down_b)


def _trunk(x, mod, s0, prm, *, t, n_ph, row_blk, n_fc, n_parts, want_v, f32_weights=None):
    b, seq, _ = x.shape
    n_chunks = seq // t
    rows = b * n_chunks
    scan = s0 is None
    assert (b == 1) if scan else (n_chunks == 1)
    n_sub = t // SUB

    zt4, zuv = _inproj_call(
        x.reshape(rows, t, D_MODEL), mod[:, 0], mod[:, 1], prm["norm1_g"], prm["w_in"],
        n_ph=n_ph, row_blk=row_blk)
    lane_rows = zt4.shape[-1]

    if scan:
        s0_l = None
    else:
        s0_l = jnp.transpose(s0, (1, 2, 0))
        s0_l = jnp.pad(s0_l, ((0, 0), (0, 0), (0, lane_rows - rows)))
    pending = dict(f32_weights or {})
    yt, sfin, *cast = _s5_call(zt4.reshape(S5_GROUPS, t * S5_GROUP, lane_rows),
                               prm["wk"], prm["qm"], prm["tabr"], prm["tabi"], s0_l,
                               n_sub=n_sub, scan=scan, cast_riders=tuple(pending.values()))
    prm = {**prm, **dict(zip(pending.keys(), cast))}
    m = _glu_call(yt.reshape(S5_GROUPS, t, S5_GROUP, lane_rows), prm["w_glu"], prm["b_glu"],
                  rows=rows, n_ph=n_ph, row_blk=row_blk)

    n_tok = b * seq
    y, *maybe_v = _main_call(
        x.reshape(n_tok, D_MODEL), m, zuv.reshape(n_tok, 2 * GM_WIDTH),
        mod, prm["norm2_g"], prm["final_g"], prm["ln_g"], prm["ln_b"],
        prm["gm_w"], prm["gm_bt"], prm["w_out"], prm["w_gu"], prm["w_down"],
        seq=seq, n_fc=n_fc, n_parts=n_parts, want_v=want_v)

    first_kept = lane_rows - LANES
    if scan:
        fin = sfin[:, :, rows - 1 - first_kept][None]
    else:
        assert first_kept == 0
        fin = jnp.transpose(sfin[:, :, :rows], (2, 0, 1))
    v = maybe_v[0].reshape(b, seq, GM_WIDTH) if want_v else None
    return y.reshape(b, seq, D_MODEL), fin[..., :S5_STATE], fin[..., S5_STATE:], v, prm


def kernel(x_prompt, x_sample, state_s5_re, state_s5_im, c_prompt, c_sample, norm1_g, norm2_g, w_ada, b_ada, w_in, s5_lambda_re, s5_lambda_im, s5_log_step, s5_b_re, s5_b_im, s5_c_re, s5_c_im, s5_d, s5_w_glu, s5_b_glu, gm_ln_g, gm_ln_b, gm_w_s, gm_b_s, w_out, ffn_w_gu, ffn_w_down, final_g):
    depth = w_in.shape[0]
    assert depth == 1
    l = 0
    n_p = c_prompt.shape[0]
    n_s = c_sample.shape[0]

    c_all = jnp.concatenate([c_prompt, c_sample], axis=0)
    c_pad = jnp.pad(c_all, ((0, -c_all.shape[0] % 8), (0, 0)))
    mod_all = _ada_call(c_pad, w_ada[l], b_ada[l][None, :])
    mod_p = mod_all[:n_p].reshape(n_p, N_MOD, D_MODEL)
    mod_s = mod_all[n_p:n_p + n_s].reshape(n_s, N_MOD, D_MODEL)

    wk, qm, tabr, tabi, w_in_b = _s5_prep_call(
        s5_lambda_re[l], s5_lambda_im[l], s5_log_step[l][None, :], s5_b_re[l], s5_b_im[l],
        s5_c_re[l], s5_c_im[l], s5_d[l], cast_riders=(w_in[l],))

    prm = dict(
        norm1_g=norm1_g[l][None, :], norm2_g=norm2_g[l][None, :], final_g=final_g[None, :],
        w_in=w_in_b, ln_g=gm_ln_g[l][None, :], ln_b=gm_ln_b[l][None, :],
        wk=wk, qm=qm, tabr=tabr, tabi=tabi, b_glu=s5_b_glu[l][None, :],
        gm_w=gm_w_s[l], gm_bt=jnp.transpose(gm_b_s[l]),
    )
    later_weights = dict(w_glu=s5_w_glu[l], w_out=w_out[l], w_gu=ffn_w_gu[l], w_down=ffn_w_down[l])

    yp, pre, pim, _, prm = _trunk(x_prompt, mod_p, None, prm, f32_weights=later_weights,
                                  t=S5_LONG_SUBS * SUB, n_ph=8, row_blk=LANES, n_fc=2, n_parts=1, want_v=False)
    s0 = jnp.concatenate([state_s5_re[l], state_s5_im[l]], axis=-1)
    n_b, seq_s, _ = x_sample.shape
    ys, sre, sim, vs, _ = _trunk(x_sample, mod_s, s0, prm,
                                 t=seq_s, n_ph=seq_s, row_blk=n_b, n_fc=2, n_parts=1, want_v=True)
    return (yp, ys, pre[None], pim[None], sre[None], sim[None], vs[None])
```

```python
import functools
import math

import jax
import jax.numpy as jnp
from jax import lax
from jax.experimental import pallas as pl
from jax.experimental.pallas import tpu as pltpu

D_MODEL = 1024
S5_WIDTH = 512
S5_GROUP = 16
S5_GROUPS = 32
S5_STATE = 64
GM_WIDTH = 512
GM_CHUNK = 128
GM_HEADS = 8
GM_HEAD_DIM = 64
CHUNK = 64
IN_WIDTH = S5_WIDTH + 2 * GM_WIDTH
D_FF = 2816
N_MOD = 6
EPS = 1e-6

LANES = 128
BF16_SUBLANES = 16
MXU_DIM = 256
VMEM_LIMIT_BYTES = 56 * 1024 * 1024

SUB = MXU_DIM // S5_GROUP
S5_LONG_SUBS = 4
PREP_GROUPS_PER_STEP = 8
S5_GROUPS_PER_STEP_SHORT = 8
ADA_K_BLOCK = 128
S5_GROUPS_PER_STEP_LONG = 4

F32 = jnp.float32
BF16 = jnp.bfloat16


def _cparams(n_grid_axes):
    return pltpu.CompilerParams(
        dimension_semantics=("arbitrary",) * n_grid_axes,
        vmem_limit_bytes=VMEM_LIMIT_BYTES,
    )


def _const_spec(shape):
    nd = len(shape)
    return pl.BlockSpec(shape, lambda *_: (0,) * nd, pipeline_mode=pl.Buffered(1))


def _rms_scale(x):
    return lax.rsqrt(jnp.mean(x * x, axis=-1, keepdims=True) + EPS)


_GELU_C0 = math.sqrt(2.0 / math.pi)
_GELU_C1 = 0.044715 * _GELU_C0


def _gelu(x):
    hx = 0.5 * x
    return hx + hx * jnp.tanh(x * (_GELU_C0 + _GELU_C1 * (x * x)))


def _sigmoid(x):
    return 0.5 * jnp.tanh(0.5 * x) + 0.5


def _cmul(ar, ai, xr, xi):
    return ar * xr - ai * xi, ar * xi + ai * xr


OCT = 8


def _oct_shape(n_rows, t, width):
    assert n_rows % OCT == 0 and width % LANES == 0
    return (n_rows // OCT, width // LANES, t, OCT, LANES)


def _oct_store(ref, ph, val):
    n_oct, n_lb = ref.shape[0], ref.shape[1]
    for lb in range(n_lb):
        ref[:, lb, ph, :, :] = val[:, lb * LANES:(lb + 1) * LANES].reshape(n_oct, OCT, LANES)


def _oct_load(ref, t, n_lb, c0, n_chunks):
    chunks = []
    for c in range(c0, c0 + n_chunks):
        base = (c // OCT) * n_lb * t * OCT + c % OCT
        chunks.append(jnp.concatenate(
            [ref[pl.ds(base + lb * t * OCT, t, stride=OCT), :] for lb in range(n_lb)], axis=1))
    return jnp.concatenate(chunks, axis=0)


def _cast_rider_specs(arrays, grid):
    n_steps = math.prod(grid)

    def row_block(*idx):
        step = 0
        for i, extent in zip(idx, grid):
            step = step * extent + i
        return (step, 0)

    specs, shapes = [], []
    for a in arrays:
        rows, cols = a.shape
        blk = rows // n_steps
        assert blk * n_steps == rows and blk % BF16_SUBLANES == 0
        specs.append(pl.BlockSpec((blk, cols), row_block))
        shapes.append(jax.ShapeDtypeStruct((rows, cols), BF16))
    return specs, shapes


def _split_riders(refs, n_in, n_out, n_cast):
    ins, refs = refs[:n_in], refs[n_in:]
    cast_in, refs = refs[:n_cast], refs[n_cast:]
    outs, refs = refs[:n_out], refs[n_out:]
    cast_out, scratch = refs[:n_cast], refs[n_cast:]
    return (*ins, *outs, *scratch), list(zip(cast_in, cast_out))


def _run_riders(pairs):
    for src, dst in pairs:
        dst[...] = src[...].astype(BF16)


def _div_pow2(idx, divisor):
    shift = divisor.bit_length() - 1
    assert divisor == 1 << shift
    return lax.shift_right_logical(idx, shift)


def _mod_pow2(idx, divisor):
    assert divisor & (divisor - 1) == 0
    return lax.bitwise_and(idx, divisor - 1)


def _ada_kernel(c_ref, w_ref, b_ref, o_ref):
    @pl.when(pl.program_id(0) == 0)
    def _():
        o_ref[...] = jnp.broadcast_to(b_ref[...], o_ref.shape)

    c = c_ref[...]
    o_ref[...] += _dot_split(c * jax.nn.sigmoid(c), w_ref[...])


def _ada_call(c_pad, w_ada, b_ada):
    rows = c_pad.shape[0]
    n_in, n_out = w_ada.shape
    bk = ADA_K_BLOCK
    return pl.pallas_call(
        _ada_kernel,
        out_shape=jax.ShapeDtypeStruct((rows, n_out), F32),
        grid=(n_in // bk,),
        in_specs=[
            pl.BlockSpec((rows, bk), lambda k: (0, k)),
            pl.BlockSpec((bk, n_out), lambda k: (k, 0)),
            pl.BlockSpec((1, n_out), lambda k: (0, 0)),
        ],
        out_specs=pl.BlockSpec((rows, n_out), lambda k: (0, 0)),
        compiler_params=_cparams(1),
        name="ada",
    )(c_pad, w_ada, b_ada)


def _discretise(lr, li, ls):
    step = jnp.exp(ls)
    mag = jnp.exp(lr * step)
    ar = mag * jnp.cos(li * step)
    ai = mag * jnp.sin(li * step)
    den = lr * lr + li * li
    fr = ((ar - 1.0) * lr + ai * li) / den
    fi = (ai * lr - (ar - 1.0) * li) / den
    return ar, ai, fr, fi


def _s5_prep_kernel(*refs, n_cast):
    refs, riders = _split_riders(refs, n_in=8, n_out=4, n_cast=n_cast)
    lam_re_ref, lam_im_ref, ls_ref, d_ref, *grouped = refs
    n, sub, p = S5_STATE, SUB, S5_GROUP
    width = sub * p

    def col_source(c):
        return jnp.where(c < width, (sub - 1) - _div_pow2(c, p), jnp.where(c < 2 * width, sub + 1, sub))

    sels = dict(
        twice=_selection(n, 2 * n, lambda c: _mod_pow2(c, n)),
        tile=_selection(p, width, lambda c: _mod_pow2(c, p)),
        cols=_selection(2 * n, 2 * width + LANES, col_source),
    )
    gps = grouped[0].shape[0]
    pending = [_s5_prep_group(pl.program_id(0) * gps + gi, sels, lam_re_ref, lam_im_ref, ls_ref, d_ref,
                              *[r.at[gi] for r in grouped]) for gi in range(gps)]
    while pending:
        pending = [gen for gen in pending if next(gen, "done") != "done"]
    _run_riders(riders)


def _selection(rows, cols, row_of_col):
    r = lax.broadcasted_iota(jnp.int32, (rows, cols), 0)
    c = lax.broadcasted_iota(jnp.int32, (rows, cols), 1)
    return jnp.where(r == row_of_col(c), 1.0, 0.0).astype(BF16)


def _place(x, sel):
    hi = x.astype(BF16)
    rest = x - hi.astype(F32)
    mid = rest.astype(BF16)
    lo = (rest - mid.astype(F32)).astype(BF16)
    dot = functools.partial(jnp.dot, preferred_element_type=F32)
    return dot(hi, sel) + (dot(mid, sel) + dot(lo, sel))


def _dot_split(x, y):
    x_hi, y_hi = x.astype(BF16), y.astype(BF16)
    x_lo = (x - x_hi.astype(F32)).astype(BF16)
    y_lo = (y - y_hi.astype(F32)).astype(BF16)
    dot = functools.partial(jnp.dot, preferred_element_type=F32)
    return dot(x_hi, y_hi) + (dot(x_lo, y_hi) + dot(x_hi, y_lo))


def _to_column(row):
    k = row.shape[1]
    r = lax.broadcasted_iota(jnp.int32, (k, k), 0)
    c = lax.broadcasted_iota(jnp.int32, (k, k), 1)
    return jnp.sum(jnp.where(r == c, jnp.broadcast_to(row, (k, k)), 0.0), axis=1, keepdims=True)


def _s5_prep_group(g, sels, lam_re_ref, lam_im_ref, ls_ref, d_ref, b_re_ref, b_im_ref, c_re_ref, c_im_ref,
                   wk_ref, qm_ref, tabr_ref, tabi_ref):
    n, sub, p = S5_STATE, SUB, S5_GROUP
    width = sub * p

    lr_row = lam_re_ref[pl.ds(g, 1), :]
    li_row = lam_im_ref[pl.ds(g, 1), :]
    ls_all = ls_ref[...]
    grp_lane = lax.broadcasted_iota(jnp.int32, ls_all.shape, 1)
    ls = jnp.sum(jnp.where(grp_lane == g, ls_all, 0.0), axis=1, keepdims=True)

    ar8, ai8, fr8, fi8 = _discretise(jnp.broadcast_to(lr_row, (8, n)), jnp.broadcast_to(li_row, (8, n)), ls)
    twice = lambda t8: _place(jnp.concatenate([t8] * (p // 8), axis=0), sels["twice"])
    a2r, a2i = twice(ar8), twice(ai8)
    lane2 = lax.broadcasted_iota(jnp.int32, (p, 2 * n), 1)
    first = lane2 < n
    c2r = _place(c_re_ref[...], sels["twice"])
    c2i = _place(c_im_ref[...], sels["twice"])
    pr = jnp.ones_like(a2r)
    pi = jnp.zeros_like(a2r)
    ccat = []
    tbl_rows = 8 * (-(-(sub + 2) // 8))
    tbl_row = lax.broadcasted_iota(jnp.int32, (tbl_rows, 2 * n), 0)
    first_tbl = lax.broadcasted_iota(jnp.int32, (tbl_rows, 2 * n), 1) < n
    as_tbl_row = lambda re2, im2: jnp.concatenate([jnp.where(first, re2, im2)] * (tbl_rows // p + 1),
                                                  axis=0)[:tbl_rows]
    tbl = jnp.zeros((tbl_rows, 2 * n), F32)
    for d in range(sub + 1):
        ccat.append(c2r * jnp.where(first, pr, -pi) + c2i * jnp.where(first, -pi, -pr))
        tbl = jnp.where(tbl_row == d, as_tbl_row(pr, pi), tbl)
        pr, pi = _cmul(a2r, a2i, pr, pi)
    tbl = jnp.where(tbl_row == sub + 1, as_tbl_row(twice(fr8), twice(fi8)), tbl)
    yield
    qm_ref[...] = jnp.concatenate(ccat[1:], axis=0).astype(BF16)
    rcat = jnp.concatenate(ccat[:sub], axis=0)

    tbl_t = jnp.concatenate([tbl, jnp.zeros((2 * n - tbl_rows, 2 * n), F32)], axis=0).T
    cols = _place(tbl_t, sels["cols"])
    yield
    apr, fr, a16r = cols[:n, :width], cols[:n, width:2 * width], cols[:n, 2 * width:]
    api, fi, a16i = cols[n:, :width], cols[n:, width:2 * width], cols[n:, 2 * width:]
    btr = _place(b_re_ref[...], sels["tile"])
    bti = _place(b_im_ref[...], sels["tile"])
    bbr, bbi = _cmul(fr, fi, btr, bti)
    pmr, pmi = _cmul(apr, api, bbr, bbi)
    wk_ref[width:width + n, :] = pmr.astype(BF16)
    wk_ref[width + n:, :] = pmi.astype(BF16)
    yield

    bbcat = jnp.concatenate([bbr, bbi], axis=0)
    kt = _dot_split(rcat, bbcat)
    yield
    row = lax.broadcasted_iota(jnp.int32, (width, width), 0)
    lane = lax.broadcasted_iota(jnp.int32, (width, width), 1)
    d_col = _to_column(d_ref[pl.ds(g, 1), :])
    row_p = lax.broadcasted_iota(jnp.int32, (p, width), 0)
    lane_p = lax.broadcasted_iota(jnp.int32, (p, width), 1)
    d_diag = jnp.where(row_p == _mod_pow2(lane_p, p), d_col, 0.0)
    kt = jnp.concatenate([kt[:p] + d_diag, kt[p:]], axis=0)
    col_blk = _div_pow2(lane, p)
    m16 = jnp.zeros((width, width), F32)
    for k in range(sub):
        if k == 0:
            shifted = kt
        else:
            shifted = jnp.concatenate(
                [jnp.zeros((k * p, width), F32), kt[:width - k * p]], axis=0)
        m16 = jnp.where(col_blk == k, shifted, m16)
    wk_ref[:width, :] = m16.astype(BF16)

    tabr_ref[...] = a16r
    tabi_ref[...] = a16i


def _s5_prep_call(lam_re, lam_im, log_step, b_re, b_im, c_re, c_im, d, cast_riders=()):
    g, n, p, sub = S5_GROUPS, S5_STATE, S5_GROUP, SUB
    width = sub * p
    grp = lambda shape: pl.BlockSpec((PREP_GROUPS_PER_STEP,) + shape, lambda i: (i, 0, 0))
    whole = lambda a: pl.BlockSpec(a.shape, lambda i: (0,) * a.ndim)
    n_steps = g // PREP_GROUPS_PER_STEP
    rider_specs, rider_shapes = _cast_rider_specs(cast_riders, (n_steps,))
    return pl.pallas_call(
        functools.partial(_s5_prep_kernel, n_cast=len(cast_riders)),
        out_shape=(
            jax.ShapeDtypeStruct((g, width + 2 * n, width), BF16),
            jax.ShapeDtypeStruct((g, width, 2 * n), BF16),
            jax.ShapeDtypeStruct((g, n, LANES), F32),
            jax.ShapeDtypeStruct((g, n, LANES), F32),
            *rider_shapes,
        ),
        grid=(n_steps,),
        in_specs=[whole(lam_re), whole(lam_im), whole(log_step), whole(d)]
        + [grp((n, p))] * 2 + [grp((p, n))] * 2 + rider_specs,
        out_specs=(grp((width + 2 * n, width)), grp((width, 2 * n)),
                   grp((n, LANES)), grp((n, LANES)), *rider_specs),
        compiler_params=_cparams(1),
        name="s5_prep",
    )(lam_re, lam_im, log_step, d, b_re, b_im, c_re, c_im, *cast_riders)


def _inproj_kernel(x_ref, sh_ref, sc_ref, g1_ref, w_ref, zt_ref, zuv_ref, hs_ref,
                   *, n_ph, rows, lane_rows):
    m = rows * n_ph
    gain = (g1_ref[...] * (1.0 + sc_ref[...]))[:, None, :]
    shift = sh_ref[...][:, None, :]
    x3 = x_ref[...]
    h = (x3 * _rms_scale(x3) * gain + shift).reshape(m, D_MODEL)
    zuv = jnp.dot(h.astype(BF16), w_ref[:, S5_WIDTH:], preferred_element_type=F32)
    zuv_ref[...] = zuv.reshape(rows, n_ph, 2 * GM_WIDTH)

    n_lb = D_MODEL // LANES
    for lb in range(n_lb):
        hs_ref[lb] = h[:, lb * LANES:(lb + 1) * LANES]
    hp = jnp.concatenate(
        [jnp.concatenate([hs_ref[lb, pl.ds(ph, rows, stride=n_ph), :] for lb in range(n_lb)], axis=1)
         for ph in range(n_ph)], axis=0).astype(BF16)
    z5 = jnp.dot(hp, w_ref[:, :S5_WIDTH], preferred_element_type=F32)
    for ph in range(n_ph):
        zz = z5[ph * rows:(ph + 1) * rows]
        if lane_rows > rows:
            zz = jnp.concatenate([zz, jnp.zeros((lane_rows - rows, S5_WIDTH), F32)], axis=0)
        zt = zz.T.reshape(S5_GROUPS, S5_GROUP, lane_rows)
        zt_ref[:, ph, :, :] = zt.astype(BF16)


def _phase_blocks(n_rows, row_blk):
    if row_blk % LANES == 0:
        assert n_rows % row_blk == 0
        return n_rows // row_blk, row_blk, n_rows
    assert row_blk == n_rows
    lane_rows = -(-n_rows // LANES) * LANES
    return 1, lane_rows, lane_rows


def _inproj_call(x3, shift1, scale1, norm1_g, w_in_b, *, n_ph, row_blk):
    rows, t, _ = x3.shape
    mrows = shift1.shape[0]
    assert mrows in (1, row_blk)
    n_rb, lane_blk, lane_rows = _phase_blocks(rows, row_blk)
    kern = functools.partial(_inproj_kernel, n_ph=n_ph, rows=row_blk, lane_rows=lane_blk)
    return pl.pallas_call(
        kern,
        out_shape=(
            jax.ShapeDtypeStruct((S5_GROUPS, t, S5_GROUP, lane_rows), BF16),
            jax.ShapeDtypeStruct((rows, t, 2 * GM_WIDTH), F32),
        ),
        grid=(n_rb, t // n_ph),
        in_specs=[
            pl.BlockSpec((row_blk, n_ph, D_MODEL), lambda i, j: (i, j, 0)),
            _const_spec((mrows, D_MODEL)),
            _const_spec((mrows, D_MODEL)),
            _const_spec((1, D_MODEL)),
            _const_spec((D_MODEL, IN_WIDTH)),
        ],
        out_specs=(
            pl.BlockSpec((S5_GROUPS, n_ph, S5_GROUP, lane_blk), lambda i, j: (0, j, 0, i)),
            pl.BlockSpec((row_blk, n_ph, 2 * GM_WIDTH), lambda i, j: (i, j, 0)),
        ),
        scratch_shapes=[pltpu.VMEM((D_MODEL // LANES, row_blk * n_ph, LANES), F32)],
        compiler_params=_cparams(2),
        name="inproj",
    )(x3, shift1, scale1, norm1_g, w_in_b)


def _s5_kernel(*refs, n_sub, lanes, scan, n_cast):
    refs, riders = _split_riders(refs, n_in=5 if scan else 6, n_out=2, n_cast=n_cast)
    if scan:
        zt_ref, wk_ref, qm_ref, tabr_ref, tabi_ref, yt_ref, sf_ref, ybuf, lbuf = refs
        s0_ref = None
    else:
        zt_ref, wk_ref, qm_ref, tabr_ref, tabi_ref, s0_ref, yt_ref, sf_ref, ybuf, lbuf = refs
    groups = range(zt_ref.shape[0])
    n = S5_STATE
    width = SUB * S5_GROUP

    widen = lambda tile: jnp.concatenate([tile] * (lanes // LANES), axis=1)
    sub_pows = []
    for g in groups:
        a1 = (widen(tabr_ref[g]), widen(tabi_ref[g]))
        pows = [a1]
        for _ in range(n_sub - 1):
            pows.append(_cmul(*a1, *pows[-1]))
        sub_pows.append(pows)

    local = []
    for g in groups:
        wk = wk_ref[g]
        lr = li = None
        for j in range(n_sub):
            u = zt_ref[g, j * width:(j + 1) * width, :]
            r = jnp.dot(wk, u, preferred_element_type=F32)
            ybuf[g, j * width:(j + 1) * width, :] = r[:width]
            wr = r[width:width + n]
            wi = r[width + n:]
            if j == 0:
                lr, li = wr, wi
            else:
                tr, ti = _cmul(*sub_pows[g][0], lr, li)
                lr, li = tr + wr, ti + wi
            lbuf[g, j, :n, :] = lr
            lbuf[g, j, n:, :] = li
        local.append((lr, li))

    if scan:
        lane = lax.broadcasted_iota(jnp.int32, (n, lanes), 1)
        xs = list(local)
        ms = [sub_pows[g][n_sub - 1] for g in groups]
        for i in range(int(math.log2(lanes))):
            sh = 1 << i
            for g in groups:
                xr, xi = xs[g]
                rr = jnp.where(lane >= sh, pltpu.roll(xr, sh, 1), 0.0)
                ri = jnp.where(lane >= sh, pltpu.roll(xi, sh, 1), 0.0)
                tr, ti = _cmul(*ms[g], rr, ri)
                xs[g] = (xr + tr, xi + ti)
                ms[g] = _cmul(*ms[g], *ms[g])
        entering = [(jnp.where(lane >= 1, pltpu.roll(xr, 1, 1), 0.0),
                     jnp.where(lane >= 1, pltpu.roll(xi, 1, 1), 0.0)) for xr, xi in xs]
    else:
        entering = [(s0_ref[g, :n, :], s0_ref[g, n:, :]) for g in groups]

    keep = slice(lanes - LANES, lanes)
    for g in groups:
        qm = qm_ref[g]
        sr, si = entering[g]
        for j in range(n_sub):
            if j == 0:
                pr, pi = sr, si
            else:
                tr, ti = _cmul(*sub_pows[g][j - 1], sr, si)
                pr, pi = lbuf[g, j - 1, :n, :] + tr, lbuf[g, j - 1, n:, :] + ti
            sp = jnp.concatenate([pr, pi], axis=0).astype(BF16)
            y = ybuf[g, j * width:(j + 1) * width, :] + jnp.dot(qm, sp, preferred_element_type=F32)
            yt_ref[g, j * width:(j + 1) * width, :] = y.astype(BF16)
        tr, ti = _cmul(*sub_pows[g][n_sub - 1], sr, si)
        sf_ref[g, :n, :] = lbuf[g, n_sub - 1, :n, keep] + tr[:, keep]
        sf_ref[g, n:, :] = lbuf[g, n_sub - 1, n:, keep] + ti[:, keep]
    _run_riders(riders)


def _s5_call(zt, wk, qm, tabr, tabi, s0, *, n_sub, scan, cast_riders=()):
    g, rows, lanes = zt.shape
    n = S5_STATE
    width = SUB * S5_GROUP
    assert lanes % LANES == 0 and lanes & (lanes - 1) == 0
    gps = S5_GROUPS_PER_STEP_LONG if scan else S5_GROUPS_PER_STEP_SHORT
    grp = lambda shape: pl.BlockSpec((gps,) + shape, lambda i: (i, 0, 0))
    in_specs = [grp((rows, lanes)), grp((width + 2 * n, width)), grp((width, 2 * n)),
                grp((n, LANES)), grp((n, LANES))]
    args = [zt, wk, qm, tabr, tabi]
    if not scan:
        in_specs.append(grp((2 * n, lanes)))
        args.append(s0)
    rider_specs, rider_shapes = _cast_rider_specs(cast_riders, (g // gps,))
    kern = functools.partial(_s5_kernel, n_sub=n_sub, lanes=lanes, scan=scan,
                             n_cast=len(cast_riders))
    return pl.pallas_call(
        kern,
        out_shape=(jax.ShapeDtypeStruct((g, rows, lanes), BF16),
                   jax.ShapeDtypeStruct((g, 2 * n, LANES), F32), *rider_shapes),
        grid=(g // gps,),
        in_specs=in_specs + rider_specs,
        out_specs=(grp((rows, lanes)), grp((2 * n, LANES)), *rider_specs),
        scratch_shapes=[pltpu.VMEM((gps, rows, lanes), F32),
                        pltpu.VMEM((gps, n_sub, 2 * n, lanes), F32)],
        compiler_params=_cparams(1),
        name="s5",
    )(*args, *cast_riders)


def _glu_kernel(*refs, n_ph, rows, lane_rows, n_cast):
    (yt_ref, w_ref, b_ref, m_ref), riders = _split_riders(refs, n_in=3, n_out=1, n_cast=n_cast)
    _run_riders(riders)
    gs = []
    for ph in range(n_ph):
        yt = yt_ref[:, ph, :, :].astype(F32).reshape(S5_WIDTH, lane_rows)
        gs.append(_gelu(yt.T[:rows]))
    gy = jnp.concatenate(gs, axis=0)
    gate = jnp.dot(gy.astype(BF16), w_ref[...], preferred_element_type=F32) + b_ref[...]
    m = gy * _sigmoid(gate)
    for ph in range(n_ph):
        _oct_store(m_ref, ph, m[ph * rows:(ph + 1) * rows])


def _glu_call(yt4, w_glu_b, b_glu, *, rows, n_ph, row_blk, cast_riders=()):
    g, t, p, lane_rows = yt4.shape
    n_rb, lane_blk, lane_rows_expected = _phase_blocks(rows, row_blk)
    assert lane_rows == lane_rows_expected
    grid = (n_rb, t // n_ph)
    rider_specs, rider_shapes = _cast_rider_specs(cast_riders, grid)
    kern = functools.partial(_glu_kernel, n_ph=n_ph, rows=row_blk, lane_rows=lane_blk,
                             n_cast=len(cast_riders))
    oct_shape = _oct_shape(rows, t, S5_WIDTH)
    return pl.pallas_call(
        kern,
        out_shape=(jax.ShapeDtypeStruct(oct_shape, F32), *rider_shapes),
        grid=grid,
        in_specs=[
            pl.BlockSpec((g, n_ph, p, lane_blk), lambda i, j: (0, j, 0, i)),
            _const_spec((S5_WIDTH, S5_WIDTH)),
            _const_spec((1, S5_WIDTH)),
            *rider_specs,
        ],
        out_specs=(pl.BlockSpec((row_blk // OCT, oct_shape[1], n_ph, OCT, LANES),
                                lambda i, j: (i, 0, j, 0, 0)), *rider_specs),
        compiler_params=_cparams(2),
        name="glu",
    )(yt4, w_glu_b, b_glu, *cast_riders)


def _ff_chunks(n_fc):
    tiles = D_FF // MXU_DIM
    assert tiles * MXU_DIM == D_FF
    bounds = [MXU_DIM * ((tiles * c + n_fc - 1) // n_fc) for c in range(n_fc + 1)]
    return list(zip(bounds[:-1], bounds[1:]))


def _main_kernel(*refs, tm, n_parts, **static):
    rows = tm // n_parts
    for part in range(n_parts):
        _main_rows(*refs, r0=part * rows, rows=rows, tm=tm, **static)


def _main_rows(x_ref, m_ref, zuv_ref, mod_ref, g2_ref, gf_ref, lng_ref, lnb_ref, gw_ref, gbt_ref,
               wo_ref, wgu_ref, wd_ref, y_ref, *rest, r0, rows, tm, t, cl, seq, n_fc):
    *maybe_v_out_ref, v_ref, ygm_ref, attn_ref = rest
    hd = GM_HEAD_DIM
    rr = slice(r0, r0 + rows)
    assert rows % t == 0 and rows % cl == 0

    m_rows = _oct_load(m_ref, t, S5_WIDTH // LANES, r0 // t, rows // t)
    attn_ref[rr, :] = jnp.dot(m_rows.astype(BF16), wo_ref[:S5_WIDTH, :], preferred_element_type=F32)

    gv = _gelu(zuv_ref[rr, GM_WIDTH:])
    cen = gv - jnp.mean(gv, axis=-1, keepdims=True)
    var = jnp.mean(cen * cen, axis=-1, keepdims=True)
    v = cen * lax.rsqrt(var + EPS) * lng_ref[...] + lnb_ref[...]
    for v_out_ref in maybe_v_out_ref:
        v_out_ref[rr, :] = v
    v_ref[rr, :] = v.astype(BF16)

    blk_i = _div_pow2(lax.broadcasted_iota(jnp.int32, (cl, cl), 0), CHUNK)
    blk_j = _div_pow2(lax.broadcasted_iota(jnp.int32, (cl, cl), 1), CHUNK)
    causal = blk_j <= blk_i
    first_head = lax.broadcasted_iota(jnp.int32, (cl, 2 * hd), 1) < hd
    for pr in range(GM_HEADS // 2):
        h0, h1 = 2 * pr, 2 * pr + 1
        wm = jnp.concatenate(
            [jnp.where(causal, gw_ref[h, :cl, :cl], 0.0) for h in (h0, h1)], axis=1).astype(BF16)
        bias = jnp.where(first_head, gbt_ref[:cl, h0:h0 + 1], gbt_ref[:cl, h1:h1 + 1])
        cs = slice(h0 * hd, (h1 + 1) * hd)
        for ci in range(rows // cl):
            rs = slice(r0 + ci * cl, r0 + (ci + 1) * cl)
            vv = v_ref[rs, cs]
            zero = jnp.zeros_like(vv)
            rhs = jnp.concatenate([jnp.where(first_head, vv, zero), jnp.where(first_head, zero, vv)],
                                  axis=0)
            mixed = jnp.dot(wm, rhs, preferred_element_type=F32) + bias
            ygm_ref[rs, cs] = (_gelu(zuv_ref[rs, cs]) * mixed).astype(BF16)

    attn = attn_ref[rr, :] + jnp.dot(ygm_ref[rr, :], wo_ref[S5_WIDTH:, :], preferred_element_type=F32)

    def mod_rows(idx):
        if mod_ref.shape[0] == 1:
            return mod_ref[0, idx:idx + 1, :]
        assert rows % seq == 0
        return jnp.concatenate(
            [jnp.broadcast_to(mod_ref[s, idx:idx + 1, :], (seq, D_MODEL))
             for s in range(r0 // seq, (r0 + rows) // seq)], axis=0)

    gate1, shift2, scale2, gate2 = mod_rows(2), mod_rows(3), mod_rows(4), mod_rows(5)
    x1 = x_ref[rr, :] + gate1 * attn
    h2 = (x1 * _rms_scale(x1) * (g2_ref[...] * (1.0 + scale2)) + shift2).astype(BF16)

    acc = None
    for lo, hi in _ff_chunks(n_fc):
        gg = jnp.dot(h2, wgu_ref[:, lo:hi], preferred_element_type=F32)
        up = jnp.dot(h2, wgu_ref[:, D_FF + lo:D_FF + hi], preferred_element_type=F32)
        act = (gg * jax.nn.sigmoid(gg) * up).astype(BF16)
        part = jnp.dot(act, wd_ref[lo:hi, :], preferred_element_type=F32)
        acc = part if acc is None else acc + part
    x2 = x1 + gate2 * acc
    y_ref[rr, :] = x2 * _rms_scale(x2) * gf_ref[...]


def _main_call(x2d, m, zuv, mod, norm2_g, final_g, ln_g, ln_b, gm_w, gm_bt, w_out_b, w_gu_b, w_down_b,
               *, seq, n_fc, n_parts, want_v):
    n_tok = x2d.shape[0]
    n_oct, n_lb, t, _, _ = m.shape
    assert n_oct % n_parts == 0 and n_oct * OCT * t == n_tok
    tm = n_parts * OCT * t
    streams_per_tile = max(1, tm // seq)
    tiles_per_stream = max(1, seq // tm)
    assert streams_per_tile * seq == tm or tiles_per_stream * tm == seq
    cl = min(GM_CHUNK, seq)
    kern = functools.partial(_main_kernel, tm=tm, t=t, cl=cl, seq=seq, n_fc=n_fc, n_parts=n_parts)
    tok = lambda width: pl.BlockSpec((tm, width), lambda i: (i, 0))
    oct_rows = n_lb * t * OCT
    octs = pl.BlockSpec((n_parts * oct_rows, LANES), lambda i: (i, 0))
    m = m.reshape(n_oct * oct_rows, LANES)
    out_shape = [jax.ShapeDtypeStruct((n_tok, D_MODEL), F32)]
    out_specs = [tok(D_MODEL)]
    if want_v:
        out_shape.append(jax.ShapeDtypeStruct((n_tok, GM_WIDTH), F32))
        out_specs.append(tok(GM_WIDTH))
    return pl.pallas_call(
        kern,
        out_shape=tuple(out_shape),
        grid=(n_oct // n_parts,),
        in_specs=[
            tok(D_MODEL), octs, tok(2 * GM_WIDTH),
            pl.BlockSpec((streams_per_tile, N_MOD, D_MODEL), lambda i: (i // tiles_per_stream, 0, 0)),
            _const_spec((1, D_MODEL)),
            _const_spec((1, D_MODEL)),
            _const_spec((1, GM_WIDTH)),
            _const_spec((1, GM_WIDTH)),
            _const_spec((GM_HEADS, GM_CHUNK, GM_CHUNK)),
            _const_spec((GM_CHUNK, GM_HEADS)),
            _const_spec((D_MODEL, D_MODEL)),
            _const_spec((D_MODEL, 2 * D_FF)),
            _const_spec((D_FF, D_MODEL)),
        ],
        out_specs=tuple(out_specs),
        scratch_shapes=[pltpu.VMEM((tm, GM_WIDTH), BF16), pltpu.VMEM((tm, GM_WIDTH), BF16),
                        pltpu.VMEM((tm, D_MODEL), F32)],
        compiler_params=_cparams(1),
        name="main",
    )(x2d, m, zuv, mod, norm2_g, final_g, ln_g, ln_b, gm_w, gm_bt, w_out_b, w_gu_b, w_down_b)


def _trunk(x, mod, s0, prm, *, t, n_ph, row_blk, n_fc, n_parts, want_v, f32_weights=None):
    b, seq, _ = x.shape
    n_chunks = seq // t
    rows = b * n_chunks
    scan = s0 is None
    assert (b == 1) if scan else (n_chunks == 1)
    n_sub = t // SUB

    zt4, zuv = _inproj_call(
        x.reshape(rows, t, D_MODEL), mod[:, 0], mod[:, 1], prm["norm1_g"], prm["w_in"],
        n_ph=n_ph, row_blk=row_blk)
    lane_rows = zt4.shape[-1]

    if scan:
        s0_l = None
    else:
        s0_l = jnp.transpose(s0, (1, 2, 0))
        s0_l = jnp.pad(s0_l, ((0, 0), (0, 0), (0, lane_rows - rows)))
    pending = dict(f32_weights or {})
    on_glu = {k: pending.pop(k) for k in ("w_gu",) if k in pending}
    yt, sfin, *cast = _s5_call(zt4.reshape(S5_GROUPS, t * S5_GROUP, lane_rows),
                               prm["wk"], prm["qm"], prm["tabr"], prm["tabi"], s0_l,
                               n_sub=n_sub, scan=scan, cast_riders=tuple(pending.values()))
    prm = {**prm, **dict(zip(pending.keys(), cast))}
    m, *cast = _glu_call(yt.reshape(S5_GROUPS, t, S5_GROUP, lane_rows), prm["w_glu"], prm["b_glu"],
                         rows=rows, n_ph=n_ph, row_blk=row_blk, cast_riders=tuple(on_glu.values()))
    prm = {**prm, **dict(zip(on_glu.keys(), cast))}

    n_tok = b * seq
    y, *maybe_v = _main_call(
        x.reshape(n_tok, D_MODEL), m, zuv.reshape(n_tok, 2 * GM_WIDTH),
        mod, prm["norm2_g"], prm["final_g"], prm["ln_g"], prm["ln_b"],
        prm["gm_w"], prm["gm_bt"], prm["w_out"], prm["w_gu"], prm["w_down"],
        seq=seq, n_fc=n_fc, n_parts=n_parts, want_v=want_v)

    first_kept = lane_rows - LANES
    if scan:
        fin = sfin[:, :, rows - 1 - first_kept][None]
    else:
        assert first_kept == 0
        fin = jnp.transpose(sfin[:, :, :rows], (2, 0, 1))
    v = maybe_v[0].reshape(b, seq, GM_WIDTH) if want_v else None
    return y.reshape(b, seq, D_MODEL), fin[..., :S5_STATE], fin[..., S5_STATE:], v, prm


def kernel(x_prompt, x_sample, state_s5_re, state_s5_im, c_prompt, c_sample, norm1_g, norm2_g, w_ada, b_ada, w_in, s5_lambda_re, s5_lambda_im, s5_log_step, s5_b_re, s5_b_im, s5_c_re, s5_c_im, s5_d, s5_w_glu, s5_b_glu, gm_ln_g, gm_ln_b, gm_w_s, gm_b_s, w_out, ffn_w_gu, ffn_w_down, final_g):
    depth = w_in.shape[0]
    assert depth == 1
    l = 0
    n_p = c_prompt.shape[0]
    n_s = c_sample.shape[0]

    c_all = jnp.concatenate([c_prompt, c_sample], axis=0)
    c_pad = jnp.pad(c_all, ((0, -c_all.shape[0] % 8), (0, 0)))
    mod_all = _ada_call(c_pad, w_ada[l], b_ada[l][None, :])
    mod_p = mod_all[:n_p].reshape(n_p, N_MOD, D_MODEL)
    mod_s = mod_all[n_p:n_p + n_s].reshape(n_s, N_MOD, D_MODEL)

    wk, qm, tabr, tabi, w_in_b = _s5_prep_call(
        s5_lambda_re[l], s5_lambda_im[l], s5_log_step[l][None, :], s5_b_re[l], s5_b_im[l],
        s5_c_re[l], s5_c_im[l], s5_d[l], cast_riders=(w_in[l],))

    prm = dict(
        norm1_g=norm1_g[l][None, :], norm2_g=norm2_g[l][None, :], final_g=final_g[None, :],
        w_in=w_in_b, ln_g=gm_ln_g[l][None, :], ln_b=gm_ln_b[l][None, :],
        wk=wk, qm=qm, tabr=tabr, tabi=tabi, b_glu=s5_b_glu[l][None, :],
        gm_w=gm_w_s[l], gm_bt=jnp.transpose(gm_b_s[l]),
    )
    later_weights = dict(w_glu=s5_w_glu[l], w_out=w_out[l], w_gu=ffn_w_gu[l], w_down=ffn_w_down[l])

    yp, pre, pim, _, prm = _trunk(x_prompt, mod_p, None, prm, f32_weights=later_weights,
                                  t=S5_LONG_SUBS * SUB, n_ph=8, row_blk=LANES, n_fc=2, n_parts=1, want_v=False)
    s0 = jnp.concatenate([state_s5_re[l], state_s5_im[l]], axis=-1)
    n_b, seq_s, _ = x_sample.shape
    ys, sre, sim, vs, _ = _trunk(x_sample, mod_s, s0, prm,
                                 t=seq_s, n_ph=seq_s, row_blk=n_b, n_fc=2, n_parts=1, want_v=True)
    return (yp, ys, pre[None], pim[None], sre[None], sim[None], vs[None])
```

```python
import functools
import math

import jax
import jax.numpy as jnp
from jax import lax
from jax.experimental import pallas as pl
from jax.experimental.pallas import tpu as pltpu

D_MODEL = 1024
S5_WIDTH = 512
S5_GROUP = 16
S5_GROUPS = 32
S5_STATE = 64
GM_WIDTH = 512
GM_CHUNK = 128
GM_HEADS = 8
GM_HEAD_DIM = 64
CHUNK = 64
IN_WIDTH = S5_WIDTH + 2 * GM_WIDTH
D_FF = 2816
EPS = 1e-6

LANES = 128
SUBLANES = 8
BF16_SUBLANES = 16
MXU_DIM = 256
VMEM_LIMIT_BYTES = 56 * 1024 * 1024

SUB = MXU_DIM // S5_GROUP
S5_LONG_SUBS = 4
PREP_GROUPS_PER_STEP = 8
S5_GROUPS_PER_STEP_LONG = 4
S5_GROUPS_PER_STEP_SHORT = 8
ADA_K_BLOCK = 128
FFN_CHUNKS = 2

F32 = jnp.float32
BF16 = jnp.bfloat16


def _cparams(n_grid_axes):
    return pltpu.CompilerParams(
        dimension_semantics=("arbitrary",) * n_grid_axes,
        vmem_limit_bytes=VMEM_LIMIT_BYTES,
    )


def _const_spec(shape):
    nd = len(shape)
    return pl.BlockSpec(shape, lambda *_: (0,) * nd, pipeline_mode=pl.Buffered(1))


def _rms_scale(x):
    return lax.rsqrt(jnp.mean(x * x, axis=-1, keepdims=True) + EPS)


_GELU_C0 = math.sqrt(2.0 / math.pi)
_GELU_C1 = 0.044715 * _GELU_C0


def _gelu(x):
    hx = 0.5 * x
    return hx + hx * jnp.tanh(x * (_GELU_C0 + _GELU_C1 * (x * x)))


def _sigmoid(x):
    return 0.5 * jnp.tanh(0.5 * x) + 0.5


def _cmul(ar, ai, xr, xi):
    return ar * xr - ai * xi, ar * xi + ai * xr


def _div_pow2(idx, divisor):
    shift = divisor.bit_length() - 1
    assert divisor == 1 << shift
    return lax.shift_right_logical(idx, shift)


def _mod_pow2(idx, divisor):
    assert divisor & (divisor - 1) == 0
    return lax.bitwise_and(idx, divisor - 1)


def _dot_split(x, y):
    x_hi, y_hi = x.astype(BF16), y.astype(BF16)
    x_lo = (x - x_hi.astype(F32)).astype(BF16)
    y_lo = (y - y_hi.astype(F32)).astype(BF16)
    dot = functools.partial(jnp.dot, preferred_element_type=F32)
    return dot(x_hi, y_hi) + (dot(x_lo, y_hi) + dot(x_hi, y_lo))


OCT = SUBLANES


def _oct_shape(n_rows, t, width):
    assert n_rows % OCT == 0 and width % LANES == 0
    return (n_rows // OCT, width // LANES, t, OCT, LANES)


def _oct_store(ref, ph, val):
    n_oct, n_lb = ref.shape[0], ref.shape[1]
    for lb in range(n_lb):
        ref[:, lb, ph, :, :] = val[:, lb * LANES:(lb + 1) * LANES].reshape(n_oct, OCT, LANES)


def _oct_load(ref, t):
    n_lb = ref.shape[0] // (t * OCT)
    chunks = []
    for c in range(OCT):
        chunks.append(jnp.concatenate(
            [ref[pl.ds(lb * t * OCT + c, t, stride=OCT), :] for lb in range(n_lb)], axis=1))
    return jnp.concatenate(chunks, axis=0)


def _cast_rider_specs(arrays, grid):
    n_steps = math.prod(grid)

    def row_block(*idx):
        step = 0
        for i, extent in zip(idx, grid):
            step = step * extent + i
        return (step, 0)

    specs, shapes = [], []
    for a in arrays:
        rows, cols = a.shape
        blk = rows // n_steps
        assert blk * n_steps == rows and blk % BF16_SUBLANES == 0
        specs.append(pl.BlockSpec((blk, cols), row_block))
        shapes.append(jax.ShapeDtypeStruct((rows, cols), BF16))
    return specs, shapes


def _split_riders(refs, n_in, n_out, n_cast):
    ins, refs = refs[:n_in], refs[n_in:]
    cast_in, refs = refs[:n_cast], refs[n_cast:]
    outs, refs = refs[:n_out], refs[n_out:]
    cast_out, scratch = refs[:n_cast], refs[n_cast:]
    return (*ins, *outs, *scratch), list(zip(cast_in, cast_out))


def _run_riders(pairs):
    for src, dst in pairs:
        dst[...] = src[...].astype(BF16)


def _ada_kernel(c_ref, w_ref, b_ref, o_ref):
    @pl.when(pl.program_id(0) == 0)
    def _():
        o_ref[...] = jnp.broadcast_to(b_ref[...], o_ref.shape)

    c = c_ref[...]
    o_ref[...] += _dot_split(c * jax.nn.sigmoid(c), w_ref[...])


def _ada_call(c_pad, w_ada, b_ada):
    rows = c_pad.shape[0]
    n_in, n_out = w_ada.shape
    bk = ADA_K_BLOCK
    return pl.pallas_call(
        _ada_kernel,
        out_shape=jax.ShapeDtypeStruct((rows, n_out), F32),
        grid=(n_in // bk,),
        in_specs=[
            pl.BlockSpec((rows, bk), lambda k: (0, k)),
            pl.BlockSpec((bk, n_out), lambda k: (k, 0)),
            pl.BlockSpec((1, n_out), lambda k: (0, 0)),
        ],
        out_specs=pl.BlockSpec((rows, n_out), lambda k: (0, 0)),
        compiler_params=_cparams(1),
        name="ada",
    )(c_pad, w_ada, b_ada)


def _discretise(lr, li, ls):
    step = jnp.exp(ls)
    mag = jnp.exp(lr * step)
    ar = mag * jnp.cos(li * step)
    ai = mag * jnp.sin(li * step)
    den = lr * lr + li * li
    fr = ((ar - 1.0) * lr + ai * li) / den
    fi = (ai * lr - (ar - 1.0) * li) / den
    return ar, ai, fr, fi


def _selection(rows, cols, row_of_col):
    r = lax.broadcasted_iota(jnp.int32, (rows, cols), 0)
    c = lax.broadcasted_iota(jnp.int32, (rows, cols), 1)
    return jnp.where(r == row_of_col(c), 1.0, 0.0).astype(BF16)


def _place(x, sel):
    hi = x.astype(BF16)
    rest = x - hi.astype(F32)
    mid = rest.astype(BF16)
    lo = (rest - mid.astype(F32)).astype(BF16)
    dot = functools.partial(jnp.dot, preferred_element_type=F32)
    return dot(hi, sel) + (dot(mid, sel) + dot(lo, sel))


def _to_column(row):
    k = row.shape[1]
    r = lax.broadcasted_iota(jnp.int32, (k, k), 0)
    c = lax.broadcasted_iota(jnp.int32, (k, k), 1)
    return jnp.sum(jnp.where(r == c, jnp.broadcast_to(row, (k, k)), 0.0), axis=1, keepdims=True)


def _s5_prep_kernel(*refs, n_cast):
    refs, riders = _split_riders(refs, n_in=8, n_out=4, n_cast=n_cast)
    lam_re_ref, lam_im_ref, ls_ref, d_ref, *grouped = refs
    n, sub, p = S5_STATE, SUB, S5_GROUP
    width = sub * p

    def col_source(c):
        return jnp.where(c < width, (sub - 1) - _div_pow2(c, p), jnp.where(c < 2 * width, sub + 1, sub))

    sels = dict(
        twice=_selection(n, 2 * n, lambda c: _mod_pow2(c, n)),
        tile=_selection(p, width, lambda c: _mod_pow2(c, p)),
        cols=_selection(2 * n, 2 * width + LANES, col_source),
    )
    gps = grouped[0].shape[0]
    pending = [_s5_prep_group(pl.program_id(0) * gps + gi, sels, lam_re_ref, lam_im_ref, ls_ref, d_ref,
                              *[r.at[gi] for r in grouped]) for gi in range(gps)]
    while pending:
        pending = [gen for gen in pending if next(gen, "done") != "done"]
    _run_riders(riders)


def _s5_prep_group(g, sels, lam_re_ref, lam_im_ref, ls_ref, d_ref, b_re_ref, b_im_ref, c_re_ref, c_im_ref,
                   wk_ref, qm_ref, tabr_ref, tabi_ref):
    n, sub, p = S5_STATE, SUB, S5_GROUP
    width = sub * p

    lr_row = lam_re_ref[pl.ds(g, 1), :]
    li_row = lam_im_ref[pl.ds(g, 1), :]
    ls_all = ls_ref[...]
    grp_lane = lax.broadcasted_iota(jnp.int32, ls_all.shape, 1)
    ls = jnp.sum(jnp.where(grp_lane == g, ls_all, 0.0), axis=1, keepdims=True)

    ar8, ai8, fr8, fi8 = _discretise(jnp.broadcast_to(lr_row, (SUBLANES, n)),
                                     jnp.broadcast_to(li_row, (SUBLANES, n)), ls)
    twice = lambda t8: _place(jnp.concatenate([t8] * (p // SUBLANES), axis=0), sels["twice"])
    a2r, a2i = twice(ar8), twice(ai8)
    first = lax.broadcasted_iota(jnp.int32, (p, 2 * n), 1) < n
    c2r = _place(c_re_ref[...], sels["twice"])
    c2i = _place(c_im_ref[...], sels["twice"])
    pr = jnp.ones_like(a2r)
    pi = jnp.zeros_like(a2r)
    ccat = []
    tbl_rows = SUBLANES * (-(-(sub + 2) // SUBLANES))
    tbl_row = lax.broadcasted_iota(jnp.int32, (tbl_rows, 2 * n), 0)
    as_tbl_row = lambda re2, im2: jnp.concatenate([jnp.where(first, re2, im2)] * (tbl_rows // p + 1),
                                                  axis=0)[:tbl_rows]
    tbl = jnp.zeros((tbl_rows, 2 * n), F32)
    for d in range(sub + 1):
        ccat.append(c2r * jnp.where(first, pr, -pi) + c2i * jnp.where(first, -pi, -pr))
        tbl = jnp.where(tbl_row == d, as_tbl_row(pr, pi), tbl)
        pr, pi = _cmul(a2r, a2i, pr, pi)
    tbl = jnp.where(tbl_row == sub + 1, as_tbl_row(twice(fr8), twice(fi8)), tbl)
    yield
    qm_ref[...] = jnp.concatenate(ccat[1:], axis=0).astype(BF16)
    rcat = jnp.concatenate(ccat[:sub], axis=0)

    tbl_t = jnp.concatenate([tbl, jnp.zeros((2 * n - tbl_rows, 2 * n), F32)], axis=0).T
    cols = _place(tbl_t, sels["cols"])
    yield
    apr, fr, a16r = cols[:n, :width], cols[:n, width:2 * width], cols[:n, 2 * width:]
    api, fi, a16i = cols[n:, :width], cols[n:, width:2 * width], cols[n:, 2 * width:]
    btr = _place(b_re_ref[...], sels["tile"])
    bti = _place(b_im_ref[...], sels["tile"])
    bbr, bbi = _cmul(fr, fi, btr, bti)
    pmr, pmi = _cmul(apr, api, bbr, bbi)
    wk_ref[width:width + n, :] = pmr.astype(BF16)
    wk_ref[width + n:, :] = pmi.astype(BF16)
    yield

    bbcat = jnp.concatenate([bbr, bbi], axis=0)
    kt = _dot_split(rcat, bbcat)
    yield
    d_col = _to_column(d_ref[pl.ds(g, 1), :])
    row_p = lax.broadcasted_iota(jnp.int32, (p, width), 0)
    lane_p = lax.broadcasted_iota(jnp.int32, (p, width), 1)
    d_diag = jnp.where(row_p == _mod_pow2(lane_p, p), d_col, 0.0)
    kt = jnp.concatenate([kt[:p] + d_diag, kt[p:]], axis=0)
    col_blk = _div_pow2(lax.broadcasted_iota(jnp.int32, (width, width), 1), p)
    m16 = jnp.zeros((width, width), F32)
    for k in range(sub):
        if k == 0:
            shifted = kt
        else:
            shifted = jnp.concatenate(
                [jnp.zeros((k * p, width), F32), kt[:width - k * p]], axis=0)
        m16 = jnp.where(col_blk == k, shifted, m16)
    wk_ref[:width, :] = m16.astype(BF16)

    tabr_ref[...] = a16r
    tabi_ref[...] = a16i


def _s5_prep_call(lam_re, lam_im, log_step, b_re, b_im, c_re, c_im, d, cast_riders=()):
    g, n, p, sub = S5_GROUPS, S5_STATE, S5_GROUP, SUB
    width = sub * p
    grp = lambda shape: pl.BlockSpec((PREP_GROUPS_PER_STEP,) + shape, lambda i: (i, 0, 0))
    whole = lambda a: pl.BlockSpec(a.shape, lambda i: (0,) * a.ndim)
    n_steps = g // PREP_GROUPS_PER_STEP
    rider_specs, rider_shapes = _cast_rider_specs(cast_riders, (n_steps,))
    return pl.pallas_call(
        functools.partial(_s5_prep_kernel, n_cast=len(cast_riders)),
        out_shape=(
            jax.ShapeDtypeStruct((g, width + 2 * n, width), BF16),
            jax.ShapeDtypeStruct((g, width, 2 * n), BF16),
            jax.ShapeDtypeStruct((g, n, LANES), F32),
            jax.ShapeDtypeStruct((g, n, LANES), F32),
            *rider_shapes,
        ),
        grid=(n_steps,),
        in_specs=[whole(lam_re), whole(lam_im), whole(log_step), whole(d)]
        + [grp((n, p))] * 2 + [grp((p, n))] * 2 + rider_specs,
        out_specs=(grp((width + 2 * n, width)), grp((width, 2 * n)),
                   grp((n, LANES)), grp((n, LANES)), *rider_specs),
        compiler_params=_cparams(1),
        name="s5_prep",
    )(lam_re, lam_im, log_step, d, b_re, b_im, c_re, c_im, *cast_riders)


def _mod_vec(mod_ref, stream, idx):
    return mod_ref[stream:stream + 1, idx * D_MODEL:(idx + 1) * D_MODEL]


def _inproj_kernel(x_ref, mod_ref, g1_ref, w_ref, zt_ref, zuv_ref, hs_ref,
                   *, n_ph, rows, lane_rows, streams):
    m = rows * n_ph
    assert len(streams) in (1, rows)
    shift, scale = (jnp.concatenate([_mod_vec(mod_ref, s, idx) for s in streams], axis=0)
                    for idx in (0, 1))
    gain = (g1_ref[...] * (1.0 + scale))[:, None, :]
    shift = shift[:, None, :]
    x3 = x_ref[...]
    h = (x3 * _rms_scale(x3) * gain + shift).reshape(m, D_MODEL)
    zuv = jnp.dot(h.astype(BF16), w_ref[:, S5_WIDTH:], preferred_element_type=F32)
    zuv_ref[...] = zuv.reshape(rows, n_ph, 2 * GM_WIDTH)

    n_lb = D_MODEL // LANES
    for lb in range(n_lb):
        hs_ref[lb] = h[:, lb * LANES:(lb + 1) * LANES]
    hp = jnp.concatenate(
        [jnp.concatenate([hs_ref[lb, pl.ds(ph, rows, stride=n_ph), :] for lb in range(n_lb)], axis=1)
         for ph in range(n_ph)], axis=0).astype(BF16)
    z5 = jnp.dot(hp, w_ref[:, :S5_WIDTH], preferred_element_type=F32)
    for ph in range(n_ph):
        zz = z5[ph * rows:(ph + 1) * rows]
        if lane_rows > rows:
            zz = jnp.concatenate([zz, jnp.zeros((lane_rows - rows, S5_WIDTH), F32)], axis=0)
        zt = zz.T.reshape(S5_GROUPS, S5_GROUP, lane_rows)
        zt_ref[:, ph, :, :] = zt.astype(BF16)


def _phase_blocks(n_rows, row_blk):
    if row_blk % LANES == 0:
        assert n_rows % row_blk == 0
        return n_rows // row_blk, row_blk, n_rows
    assert row_blk == n_rows
    lane_rows = -(-n_rows // LANES) * LANES
    return 1, lane_rows, lane_rows


def _inproj_call(x3, mod_all, streams, norm1_g, w_in_b, *, n_ph, row_blk):
    rows, t, _ = x3.shape
    n_rb, lane_blk, lane_rows = _phase_blocks(rows, row_blk)
    kern = functools.partial(_inproj_kernel, n_ph=n_ph, rows=row_blk, lane_rows=lane_blk,
                             streams=streams)
    return pl.pallas_call(
        kern,
        out_shape=(
            jax.ShapeDtypeStruct((S5_GROUPS, t, S5_GROUP, lane_rows), BF16),
            jax.ShapeDtypeStruct((rows, t, 2 * GM_WIDTH), F32),
        ),
        grid=(n_rb, t // n_ph),
        in_specs=[
            pl.BlockSpec((row_blk, n_ph, D_MODEL), lambda i, j: (i, j, 0)),
            _const_spec(mod_all.shape),
            _const_spec((1, D_MODEL)),
            _const_spec((D_MODEL, IN_WIDTH)),
        ],
        out_specs=(
            pl.BlockSpec((S5_GROUPS, n_ph, S5_GROUP, lane_blk), lambda i, j: (0, j, 0, i)),
            pl.BlockSpec((row_blk, n_ph, 2 * GM_WIDTH), lambda i, j: (i, j, 0)),
        ),
        scratch_shapes=[pltpu.VMEM((D_MODEL // LANES, row_blk * n_ph, LANES), F32)],
        compiler_params=_cparams(2),
        name="inproj",
    )(x3, mod_all, norm1_g, w_in_b)


def _s5_kernel(*refs, n_sub, lanes, scan, n_cast):
    refs, riders = _split_riders(refs, n_in=5 if scan else 6, n_out=2, n_cast=n_cast)
    if scan:
        zt_ref, wk_ref, qm_ref, tabr_ref, tabi_ref, yt_ref, sf_ref, ybuf, lbuf = refs
        s0_ref = None
    else:
        zt_ref, wk_ref, qm_ref, tabr_ref, tabi_ref, s0_ref, yt_ref, sf_ref, ybuf, lbuf = refs
    groups = range(zt_ref.shape[0])
    n = S5_STATE
    width = SUB * S5_GROUP

    widen = lambda tile: jnp.concatenate([tile] * (lanes // LANES), axis=1)
    sub_pows = []
    for g in groups:
        a1 = (widen(tabr_ref[g]), widen(tabi_ref[g]))
        pows = [a1]
        for _ in range(n_sub - 1):
            pows.append(_cmul(*a1, *pows[-1]))
        sub_pows.append(pows)

    local = []
    for g in groups:
        wk = wk_ref[g]
        lr = li = None
        for j in range(n_sub):
            u = zt_ref[g, j * width:(j + 1) * width, :]
            r = jnp.dot(wk, u, preferred_element_type=F32)
            ybuf[g, j * width:(j + 1) * width, :] = r[:width]
            wr = r[width:width + n]
            wi = r[width + n:]
            if j == 0:
                lr, li = wr, wi
            else:
                tr, ti = _cmul(*sub_pows[g][0], lr, li)
                lr, li = tr + wr, ti + wi
            lbuf[g, j, :n, :] = lr
            lbuf[g, j, n:, :] = li
        local.append((lr, li))

    if scan:
        lane = lax.broadcasted_iota(jnp.int32, (n, lanes), 1)
        xs = list(local)
        ms = [sub_pows[g][n_sub - 1] for g in groups]
        for i in range(int(math.log2(lanes))):
            sh = 1 << i
            for g in groups:
                xr, xi = xs[g]
                rr = jnp.where(lane >= sh, pltpu.roll(xr, sh, 1), 0.0)
                ri = jnp.where(lane >= sh, pltpu.roll(xi, sh, 1), 0.0)
                tr, ti = _cmul(*ms[g], rr, ri)
                xs[g] = (xr + tr, xi + ti)
                ms[g] = _cmul(*ms[g], *ms[g])
        entering = [(jnp.where(lane >= 1, pltpu.roll(xr, 1, 1), 0.0),
                     jnp.where(lane >= 1, pltpu.roll(xi, 1, 1), 0.0)) for xr, xi in xs]
    else:
        entering = [(s0_ref[g, :n, :], s0_ref[g, n:, :]) for g in groups]

    keep = slice(lanes - LANES, lanes)
    for g in groups:
        qm = qm_ref[g]
        sr, si = entering[g]
        for j in range(n_sub):
            if j == 0:
                pr, pi = sr, si
            else:
                tr, ti = _cmul(*sub_pows[g][j - 1], sr, si)
                pr, pi = lbuf[g, j - 1, :n, :] + tr, lbuf[g, j - 1, n:, :] + ti
            sp = jnp.concatenate([pr, pi], axis=0).astype(BF16)
            y = ybuf[g, j * width:(j + 1) * width, :] + jnp.dot(qm, sp, preferred_element_type=F32)
            yt_ref[g, j * width:(j + 1) * width, :] = y.astype(BF16)
        tr, ti = _cmul(*sub_pows[g][n_sub - 1], sr, si)
        sf_ref[g, :n, :] = lbuf[g, n_sub - 1, :n, keep] + tr[:, keep]
        sf_ref[g, n:, :] = lbuf[g, n_sub - 1, n:, keep] + ti[:, keep]
    _run_riders(riders)


def _s5_call(zt, wk, qm, tabr, tabi, s0, *, n_sub, scan, cast_riders=()):
    g, rows, lanes = zt.shape
    n = S5_STATE
    width = SUB * S5_GROUP
    assert lanes % LANES == 0 and lanes & (lanes - 1) == 0
    gps = S5_GROUPS_PER_STEP_LONG if scan else S5_GROUPS_PER_STEP_SHORT
    grp = lambda shape: pl.BlockSpec((gps,) + shape, lambda i: (i, 0, 0))
    in_specs = [grp((rows, lanes)), grp((width + 2 * n, width)), grp((width, 2 * n)),
                grp((n, LANES)), grp((n, LANES))]
    args = [zt, wk, qm, tabr, tabi]
    if not scan:
        in_specs.append(grp((2 * n, lanes)))
        args.append(s0)
    rider_specs, rider_shapes = _cast_rider_specs(cast_riders, (g // gps,))
    kern = functools.partial(_s5_kernel, n_sub=n_sub, lanes=lanes, scan=scan,
                             n_cast=len(cast_riders))
    return pl.pallas_call(
        kern,
        out_shape=(jax.ShapeDtypeStruct((g, rows, lanes), BF16),
                   jax.ShapeDtypeStruct((g, 2 * n, LANES), F32), *rider_shapes),
        grid=(g // gps,),
        in_specs=in_specs + rider_specs,
        out_specs=(grp((rows, lanes)), grp((2 * n, LANES)), *rider_specs),
        scratch_shapes=[pltpu.VMEM((gps, rows, lanes), F32),
                        pltpu.VMEM((gps, n_sub, 2 * n, lanes), F32)],
        compiler_params=_cparams(1),
        name="s5",
    )(*args, *cast_riders)


def _glu_kernel(*refs, n_ph, rows, lane_rows, n_cast):
    (yt_ref, w_ref, b_ref, m_ref), riders = _split_riders(refs, n_in=3, n_out=1, n_cast=n_cast)
    _run_riders(riders)
    gs = []
    for ph in range(n_ph):
        yt = yt_ref[:, ph, :, :].astype(F32).reshape(S5_WIDTH, lane_rows)
        gs.append(_gelu(yt.T[:rows]))
    gy = jnp.concatenate(gs, axis=0)
    gate = jnp.dot(gy.astype(BF16), w_ref[...], preferred_element_type=F32) + b_ref[...]
    m = gy * _sigmoid(gate)
    for ph in range(n_ph):
        _oct_store(m_ref, ph, m[ph * rows:(ph + 1) * rows])


def _glu_call(yt4, w_glu_b, b_glu, *, rows, n_ph, row_blk, cast_riders=()):
    g, t, p, lane_rows = yt4.shape
    n_rb, lane_blk, lane_rows_expected = _phase_blocks(rows, row_blk)
    assert lane_rows == lane_rows_expected
    grid = (n_rb, t // n_ph)
    rider_specs, rider_shapes = _cast_rider_specs(cast_riders, grid)
    kern = functools.partial(_glu_kernel, n_ph=n_ph, rows=row_blk, lane_rows=lane_blk,
                             n_cast=len(cast_riders))
    oct_shape = _oct_shape(rows, t, S5_WIDTH)
    return pl.pallas_call(
        kern,
        out_shape=(jax.ShapeDtypeStruct(oct_shape, F32), *rider_shapes),
        grid=grid,
        in_specs=[
            pl.BlockSpec((g, n_ph, p, lane_blk), lambda i, j: (0, j, 0, i)),
            _const_spec((S5_WIDTH, S5_WIDTH)),
            _const_spec((1, S5_WIDTH)),
            *rider_specs,
        ],
        out_specs=(pl.BlockSpec((row_blk // OCT, oct_shape[1], n_ph, OCT, LANES),
                                lambda i, j: (i, 0, j, 0, 0)), *rider_specs),
        compiler_params=_cparams(2),
        name="glu",
    )(yt4, w_glu_b, b_glu, *cast_riders)


def _ff_chunks():
    tiles = D_FF // MXU_DIM
    assert tiles * MXU_DIM == D_FF
    bounds = [MXU_DIM * ((tiles * c + FFN_CHUNKS - 1) // FFN_CHUNKS) for c in range(FFN_CHUNKS + 1)]
    return list(zip(bounds[:-1], bounds[1:]))


def _main_kernel(x_ref, m_ref, zuv_ref, mod_ref, g2_ref, gf_ref, lng_ref, lnb_ref, gw_ref, gbt_ref,
                 wo_ref, wgu_ref, wd_ref, y_ref, *rest, tm, cl, seq, streams):
    *maybe_v_out_ref, v_ref, ygm_ref, attn_ref = rest
    t = tm // OCT
    hd = GM_HEAD_DIM

    attn_ref[...] = jnp.dot(_oct_load(m_ref, t).astype(BF16), wo_ref[:S5_WIDTH, :],
                            preferred_element_type=F32)

    gv = _gelu(zuv_ref[:, GM_WIDTH:])
    cen = gv - jnp.mean(gv, axis=-1, keepdims=True)
    var = jnp.mean(cen * cen, axis=-1, keepdims=True)
    v = cen * lax.rsqrt(var + EPS) * lng_ref[...] + lnb_ref[...]
    for v_out_ref in maybe_v_out_ref:
        v_out_ref[...] = v
    v_ref[...] = v.astype(BF16)

    blk_i = _div_pow2(lax.broadcasted_iota(jnp.int32, (cl, cl), 0), CHUNK)
    blk_j = _div_pow2(lax.broadcasted_iota(jnp.int32, (cl, cl), 1), CHUNK)
    causal = blk_j <= blk_i
    first_head = lax.broadcasted_iota(jnp.int32, (cl, 2 * hd), 1) < hd
    for pr in range(GM_HEADS // 2):
        h0, h1 = 2 * pr, 2 * pr + 1
        wm = jnp.concatenate(
            [jnp.where(causal, gw_ref[h, :cl, :cl], 0.0) for h in (h0, h1)], axis=1).astype(BF16)
        bias = jnp.where(first_head, gbt_ref[:cl, h0:h0 + 1], gbt_ref[:cl, h1:h1 + 1])
        cs = slice(h0 * hd, (h1 + 1) * hd)
        for ci in range(tm // cl):
            rs = slice(ci * cl, (ci + 1) * cl)
            vv = v_ref[rs, cs]
            zero = jnp.zeros_like(vv)
            rhs = jnp.concatenate([jnp.where(first_head, vv, zero), jnp.where(first_head, zero, vv)],
                                  axis=0)
            mixed = jnp.dot(wm, rhs, preferred_element_type=F32) + bias
            ygm_ref[rs, cs] = (_gelu(zuv_ref[rs, cs]) * mixed).astype(BF16)

    attn = attn_ref[...] + jnp.dot(ygm_ref[...], wo_ref[S5_WIDTH:, :], preferred_element_type=F32)

    def mod_rows(idx):
        if len(streams) == 1:
            return _mod_vec(mod_ref, streams[0], idx)
        return jnp.concatenate(
            [jnp.broadcast_to(_mod_vec(mod_ref, s, idx), (seq, D_MODEL)) for s in streams], axis=0)

    gate1, shift2, scale2, gate2 = mod_rows(2), mod_rows(3), mod_rows(4), mod_rows(5)
    x1 = x_ref[...] + gate1 * attn
    h2 = (x1 * _rms_scale(x1) * (g2_ref[...] * (1.0 + scale2)) + shift2).astype(BF16)

    acc = None
    for lo, hi in _ff_chunks():
        gg = jnp.dot(h2, wgu_ref[:, lo:hi], preferred_element_type=F32)
        up = jnp.dot(h2, wgu_ref[:, D_FF + lo:D_FF + hi], preferred_element_type=F32)
        act = (gg * jax.nn.sigmoid(gg) * up).astype(BF16)
        part = jnp.dot(act, wd_ref[lo:hi, :], preferred_element_type=F32)
        acc = part if acc is None else acc + part
    x2 = x1 + gate2 * acc
    y_ref[...] = x2 * _rms_scale(x2) * gf_ref[...]


def _main_call(x2d, m, zuv, mod_all, streams, norm2_g, final_g, ln_g, ln_b, gm_w, gm_bt,
               w_out_b, w_gu_b, w_down_b, *, seq, want_v):
    n_tok = x2d.shape[0]
    n_oct, n_lb, t, _, _ = m.shape
    tm = OCT * t
    assert n_oct * tm == n_tok
    assert len(streams) == 1 or (n_oct == 1 and len(streams) * seq == tm)
    cl = min(GM_CHUNK, seq)
    kern = functools.partial(_main_kernel, tm=tm, cl=cl, seq=seq, streams=streams)
    tok = lambda width: pl.BlockSpec((tm, width), lambda i: (i, 0))
    oct_rows = n_lb * t * OCT
    octs = pl.BlockSpec((oct_rows, LANES), lambda i: (i, 0))
    m = m.reshape(n_oct * oct_rows, LANES)
    out_shape = [jax.ShapeDtypeStruct((n_tok, D_MODEL), F32)]
    out_specs = [tok(D_MODEL)]
    if want_v:
        out_shape.append(jax.ShapeDtypeStruct((n_tok, GM_WIDTH), F32))
        out_specs.append(tok(GM_WIDTH))
    return pl.pallas_call(
        kern,
        out_shape=tuple(out_shape),
        grid=(n_oct,),
        in_specs=[
            tok(D_MODEL), octs, tok(2 * GM_WIDTH),
            _const_spec(mod_all.shape),
            _const_spec((1, D_MODEL)),
            _const_spec((1, D_MODEL)),
            _const_spec((1, GM_WIDTH)),
            _const_spec((1, GM_WIDTH)),
            _const_spec((GM_HEADS, GM_CHUNK, GM_CHUNK)),
            _const_spec((GM_CHUNK, GM_HEADS)),
            _const_spec((D_MODEL, D_MODEL)),
            _const_spec((D_MODEL, 2 * D_FF)),
            _const_spec((D_FF, D_MODEL)),
        ],
        out_specs=tuple(out_specs),
        scratch_shapes=[pltpu.VMEM((tm, GM_WIDTH), BF16), pltpu.VMEM((tm, GM_WIDTH), BF16),
                        pltpu.VMEM((tm, D_MODEL), F32)],
        compiler_params=_cparams(1),
        name="main",
    )(x2d, m, zuv, mod_all, norm2_g, final_g, ln_g, ln_b, gm_w, gm_bt, w_out_b, w_gu_b, w_down_b)


def _trunk(x, mod_all, streams, s0, prm, *, t, n_ph, row_blk, want_v, f32_weights=None):
    b, seq, _ = x.shape
    n_chunks = seq // t
    rows = b * n_chunks
    scan = s0 is None
    assert (b == 1) if scan else (n_chunks == 1)
    n_sub = t // SUB

    zt4, zuv = _inproj_call(
        x.reshape(rows, t, D_MODEL), mod_all, streams, prm["norm1_g"], prm["w_in"],
        n_ph=n_ph, row_blk=row_blk)
    lane_rows = zt4.shape[-1]

    if scan:
        s0_l = None
    else:
        s0_l = jnp.transpose(s0, (1, 2, 0))
        s0_l = jnp.pad(s0_l, ((0, 0), (0, 0), (0, lane_rows - rows)))
    pending = dict(f32_weights or {})
    on_glu = {k: pending.pop(k) for k in ("w_gu",) if k in pending}
    yt, sfin, *cast = _s5_call(zt4.reshape(S5_GROUPS, t * S5_GROUP, lane_rows),
                               prm["wk"], prm["qm"], prm["tabr"], prm["tabi"], s0_l,
                               n_sub=n_sub, scan=scan, cast_riders=tuple(pending.values()))
    prm = {**prm, **dict(zip(pending.keys(), cast))}
    m, *cast = _glu_call(yt.reshape(S5_GROUPS, t, S5_GROUP, lane_rows), prm["w_glu"], prm["b_glu"],
                         rows=rows, n_ph=n_ph, row_blk=row_blk, cast_riders=tuple(on_glu.values()))
    prm = {**prm, **dict(zip(on_glu.keys(), cast))}

    n_tok = b * seq
    y, *maybe_v = _main_call(
        x.reshape(n_tok, D_MODEL), m, zuv.reshape(n_tok, 2 * GM_WIDTH),
        mod_all, streams, prm["norm2_g"], prm["final_g"], prm["ln_g"], prm["ln_b"],
        prm["gm_w"], prm["gm_bt"], prm["w_out"], prm["w_gu"], prm["w_down"],
        seq=seq, want_v=want_v)

    first_kept = lane_rows - LANES
    if scan:
        fin = sfin[:, :, rows - 1 - first_kept][None]
    else:
        assert first_kept == 0
        fin = jnp.transpose(sfin[:, :, :rows], (2, 0, 1))
    v = maybe_v[0].reshape(b, seq, GM_WIDTH) if want_v else None
    return y.reshape(b, seq, D_MODEL), fin[..., :S5_STATE], fin[..., S5_STATE:], v, prm


def kernel(x_prompt, x_sample, state_s5_re, state_s5_im, c_prompt, c_sample, norm1_g, norm2_g, w_ada, b_ada, w_in, s5_lambda_re, s5_lambda_im, s5_log_step, s5_b_re, s5_b_im, s5_c_re, s5_c_im, s5_d, s5_w_glu, s5_b_glu, gm_ln_g, gm_ln_b, gm_w_s, gm_b_s, w_out, ffn_w_gu, ffn_w_down, final_g):
    depth = w_in.shape[0]
    assert depth == 1
    l = 0
    n_p = c_prompt.shape[0]
    n_s = c_sample.shape[0]

    c_all = jnp.concatenate([c_prompt, c_sample], axis=0)
    c_pad = jnp.pad(c_all, ((0, -c_all.shape[0] % SUBLANES), (0, 0)))
    mod_all = _ada_call(c_pad, w_ada[l], b_ada[l][None, :])
    streams_p = tuple(range(n_p))
    streams_s = tuple(range(n_p, n_p + n_s))

    wk, qm, tabr, tabi, w_in_b = _s5_prep_call(
        s5_lambda_re[l], s5_lambda_im[l], s5_log_step[l][None, :], s5_b_re[l], s5_b_im[l],
        s5_c_re[l], s5_c_im[l], s5_d[l], cast_riders=(w_in[l],))

    prm = dict(
        norm1_g=norm1_g[l][None, :], norm2_g=norm2_g[l][None, :], final_g=final_g[None, :],
        w_in=w_in_b, ln_g=gm_ln_g[l][None, :], ln_b=gm_ln_b[l][None, :],
        wk=wk, qm=qm, tabr=tabr, tabi=tabi, b_glu=s5_b_glu[l][None, :],
        gm_w=gm_w_s[l], gm_bt=jnp.transpose(gm_b_s[l]),
    )
    later_weights = dict(w_glu=s5_w_glu[l], w_out=w_out[l], w_gu=ffn_w_gu[l], w_down=ffn_w_down[l])

    yp, pre, pim, _, prm = _trunk(x_prompt, mod_all, streams_p, None, prm, f32_weights=later_weights,
                                  t=S5_LONG_SUBS * SUB, n_ph=SUBLANES, row_blk=LANES, want_v=False)
    s0 = jnp.concatenate([state_s5_re[l], state_s5_im[l]], axis=-1)
    n_b, seq_s, _ = x_sample.shape
    ys, sre, sim, vs, _ = _trunk(x_sample, mod_all, streams_s, s0, prm,
                                 t=seq_s, n_ph=seq_s, row_blk=n_b, want_v=True)
    return (yp, ys, pre[None], pim[None], sre[None], sim[None], vs[None])
```

```python
import functools
import math

import jax
import jax.numpy as jnp
from jax import lax
from jax.experimental import pallas as pl
from jax.experimental.pallas import tpu as pltpu

D_MODEL = 1024
S5_WIDTH = 512
S5_GROUP = 16
S5_GROUPS = 32
S5_STATE = 64
GM_WIDTH = 512
GM_CHUNK = 128
GM_HEADS = 8
GM_HEAD_DIM = 64
CHUNK = 64
IN_WIDTH = S5_WIDTH + 2 * GM_WIDTH
D_FF = 2816
EPS = 1e-6

LANES = 128
SUBLANES = 8
BF16_SUBLANES = 16
MXU_DIM = 256
VMEM_LIMIT_BYTES = 56 * 1024 * 1024

SUB = MXU_DIM // S5_GROUP
S5_LONG_SUBS = 4
PREP_GROUPS_PER_STEP = 8
S5_GROUPS_PER_STEP_LONG = 4
S5_GROUPS_PER_STEP_SHORT = 8
ADA_K_BLOCK = 128
FFN_CHUNKS = 2

F32 = jnp.float32
BF16 = jnp.bfloat16


def _cparams(n_grid_axes):
    return pltpu.CompilerParams(
        dimension_semantics=("arbitrary",) * n_grid_axes,
        vmem_limit_bytes=VMEM_LIMIT_BYTES,
    )


def _const_spec(shape):
    nd = len(shape)
    return pl.BlockSpec(shape, lambda *_: (0,) * nd, pipeline_mode=pl.Buffered(1))


def _rms_scale(x):
    return lax.rsqrt(jnp.mean(x * x, axis=-1, keepdims=True) + EPS)


_GELU_C0 = math.sqrt(2.0 / math.pi)
_GELU_C1 = 0.044715 * _GELU_C0


def _gelu(x):
    hx = 0.5 * x
    return hx + hx * jnp.tanh(x * (_GELU_C0 + _GELU_C1 * (x * x)))


def _sigmoid(x):
    return 0.5 * jnp.tanh(0.5 * x) + 0.5


def _cmul(ar, ai, xr, xi):
    return ar * xr - ai * xi, ar * xi + ai * xr


def _div_pow2(idx, divisor):
    shift = divisor.bit_length() - 1
    assert divisor == 1 << shift
    return lax.shift_right_logical(idx, shift)


def _mod_pow2(idx, divisor):
    assert divisor & (divisor - 1) == 0
    return lax.bitwise_and(idx, divisor - 1)


def _dot_split(x, y):
    x_hi, y_hi = x.astype(BF16), y.astype(BF16)
    x_lo = (x - x_hi.astype(F32)).astype(BF16)
    y_lo = (y - y_hi.astype(F32)).astype(BF16)
    dot = functools.partial(jnp.dot, preferred_element_type=F32)
    return dot(x_hi, y_hi) + (dot(x_lo, y_hi) + dot(x_hi, y_lo))


OCT = SUBLANES


def _oct_shape(n_rows, t, width):
    assert n_rows % OCT == 0 and width % LANES == 0
    return (n_rows // OCT, width // LANES, t, OCT, LANES)


def _oct_store(ref, ph, val):
    n_oct, n_lb = ref.shape[0], ref.shape[1]
    for lb in range(n_lb):
        ref[:, lb, ph, :, :] = val[:, lb * LANES:(lb + 1) * LANES].reshape(n_oct, OCT, LANES)


def _oct_load(ref, t):
    n_lb = ref.shape[0] // (t * OCT)
    chunks = []
    for c in range(OCT):
        chunks.append(jnp.concatenate(
            [ref[pl.ds(lb * t * OCT + c, t, stride=OCT), :] for lb in range(n_lb)], axis=1))
    return jnp.concatenate(chunks, axis=0)


def _cast_rider_specs(arrays, grid):
    n_steps = math.prod(grid)

    def row_block(*idx):
        step = 0
        for i, extent in zip(idx, grid):
            step = step * extent + i
        return (step, 0)

    specs, shapes = [], []
    for a in arrays:
        rows, cols = a.shape
        blk = rows // n_steps
        assert blk * n_steps == rows and blk % BF16_SUBLANES == 0
        specs.append(pl.BlockSpec((blk, cols), row_block))
        shapes.append(jax.ShapeDtypeStruct((rows, cols), BF16))
    return specs, shapes


def _split_riders(refs, n_in, n_out, n_cast):
    ins, refs = refs[:n_in], refs[n_in:]
    cast_in, refs = refs[:n_cast], refs[n_cast:]
    outs, refs = refs[:n_out], refs[n_out:]
    cast_out, scratch = refs[:n_cast], refs[n_cast:]
    return (*ins, *outs, *scratch), list(zip(cast_in, cast_out))


def _run_riders(pairs):
    for src, dst in pairs:
        dst[...] = src[...].astype(BF16)


def _ada_kernel(c_ref, w_ref, b_ref, o_ref):
    @pl.when(pl.program_id(0) == 0)
    def _():
        o_ref[...] = jnp.broadcast_to(b_ref[...], o_ref.shape)

    c = c_ref[...]
    o_ref[...] += _dot_split(c * jax.nn.sigmoid(c), w_ref[...])


def _ada_call(c_pad, w_ada, b_ada):
    rows = c_pad.shape[0]
    n_in, n_out = w_ada.shape
    bk = ADA_K_BLOCK
    return pl.pallas_call(
        _ada_kernel,
        out_shape=jax.ShapeDtypeStruct((rows, n_out), F32),
        grid=(n_in // bk,),
        in_specs=[
            pl.BlockSpec((rows, bk), lambda k: (0, k)),
            pl.BlockSpec((bk, n_out), lambda k: (k, 0)),
            pl.BlockSpec((1, n_out), lambda k: (0, 0)),
        ],
        out_specs=pl.BlockSpec((rows, n_out), lambda k: (0, 0)),
        compiler_params=_cparams(1),
        name="ada",
    )(c_pad, w_ada, b_ada)


def _discretise(lr, li, ls):
    step = jnp.exp(ls)
    mag = jnp.exp(lr * step)
    ar = mag * jnp.cos(li * step)
    ai = mag * jnp.sin(li * step)
    den = lr * lr + li * li
    fr = ((ar - 1.0) * lr + ai * li) / den
    fi = (ai * lr - (ar - 1.0) * li) / den
    return ar, ai, fr, fi


def _selection(rows, cols, row_of_col):
    r = lax.broadcasted_iota(jnp.int32, (rows, cols), 0)
    c = lax.broadcasted_iota(jnp.int32, (rows, cols), 1)
    return jnp.where(r == row_of_col(c), 1.0, 0.0).astype(BF16)


def _place(x, sel):
    hi = x.astype(BF16)
    rest = x - hi.astype(F32)
    mid = rest.astype(BF16)
    lo = (rest - mid.astype(F32)).astype(BF16)
    dot = functools.partial(jnp.dot, preferred_element_type=F32)
    return dot(hi, sel) + (dot(mid, sel) + dot(lo, sel))


def _to_column(row):
    k = row.shape[1]
    r = lax.broadcasted_iota(jnp.int32, (k, k), 0)
    c = lax.broadcasted_iota(jnp.int32, (k, k), 1)
    return jnp.sum(jnp.where(r == c, jnp.broadcast_to(row, (k, k)), 0.0), axis=1, keepdims=True)


def _s5_prep_kernel(*refs, n_cast):
    refs, riders = _split_riders(refs, n_in=8, n_out=4, n_cast=n_cast)
    lam_re_ref, lam_im_ref, ls_ref, d_ref, *grouped = refs
    n, sub, p = S5_STATE, SUB, S5_GROUP
    width = sub * p

    def col_source(c):
        return jnp.where(c < width, (sub - 1) - _div_pow2(c, p), jnp.where(c < 2 * width, sub + 1, sub))

    sels = dict(
        twice=_selection(n, 2 * n, lambda c: _mod_pow2(c, n)),
        tile=_selection(p, width, lambda c: _mod_pow2(c, p)),
        cols=_selection(2 * n, 2 * width + LANES, col_source),
    )
    gps = grouped[0].shape[0]
    pending = [_s5_prep_group(pl.program_id(0) * gps + gi, sels, lam_re_ref, lam_im_ref, ls_ref, d_ref,
                              *[r.at[gi] for r in grouped]) for gi in range(gps)]
    while pending:
        pending = [gen for gen in pending if next(gen, "done") != "done"]
    _run_riders(riders)


def _s5_prep_group(g, sels, lam_re_ref, lam_im_ref, ls_ref, d_ref, b_re_ref, b_im_ref, c_re_ref, c_im_ref,
                   wk_ref, qm_ref, tabr_ref, tabi_ref):
    n, sub, p = S5_STATE, SUB, S5_GROUP
    width = sub * p

    lr_row = lam_re_ref[pl.ds(g, 1), :]
    li_row = lam_im_ref[pl.ds(g, 1), :]
    ls_all = ls_ref[...]
    grp_lane = lax.broadcasted_iota(jnp.int32, ls_all.shape, 1)
    ls = jnp.sum(jnp.where(grp_lane == g, ls_all, 0.0), axis=1, keepdims=True)

    ar8, ai8, fr8, fi8 = _discretise(jnp.broadcast_to(lr_row, (SUBLANES, n)),
                                     jnp.broadcast_to(li_row, (SUBLANES, n)), ls)
    twice = lambda t8: _place(jnp.concatenate([t8] * (p // SUBLANES), axis=0), sels["twice"])
    a2r, a2i = twice(ar8), twice(ai8)
    first = lax.broadcasted_iota(jnp.int32, (p, 2 * n), 1) < n
    c2r = _place(c_re_ref[...], sels["twice"])
    c2i = _place(c_im_ref[...], sels["twice"])
    pr = jnp.ones_like(a2r)
    pi = jnp.zeros_like(a2r)
    ccat = []
    tbl_rows = SUBLANES * (-(-(sub + 2) // SUBLANES))
    tbl_row = lax.broadcasted_iota(jnp.int32, (tbl_rows, 2 * n), 0)
    as_tbl_row = lambda re2, im2: jnp.concatenate([jnp.where(first, re2, im2)] * (tbl_rows // p + 1),
                                                  axis=0)[:tbl_rows]
    tbl = jnp.zeros((tbl_rows, 2 * n), F32)
    for d in range(sub + 1):
        ccat.append(c2r * jnp.where(first, pr, -pi) + c2i * jnp.where(first, -pi, -pr))
        tbl = jnp.where(tbl_row == d, as_tbl_row(pr, pi), tbl)
        pr, pi = _cmul(a2r, a2i, pr, pi)
    tbl = jnp.where(tbl_row == sub + 1, as_tbl_row(twice(fr8), twice(fi8)), tbl)
    yield
    qm_ref[...] = jnp.concatenate(ccat[1:], axis=0).astype(BF16)
    rcat = jnp.concatenate(ccat[:sub], axis=0)

    tbl_t = jnp.concatenate([tbl, jnp.zeros((2 * n - tbl_rows, 2 * n), F32)], axis=0).T
    cols = _place(tbl_t, sels["cols"])
    yield
    apr, fr, a16r = cols[:n, :width], cols[:n, width:2 * width], cols[:n, 2 * width:]
    api, fi, a16i = cols[n:, :width], cols[n:, width:2 * width], cols[n:, 2 * width:]
    btr = _place(b_re_ref[...], sels["tile"])
    bti = _place(b_im_ref[...], sels["tile"])
    bbr, bbi = _cmul(fr, fi, btr, bti)
    pmr, pmi = _cmul(apr, api, bbr, bbi)
    wk_ref[width:width + n, :] = pmr.astype(BF16)
    wk_ref[width + n:, :] = pmi.astype(BF16)
    yield

    bbcat = jnp.concatenate([bbr, bbi], axis=0)
    kt = _dot_split(rcat, bbcat)
    yield
    d_col = _to_column(d_ref[pl.ds(g, 1), :])
    row_p = lax.broadcasted_iota(jnp.int32, (p, width), 0)
    lane_p = lax.broadcasted_iota(jnp.int32, (p, width), 1)
    d_diag = jnp.where(row_p == _mod_pow2(lane_p, p), d_col, 0.0)
    kt = jnp.concatenate([kt[:p] + d_diag, kt[p:]], axis=0)
    col_blk = _div_pow2(lax.broadcasted_iota(jnp.int32, (width, width), 1), p)
    m16 = jnp.zeros((width, width), F32)
    for k in range(sub):
        if k == 0:
            shifted = kt
        else:
            shifted = jnp.concatenate(
                [jnp.zeros((k * p, width), F32), kt[:width - k * p]], axis=0)
        m16 = jnp.where(col_blk == k, shifted, m16)
    wk_ref[:width, :] = m16.astype(BF16)

    tabr_ref[...] = a16r
    tabi_ref[...] = a16i


def _s5_prep_call(lam_re, lam_im, log_step, b_re, b_im, c_re, c_im, d, cast_riders=()):
    g, n, p, sub = S5_GROUPS, S5_STATE, S5_GROUP, SUB
    width = sub * p
    grp = lambda shape: pl.BlockSpec((PREP_GROUPS_PER_STEP,) + shape, lambda i: (i, 0, 0))
    whole = lambda a: pl.BlockSpec(a.shape, lambda i: (0,) * a.ndim)
    n_steps = g // PREP_GROUPS_PER_STEP
    rider_specs, rider_shapes = _cast_rider_specs(cast_riders, (n_steps,))
    return pl.pallas_call(
        functools.partial(_s5_prep_kernel, n_cast=len(cast_riders)),
        out_shape=(
            jax.ShapeDtypeStruct((g, width + 2 * n, width), BF16),
            jax.ShapeDtypeStruct((g, width, 2 * n), BF16),
            jax.ShapeDtypeStruct((g, n, LANES), F32),
            jax.ShapeDtypeStruct((g, n, LANES), F32),
            *rider_shapes,
        ),
        grid=(n_steps,),
        in_specs=[whole(lam_re), whole(lam_im), whole(log_step), whole(d)]
        + [grp((n, p))] * 2 + [grp((p, n))] * 2 + rider_specs,
        out_specs=(grp((width + 2 * n, width)), grp((width, 2 * n)),
                   grp((n, LANES)), grp((n, LANES)), *rider_specs),
        compiler_params=_cparams(1),
        name="s5_prep",
    )(lam_re, lam_im, log_step, d, b_re, b_im, c_re, c_im, *cast_riders)


def _mod_vec(mod_ref, stream, idx):
    return mod_ref[stream:stream + 1, idx * D_MODEL:(idx + 1) * D_MODEL]


def _inproj_kernel(x_ref, mod_ref, g1_ref, w_ref, zt_ref, zuv_ref, hs_ref,
                   *, n_ph, rows, lane_rows, streams):
    m = rows * n_ph
    assert len(streams) in (1, rows)
    shift, scale = (jnp.concatenate([_mod_vec(mod_ref, s, idx) for s in streams], axis=0)
                    for idx in (0, 1))
    gain = (g1_ref[...] * (1.0 + scale))[:, None, :]
    shift = shift[:, None, :]
    x3 = x_ref[...]
    h = (x3 * _rms_scale(x3) * gain + shift).reshape(m, D_MODEL)
    zuv = jnp.dot(h.astype(BF16), w_ref[:, S5_WIDTH:], preferred_element_type=F32)
    zuv_ref[...] = zuv.reshape(rows, n_ph, 2 * GM_WIDTH)

    n_lb = D_MODEL // LANES
    for lb in range(n_lb):
        hs_ref[lb] = h[:, lb * LANES:(lb + 1) * LANES]
    hp = jnp.concatenate(
        [jnp.concatenate([hs_ref[lb, pl.ds(ph, rows, stride=n_ph), :] for lb in range(n_lb)], axis=1)
         for ph in range(n_ph)], axis=0).astype(BF16)
    z5 = jnp.dot(hp, w_ref[:, :S5_WIDTH], preferred_element_type=F32)
    for ph in range(n_ph):
        zz = z5[ph * rows:(ph + 1) * rows]
        if lane_rows > rows:
            zz = jnp.concatenate([zz, jnp.zeros((lane_rows - rows, S5_WIDTH), F32)], axis=0)
        zt = zz.T.reshape(S5_GROUPS, S5_GROUP, lane_rows)
        zt_ref[:, ph, :, :] = zt.astype(BF16)


def _phase_blocks(n_rows, row_blk):
    if row_blk % LANES == 0:
        assert n_rows % row_blk == 0
        return n_rows // row_blk, row_blk, n_rows
    assert row_blk == n_rows
    lane_rows = -(-n_rows // LANES) * LANES
    return 1, lane_rows, lane_rows


def _inproj_call(x3, mod_all, streams, norm1_g, w_in_b, *, n_ph, row_blk):
    rows, t, _ = x3.shape
    n_rb, lane_blk, lane_rows = _phase_blocks(rows, row_blk)
    kern = functools.partial(_inproj_kernel, n_ph=n_ph, rows=row_blk, lane_rows=lane_blk,
                             streams=streams)
    return pl.pallas_call(
        kern,
        out_shape=(
            jax.ShapeDtypeStruct((S5_GROUPS, t, S5_GROUP, lane_rows), BF16),
            jax.ShapeDtypeStruct((rows, t, 2 * GM_WIDTH), F32),
        ),
        grid=(n_rb, t // n_ph),
        in_specs=[
            pl.BlockSpec((row_blk, n_ph, D_MODEL), lambda i, j: (i, j, 0)),
            _const_spec(mod_all.shape),
            _const_spec((1, D_MODEL)),
            _const_spec((D_MODEL, IN_WIDTH)),
        ],
        out_specs=(
            pl.BlockSpec((S5_GROUPS, n_ph, S5_GROUP, lane_blk), lambda i, j: (0, j, 0, i)),
            pl.BlockSpec((row_blk, n_ph, 2 * GM_WIDTH), lambda i, j: (i, j, 0)),
        ),
        scratch_shapes=[pltpu.VMEM((D_MODEL // LANES, row_blk * n_ph, LANES), F32)],
        compiler_params=_cparams(2),
        name="inproj",
    )(x3, mod_all, norm1_g, w_in_b)


def _s5_kernel(*refs, n_sub, lanes, scan, n_cast):
    refs, riders = _split_riders(refs, n_in=5 if scan else 6, n_out=2, n_cast=n_cast)
    if scan:
        zt_ref, wk_ref, qm_ref, tabr_ref, tabi_ref, yt_ref, sf_ref, ybuf, lbuf = refs
        s0_ref = None
    else:
        zt_ref, wk_ref, qm_ref, tabr_ref, tabi_ref, s0_ref, yt_ref, sf_ref, ybuf, lbuf = refs
    groups = range(zt_ref.shape[0])
    n = S5_STATE
    width = SUB * S5_GROUP

    widen = lambda tile: jnp.concatenate([tile] * (lanes // LANES), axis=1)
    sub_pows = []
    for g in groups:
        a1 = (widen(tabr_ref[g]), widen(tabi_ref[g]))
        pows = [a1]
        for _ in range(n_sub - 1):
            pows.append(_cmul(*a1, *pows[-1]))
        sub_pows.append(pows)

    local = []
    for g in groups:
        wk = wk_ref[g]
        lr = li = None
        for j in range(n_sub):
            u = zt_ref[g, j * width:(j + 1) * width, :]
            r = jnp.dot(wk, u, preferred_element_type=F32)
            ybuf[g, j * width:(j + 1) * width, :] = r[:width]
            wr = r[width:width + n]
            wi = r[width + n:]
            if j == 0:
                lr, li = wr, wi
            else:
                tr, ti = _cmul(*sub_pows[g][0], lr, li)
                lr, li = tr + wr, ti + wi
            lbuf[g, j, :n, :] = lr
            lbuf[g, j, n:, :] = li
        local.append((lr, li))

    if scan:
        lane = lax.broadcasted_iota(jnp.int32, (n, lanes), 1)
        xs = list(local)
        ms = [sub_pows[g][n_sub - 1] for g in groups]
        for i in range(int(math.log2(lanes))):
            sh = 1 << i
            for g in groups:
                xr, xi = xs[g]
                rr = jnp.where(lane >= sh, pltpu.roll(xr, sh, 1), 0.0)
                ri = jnp.where(lane >= sh, pltpu.roll(xi, sh, 1), 0.0)
                tr, ti = _cmul(*ms[g], rr, ri)
                xs[g] = (xr + tr, xi + ti)
                ms[g] = _cmul(*ms[g], *ms[g])
        entering = [(jnp.where(lane >= 1, pltpu.roll(xr, 1, 1), 0.0),
                     jnp.where(lane >= 1, pltpu.roll(xi, 1, 1), 0.0)) for xr, xi in xs]
    else:
        entering = [(s0_ref[g, :n, :], s0_ref[g, n:, :]) for g in groups]

    keep = slice(lanes - LANES, lanes)
    for g in groups:
        qm = qm_ref[g]
        sr, si = entering[g]
        for j in range(n_sub):
            if j == 0:
                pr, pi = sr, si
            else:
                tr, ti = _cmul(*sub_pows[g][j - 1], sr, si)
                pr, pi = lbuf[g, j - 1, :n, :] + tr, lbuf[g, j - 1, n:, :] + ti
            sp = jnp.concatenate([pr, pi], axis=0).astype(BF16)
            y = ybuf[g, j * width:(j + 1) * width, :] + jnp.dot(qm, sp, preferred_element_type=F32)
            yt_ref[g, j * width:(j + 1) * width, :] = y.astype(BF16)
        tr, ti = _cmul(*sub_pows[g][n_sub - 1], sr, si)
        sf_ref[g, :n, :] = lbuf[g, n_sub - 1, :n, keep] + tr[:, keep]
        sf_ref[g, n:, :] = lbuf[g, n_sub - 1, n:, keep] + ti[:, keep]
    _run_riders(riders)


def _s5_call(zt, wk, qm, tabr, tabi, s0, *, n_sub, scan, cast_riders=()):
    g, rows, lanes = zt.shape
    n = S5_STATE
    width = SUB * S5_GROUP
    assert lanes % LANES == 0 and lanes & (lanes - 1) == 0
    gps = S5_GROUPS_PER_STEP_LONG if scan else S5_GROUPS_PER_STEP_SHORT
    grp = lambda shape: pl.BlockSpec((gps,) + shape, lambda i: (i, 0, 0))
    in_specs = [grp((rows, lanes)), grp((width + 2 * n, width)), grp((width, 2 * n)),
                grp((n, LANES)), grp((n, LANES))]
    args = [zt, wk, qm, tabr, tabi]
    if not scan:
        in_specs.append(grp((2 * n, lanes)))
        args.append(s0)
    rider_specs, rider_shapes = _cast_rider_specs(cast_riders, (g // gps,))
    kern = functools.partial(_s5_kernel, n_sub=n_sub, lanes=lanes, scan=scan,
                             n_cast=len(cast_riders))
    return pl.pallas_call(
        kern,
        out_shape=(jax.ShapeDtypeStruct((g, rows, lanes), BF16),
                   jax.ShapeDtypeStruct((g, 2 * n, LANES), F32), *rider_shapes),
        grid=(g // gps,),
        in_specs=in_specs + rider_specs,
        out_specs=(grp((rows, lanes)), grp((2 * n, LANES)), *rider_specs),
        scratch_shapes=[pltpu.VMEM((gps, rows, lanes), F32),
                        pltpu.VMEM((gps, n_sub, 2 * n, lanes), F32)],
        compiler_params=_cparams(1),
        name="s5",
    )(*args, *cast_riders)


def _glu_kernel(*refs, n_ph, rows, lane_rows, n_cast):
    (yt_ref, w_ref, b_ref, m_ref), riders = _split_riders(refs, n_in=3, n_out=1, n_cast=n_cast)
    _run_riders(riders)
    gs = []
    for ph in range(n_ph):
        yt = yt_ref[:, ph, :, :].astype(F32).reshape(S5_WIDTH, lane_rows)
        gs.append(_gelu(yt.T[:rows]))
    gy = jnp.concatenate(gs, axis=0)
    gate = jnp.dot(gy.astype(BF16), w_ref[...], preferred_element_type=F32) + b_ref[...]
    m = gy * _sigmoid(gate)
    for ph in range(n_ph):
        _oct_store(m_ref, ph, m[ph * rows:(ph + 1) * rows])


def _glu_call(yt4, w_glu_b, b_glu, *, rows, n_ph, row_blk, cast_riders=()):
    g, t, p, lane_rows = yt4.shape
    n_rb, lane_blk, lane_rows_expected = _phase_blocks(rows, row_blk)
    assert lane_rows == lane_rows_expected
    grid = (n_rb, t // n_ph)
    rider_specs, rider_shapes = _cast_rider_specs(cast_riders, grid)
    kern = functools.partial(_glu_kernel, n_ph=n_ph, rows=row_blk, lane_rows=lane_blk,
                             n_cast=len(cast_riders))
    oct_shape = _oct_shape(rows, t, S5_WIDTH)
    return pl.pallas_call(
        kern,
        out_shape=(jax.ShapeDtypeStruct(oct_shape, F32), *rider_shapes),
        grid=grid,
        in_specs=[
            pl.BlockSpec((g, n_ph, p, lane_blk), lambda i, j: (0, j, 0, i)),
            _const_spec((S5_WIDTH, S5_WIDTH)),
            _const_spec((1, S5_WIDTH)),
            *rider_specs,
        ],
        out_specs=(pl.BlockSpec((row_blk // OCT, oct_shape[1], n_ph, OCT, LANES),
                                lambda i, j: (i, 0, j, 0, 0)), *rider_specs),
        compiler_params=_cparams(2),
        name="glu",
    )(yt4, w_glu_b, b_glu, *cast_riders)


def _ff_chunks():
    tiles = D_FF // MXU_DIM
    assert tiles * MXU_DIM == D_FF
    bounds = [MXU_DIM * ((tiles * c + FFN_CHUNKS - 1) // FFN_CHUNKS) for c in range(FFN_CHUNKS + 1)]
    return list(zip(bounds[:-1], bounds[1:]))


def _main_kernel(x_ref, m_ref, zuv_ref, mod_ref, g2_ref, gf_ref, lng_ref, lnb_ref, gw_ref, gbt_ref,
                 wo_ref, wgu_ref, wd_ref, y_ref, *rest, tm, cl, seq, streams):
    *maybe_v_out_ref, v_ref, ygm_ref, attn_ref = rest
    t = tm // OCT
    hd = GM_HEAD_DIM

    attn_ref[...] = jnp.dot(_oct_load(m_ref, t).astype(BF16), wo_ref[:S5_WIDTH, :],
                            preferred_element_type=F32)

    gv = _gelu(zuv_ref[:, GM_WIDTH:])
    cen = gv - jnp.mean(gv, axis=-1, keepdims=True)
    var = jnp.mean(cen * cen, axis=-1, keepdims=True)
    v = cen * lax.rsqrt(var + EPS) * lng_ref[...] + lnb_ref[...]
    for v_out_ref in maybe_v_out_ref:
        v_out_ref[...] = v
    v_ref[...] = v.astype(BF16)

    blk_i = _div_pow2(lax.broadcasted_iota(jnp.int32, (cl, cl), 0), CHUNK)
    blk_j = _div_pow2(lax.broadcasted_iota(jnp.int32, (cl, cl), 1), CHUNK)
    causal = blk_j <= blk_i
    first_head = lax.broadcasted_iota(jnp.int32, (cl, 2 * hd), 1) < hd
    for pr in range(GM_HEADS // 2):
        h0, h1 = 2 * pr, 2 * pr + 1
        wm = jnp.concatenate(
            [jnp.where(causal, gw_ref[h, :cl, :cl], 0.0) for h in (h0, h1)], axis=1).astype(BF16)
        bias = jnp.where(first_head, gbt_ref[:cl, h0:h0 + 1], gbt_ref[:cl, h1:h1 + 1])
        cs = slice(h0 * hd, (h1 + 1) * hd)
        for ci in range(tm // cl):
            rs = slice(ci * cl, (ci + 1) * cl)
            vv = v_ref[rs, cs]
            zero = jnp.zeros_like(vv)
            rhs = jnp.concatenate([jnp.where(first_head, vv, zero), jnp.where(first_head, zero, vv)],
                                  axis=0)
            mixed = jnp.dot(wm, rhs, preferred_element_type=F32) + bias
            ygm_ref[rs, cs] = (_gelu(zuv_ref[rs, cs]) * mixed).astype(BF16)

    attn = attn_ref[...] + jnp.dot(ygm_ref[...], wo_ref[S5_WIDTH:, :], preferred_element_type=F32)

    def mod_rows(idx):
        if len(streams) == 1:
            return _mod_vec(mod_ref, streams[0], idx)
        return jnp.concatenate(
            [jnp.broadcast_to(_mod_vec(mod_ref, s, idx), (seq, D_MODEL)) for s in streams], axis=0)

    gate1, shift2, scale2, gate2 = mod_rows(2), mod_rows(3), mod_rows(4), mod_rows(5)
    x1 = x_ref[...] + gate1 * attn
    h2 = (x1 * _rms_scale(x1) * (g2_ref[...] * (1.0 + scale2)) + shift2).astype(BF16)

    acc = None
    for lo, hi in _ff_chunks():
        gg = jnp.dot(h2, wgu_ref[:, lo:hi], preferred_element_type=F32)
        up = jnp.dot(h2, wgu_ref[:, D_FF + lo:D_FF + hi], preferred_element_type=F32)
        act = (gg * jax.nn.sigmoid(gg) * up).astype(BF16)
        part = jnp.dot(act, wd_ref[lo:hi, :], preferred_element_type=F32)
        acc = part if acc is None else acc + part
    x2 = x1 + gate2 * acc
    y_ref[...] = x2 * _rms_scale(x2) * gf_ref[...]


def _main_sets_kernel(*refs, sets, n_shared):
    shared = refs[:n_shared]
    scratch = refs[-3:]
    step = pl.program_id(0)
    pos = n_shared
    out_pos = n_shared + 3 * len(sets)
    for st in sets:
        ins = refs[pos:pos + 3]
        pos += 3
        n_out = 2 if st["want_v"] else 1
        outs = refs[out_pos:out_pos + n_out]
        out_pos += n_out
        tm = st["tm"]

        @pl.when(jnp.logical_and(step >= st["start"], step < st["start"] + st["n_tiles"]))
        def _(ins=ins, outs=outs, st=st, tm=tm):
            _main_kernel(*ins, *shared, *outs, *[r.at[:tm] for r in scratch],
                         tm=tm, cl=st["cl"], seq=st["seq"], streams=st["streams"])


def _main_call(stream_sets, mod_all, norm2_g, final_g, ln_g, ln_b, gm_w, gm_bt, w_out_b, w_gu_b, w_down_b):
    shared = (mod_all, norm2_g, final_g, ln_g, ln_b, gm_w, gm_bt, w_out_b, w_gu_b, w_down_b)
    set_args, set_in_specs, out_shape, out_specs, statics = [], [], [], [], []
    start = 0
    for st in stream_sets:
        n_tok = st["x2d"].shape[0]
        n_oct, n_lb, t, _, _ = st["m"].shape
        tm = OCT * t
        seq, streams = st["seq"], st["streams"]
        assert n_oct * tm == n_tok
        assert len(streams) == 1 or (n_oct == 1 and len(streams) * seq == tm)

        def tile_index(i, start=start, n_oct=n_oct):
            return (jnp.clip(i - start, 0, n_oct - 1), 0)

        tok = lambda width: pl.BlockSpec((tm, width), tile_index)
        oct_rows = n_lb * t * OCT
        set_args += [st["x2d"], st["m"].reshape(n_oct * oct_rows, LANES), st["zuv"]]
        set_in_specs += [tok(D_MODEL), pl.BlockSpec((oct_rows, LANES), tile_index), tok(2 * GM_WIDTH)]
        out_shape.append(jax.ShapeDtypeStruct((n_tok, D_MODEL), F32))
        out_specs.append(tok(D_MODEL))
        if st["want_v"]:
            out_shape.append(jax.ShapeDtypeStruct((n_tok, GM_WIDTH), F32))
            out_specs.append(tok(GM_WIDTH))
        statics.append(dict(start=start, n_tiles=n_oct, tm=tm, cl=min(GM_CHUNK, seq), seq=seq,
                            streams=streams, want_v=st["want_v"]))
        start += n_oct
    tm_max = max(s["tm"] for s in statics)
    outs = pl.pallas_call(
        functools.partial(_main_sets_kernel, sets=statics, n_shared=len(shared)),
        out_shape=tuple(out_shape),
        grid=(start,),
        in_specs=[_const_spec(a.shape) for a in shared] + set_in_specs,
        out_specs=tuple(out_specs),
        scratch_shapes=[pltpu.VMEM((tm_max, GM_WIDTH), BF16), pltpu.VMEM((tm_max, GM_WIDTH), BF16),
                        pltpu.VMEM((tm_max, D_MODEL), F32)],
        compiler_params=_cparams(1),
        name="main",
    )(*shared, *set_args)
    results, pos = [], 0
    for s in statics:
        n_out = 2 if s["want_v"] else 1
        results.append(tuple(outs[pos:pos + n_out]))
        pos += n_out
    return results


def _mixer_front(x, mod_all, streams, s0, prm, *, t, n_ph, row_blk, want_v, f32_weights=None):
    b, seq, _ = x.shape
    n_chunks = seq // t
    rows = b * n_chunks
    scan = s0 is None
    assert (b == 1) if scan else (n_chunks == 1)
    n_sub = t // SUB

    zt4, zuv = _inproj_call(
        x.reshape(rows, t, D_MODEL), mod_all, streams, prm["norm1_g"], prm["w_in"],
        n_ph=n_ph, row_blk=row_blk)
    lane_rows = zt4.shape[-1]

    if scan:
        s0_l = None
    else:
        s0_l = jnp.transpose(s0, (1, 2, 0))
        s0_l = jnp.pad(s0_l, ((0, 0), (0, 0), (0, lane_rows - rows)))
    pending = dict(f32_weights or {})
    on_glu = {k: pending.pop(k) for k in ("w_gu",) if k in pending}
    yt, sfin, *cast = _s5_call(zt4.reshape(S5_GROUPS, t * S5_GROUP, lane_rows),
                               prm["wk"], prm["qm"], prm["tabr"], prm["tabi"], s0_l,
                               n_sub=n_sub, scan=scan, cast_riders=tuple(pending.values()))
    prm = {**prm, **dict(zip(pending.keys(), cast))}
    m, *cast = _glu_call(yt.reshape(S5_GROUPS, t, S5_GROUP, lane_rows), prm["w_glu"], prm["b_glu"],
                         rows=rows, n_ph=n_ph, row_blk=row_blk, cast_riders=tuple(on_glu.values()))
    prm = {**prm, **dict(zip(on_glu.keys(), cast))}

    n_tok = b * seq
    stream_set = dict(x2d=x.reshape(n_tok, D_MODEL), m=m, zuv=zuv.reshape(n_tok, 2 * GM_WIDTH),
                      streams=streams, seq=seq, want_v=want_v)

    first_kept = lane_rows - LANES
    if scan:
        fin = sfin[:, :, rows - 1 - first_kept][None]
    else:
        assert first_kept == 0
        fin = jnp.transpose(sfin[:, :, :rows], (2, 0, 1))
    return stream_set, fin[..., :S5_STATE], fin[..., S5_STATE:], prm


def kernel(x_prompt, x_sample, state_s5_re, state_s5_im, c_prompt, c_sample, norm1_g, norm2_g, w_ada, b_ada, w_in, s5_lambda_re, s5_lambda_im, s5_log_step, s5_b_re, s5_b_im, s5_c_re, s5_c_im, s5_d, s5_w_glu, s5_b_glu, gm_ln_g, gm_ln_b, gm_w_s, gm_b_s, w_out, ffn_w_gu, ffn_w_down, final_g):
    depth = w_in.shape[0]
    assert depth == 1
    l = 0
    n_p = c_prompt.shape[0]
    n_s = c_sample.shape[0]

    c_all = jnp.concatenate([c_prompt, c_sample], axis=0)
    c_pad = jnp.pad(c_all, ((0, -c_all.shape[0] % SUBLANES), (0, 0)))
    mod_all = _ada_call(c_pad, w_ada[l], b_ada[l][None, :])
    streams_p = tuple(range(n_p))
    streams_s = tuple(range(n_p, n_p + n_s))

    wk, qm, tabr, tabi, w_in_b = _s5_prep_call(
        s5_lambda_re[l], s5_lambda_im[l], s5_log_step[l][None, :], s5_b_re[l], s5_b_im[l],
        s5_c_re[l], s5_c_im[l], s5_d[l], cast_riders=(w_in[l],))

    prm = dict(
        norm1_g=norm1_g[l][None, :], norm2_g=norm2_g[l][None, :], final_g=final_g[None, :],
        w_in=w_in_b, ln_g=gm_ln_g[l][None, :], ln_b=gm_ln_b[l][None, :],
        wk=wk, qm=qm, tabr=tabr, tabi=tabi, b_glu=s5_b_glu[l][None, :],
        gm_w=gm_w_s[l], gm_bt=jnp.transpose(gm_b_s[l]),
    )
    later_weights = dict(w_glu=s5_w_glu[l], w_out=w_out[l], w_gu=ffn_w_gu[l], w_down=ffn_w_down[l])

    set_p, pre, pim, prm = _mixer_front(
        x_prompt, mod_all, streams_p, None, prm, f32_weights=later_weights,
        t=S5_LONG_SUBS * SUB, n_ph=SUBLANES, row_blk=LANES, want_v=False)
    s0 = jnp.concatenate([state_s5_re[l], state_s5_im[l]], axis=-1)
    n_b, seq_s, _ = x_sample.shape
    set_s, sre, sim, _ = _mixer_front(
        x_sample, mod_all, streams_s, s0, prm, t=seq_s, n_ph=seq_s, row_blk=n_b, want_v=True)

    (yp,), (ys, vs) = _main_call(
        [set_p, set_s], mod_all, prm["norm2_g"], prm["final_g"], prm["ln_g"], prm["ln_b"],
        prm["gm_w"], prm["gm_bt"], prm["w_out"], prm["w_gu"], prm["w_down"])
    return (yp.reshape(x_prompt.shape), ys.reshape(x_sample.shape), pre[None], pim[None],
            sre[None], sim[None], vs.reshape(n_b, seq_s, GM_WIDTH)[None])
```

```python
import functools
import math

import jax
import jax.numpy as jnp
from jax import lax
from jax.experimental import pallas as pl
from jax.experimental.pallas import tpu as pltpu

D_MODEL = 1024
S5_WIDTH = 512
S5_GROUP = 16
S5_GROUPS = 32
S5_STATE = 64
GM_WIDTH = 512
GM_CHUNK = 128
GM_HEADS = 8
GM_HEAD_DIM = 64
CHUNK = 64
IN_WIDTH = S5_WIDTH + 2 * GM_WIDTH
D_FF = 2816
EPS = 1e-6

LANES = 128
SUBLANES = 8
BF16_SUBLANES = 16
MXU_DIM = 256
VMEM_LIMIT_BYTES = 56 * 1024 * 1024

SUB = MXU_DIM // S5_GROUP
S5_LONG_SUBS = 4
PREP_GROUPS_PER_STEP = 8
S5_GROUPS_PER_STEP_LONG = 4
S5_GROUPS_PER_STEP_SHORT = 8
ADA_K_BLOCK = 256
FFN_CHUNKS = 2

F32 = jnp.float32
BF16 = jnp.bfloat16


def _cparams(n_grid_axes):
    return pltpu.CompilerParams(
        dimension_semantics=("arbitrary",) * n_grid_axes,
        vmem_limit_bytes=VMEM_LIMIT_BYTES,
    )


def _const_spec(shape):
    nd = len(shape)
    return pl.BlockSpec(shape, lambda *_: (0,) * nd, pipeline_mode=pl.Buffered(1))


def _rms_scale(x):
    return lax.rsqrt(jnp.mean(x * x, axis=-1, keepdims=True) + EPS)


_GELU_C0 = math.sqrt(2.0 / math.pi)
_GELU_C1 = 0.044715 * _GELU_C0


def _gelu(x):
    hx = 0.5 * x
    return hx + hx * jnp.tanh(x * (_GELU_C0 + _GELU_C1 * (x * x)))


def _sigmoid(x):
    return 0.5 * jnp.tanh(0.5 * x) + 0.5


def _cmul(ar, ai, xr, xi):
    return ar * xr - ai * xi, ar * xi + ai * xr


def _div_pow2(idx, divisor):
    shift = divisor.bit_length() - 1
    assert divisor == 1 << shift
    return lax.shift_right_logical(idx, shift)


def _mod_pow2(idx, divisor):
    assert divisor & (divisor - 1) == 0
    return lax.bitwise_and(idx, divisor - 1)


def _dot_split(x, y):
    x_hi, y_hi = x.astype(BF16), y.astype(BF16)
    x_lo = (x - x_hi.astype(F32)).astype(BF16)
    y_lo = (y - y_hi.astype(F32)).astype(BF16)
    dot = functools.partial(jnp.dot, preferred_element_type=F32)
    return dot(x_hi, y_hi) + (dot(x_lo, y_hi) + dot(x_hi, y_lo))


OCT = SUBLANES


def _oct_shape(n_rows, t, width):
    assert n_rows % OCT == 0 and width % LANES == 0
    return (n_rows // OCT, width // LANES, t, OCT, LANES)


def _oct_store(ref, ph, val):
    n_oct, n_lb = ref.shape[0], ref.shape[1]
    for lb in range(n_lb):
        ref[:, lb, ph, :, :] = val[:, lb * LANES:(lb + 1) * LANES].reshape(n_oct, OCT, LANES)


def _oct_load(ref, t):
    n_lb = ref.shape[0] // (t * OCT)
    chunks = []
    for c in range(OCT):
        chunks.append(jnp.concatenate(
            [ref[pl.ds(lb * t * OCT + c, t, stride=OCT), :] for lb in range(n_lb)], axis=1))
    return jnp.concatenate(chunks, axis=0)


def _cast_rider_specs(arrays, grid):
    n_steps = math.prod(grid)

    def row_block(*idx):
        step = 0
        for i, extent in zip(idx, grid):
            step = step * extent + i
        return (step, 0)

    specs, shapes = [], []
    for a in arrays:
        rows, cols = a.shape
        blk = rows // n_steps
        assert blk * n_steps == rows and blk % BF16_SUBLANES == 0
        specs.append(pl.BlockSpec((blk, cols), row_block))
        shapes.append(jax.ShapeDtypeStruct((rows, cols), BF16))
    return specs, shapes


def _split_riders(refs, n_in, n_out, n_cast):
    ins, refs = refs[:n_in], refs[n_in:]
    cast_in, refs = refs[:n_cast], refs[n_cast:]
    outs, refs = refs[:n_out], refs[n_out:]
    cast_out, scratch = refs[:n_cast], refs[n_cast:]
    return (*ins, *outs, *scratch), list(zip(cast_in, cast_out))


def _run_riders(pairs):
    for src, dst in pairs:
        dst[...] = src[...].astype(BF16)


def _ada_kernel(c_ref, w_ref, b_ref, o_ref):
    @pl.when(pl.program_id(0) == 0)
    def _():
        o_ref[...] = jnp.broadcast_to(b_ref[...], o_ref.shape)

    c = c_ref[...]
    o_ref[...] += _dot_split(c * jax.nn.sigmoid(c), w_ref[...])


def _ada_call(c_pad, w_ada, b_ada):
    rows = c_pad.shape[0]
    n_in, n_out = w_ada.shape
    bk = ADA_K_BLOCK
    return pl.pallas_call(
        _ada_kernel,
        out_shape=jax.ShapeDtypeStruct((rows, n_out), F32),
        grid=(n_in // bk,),
        in_specs=[
            pl.BlockSpec((rows, bk), lambda k: (0, k)),
            pl.BlockSpec((bk, n_out), lambda k: (k, 0)),
            pl.BlockSpec((1, n_out), lambda k: (0, 0)),
        ],
        out_specs=pl.BlockSpec((rows, n_out), lambda k: (0, 0)),
        compiler_params=_cparams(1),
        name="ada",
    )(c_pad, w_ada, b_ada)


def _discretise(lr, li, ls):
    step = jnp.exp(ls)
    mag = jnp.exp(lr * step)
    ar = mag * jnp.cos(li * step)
    ai = mag * jnp.sin(li * step)
    den = lr * lr + li * li
    fr = ((ar - 1.0) * lr + ai * li) / den
    fi = (ai * lr - (ar - 1.0) * li) / den
    return ar, ai, fr, fi


def _selection(rows, cols, row_of_col):
    r = lax.broadcasted_iota(jnp.int32, (rows, cols), 0)
    c = lax.broadcasted_iota(jnp.int32, (rows, cols), 1)
    return jnp.where(r == row_of_col(c), 1.0, 0.0).astype(BF16)


def _place(x, sel):
    hi = x.astype(BF16)
    rest = x - hi.astype(F32)
    mid = rest.astype(BF16)
    lo = (rest - mid.astype(F32)).astype(BF16)
    dot = functools.partial(jnp.dot, preferred_element_type=F32)
    return dot(hi, sel) + (dot(mid, sel) + dot(lo, sel))


def _to_column(row):
    k = row.shape[1]
    r = lax.broadcasted_iota(jnp.int32, (k, k), 0)
    c = lax.broadcasted_iota(jnp.int32, (k, k), 1)
    return jnp.sum(jnp.where(r == c, jnp.broadcast_to(row, (k, k)), 0.0), axis=1, keepdims=True)


def _s5_prep_kernel(*refs, n_cast):
    refs, riders = _split_riders(refs, n_in=8, n_out=4, n_cast=n_cast)
    lam_re_ref, lam_im_ref, ls_ref, d_ref, *grouped = refs
    n, sub, p = S5_STATE, SUB, S5_GROUP
    width = sub * p

    def col_source(c):
        return jnp.where(c < width, (sub - 1) - _div_pow2(c, p), jnp.where(c < 2 * width, sub + 1, sub))

    sels = dict(
        twice=_selection(n, 2 * n, lambda c: _mod_pow2(c, n)),
        tile=_selection(p, width, lambda c: _mod_pow2(c, p)),
        cols=_selection(2 * n, 2 * width + LANES, col_source),
    )
    gps = grouped[0].shape[0]
    pending = [_s5_prep_group(pl.program_id(0) * gps + gi, sels, lam_re_ref, lam_im_ref, ls_ref, d_ref,
                              *[r.at[gi] for r in grouped]) for gi in range(gps)]
    while pending:
        pending = [gen for gen in pending if next(gen, "done") != "done"]
    _run_riders(riders)


def _s5_prep_group(g, sels, lam_re_ref, lam_im_ref, ls_ref, d_ref, b_re_ref, b_im_ref, c_re_ref, c_im_ref,
                   wk_ref, qm_ref, tabr_ref, tabi_ref):
    n, sub, p = S5_STATE, SUB, S5_GROUP
    width = sub * p

    lr_row = lam_re_ref[pl.ds(g, 1), :]
    li_row = lam_im_ref[pl.ds(g, 1), :]
    ls_all = ls_ref[...]
    grp_lane = lax.broadcasted_iota(jnp.int32, ls_all.shape, 1)
    ls = jnp.sum(jnp.where(grp_lane == g, ls_all, 0.0), axis=1, keepdims=True)

    ar8, ai8, fr8, fi8 = _discretise(jnp.broadcast_to(lr_row, (SUBLANES, n)),
                                     jnp.broadcast_to(li_row, (SUBLANES, n)), ls)
    twice = lambda t8: _place(jnp.concatenate([t8] * (p // SUBLANES), axis=0), sels["twice"])
    a2r, a2i = twice(ar8), twice(ai8)
    first = lax.broadcasted_iota(jnp.int32, (p, 2 * n), 1) < n
    c2r = _place(c_re_ref[...], sels["twice"])
    c2i = _place(c_im_ref[...], sels["twice"])
    pr = jnp.ones_like(a2r)
    pi = jnp.zeros_like(a2r)
    ccat = []
    tbl_rows = SUBLANES * (-(-(sub + 2) // SUBLANES))
    tbl_row = lax.broadcasted_iota(jnp.int32, (tbl_rows, 2 * n), 0)
    as_tbl_row = lambda re2, im2: jnp.concatenate([jnp.where(first, re2, im2)] * (tbl_rows // p + 1),
                                                  axis=0)[:tbl_rows]
    tbl = jnp.zeros((tbl_rows, 2 * n), F32)
    for d in range(sub + 1):
        ccat.append(c2r * jnp.where(first, pr, -pi) + c2i * jnp.where(first, -pi, -pr))
        tbl = jnp.where(tbl_row == d, as_tbl_row(pr, pi), tbl)
        pr, pi = _cmul(a2r, a2i, pr, pi)
    tbl = jnp.where(tbl_row == sub + 1, as_tbl_row(twice(fr8), twice(fi8)), tbl)
    yield
    qm_ref[...] = jnp.concatenate(ccat[1:], axis=0).astype(BF16)
    rcat = jnp.concatenate(ccat[:sub], axis=0)

    tbl_t = jnp.concatenate([tbl, jnp.zeros((2 * n - tbl_rows, 2 * n), F32)], axis=0).T
    cols = _place(tbl_t, sels["cols"])
    yield
    apr, fr, a16r = cols[:n, :width], cols[:n, width:2 * width], cols[:n, 2 * width:]
    api, fi, a16i = cols[n:, :width], cols[n:, width:2 * width], cols[n:, 2 * width:]
    btr = _place(b_re_ref[...], sels["tile"])
    bti = _place(b_im_ref[...], sels["tile"])
    bbr, bbi = _cmul(fr, fi, btr, bti)
    pmr, pmi = _cmul(apr, api, bbr, bbi)
    wk_ref[width:width + n, :] = pmr.astype(BF16)
    wk_ref[width + n:, :] = pmi.astype(BF16)
    yield

    bbcat = jnp.concatenate([bbr, bbi], axis=0)
    kt = _dot_split(rcat, bbcat)
    yield
    d_col = _to_column(d_ref[pl.ds(g, 1), :])
    row_p = lax.broadcasted_iota(jnp.int32, (p, width), 0)
    lane_p = lax.broadcasted_iota(jnp.int32, (p, width), 1)
    d_diag = jnp.where(row_p == _mod_pow2(lane_p, p), d_col, 0.0)
    kt = jnp.concatenate([kt[:p] + d_diag, kt[p:]], axis=0)
    col_blk = _div_pow2(lax.broadcasted_iota(jnp.int32, (width, width), 1), p)
    m16 = jnp.zeros((width, width), F32)
    for k in range(sub):
        if k == 0:
            shifted = kt
        else:
            shifted = jnp.concatenate(
                [jnp.zeros((k * p, width), F32), kt[:width - k * p]], axis=0)
        m16 = jnp.where(col_blk == k, shifted, m16)
    wk_ref[:width, :] = m16.astype(BF16)

    tabr_ref[...] = a16r
    tabi_ref[...] = a16i


def _s5_prep_call(lam_re, lam_im, log_step, b_re, b_im, c_re, c_im, d, cast_riders=()):
    g, n, p, sub = S5_GROUPS, S5_STATE, S5_GROUP, SUB
    width = sub * p
    grp = lambda shape: pl.BlockSpec((PREP_GROUPS_PER_STEP,) + shape, lambda i: (i, 0, 0))
    whole = lambda a: pl.BlockSpec(a.shape, lambda i: (0,) * a.ndim)
    n_steps = g // PREP_GROUPS_PER_STEP
    rider_specs, rider_shapes = _cast_rider_specs(cast_riders, (n_steps,))
    return pl.pallas_call(
        functools.partial(_s5_prep_kernel, n_cast=len(cast_riders)),
        out_shape=(
            jax.ShapeDtypeStruct((g, width + 2 * n, width), BF16),
            jax.ShapeDtypeStruct((g, width, 2 * n), BF16),
            jax.ShapeDtypeStruct((g, n, LANES), F32),
            jax.ShapeDtypeStruct((g, n, LANES), F32),
            *rider_shapes,
        ),
        grid=(n_steps,),
        in_specs=[whole(lam_re), whole(lam_im), whole(log_step), whole(d)]
        + [grp((n, p))] * 2 + [grp((p, n))] * 2 + rider_specs,
        out_specs=(grp((width + 2 * n, width)), grp((width, 2 * n)),
                   grp((n, LANES)), grp((n, LANES)), *rider_specs),
        compiler_params=_cparams(1),
        name="s5_prep",
    )(lam_re, lam_im, log_step, d, b_re, b_im, c_re, c_im, *cast_riders)


def _mod_vec(mod_ref, stream, idx):
    return mod_ref[stream:stream + 1, idx * D_MODEL:(idx + 1) * D_MODEL]


def _inproj_kernel(x_ref, mod_ref, g1_ref, w_ref, zt_ref, zuv_ref, hs_ref,
                   *, n_ph, rows, lane_rows, streams):
    m = rows * n_ph
    assert len(streams) in (1, rows)
    shift, scale = (jnp.concatenate([_mod_vec(mod_ref, s, idx) for s in streams], axis=0)
                    for idx in (0, 1))
    gain = (g1_ref[...] * (1.0 + scale))[:, None, :]
    shift = shift[:, None, :]
    x3 = x_ref[...]
    h = (x3 * _rms_scale(x3) * gain + shift).reshape(m, D_MODEL)
    zuv = jnp.dot(h.astype(BF16), w_ref[:, S5_WIDTH:], preferred_element_type=F32)
    zuv_ref[...] = zuv.reshape(rows, n_ph, 2 * GM_WIDTH)

    n_lb = D_MODEL // LANES
    for lb in range(n_lb):
        hs_ref[lb] = h[:, lb * LANES:(lb + 1) * LANES]
    hp = jnp.concatenate(
        [jnp.concatenate([hs_ref[lb, pl.ds(ph, rows, stride=n_ph), :] for lb in range(n_lb)], axis=1)
         for ph in range(n_ph)], axis=0).astype(BF16)
    z5 = jnp.dot(hp, w_ref[:, :S5_WIDTH], preferred_element_type=F32)
    for ph in range(n_ph):
        zz = z5[ph * rows:(ph + 1) * rows]
        if lane_rows > rows:
            zz = jnp.concatenate([zz, jnp.zeros((lane_rows - rows, S5_WIDTH), F32)], axis=0)
        zt = zz.T.reshape(S5_GROUPS, S5_GROUP, lane_rows)
        zt_ref[:, ph, :, :] = zt.astype(BF16)


def _phase_blocks(n_rows, row_blk):
    if row_blk % LANES == 0:
        assert n_rows % row_blk == 0
        return n_rows // row_blk, row_blk, n_rows
    assert row_blk == n_rows
    lane_rows = -(-n_rows // LANES) * LANES
    return 1, lane_rows, lane_rows


def _inproj_call(x3, mod_all, streams, norm1_g, w_in_b, *, n_ph, row_blk):
    rows, t, _ = x3.shape
    n_rb, lane_blk, lane_rows = _phase_blocks(rows, row_blk)
    kern = functools.partial(_inproj_kernel, n_ph=n_ph, rows=row_blk, lane_rows=lane_blk,
                             streams=streams)
    return pl.pallas_call(
        kern,
        out_shape=(
            jax.ShapeDtypeStruct((S5_GROUPS, t, S5_GROUP, lane_rows), BF16),
            jax.ShapeDtypeStruct((rows, t, 2 * GM_WIDTH), F32),
        ),
        grid=(n_rb, t // n_ph),
        in_specs=[
            pl.BlockSpec((row_blk, n_ph, D_MODEL), lambda i, j: (i, j, 0)),
            _const_spec(mod_all.shape),
            _const_spec((1, D_MODEL)),
            _const_spec((D_MODEL, IN_WIDTH)),
        ],
        out_specs=(
            pl.BlockSpec((S5_GROUPS, n_ph, S5_GROUP, lane_blk), lambda i, j: (0, j, 0, i)),
            pl.BlockSpec((row_blk, n_ph, 2 * GM_WIDTH), lambda i, j: (i, j, 0)),
        ),
        scratch_shapes=[pltpu.VMEM((D_MODEL // LANES, row_blk * n_ph, LANES), F32)],
        compiler_params=_cparams(2),
        name="inproj",
    )(x3, mod_all, norm1_g, w_in_b)


def _s5_kernel(*refs, n_sub, lanes, scan, n_cast):
    refs, riders = _split_riders(refs, n_in=5 if scan else 6, n_out=2, n_cast=n_cast)
    if scan:
        zt_ref, wk_ref, qm_ref, tabr_ref, tabi_ref, yt_ref, sf_ref, ybuf, lbuf = refs
        s0_ref = None
    else:
        zt_ref, wk_ref, qm_ref, tabr_ref, tabi_ref, s0_ref, yt_ref, sf_ref, ybuf, lbuf = refs
    groups = range(zt_ref.shape[0])
    n = S5_STATE
    width = SUB * S5_GROUP

    widen = lambda tile: jnp.concatenate([tile] * (lanes // LANES), axis=1)
    sub_pows = []
    for g in groups:
        a1 = (widen(tabr_ref[g]), widen(tabi_ref[g]))
        pows = [a1]
        for _ in range(n_sub - 1):
            pows.append(_cmul(*a1, *pows[-1]))
        sub_pows.append(pows)

    local = []
    for g in groups:
        wk = wk_ref[g]
        lr = li = None
        for j in range(n_sub):
            u = zt_ref[g, j * width:(j + 1) * width, :]
            r = jnp.dot(wk, u, preferred_element_type=F32)
            ybuf[g, j * width:(j + 1) * width, :] = r[:width]
            wr = r[width:width + n]
            wi = r[width + n:]
            if j == 0:
                lr, li = wr, wi
            else:
                tr, ti = _cmul(*sub_pows[g][0], lr, li)
                lr, li = tr + wr, ti + wi
            lbuf[g, j, :n, :] = lr
            lbuf[g, j, n:, :] = li
        local.append((lr, li))

    if scan:
        lane = lax.broadcasted_iota(jnp.int32, (n, lanes), 1)
        xs = list(local)
        ms = [sub_pows[g][n_sub - 1] for g in groups]
        for i in range(int(math.log2(lanes))):
            sh = 1 << i
            for g in groups:
                xr, xi = xs[g]
                rr = jnp.where(lane >= sh, pltpu.roll(xr, sh, 1), 0.0)
                ri = jnp.where(lane >= sh, pltpu.roll(xi, sh, 1), 0.0)
                tr, ti = _cmul(*ms[g], rr, ri)
                xs[g] = (xr + tr, xi + ti)
                ms[g] = _cmul(*ms[g], *ms[g])
        entering = [(jnp.where(lane >= 1, pltpu.roll(xr, 1, 1), 0.0),
                     jnp.where(lane >= 1, pltpu.roll(xi, 1, 1), 0.0)) for xr, xi in xs]
    else:
        entering = [(s0_ref[g, :n, :], s0_ref[g, n:, :]) for g in groups]

    keep = slice(lanes - LANES, lanes)
    for g in groups:
        qm = qm_ref[g]
        sr, si = entering[g]
        for j in range(n_sub):
            if j == 0:
                pr, pi = sr, si
            else:
                tr, ti = _cmul(*sub_pows[g][j - 1], sr, si)
                pr, pi = lbuf[g, j - 1, :n, :] + tr, lbuf[g, j - 1, n:, :] + ti
            sp = jnp.concatenate([pr, pi], axis=0).astype(BF16)
            y = ybuf[g, j * width:(j + 1) * width, :] + jnp.dot(qm, sp, preferred_element_type=F32)
            yt_ref[g, j * width:(j + 1) * width, :] = y.astype(BF16)
        tr, ti = _cmul(*sub_pows[g][n_sub - 1], sr, si)
        sf_ref[g, :n, :] = lbuf[g, n_sub - 1, :n, keep] + tr[:, keep]
        sf_ref[g, n:, :] = lbuf[g, n_sub - 1, n:, keep] + ti[:, keep]
    _run_riders(riders)


def _s5_call(zt, wk, qm, tabr, tabi, s0, *, n_sub, scan, cast_riders=()):
    g, rows, lanes = zt.shape
    n = S5_STATE
    width = SUB * S5_GROUP
    assert lanes % LANES == 0 and lanes & (lanes - 1) == 0
    gps = S5_GROUPS_PER_STEP_LONG if scan else S5_GROUPS_PER_STEP_SHORT
    grp = lambda shape: pl.BlockSpec((gps,) + shape, lambda i: (i, 0, 0))
    in_specs = [grp((rows, lanes)), grp((width + 2 * n, width)), grp((width, 2 * n)),
                grp((n, LANES)), grp((n, LANES))]
    args = [zt, wk, qm, tabr, tabi]
    if not scan:
        in_specs.append(grp((2 * n, lanes)))
        args.append(s0)
    rider_specs, rider_shapes = _cast_rider_specs(cast_riders, (g // gps,))
    kern = functools.partial(_s5_kernel, n_sub=n_sub, lanes=lanes, scan=scan,
                             n_cast=len(cast_riders))
    return pl.pallas_call(
        kern,
        out_shape=(jax.ShapeDtypeStruct((g, rows, lanes), BF16),
                   jax.ShapeDtypeStruct((g, 2 * n, LANES), F32), *rider_shapes),
        grid=(g // gps,),
        in_specs=in_specs + rider_specs,
        out_specs=(grp((rows, lanes)), grp((2 * n, LANES)), *rider_specs),
        scratch_shapes=[pltpu.VMEM((gps, rows, lanes), F32),
                        pltpu.VMEM((gps, n_sub, 2 * n, lanes), F32)],
        compiler_params=_cparams(1),
        name="s5",
    )(*args, *cast_riders)


def _glu_kernel(*refs, n_ph, rows, lane_rows, n_cast):
    (yt_ref, w_ref, b_ref, m_ref), riders = _split_riders(refs, n_in=3, n_out=1, n_cast=n_cast)
    _run_riders(riders)
    gs = []
    for ph in range(n_ph):
        yt = yt_ref[:, ph, :, :].astype(F32).reshape(S5_WIDTH, lane_rows)
        gs.append(_gelu(yt.T[:rows]))
    gy = jnp.concatenate(gs, axis=0)
    gate = jnp.dot(gy.astype(BF16), w_ref[...], preferred_element_type=F32) + b_ref[...]
    m = gy * _sigmoid(gate)
    for ph in range(n_ph):
        _oct_store(m_ref, ph, m[ph * rows:(ph + 1) * rows])


def _glu_call(yt4, w_glu_b, b_glu, *, rows, n_ph, row_blk, cast_riders=()):
    g, t, p, lane_rows = yt4.shape
    n_rb, lane_blk, lane_rows_expected = _phase_blocks(rows, row_blk)
    assert lane_rows == lane_rows_expected
    grid = (n_rb, t // n_ph)
    rider_specs, rider_shapes = _cast_rider_specs(cast_riders, grid)
    kern = functools.partial(_glu_kernel, n_ph=n_ph, rows=row_blk, lane_rows=lane_blk,
                             n_cast=len(cast_riders))
    oct_shape = _oct_shape(rows, t, S5_WIDTH)
    return pl.pallas_call(
        kern,
        out_shape=(jax.ShapeDtypeStruct(oct_shape, F32), *rider_shapes),
        grid=grid,
        in_specs=[
            pl.BlockSpec((g, n_ph, p, lane_blk), lambda i, j: (0, j, 0, i)),
            _const_spec((S5_WIDTH, S5_WIDTH)),
            _const_spec((1, S5_WIDTH)),
            *rider_specs,
        ],
        out_specs=(pl.BlockSpec((row_blk // OCT, oct_shape[1], n_ph, OCT, LANES),
                                lambda i, j: (i, 0, j, 0, 0)), *rider_specs),
        compiler_params=_cparams(2),
        name="glu",
    )(yt4, w_glu_b, b_glu, *cast_riders)


def _ff_chunks():
    tiles = D_FF // MXU_DIM
    assert tiles * MXU_DIM == D_FF
    bounds = [MXU_DIM * ((tiles * c + FFN_CHUNKS - 1) // FFN_CHUNKS) for c in range(FFN_CHUNKS + 1)]
    return list(zip(bounds[:-1], bounds[1:]))


def _main_kernel(x_ref, m_ref, zuv_ref, mod_ref, g2_ref, gf_ref, lng_ref, lnb_ref, gw_ref, gbt_ref,
                 wo_ref, wgu_ref, wd_ref, y_ref, *rest, tm, cl, seq, streams):
    *maybe_v_out_ref, v_ref, ygm_ref, attn_ref = rest
    t = tm // OCT
    hd = GM_HEAD_DIM

    attn_ref[...] = jnp.dot(_oct_load(m_ref, t).astype(BF16), wo_ref[:S5_WIDTH, :],
                            preferred_element_type=F32)

    gv = _gelu(zuv_ref[:, GM_WIDTH:])
    cen = gv - jnp.mean(gv, axis=-1, keepdims=True)
    var = jnp.mean(cen * cen, axis=-1, keepdims=True)
    v = cen * lax.rsqrt(var + EPS) * lng_ref[...] + lnb_ref[...]
    for v_out_ref in maybe_v_out_ref:
        v_out_ref[...] = v
    v_ref[...] = v.astype(BF16)

    blk_i = _div_pow2(lax.broadcasted_iota(jnp.int32, (cl, cl), 0), CHUNK)
    blk_j = _div_pow2(lax.broadcasted_iota(jnp.int32, (cl, cl), 1), CHUNK)
    causal = blk_j <= blk_i
    first_head = lax.broadcasted_iota(jnp.int32, (cl, 2 * hd), 1) < hd
    for pr in range(GM_HEADS // 2):
        h0, h1 = 2 * pr, 2 * pr + 1
        wm = jnp.concatenate(
            [jnp.where(causal, gw_ref[h, :cl, :cl], 0.0) for h in (h0, h1)], axis=1).astype(BF16)
        bias = jnp.where(first_head, gbt_ref[:cl, h0:h0 + 1], gbt_ref[:cl, h1:h1 + 1])
        cs = slice(h0 * hd, (h1 + 1) * hd)
        for ci in range(tm // cl):
            rs = slice(ci * cl, (ci + 1) * cl)
            vv = v_ref[rs, cs]
            zero = jnp.zeros_like(vv)
            rhs = jnp.concatenate([jnp.where(first_head, vv, zero), jnp.where(first_head, zero, vv)],
                                  axis=0)
            mixed = jnp.dot(wm, rhs, preferred_element_type=F32) + bias
            ygm_ref[rs, cs] = (_gelu(zuv_ref[rs, cs]) * mixed).astype(BF16)

    attn = attn_ref[...] + jnp.dot(ygm_ref[...], wo_ref[S5_WIDTH:, :], preferred_element_type=F32)

    def mod_rows(idx):
        if len(streams) == 1:
            return _mod_vec(mod_ref, streams[0], idx)
        return jnp.concatenate(
            [jnp.broadcast_to(_mod_vec(mod_ref, s, idx), (seq, D_MODEL)) for s in streams], axis=0)

    gate1, shift2, scale2, gate2 = mod_rows(2), mod_rows(3), mod_rows(4), mod_rows(5)
    x1 = x_ref[...] + gate1 * attn
    h2 = (x1 * _rms_scale(x1) * (g2_ref[...] * (1.0 + scale2)) + shift2).astype(BF16)

    acc = None
    for lo, hi in _ff_chunks():
        gg = jnp.dot(h2, wgu_ref[:, lo:hi], preferred_element_type=F32)
        up = jnp.dot(h2, wgu_ref[:, D_FF + lo:D_FF + hi], preferred_element_type=F32)
        act = (gg * jax.nn.sigmoid(gg) * up).astype(BF16)
        part = jnp.dot(act, wd_ref[lo:hi, :], preferred_element_type=F32)
        acc = part if acc is None else acc + part
    x2 = x1 + gate2 * acc
    y_ref[...] = x2 * _rms_scale(x2) * gf_ref[...]


def _main_sets_kernel(*refs, sets, n_shared):
    shared = refs[:n_shared]
    scratch = refs[-3:]
    step = pl.program_id(0)
    pos = n_shared
    out_pos = n_shared + 3 * len(sets)
    for st in sets:
        ins = refs[pos:pos + 3]
        pos += 3
        n_out = 2 if st["want_v"] else 1
        outs = refs[out_pos:out_pos + n_out]
        out_pos += n_out
        tm = st["tm"]

        @pl.when(jnp.logical_and(step >= st["start"], step < st["start"] + st["n_tiles"]))
        def _(ins=ins, outs=outs, st=st, tm=tm):
            _main_kernel(*ins, *shared, *outs, *[r.at[:tm] for r in scratch],
                         tm=tm, cl=st["cl"], seq=st["seq"], streams=st["streams"])


def _main_call(stream_sets, mod_all, norm2_g, final_g, ln_g, ln_b, gm_w, gm_bt, w_out_b, w_gu_b, w_down_b):
    shared = (mod_all, norm2_g, final_g, ln_g, ln_b, gm_w, gm_bt, w_out_b, w_gu_b, w_down_b)
    set_args, set_in_specs, out_shape, out_specs, statics = [], [], [], [], []
    start = 0
    for st in stream_sets:
        n_tok = st["x2d"].shape[0]
        n_oct, n_lb, t, _, _ = st["m"].shape
        tm = OCT * t
        seq, streams = st["seq"], st["streams"]
        assert n_oct * tm == n_tok
        assert len(streams) == 1 or (n_oct == 1 and len(streams) * seq == tm)

        def tile_index(i, start=start, n_oct=n_oct):
            return (jnp.clip(i - start, 0, n_oct - 1), 0)

        tok = lambda width: pl.BlockSpec((tm, width), tile_index)
        oct_rows = n_lb * t * OCT
        set_args += [st["x2d"], st["m"].reshape(n_oct * oct_rows, LANES), st["zuv"]]
        set_in_specs += [tok(D_MODEL), pl.BlockSpec((oct_rows, LANES), tile_index), tok(2 * GM_WIDTH)]
        out_shape.append(jax.ShapeDtypeStruct((n_tok, D_MODEL), F32))
        out_specs.append(tok(D_MODEL))
        if st["want_v"]:
            out_shape.append(jax.ShapeDtypeStruct((n_tok, GM_WIDTH), F32))
            out_specs.append(tok(GM_WIDTH))
        statics.append(dict(start=start, n_tiles=n_oct, tm=tm, cl=min(GM_CHUNK, seq), seq=seq,
                            streams=streams, want_v=st["want_v"]))
        start += n_oct
    tm_max = max(s["tm"] for s in statics)
    outs = pl.pallas_call(
        functools.partial(_main_sets_kernel, sets=statics, n_shared=len(shared)),
        out_shape=tuple(out_shape),
        grid=(start,),
        in_specs=[_const_spec(a.shape) for a in shared] + set_in_specs,
        out_specs=tuple(out_specs),
        scratch_shapes=[pltpu.VMEM((tm_max, GM_WIDTH), BF16), pltpu.VMEM((tm_max, GM_WIDTH), BF16),
                        pltpu.VMEM((tm_max, D_MODEL), F32)],
        compiler_params=_cparams(1),
        name="main",
    )(*shared, *set_args)
    results, pos = [], 0
    for s in statics:
        n_out = 2 if s["want_v"] else 1
        results.append(tuple(outs[pos:pos + n_out]))
        pos += n_out
    return results


def _mixer_front(x, mod_all, streams, s0, prm, *, t, n_ph, row_blk, want_v, f32_weights=None):
    b, seq, _ = x.shape
    n_chunks = seq // t
    rows = b * n_chunks
    scan = s0 is None
    assert (b == 1) if scan else (n_chunks == 1)
    n_sub = t // SUB

    zt4, zuv = _inproj_call(
        x.reshape(rows, t, D_MODEL), mod_all, streams, prm["norm1_g"], prm["w_in"],
        n_ph=n_ph, row_blk=row_blk)
    lane_rows = zt4.shape[-1]

    if scan:
        s0_l = None
    else:
        s0_l = jnp.transpose(s0, (1, 2, 0))
        s0_l = jnp.pad(s0_l, ((0, 0), (0, 0), (0, lane_rows - rows)))
    pending = dict(f32_weights or {})
    on_glu = {k: pending.pop(k) for k in ("w_gu",) if k in pending}
    yt, sfin, *cast = _s5_call(zt4.reshape(S5_GROUPS, t * S5_GROUP, lane_rows),
                               prm["wk"], prm["qm"], prm["tabr"], prm["tabi"], s0_l,
                               n_sub=n_sub, scan=scan, cast_riders=tuple(pending.values()))
    prm = {**prm, **dict(zip(pending.keys(), cast))}
    m, *cast = _glu_call(yt.reshape(S5_GROUPS, t, S5_GROUP, lane_rows), prm["w_glu"], prm["b_glu"],
                         rows=rows, n_ph=n_ph, row_blk=row_blk, cast_riders=tuple(on_glu.values()))
    prm = {**prm, **dict(zip(on_glu.keys(), cast))}

    n_tok = b * seq
    stream_set = dict(x2d=x.reshape(n_tok, D_MODEL), m=m, zuv=zuv.reshape(n_tok, 2 * GM_WIDTH),
                      streams=streams, seq=seq, want_v=want_v)

    first_kept = lane_rows - LANES
    if scan:
        fin = sfin[:, :, rows - 1 - first_kept][None]
    else:
        assert first_kept == 0
        fin = jnp.transpose(sfin[:, :, :rows], (2, 0, 1))
    return stream_set, fin[..., :S5_STATE], fin[..., S5_STATE:], prm


def kernel(x_prompt, x_sample, state_s5_re, state_s5_im, c_prompt, c_sample, norm1_g, norm2_g, w_ada, b_ada, w_in, s5_lambda_re, s5_lambda_im, s5_log_step, s5_b_re, s5_b_im, s5_c_re, s5_c_im, s5_d, s5_w_glu, s5_b_glu, gm_ln_g, gm_ln_b, gm_w_s, gm_b_s, w_out, ffn_w_gu, ffn_w_down, final_g):
    depth = w_in.shape[0]
    assert depth == 1
    l = 0
    n_p = c_prompt.shape[0]
    n_s = c_sample.shape[0]

    c_all = jnp.concatenate([c_prompt, c_sample], axis=0)
    c_pad = jnp.pad(c_all, ((0, -c_all.shape[0] % SUBLANES), (0, 0)))
    mod_all = _ada_call(c_pad, w_ada[l], b_ada[l][None, :])
    streams_p = tuple(range(n_p))
    streams_s = tuple(range(n_p, n_p + n_s))

    wk, qm, tabr, tabi, w_in_b, w_out_b, w_down_b = _s5_prep_call(
        s5_lambda_re[l], s5_lambda_im[l], s5_log_step[l][None, :], s5_b_re[l], s5_b_im[l],
        s5_c_re[l], s5_c_im[l], s5_d[l], cast_riders=(w_in[l], w_out[l], ffn_w_down[l]))

    prm = dict(
        norm1_g=norm1_g[l][None, :], norm2_g=norm2_g[l][None, :], final_g=final_g[None, :],
        w_in=w_in_b, w_out=w_out_b, w_down=w_down_b, ln_g=gm_ln_g[l][None, :], ln_b=gm_ln_b[l][None, :],
        wk=wk, qm=qm, tabr=tabr, tabi=tabi, b_glu=s5_b_glu[l][None, :],
        gm_w=gm_w_s[l], gm_bt=jnp.transpose(gm_b_s[l]),
    )
    later_weights = dict(w_glu=s5_w_glu[l], w_gu=ffn_w_gu[l])

    set_p, pre, pim, prm = _mixer_front(
        x_prompt, mod_all, streams_p, None, prm, f32_weights=later_weights,
        t=S5_LONG_SUBS * SUB, n_ph=SUBLANES, row_blk=LANES, want_v=False)
    s0 = jnp.concatenate([state_s5_re[l], state_s5_im[l]], axis=-1)
    n_b, seq_s, _ = x_sample.shape
    set_s, sre, sim, _ = _mixer_front(
        x_sample, mod_all, streams_s, s0, prm, t=seq_s, n_ph=seq_s, row_blk=n_b, want_v=True)

    (yp,), (ys, vs) = _main_call(
        [set_p, set_s], mod_all, prm["norm2_g"], prm["final_g"], prm["ln_g"], prm["ln_b"],
        prm["gm_w"], prm["gm_bt"], prm["w_out"], prm["w_gu"], prm["w_down"])
    return (yp.reshape(x_prompt.shape), ys.reshape(x_sample.shape), pre[None], pim[None],
            sre[None], sim[None], vs.reshape(n_b, seq_s, GM_WIDTH)[None])
```

```python
import functools
import math

import jax
import jax.numpy as jnp
from jax import lax
from jax.experimental import pallas as pl
from jax.experimental.pallas import tpu as pltpu

D_MODEL = 1024
S5_WIDTH = 512
S5_GROUP = 16
S5_GROUPS = 32
S5_STATE = 64
GM_WIDTH = 512
GM_CHUNK = 128
GM_HEADS = 8
GM_HEAD_DIM = 64
CHUNK = 64
IN_WIDTH = S5_WIDTH + 2 * GM_WIDTH
D_FF = 2816
EPS = 1e-6

LANES = 128
SUBLANES = 8
BF16_SUBLANES = 16
MXU_DIM = 256
VMEM_LIMIT_BYTES = 56 * 1024 * 1024

SUB = MXU_DIM // S5_GROUP
S5_LONG_SUBS = 4
PREP_GROUPS_PER_STEP = 8
S5_GROUPS_PER_STEP_LONG = 4
S5_GROUPS_PER_STEP_SHORT = 16
ADA_K_BLOCK = 512
FFN_CHUNKS = 2

F32 = jnp.float32
BF16 = jnp.bfloat16


def _cparams(n_grid_axes):
    return pltpu.CompilerParams(
        dimension_semantics=("arbitrary",) * n_grid_axes,
        vmem_limit_bytes=VMEM_LIMIT_BYTES,
    )


def _const_spec(shape):
    nd = len(shape)
    return pl.BlockSpec(shape, lambda *_: (0,) * nd, pipeline_mode=pl.Buffered(1))


def _rms_scale(x):
    return lax.rsqrt(jnp.mean(x * x, axis=-1, keepdims=True) + EPS)


_GELU_C0 = math.sqrt(2.0 / math.pi)
_GELU_C1 = 0.044715 * _GELU_C0


def _gelu(x):
    hx = 0.5 * x
    return hx + hx * jnp.tanh(x * (_GELU_C0 + _GELU_C1 * (x * x)))


def _sigmoid(x):
    return 0.5 * jnp.tanh(0.5 * x) + 0.5


def _cmul(ar, ai, xr, xi):
    return ar * xr - ai * xi, ar * xi + ai * xr


def _div_pow2(idx, divisor):
    shift = divisor.bit_length() - 1
    assert divisor == 1 << shift
    return lax.shift_right_logical(idx, shift)


def _mod_pow2(idx, divisor):
    assert divisor & (divisor - 1) == 0
    return lax.bitwise_and(idx, divisor - 1)


def _dot_split(x, y):
    x_hi, y_hi = x.astype(BF16), y.astype(BF16)
    x_lo = (x - x_hi.astype(F32)).astype(BF16)
    y_lo = (y - y_hi.astype(F32)).astype(BF16)
    dot = functools.partial(jnp.dot, preferred_element_type=F32)
    return dot(x_hi, y_hi) + (dot(x_lo, y_hi) + dot(x_hi, y_lo))


OCT = SUBLANES


def _oct_shape(n_rows, t, width):
    assert n_rows % OCT == 0 and width % LANES == 0
    return (n_rows // OCT, width // LANES, t, OCT, LANES)


def _oct_store(ref, ph, val):
    n_oct, n_lb = ref.shape[0], ref.shape[1]
    for lb in range(n_lb):
        ref[:, lb, ph, :, :] = val[:, lb * LANES:(lb + 1) * LANES].reshape(n_oct, OCT, LANES)


def _oct_load(ref, t):
    n_lb = ref.shape[0] // (t * OCT)
    chunks = []
    for c in range(OCT):
        chunks.append(jnp.concatenate(
            [ref[pl.ds(lb * t * OCT + c, t, stride=OCT), :] for lb in range(n_lb)], axis=1))
    return jnp.concatenate(chunks, axis=0)


def _cast_rider_specs(arrays, grid):
    n_steps = math.prod(grid)

    def row_block(*idx):
        step = 0
        for i, extent in zip(idx, grid):
            step = step * extent + i
        return (step, 0)

    specs, shapes = [], []
    for a in arrays:
        rows, cols = a.shape
        blk = rows // n_steps
        assert blk * n_steps == rows and blk % BF16_SUBLANES == 0
        specs.append(pl.BlockSpec((blk, cols), row_block))
        shapes.append(jax.ShapeDtypeStruct((rows, cols), BF16))
    return specs, shapes


def _split_riders(refs, n_in, n_out, n_cast):
    ins, refs = refs[:n_in], refs[n_in:]
    cast_in, refs = refs[:n_cast], refs[n_cast:]
    outs, refs = refs[:n_out], refs[n_out:]
    cast_out, scratch = refs[:n_cast], refs[n_cast:]
    return (*ins, *outs, *scratch), list(zip(cast_in, cast_out))


def _run_riders(pairs):
    for src, dst in pairs:
        dst[...] = src[...].astype(BF16)


def _ada_kernel(c_ref, w_ref, b_ref, o_ref):
    @pl.when(pl.program_id(0) == 0)
    def _():
        o_ref[...] = jnp.broadcast_to(b_ref[...], o_ref.shape)

    c = c_ref[...]
    o_ref[...] += _dot_split(c * jax.nn.sigmoid(c), w_ref[...])


def _ada_call(c_pad, w_ada, b_ada):
    rows = c_pad.shape[0]
    n_in, n_out = w_ada.shape
    bk = ADA_K_BLOCK
    return pl.pallas_call(
        _ada_kernel,
        out_shape=jax.ShapeDtypeStruct((rows, n_out), F32),
        grid=(n_in // bk,),
        in_specs=[
            pl.BlockSpec((rows, bk), lambda k: (0, k)),
            pl.BlockSpec((bk, n_out), lambda k: (k, 0)),
            pl.BlockSpec((1, n_out), lambda k: (0, 0)),
        ],
        out_specs=pl.BlockSpec((rows, n_out), lambda k: (0, 0)),
        compiler_params=_cparams(1),
        name="ada",
    )(c_pad, w_ada, b_ada)


def _discretise(lr, li, ls):
    step = jnp.exp(ls)
    mag = jnp.exp(lr * step)
    ar = mag * jnp.cos(li * step)
    ai = mag * jnp.sin(li * step)
    den = lr * lr + li * li
    fr = ((ar - 1.0) * lr + ai * li) / den
    fi = (ai * lr - (ar - 1.0) * li) / den
    return ar, ai, fr, fi


def _selection(rows, cols, row_of_col):
    r = lax.broadcasted_iota(jnp.int32, (rows, cols), 0)
    c = lax.broadcasted_iota(jnp.int32, (rows, cols), 1)
    return jnp.where(r == row_of_col(c), 1.0, 0.0).astype(BF16)


def _place(x, sel):
    hi = x.astype(BF16)
    rest = x - hi.astype(F32)
    mid = rest.astype(BF16)
    lo = (rest - mid.astype(F32)).astype(BF16)
    dot = functools.partial(jnp.dot, preferred_element_type=F32)
    return dot(hi, sel) + (dot(mid, sel) + dot(lo, sel))


def _to_column(row):
    k = row.shape[1]
    r = lax.broadcasted_iota(jnp.int32, (k, k), 0)
    c = lax.broadcasted_iota(jnp.int32, (k, k), 1)
    return jnp.sum(jnp.where(r == c, jnp.broadcast_to(row, (k, k)), 0.0), axis=1, keepdims=True)


def _s5_prep_kernel(*refs, n_cast):
    refs, riders = _split_riders(refs, n_in=8, n_out=4, n_cast=n_cast)
    lam_re_ref, lam_im_ref, ls_ref, d_ref, *grouped = refs
    n, sub, p = S5_STATE, SUB, S5_GROUP
    width = sub * p

    def col_source(c):
        return jnp.where(c < width, (sub - 1) - _div_pow2(c, p), jnp.where(c < 2 * width, sub + 1, sub))

    sels = dict(
        twice=_selection(n, 2 * n, lambda c: _mod_pow2(c, n)),
        tile=_selection(p, width, lambda c: _mod_pow2(c, p)),
        cols=_selection(2 * n, 2 * width + LANES, col_source),
    )
    gps = grouped[0].shape[0]
    pending = [_s5_prep_group(pl.program_id(0) * gps + gi, sels, lam_re_ref, lam_im_ref, ls_ref, d_ref,
                              *[r.at[gi] for r in grouped]) for gi in range(gps)]
    while pending:
        pending = [gen for gen in pending if next(gen, "done") != "done"]
    _run_riders(riders)


def _s5_prep_group(g, sels, lam_re_ref, lam_im_ref, ls_ref, d_ref, b_re_ref, b_im_ref, c_re_ref, c_im_ref,
                   wk_ref, qm_ref, tabr_ref, tabi_ref):
    n, sub, p = S5_STATE, SUB, S5_GROUP
    width = sub * p

    lr_row = lam_re_ref[pl.ds(g, 1), :]
    li_row = lam_im_ref[pl.ds(g, 1), :]
    ls_all = ls_ref[...]
    grp_lane = lax.broadcasted_iota(jnp.int32, ls_all.shape, 1)
    ls = jnp.sum(jnp.where(grp_lane == g, ls_all, 0.0), axis=1, keepdims=True)

    ar8, ai8, fr8, fi8 = _discretise(jnp.broadcast_to(lr_row, (SUBLANES, n)),
                                     jnp.broadcast_to(li_row, (SUBLANES, n)), ls)
    twice = lambda t8: _place(jnp.concatenate([t8] * (p // SUBLANES), axis=0), sels["twice"])
    a2r, a2i = twice(ar8), twice(ai8)
    first = lax.broadcasted_iota(jnp.int32, (p, 2 * n), 1) < n
    c2r = _place(c_re_ref[...], sels["twice"])
    c2i = _place(c_im_ref[...], sels["twice"])
    pr = jnp.ones_like(a2r)
    pi = jnp.zeros_like(a2r)
    ccat = []
    tbl_rows = SUBLANES * (-(-(sub + 2) // SUBLANES))
    tbl_row = lax.broadcasted_iota(jnp.int32, (tbl_rows, 2 * n), 0)
    as_tbl_row = lambda re2, im2: jnp.concatenate([jnp.where(first, re2, im2)] * (tbl_rows // p + 1),
                                                  axis=0)[:tbl_rows]
    tbl = jnp.zeros((tbl_rows, 2 * n), F32)
    for d in range(sub + 1):
        ccat.append(c2r * jnp.where(first, pr, -pi) + c2i * jnp.where(first, -pi, -pr))
        tbl = jnp.where(tbl_row == d, as_tbl_row(pr, pi), tbl)
        pr, pi = _cmul(a2r, a2i, pr, pi)
    tbl = jnp.where(tbl_row == sub + 1, as_tbl_row(twice(fr8), twice(fi8)), tbl)
    yield
    qm_ref[...] = jnp.concatenate(ccat[1:], axis=0).astype(BF16)
    rcat = jnp.concatenate(ccat[:sub], axis=0)

    tbl_t = jnp.concatenate([tbl, jnp.zeros((2 * n - tbl_rows, 2 * n), F32)], axis=0).T
    cols = _place(tbl_t, sels["cols"])
    yield
    apr, fr, a16r = cols[:n, :width], cols[:n, width:2 * width], cols[:n, 2 * width:]
    api, fi, a16i = cols[n:, :width], cols[n:, width:2 * width], cols[n:, 2 * width:]
    btr = _place(b_re_ref[...], sels["tile"])
    bti = _place(b_im_ref[...], sels["tile"])
    bbr, bbi = _cmul(fr, fi, btr, bti)
    pmr, pmi = _cmul(apr, api, bbr, bbi)
    wk_ref[width:width + n, :] = pmr.astype(BF16)
    wk_ref[width + n:, :] = pmi.astype(BF16)
    yield

    bbcat = jnp.concatenate([bbr, bbi], axis=0)
    kt = _dot_split(rcat, bbcat)
    yield
    d_col = _to_column(d_ref[pl.ds(g, 1), :])
    row_p = lax.broadcasted_iota(jnp.int32, (p, width), 0)
    lane_p = lax.broadcasted_iota(jnp.int32, (p, width), 1)
    d_diag = jnp.where(row_p == _mod_pow2(lane_p, p), d_col, 0.0)
    kt = jnp.concatenate([kt[:p] + d_diag, kt[p:]], axis=0)
    col_blk = _div_pow2(lax.broadcasted_iota(jnp.int32, (width, width), 1), p)
    m16 = jnp.zeros((width, width), F32)
    for k in range(sub):
        if k == 0:
            shifted = kt
        else:
            shifted = jnp.concatenate(
                [jnp.zeros((k * p, width), F32), kt[:width - k * p]], axis=0)
        m16 = jnp.where(col_blk == k, shifted, m16)
    wk_ref[:width, :] = m16.astype(BF16)

    tabr_ref[...] = a16r
    tabi_ref[...] = a16i


def _s5_prep_call(lam_re, lam_im, log_step, b_re, b_im, c_re, c_im, d, cast_riders=()):
    g, n, p, sub = S5_GROUPS, S5_STATE, S5_GROUP, SUB
    width = sub * p
    grp = lambda shape: pl.BlockSpec((PREP_GROUPS_PER_STEP,) + shape, lambda i: (i, 0, 0))
    whole = lambda a: pl.BlockSpec(a.shape, lambda i: (0,) * a.ndim)
    n_steps = g // PREP_GROUPS_PER_STEP
    rider_specs, rider_shapes = _cast_rider_specs(cast_riders, (n_steps,))
    return pl.pallas_call(
        functools.partial(_s5_prep_kernel, n_cast=len(cast_riders)),
        out_shape=(
            jax.ShapeDtypeStruct((g, width + 2 * n, width), BF16),
            jax.ShapeDtypeStruct((g, width, 2 * n), BF16),
            jax.ShapeDtypeStruct((g, n, LANES), F32),
            jax.ShapeDtypeStruct((g, n, LANES), F32),
            *rider_shapes,
        ),
        grid=(n_steps,),
        in_specs=[whole(lam_re), whole(lam_im), whole(log_step), whole(d)]
        + [grp((n, p))] * 2 + [grp((p, n))] * 2 + rider_specs,
        out_specs=(grp((width + 2 * n, width)), grp((width, 2 * n)),
                   grp((n, LANES)), grp((n, LANES)), *rider_specs),
        compiler_params=_cparams(1),
        name="s5_prep",
    )(lam_re, lam_im, log_step, d, b_re, b_im, c_re, c_im, *cast_riders)


def _mod_vec(mod_ref, stream, idx):
    return mod_ref[stream:stream + 1, idx * D_MODEL:(idx + 1) * D_MODEL]


def _inproj_kernel(x_ref, mod_ref, g1_ref, w_ref, zt_ref, zuv_ref, hs_ref,
                   *, n_ph, rows, lane_rows, streams):
    m = rows * n_ph
    assert len(streams) in (1, rows)
    shift, scale = (jnp.concatenate([_mod_vec(mod_ref, s, idx) for s in streams], axis=0)
                    for idx in (0, 1))
    gain = (g1_ref[...] * (1.0 + scale))[:, None, :]
    shift = shift[:, None, :]
    x3 = x_ref[...]
    h = (x3 * _rms_scale(x3) * gain + shift).reshape(m, D_MODEL)
    zuv = jnp.dot(h.astype(BF16), w_ref[:, S5_WIDTH:], preferred_element_type=F32)
    zuv_ref[...] = zuv.reshape(rows, n_ph, 2 * GM_WIDTH)

    n_lb = D_MODEL // LANES
    for lb in range(n_lb):
        hs_ref[lb] = h[:, lb * LANES:(lb + 1) * LANES]
    hp = jnp.concatenate(
        [jnp.concatenate([hs_ref[lb, pl.ds(ph, rows, stride=n_ph), :] for lb in range(n_lb)], axis=1)
         for ph in range(n_ph)], axis=0).astype(BF16)
    z5 = jnp.dot(hp, w_ref[:, :S5_WIDTH], preferred_element_type=F32)
    for ph in range(n_ph):
        zz = z5[ph * rows:(ph + 1) * rows]
        if lane_rows > rows:
            zz = jnp.concatenate([zz, jnp.zeros((lane_rows - rows, S5_WIDTH), F32)], axis=0)
        zt = zz.T.reshape(S5_GROUPS, S5_GROUP, lane_rows)
        zt_ref[:, ph, :, :] = zt.astype(BF16)


def _phase_blocks(n_rows, row_blk):
    if row_blk % LANES == 0:
        assert n_rows % row_blk == 0
        return n_rows // row_blk, row_blk, n_rows
    assert row_blk == n_rows
    lane_rows = -(-n_rows // LANES) * LANES
    return 1, lane_rows, lane_rows


def _inproj_call(x3, mod_all, streams, norm1_g, w_in_b, *, n_ph, row_blk):
    rows, t, _ = x3.shape
    n_rb, lane_blk, lane_rows = _phase_blocks(rows, row_blk)
    kern = functools.partial(_inproj_kernel, n_ph=n_ph, rows=row_blk, lane_rows=lane_blk,
                             streams=streams)
    return pl.pallas_call(
        kern,
        out_shape=(
            jax.ShapeDtypeStruct((S5_GROUPS, t, S5_GROUP, lane_rows), BF16),
            jax.ShapeDtypeStruct((rows, t, 2 * GM_WIDTH), F32),
        ),
        grid=(n_rb, t // n_ph),
        in_specs=[
            pl.BlockSpec((row_blk, n_ph, D_MODEL), lambda i, j: (i, j, 0)),
            _const_spec(mod_all.shape),
            _const_spec((1, D_MODEL)),
            _const_spec((D_MODEL, IN_WIDTH)),
        ],
        out_specs=(
            pl.BlockSpec((S5_GROUPS, n_ph, S5_GROUP, lane_blk), lambda i, j: (0, j, 0, i)),
            pl.BlockSpec((row_blk, n_ph, 2 * GM_WIDTH), lambda i, j: (i, j, 0)),
        ),
        scratch_shapes=[pltpu.VMEM((D_MODEL // LANES, row_blk * n_ph, LANES), F32)],
        compiler_params=_cparams(2),
        name="inproj",
    )(x3, mod_all, norm1_g, w_in_b)


def _s5_kernel(*refs, n_sub, lanes, scan, n_cast):
    refs, riders = _split_riders(refs, n_in=5 if scan else 6, n_out=2, n_cast=n_cast)
    if scan:
        zt_ref, wk_ref, qm_ref, tabr_ref, tabi_ref, yt_ref, sf_ref, ybuf, lbuf = refs
        s0_ref = None
    else:
        zt_ref, wk_ref, qm_ref, tabr_ref, tabi_ref, s0_ref, yt_ref, sf_ref, ybuf, lbuf = refs
    groups = range(zt_ref.shape[0])
    n = S5_STATE
    width = SUB * S5_GROUP

    widen = lambda tile: jnp.concatenate([tile] * (lanes // LANES), axis=1)
    sub_pows = []
    for g in groups:
        a1 = (widen(tabr_ref[g]), widen(tabi_ref[g]))
        pows = [a1]
        for _ in range(n_sub - 1):
            pows.append(_cmul(*a1, *pows[-1]))
        sub_pows.append(pows)

    local = []
    for g in groups:
        wk = wk_ref[g]
        lr = li = None
        for j in range(n_sub):
            u = zt_ref[g, j * width:(j + 1) * width, :]
            r = jnp.dot(wk, u, preferred_element_type=F32)
            ybuf[g, j * width:(j + 1) * width, :] = r[:width]
            wr = r[width:width + n]
            wi = r[width + n:]
            if j == 0:
                lr, li = wr, wi
            else:
                tr, ti = _cmul(*sub_pows[g][0], lr, li)
                lr, li = tr + wr, ti + wi
            lbuf[g, j, :n, :] = lr
            lbuf[g, j, n:, :] = li
        local.append((lr, li))

    if scan:
        lane = lax.broadcasted_iota(jnp.int32, (n, lanes), 1)
        xs = list(local)
        ms = [sub_pows[g][n_sub - 1] for g in groups]
        for i in range(int(math.log2(lanes))):
            sh = 1 << i
            for g in groups:
                xr, xi = xs[g]
                rr = jnp.where(lane >= sh, pltpu.roll(xr, sh, 1), 0.0)
                ri = jnp.where(lane >= sh, pltpu.roll(xi, sh, 1), 0.0)
                tr, ti = _cmul(*ms[g], rr, ri)
                xs[g] = (xr + tr, xi + ti)
                ms[g] = _cmul(*ms[g], *ms[g])
        entering = [(jnp.where(lane >= 1, pltpu.roll(xr, 1, 1), 0.0),
                     jnp.where(lane >= 1, pltpu.roll(xi, 1, 1), 0.0)) for xr, xi in xs]
    else:
        entering = [(s0_ref[g, :n, :], s0_ref[g, n:, :]) for g in groups]

    keep = slice(lanes - LANES, lanes)
    for g in groups:
        qm = qm_ref[g]
        sr, si = entering[g]
        for j in range(n_sub):
            if j == 0:
                pr, pi = sr, si
            else:
                tr, ti = _cmul(*sub_pows[g][j - 1], sr, si)
                pr, pi = lbuf[g, j - 1, :n, :] + tr, lbuf[g, j - 1, n:, :] + ti
            sp = jnp.concatenate([pr, pi], axis=0).astype(BF16)
            y = ybuf[g, j * width:(j + 1) * width, :] + jnp.dot(qm, sp, preferred_element_type=F32)
            yt_ref[g, j * width:(j + 1) * width, :] = y.astype(BF16)
        tr, ti = _cmul(*sub_pows[g][n_sub - 1], sr, si)
        sf_ref[g, :n, :] = lbuf[g, n_sub - 1, :n, keep] + tr[:, keep]
        sf_ref[g, n:, :] = lbuf[g, n_sub - 1, n:, keep] + ti[:, keep]
    _run_riders(riders)


def _s5_call(zt, wk, qm, tabr, tabi, s0, *, n_sub, scan, cast_riders=()):
    g, rows, lanes = zt.shape
    n = S5_STATE
    width = SUB * S5_GROUP
    assert lanes % LANES == 0 and lanes & (lanes - 1) == 0
    gps = S5_GROUPS_PER_STEP_LONG if scan else S5_GROUPS_PER_STEP_SHORT
    grp = lambda shape: pl.BlockSpec((gps,) + shape, lambda i: (i, 0, 0))
    in_specs = [grp((rows, lanes)), grp((width + 2 * n, width)), grp((width, 2 * n)),
                grp((n, LANES)), grp((n, LANES))]
    args = [zt, wk, qm, tabr, tabi]
    if not scan:
        in_specs.append(grp((2 * n, lanes)))
        args.append(s0)
    rider_specs, rider_shapes = _cast_rider_specs(cast_riders, (g // gps,))
    kern = functools.partial(_s5_kernel, n_sub=n_sub, lanes=lanes, scan=scan,
                             n_cast=len(cast_riders))
    return pl.pallas_call(
        kern,
        out_shape=(jax.ShapeDtypeStruct((g, rows, lanes), BF16),
                   jax.ShapeDtypeStruct((g, 2 * n, LANES), F32), *rider_shapes),
        grid=(g // gps,),
        in_specs=in_specs + rider_specs,
        out_specs=(grp((rows, lanes)), grp((2 * n, LANES)), *rider_specs),
        scratch_shapes=[pltpu.VMEM((gps, rows, lanes), F32),
                        pltpu.VMEM((gps, n_sub, 2 * n, lanes), F32)],
        compiler_params=_cparams(1),
        name="s5",
    )(*args, *cast_riders)


def _glu_kernel(*refs, n_ph, rows, lane_rows, n_cast):
    (yt_ref, w_ref, b_ref, m_ref), riders = _split_riders(refs, n_in=3, n_out=1, n_cast=n_cast)
    _run_riders(riders)
    gs = []
    for ph in range(n_ph):
        yt = yt_ref[:, ph, :, :].astype(F32).reshape(S5_WIDTH, lane_rows)
        gs.append(_gelu(yt.T[:rows]))
    gy = jnp.concatenate(gs, axis=0)
    gate = jnp.dot(gy.astype(BF16), w_ref[...], preferred_element_type=F32) + b_ref[...]
    m = gy * _sigmoid(gate)
    for ph in range(n_ph):
        _oct_store(m_ref, ph, m[ph * rows:(ph + 1) * rows])


def _glu_call(yt4, w_glu_b, b_glu, *, rows, n_ph, row_blk, cast_riders=()):
    g, t, p, lane_rows = yt4.shape
    n_rb, lane_blk, lane_rows_expected = _phase_blocks(rows, row_blk)
    assert lane_rows == lane_rows_expected
    grid = (n_rb, t // n_ph)
    rider_specs, rider_shapes = _cast_rider_specs(cast_riders, grid)
    kern = functools.partial(_glu_kernel, n_ph=n_ph, rows=row_blk, lane_rows=lane_blk,
                             n_cast=len(cast_riders))
    oct_shape = _oct_shape(rows, t, S5_WIDTH)
    return pl.pallas_call(
        kern,
        out_shape=(jax.ShapeDtypeStruct(oct_shape, F32), *rider_shapes),
        grid=grid,
        in_specs=[
            pl.BlockSpec((g, n_ph, p, lane_blk), lambda i, j: (0, j, 0, i)),
            _const_spec((S5_WIDTH, S5_WIDTH)),
            _const_spec((1, S5_WIDTH)),
            *rider_specs,
        ],
        out_specs=(pl.BlockSpec((row_blk // OCT, oct_shape[1], n_ph, OCT, LANES),
                                lambda i, j: (i, 0, j, 0, 0)), *rider_specs),
        compiler_params=_cparams(2),
        name="glu",
    )(yt4, w_glu_b, b_glu, *cast_riders)


def _ff_chunks():
    tiles = D_FF // MXU_DIM
    assert tiles * MXU_DIM == D_FF
    bounds = [MXU_DIM * ((tiles * c + FFN_CHUNKS - 1) // FFN_CHUNKS) for c in range(FFN_CHUNKS + 1)]
    return list(zip(bounds[:-1], bounds[1:]))


def _main_kernel(x_ref, m_ref, zuv_ref, mod_ref, g2_ref, gf_ref, lng_ref, lnb_ref, gw_ref, gbt_ref,
                 wo_ref, wgu_ref, wd_ref, y_ref, *rest, tm, cl, seq, streams):
    *maybe_v_out_ref, v_ref, ygm_ref, attn_ref = rest
    t = tm // OCT
    hd = GM_HEAD_DIM

    attn_ref[...] = jnp.dot(_oct_load(m_ref, t).astype(BF16), wo_ref[:S5_WIDTH, :],
                            preferred_element_type=F32)

    gv = _gelu(zuv_ref[:, GM_WIDTH:])
    cen = gv - jnp.mean(gv, axis=-1, keepdims=True)
    var = jnp.mean(cen * cen, axis=-1, keepdims=True)
    v = cen * lax.rsqrt(var + EPS) * lng_ref[...] + lnb_ref[...]
    for v_out_ref in maybe_v_out_ref:
        v_out_ref[...] = v
    v_ref[...] = v.astype(BF16)

    blk_i = _div_pow2(lax.broadcasted_iota(jnp.int32, (cl, cl), 0), CHUNK)
    blk_j = _div_pow2(lax.broadcasted_iota(jnp.int32, (cl, cl), 1), CHUNK)
    causal = blk_j <= blk_i
    first_head = lax.broadcasted_iota(jnp.int32, (cl, 2 * hd), 1) < hd
    for pr in range(GM_HEADS // 2):
        h0, h1 = 2 * pr, 2 * pr + 1
        wm = jnp.concatenate(
            [jnp.where(causal, gw_ref[h, :cl, :cl], 0.0) for h in (h0, h1)], axis=1).astype(BF16)
        bias = jnp.where(first_head, gbt_ref[:cl, h0:h0 + 1], gbt_ref[:cl, h1:h1 + 1])
        cs = slice(h0 * hd, (h1 + 1) * hd)
        for ci in range(tm // cl):
            rs = slice(ci * cl, (ci + 1) * cl)
            vv = v_ref[rs, cs]
            zero = jnp.zeros_like(vv)
            rhs = jnp.concatenate([jnp.where(first_head, vv, zero), jnp.where(first_head, zero, vv)],
                                  axis=0)
            mixed = jnp.dot(wm, rhs, preferred_element_type=F32) + bias
            ygm_ref[rs, cs] = (_gelu(zuv_ref[rs, cs]) * mixed).astype(BF16)

    attn = attn_ref[...] + jnp.dot(ygm_ref[...], wo_ref[S5_WIDTH:, :], preferred_element_type=F32)

    def mod_rows(idx):
        if len(streams) == 1:
            return _mod_vec(mod_ref, streams[0], idx)
        return jnp.concatenate(
            [jnp.broadcast_to(_mod_vec(mod_ref, s, idx), (seq, D_MODEL)) for s in streams], axis=0)

    gate1, shift2, scale2, gate2 = mod_rows(2), mod_rows(3), mod_rows(4), mod_rows(5)
    x1 = x_ref[...] + gate1 * attn
    h2 = (x1 * _rms_scale(x1) * (g2_ref[...] * (1.0 + scale2)) + shift2).astype(BF16)

    acc = None
    for lo, hi in _ff_chunks():
        gg = jnp.dot(h2, wgu_ref[:, lo:hi], preferred_element_type=F32)
        up = jnp.dot(h2, wgu_ref[:, D_FF + lo:D_FF + hi], preferred_element_type=F32)
        act = (gg * jax.nn.sigmoid(gg) * up).astype(BF16)
        part = jnp.dot(act, wd_ref[lo:hi, :], preferred_element_type=F32)
        acc = part if acc is None else acc + part
    x2 = x1 + gate2 * acc
    y_ref[...] = x2 * _rms_scale(x2) * gf_ref[...]


def _main_sets_kernel(*refs, sets, n_shared):
    shared = refs[:n_shared]
    scratch = refs[-3:]
    step = pl.program_id(0)
    pos = n_shared
    out_pos = n_shared + 3 * len(sets)
    for st in sets:
        ins = refs[pos:pos + 3]
        pos += 3
        n_out = 2 if st["want_v"] else 1
        outs = refs[out_pos:out_pos + n_out]
        out_pos += n_out
        tm = st["tm"]

        @pl.when(jnp.logical_and(step >= st["start"], step < st["start"] + st["n_tiles"]))
        def _(ins=ins, outs=outs, st=st, tm=tm):
            _main_kernel(*ins, *shared, *outs, *[r.at[:tm] for r in scratch],
                         tm=tm, cl=st["cl"], seq=st["seq"], streams=st["streams"])


def _main_call(stream_sets, mod_all, norm2_g, final_g, ln_g, ln_b, gm_w, gm_bt, w_out_b, w_gu_b, w_down_b):
    shared = (mod_all, norm2_g, final_g, ln_g, ln_b, gm_w, gm_bt, w_out_b, w_gu_b, w_down_b)
    set_args, set_in_specs, out_shape, out_specs, statics = [], [], [], [], []
    start = 0
    for st in stream_sets:
        n_tok = st["x2d"].shape[0]
        n_oct, n_lb, t, _, _ = st["m"].shape
        tm = OCT * t
        seq, streams = st["seq"], st["streams"]
        assert n_oct * tm == n_tok
        assert len(streams) == 1 or (n_oct == 1 and len(streams) * seq == tm)

        def tile_index(i, start=start, n_oct=n_oct):
            return (jnp.clip(i - start, 0, n_oct - 1), 0)

        tok = lambda width: pl.BlockSpec((tm, width), tile_index)
        oct_rows = n_lb * t * OCT
        set_args += [st["x2d"], st["m"].reshape(n_oct * oct_rows, LANES), st["zuv"]]
        set_in_specs += [tok(D_MODEL), pl.BlockSpec((oct_rows, LANES), tile_index), tok(2 * GM_WIDTH)]
        out_shape.append(jax.ShapeDtypeStruct((n_tok, D_MODEL), F32))
        out_specs.append(tok(D_MODEL))
        if st["want_v"]:
            out_shape.append(jax.ShapeDtypeStruct((n_tok, GM_WIDTH), F32))
            out_specs.append(tok(GM_WIDTH))
        statics.append(dict(start=start, n_tiles=n_oct, tm=tm, cl=min(GM_CHUNK, seq), seq=seq,
                            streams=streams, want_v=st["want_v"]))
        start += n_oct
    tm_max = max(s["tm"] for s in statics)
    outs = pl.pallas_call(
        functools.partial(_main_sets_kernel, sets=statics, n_shared=len(shared)),
        out_shape=tuple(out_shape),
        grid=(start,),
        in_specs=[_const_spec(a.shape) for a in shared] + set_in_specs,
        out_specs=tuple(out_specs),
        scratch_shapes=[pltpu.VMEM((tm_max, GM_WIDTH), BF16), pltpu.VMEM((tm_max, GM_WIDTH), BF16),
                        pltpu.VMEM((tm_max, D_MODEL), F32)],
        compiler_params=_cparams(1),
        name="main",
    )(*shared, *set_args)
    results, pos = [], 0
    for s in statics:
        n_out = 2 if s["want_v"] else 1
        results.append(tuple(outs[pos:pos + n_out]))
        pos += n_out
    return results


def _mixer_front(x, mod_all, streams, s0, prm, *, t, n_ph, row_blk, want_v, f32_weights=None):
    b, seq, _ = x.shape
    n_chunks = seq // t
    rows = b * n_chunks
    scan = s0 is None
    assert (b == 1) if scan else (n_chunks == 1)
    n_sub = t // SUB

    zt4, zuv = _inproj_call(
        x.reshape(rows, t, D_MODEL), mod_all, streams, prm["norm1_g"], prm["w_in"],
        n_ph=n_ph, row_blk=row_blk)
    lane_rows = zt4.shape[-1]

    if scan:
        s0_l = None
    else:
        s0_l = jnp.transpose(s0, (1, 2, 0))
        s0_l = jnp.pad(s0_l, ((0, 0), (0, 0), (0, lane_rows - rows)))
    pending = dict(f32_weights or {})
    on_glu = {k: pending.pop(k) for k in ("w_gu",) if k in pending}
    yt, sfin, *cast = _s5_call(zt4.reshape(S5_GROUPS, t * S5_GROUP, lane_rows),
                               prm["wk"], prm["qm"], prm["tabr"], prm["tabi"], s0_l,
                               n_sub=n_sub, scan=scan, cast_riders=tuple(pending.values()))
    prm = {**prm, **dict(zip(pending.keys(), cast))}
    m, *cast = _glu_call(yt.reshape(S5_GROUPS, t, S5_GROUP, lane_rows), prm["w_glu"], prm["b_glu"],
                         rows=rows, n_ph=n_ph, row_blk=row_blk, cast_riders=tuple(on_glu.values()))
    prm = {**prm, **dict(zip(on_glu.keys(), cast))}

    n_tok = b * seq
    stream_set = dict(x2d=x.reshape(n_tok, D_MODEL), m=m, zuv=zuv.reshape(n_tok, 2 * GM_WIDTH),
                      streams=streams, seq=seq, want_v=want_v)

    first_kept = lane_rows - LANES
    if scan:
        fin = sfin[:, :, rows - 1 - first_kept][None]
    else:
        assert first_kept == 0
        fin = jnp.transpose(sfin[:, :, :rows], (2, 0, 1))
    return stream_set, fin[..., :S5_STATE], fin[..., S5_STATE:], prm


def kernel(x_prompt, x_sample, state_s5_re, state_s5_im, c_prompt, c_sample, norm1_g, norm2_g, w_ada, b_ada, w_in, s5_lambda_re, s5_lambda_im, s5_log_step, s5_b_re, s5_b_im, s5_c_re, s5_c_im, s5_d, s5_w_glu, s5_b_glu, gm_ln_g, gm_ln_b, gm_w_s, gm_b_s, w_out, ffn_w_gu, ffn_w_down, final_g):
    depth = w_in.shape[0]
    assert depth == 1
    l = 0
    n_p = c_prompt.shape[0]
    n_s = c_sample.shape[0]

    c_all = jnp.concatenate([c_prompt, c_sample], axis=0)
    c_pad = jnp.pad(c_all, ((0, -c_all.shape[0] % SUBLANES), (0, 0)))
    mod_all = _ada_call(c_pad, w_ada[l], b_ada[l][None, :])
    streams_p = tuple(range(n_p))
    streams_s = tuple(range(n_p, n_p + n_s))

    wk, qm, tabr, tabi, w_in_b, w_out_b, w_down_b = _s5_prep_call(
        s5_lambda_re[l], s5_lambda_im[l], s5_log_step[l][None, :], s5_b_re[l], s5_b_im[l],
        s5_c_re[l], s5_c_im[l], s5_d[l], cast_riders=(w_in[l], w_out[l], ffn_w_down[l]))

    prm = dict(
        norm1_g=norm1_g[l][None, :], norm2_g=norm2_g[l][None, :], final_g=final_g[None, :],
        w_in=w_in_b, w_out=w_out_b, w_down=w_down_b, ln_g=gm_ln_g[l][None, :], ln_b=gm_ln_b[l][None, :],
        wk=wk, qm=qm, tabr=tabr, tabi=tabi, b_glu=s5_b_glu[l][None, :],
        gm_w=gm_w_s[l], gm_bt=jnp.transpose(gm_b_s[l]),
    )
    later_weights = dict(w_glu=s5_w_glu[l], w_gu=ffn_w_gu[l])

    set_p, pre, pim, prm = _mixer_front(
        x_prompt, mod_all, streams_p, None, prm, f32_weights=later_weights,
        t=S5_LONG_SUBS * SUB, n_ph=SUBLANES, row_blk=LANES, want_v=False)
    s0 = jnp.concatenate([state_s5_re[l], state_s5_im[l]], axis=-1)
    n_b, seq_s, _ = x_sample.shape
    set_s, sre, sim, _ = _mixer_front(
        x_sample, mod_all, streams_s, s0, prm, t=seq_s, n_ph=seq_s, row_blk=n_b, want_v=True)

    (yp,), (ys, vs) = _main_call(
        [set_p, set_s], mod_all, prm["norm2_g"], prm["final_g"], prm["ln_g"], prm["ln_b"],
        prm["gm_w"], prm["gm_bt"], prm["w_out"], prm["w_gu"], prm["w_down"])
    return (yp.reshape(x_prompt.shape), ys.reshape(x_sample.shape), pre[None], pim[None],
            sre[None], sim[None], vs.reshape(n_b, seq_s, GM_WIDTH)[None])
```

```python
import functools
import math

import jax
import jax.numpy as jnp
from jax import lax
from jax.experimental import pallas as pl
from jax.experimental.pallas import tpu as pltpu

D_MODEL = 1024
S5_WIDTH = 512
S5_GROUP = 16
S5_GROUPS = 32
S5_STATE = 64
GM_WIDTH = 512
GM_CHUNK = 128
GM_HEADS = 8
GM_HEAD_DIM = 64
CHUNK = 64
IN_WIDTH = S5_WIDTH + 2 * GM_WIDTH
D_FF = 2816
EPS = 1e-6

LANES = 128
SUBLANES = 8
BF16_SUBLANES = 16
MXU_DIM = 256
VMEM_LIMIT_BYTES = 56 * 1024 * 1024

SUB = MXU_DIM // S5_GROUP
S5_LONG_SUBS = 4
PREP_GROUPS_PER_STEP = 8
S5_GROUPS_PER_STEP_LONG = 4
S5_GROUPS_PER_STEP_SHORT = 16
ADA_K_BLOCK = 256

F32 = jnp.float32
BF16 = jnp.bfloat16


def _cparams(n_grid_axes):
    return pltpu.CompilerParams(
        dimension_semantics=("arbitrary",) * n_grid_axes,
        vmem_limit_bytes=VMEM_LIMIT_BYTES,
    )


def _const_spec(shape):
    nd = len(shape)
    return pl.BlockSpec(shape, lambda *_: (0,) * nd, pipeline_mode=pl.Buffered(1))


def _rms_scale(x):
    return lax.rsqrt(jnp.mean(x * x, axis=-1, keepdims=True) + EPS)


_GELU_C0 = math.sqrt(2.0 / math.pi)
_GELU_C1 = 0.044715 * _GELU_C0


def _gelu(x):
    hx = 0.5 * x
    return hx + hx * jnp.tanh(x * (_GELU_C0 + _GELU_C1 * (x * x)))


def _sigmoid(x):
    return 0.5 * jnp.tanh(0.5 * x) + 0.5


def _cmul(ar, ai, xr, xi):
    return ar * xr - ai * xi, ar * xi + ai * xr


def _div_pow2(idx, divisor):
    shift = divisor.bit_length() - 1
    assert divisor == 1 << shift
    return lax.shift_right_logical(idx, shift)


def _mod_pow2(idx, divisor):
    assert divisor & (divisor - 1) == 0
    return lax.bitwise_and(idx, divisor - 1)


def _dot_split(x, y):
    x_hi, y_hi = x.astype(BF16), y.astype(BF16)
    x_lo = (x - x_hi.astype(F32)).astype(BF16)
    y_lo = (y - y_hi.astype(F32)).astype(BF16)
    dot = functools.partial(jnp.dot, preferred_element_type=F32)
    return dot(x_hi, y_hi) + (dot(x_lo, y_hi) + dot(x_hi, y_lo))


OCT = SUBLANES


def _oct_shape(n_rows, t, width):
    assert n_rows % OCT == 0 and width % LANES == 0
    return (n_rows // OCT, width // LANES, t, OCT, LANES)


def _oct_store(ref, ph, val):
    n_oct, n_lb = ref.shape[0], ref.shape[1]
    for lb in range(n_lb):
        ref[:, lb, ph, :, :] = val[:, lb * LANES:(lb + 1) * LANES].reshape(n_oct, OCT, LANES)


def _oct_load(ref, t):
    n_lb = ref.shape[0] // (t * OCT)
    chunks = []
    for c in range(OCT):
        chunks.append(jnp.concatenate(
            [ref[pl.ds(lb * t * OCT + c, t, stride=OCT), :] for lb in range(n_lb)], axis=1))
    return jnp.concatenate(chunks, axis=0)


def _cast_rider_specs(arrays, grid):
    n_steps = math.prod(grid)

    def row_block(*idx):
        step = 0
        for i, extent in zip(idx, grid):
            step = step * extent + i
        return (step, 0)

    specs, shapes = [], []
    for a in arrays:
        rows, cols = a.shape
        blk = rows // n_steps
        assert blk * n_steps == rows and blk % BF16_SUBLANES == 0
        specs.append(pl.BlockSpec((blk, cols), row_block))
        shapes.append(jax.ShapeDtypeStruct((rows, cols), BF16))
    return specs, shapes


def _split_riders(refs, n_in, n_out, n_cast):
    ins, refs = refs[:n_in], refs[n_in:]
    cast_in, refs = refs[:n_cast], refs[n_cast:]
    outs, refs = refs[:n_out], refs[n_out:]
    cast_out, scratch = refs[:n_cast], refs[n_cast:]
    return (*ins, *outs, *scratch), list(zip(cast_in, cast_out))


def _run_riders(pairs):
    for src, dst in pairs:
        dst[...] = src[...].astype(BF16)


def _ada_kernel(c_ref, w_ref, b_ref, o_ref):
    @pl.when(pl.program_id(0) == 0)
    def _():
        o_ref[...] = jnp.broadcast_to(b_ref[...], o_ref.shape)

    c = c_ref[...]
    o_ref[...] += _dot_split(c * jax.nn.sigmoid(c), w_ref[...])


def _ada_call(c_pad, w_ada, b_ada):
    rows = c_pad.shape[0]
    n_in, n_out = w_ada.shape
    bk = ADA_K_BLOCK
    return pl.pallas_call(
        _ada_kernel,
        out_shape=jax.ShapeDtypeStruct((rows, n_out), F32),
        grid=(n_in // bk,),
        in_specs=[
            pl.BlockSpec((rows, bk), lambda k: (0, k)),
            pl.BlockSpec((bk, n_out), lambda k: (k, 0)),
            pl.BlockSpec((1, n_out), lambda k: (0, 0)),
        ],
        out_specs=pl.BlockSpec((rows, n_out), lambda k: (0, 0)),
        compiler_params=_cparams(1),
        name="ada",
    )(c_pad, w_ada, b_ada)


def _discretise(lr, li, ls):
    step = jnp.exp(ls)
    mag = jnp.exp(lr * step)
    ar = mag * jnp.cos(li * step)
    ai = mag * jnp.sin(li * step)
    den = lr * lr + li * li
    fr = ((ar - 1.0) * lr + ai * li) / den
    fi = (ai * lr - (ar - 1.0) * li) / den
    return ar, ai, fr, fi


def _selection(rows, cols, row_of_col):
    r = lax.broadcasted_iota(jnp.int32, (rows, cols), 0)
    c = lax.broadcasted_iota(jnp.int32, (rows, cols), 1)
    return jnp.where(r == row_of_col(c), 1.0, 0.0).astype(BF16)


def _place(x, sel):
    hi = x.astype(BF16)
    rest = x - hi.astype(F32)
    mid = rest.astype(BF16)
    lo = (rest - mid.astype(F32)).astype(BF16)
    dot = functools.partial(jnp.dot, preferred_element_type=F32)
    return dot(hi, sel) + (dot(mid, sel) + dot(lo, sel))


def _to_column(row):
    k = row.shape[1]
    r = lax.broadcasted_iota(jnp.int32, (k, k), 0)
    c = lax.broadcasted_iota(jnp.int32, (k, k), 1)
    return jnp.sum(jnp.where(r == c, jnp.broadcast_to(row, (k, k)), 0.0), axis=1, keepdims=True)


def _s5_prep_kernel(*refs, n_cast):
    refs, riders = _split_riders(refs, n_in=8, n_out=4, n_cast=n_cast)
    lam_re_ref, lam_im_ref, ls_ref, d_ref, *grouped = refs
    n, sub, p = S5_STATE, SUB, S5_GROUP
    width = sub * p

    def col_source(c):
        return jnp.where(c < width, (sub - 1) - _div_pow2(c, p), jnp.where(c < 2 * width, sub + 1, sub))

    sels = dict(
        twice=_selection(n, 2 * n, lambda c: _mod_pow2(c, n)),
        tile=_selection(p, width, lambda c: _mod_pow2(c, p)),
        cols=_selection(2 * n, 2 * width + LANES, col_source),
    )
    gps = grouped[0].shape[0]
    pending = [_s5_prep_group(pl.program_id(0) * gps + gi, sels, lam_re_ref, lam_im_ref, ls_ref, d_ref,
                              *[r.at[gi] for r in grouped]) for gi in range(gps)]
    while pending:
        pending = [gen for gen in pending if next(gen, "done") != "done"]
    _run_riders(riders)


def _s5_prep_group(g, sels, lam_re_ref, lam_im_ref, ls_ref, d_ref, b_re_ref, b_im_ref, c_re_ref, c_im_ref,
                   wk_ref, qm_ref, tabr_ref, tabi_ref):
    n, sub, p = S5_STATE, SUB, S5_GROUP
    width = sub * p

    lr_row = lam_re_ref[pl.ds(g, 1), :]
    li_row = lam_im_ref[pl.ds(g, 1), :]
    ls_all = ls_ref[...]
    grp_lane = lax.broadcasted_iota(jnp.int32, ls_all.shape, 1)
    ls = jnp.sum(jnp.where(grp_lane == g, ls_all, 0.0), axis=1, keepdims=True)

    ar8, ai8, fr8, fi8 = _discretise(jnp.broadcast_to(lr_row, (SUBLANES, n)),
                                     jnp.broadcast_to(li_row, (SUBLANES, n)), ls)
    twice = lambda t8: _place(jnp.concatenate([t8] * (p // SUBLANES), axis=0), sels["twice"])
    a2r, a2i = twice(ar8), twice(ai8)
    first = lax.broadcasted_iota(jnp.int32, (p, 2 * n), 1) < n
    c2r = _place(c_re_ref[...], sels["twice"])
    c2i = _place(c_im_ref[...], sels["twice"])
    pr = jnp.ones_like(a2r)
    pi = jnp.zeros_like(a2r)
    ccat = []
    tbl_rows = SUBLANES * (-(-(sub + 2) // SUBLANES))
    tbl_row = lax.broadcasted_iota(jnp.int32, (tbl_rows, 2 * n), 0)
    as_tbl_row = lambda re2, im2: jnp.concatenate([jnp.where(first, re2, im2)] * (tbl_rows // p + 1),
                                                  axis=0)[:tbl_rows]
    tbl = jnp.zeros((tbl_rows, 2 * n), F32)
    for d in range(sub + 1):
        ccat.append(c2r * jnp.where(first, pr, -pi) + c2i * jnp.where(first, -pi, -pr))
        tbl = jnp.where(tbl_row == d, as_tbl_row(pr, pi), tbl)
        pr, pi = _cmul(a2r, a2i, pr, pi)
    tbl = jnp.where(tbl_row == sub + 1, as_tbl_row(twice(fr8), twice(fi8)), tbl)
    yield
    qm_ref[...] = jnp.concatenate(ccat[1:], axis=0).astype(BF16)
    rcat = jnp.concatenate(ccat[:sub], axis=0)

    tbl_t = jnp.concatenate([tbl, jnp.zeros((2 * n - tbl_rows, 2 * n), F32)], axis=0).T
    cols = _place(tbl_t, sels["cols"])
    yield
    apr, fr, a16r = cols[:n, :width], cols[:n, width:2 * width], cols[:n, 2 * width:]
    api, fi, a16i = cols[n:, :width], cols[n:, width:2 * width], cols[n:, 2 * width:]
    btr = _place(b_re_ref[...], sels["tile"])
    bti = _place(b_im_ref[...], sels["tile"])
    bbr, bbi = _cmul(fr, fi, btr, bti)
    pmr, pmi = _cmul(apr, api, bbr, bbi)
    wk_ref[width:width + n, :] = pmr.astype(BF16)
    wk_ref[width + n:, :] = pmi.astype(BF16)
    yield

    bbcat = jnp.concatenate([bbr, bbi], axis=0)
    kt = _dot_split(rcat, bbcat)
    yield
    d_col = _to_column(d_ref[pl.ds(g, 1), :])
    row_p = lax.broadcasted_iota(jnp.int32, (p, width), 0)
    lane_p = lax.broadcasted_iota(jnp.int32, (p, width), 1)
    d_diag = jnp.where(row_p == _mod_pow2(lane_p, p), d_col, 0.0)
    kt = jnp.concatenate([kt[:p] + d_diag, kt[p:]], axis=0)
    col_blk = _div_pow2(lax.broadcasted_iota(jnp.int32, (width, width), 1), p)
    m16 = jnp.zeros((width, width), F32)
    for k in range(sub):
        if k == 0:
            shifted = kt
        else:
            shifted = jnp.concatenate(
                [jnp.zeros((k * p, width), F32), kt[:width - k * p]], axis=0)
        m16 = jnp.where(col_blk == k, shifted, m16)
    wk_ref[:width, :] = m16.astype(BF16)

    tabr_ref[...] = a16r
    tabi_ref[...] = a16i


def _s5_prep_call(lam_re, lam_im, log_step, b_re, b_im, c_re, c_im, d, cast_riders=()):
    g, n, p, sub = S5_GROUPS, S5_STATE, S5_GROUP, SUB
    width = sub * p
    grp = lambda shape: pl.BlockSpec((PREP_GROUPS_PER_STEP,) + shape, lambda i: (i, 0, 0))
    whole = lambda a: pl.BlockSpec(a.shape, lambda i: (0,) * a.ndim)
    n_steps = g // PREP_GROUPS_PER_STEP
    rider_specs, rider_shapes = _cast_rider_specs(cast_riders, (n_steps,))
    return pl.pallas_call(
        functools.partial(_s5_prep_kernel, n_cast=len(cast_riders)),
        out_shape=(
            jax.ShapeDtypeStruct((g, width + 2 * n, width), BF16),
            jax.ShapeDtypeStruct((g, width, 2 * n), BF16),
            jax.ShapeDtypeStruct((g, n, LANES), F32),
            jax.ShapeDtypeStruct((g, n, LANES), F32),
            *rider_shapes,
        ),
        grid=(n_steps,),
        in_specs=[whole(lam_re), whole(lam_im), whole(log_step), whole(d)]
        + [grp((n, p))] * 2 + [grp((p, n))] * 2 + rider_specs,
        out_specs=(grp((width + 2 * n, width)), grp((width, 2 * n)),
                   grp((n, LANES)), grp((n, LANES)), *rider_specs),
        compiler_params=_cparams(1),
        name="s5_prep",
    )(lam_re, lam_im, log_step, d, b_re, b_im, c_re, c_im, *cast_riders)


def _mod_vec(mod_ref, stream, idx):
    return mod_ref[stream:stream + 1, idx * D_MODEL:(idx + 1) * D_MODEL]


def _inproj_kernel(x_ref, mod_ref, g1_ref, w_ref, zt_ref, zuv_ref, hs_ref,
                   *, n_ph, rows, lane_rows, streams):
    m = rows * n_ph
    assert len(streams) in (1, rows)
    shift, scale = (jnp.concatenate([_mod_vec(mod_ref, s, idx) for s in streams], axis=0)
                    for idx in (0, 1))
    gain = (g1_ref[...] * (1.0 + scale))[:, None, :]
    shift = shift[:, None, :]
    x3 = x_ref[...]
    h = (x3 * _rms_scale(x3) * gain + shift).reshape(m, D_MODEL)
    zuv = jnp.dot(h.astype(BF16), w_ref[:, S5_WIDTH:], preferred_element_type=F32)
    zuv_ref[...] = zuv.reshape(rows, n_ph, 2 * GM_WIDTH)

    n_lb = D_MODEL // LANES
    for lb in range(n_lb):
        hs_ref[lb] = h[:, lb * LANES:(lb + 1) * LANES]
    hp = jnp.concatenate(
        [jnp.concatenate([hs_ref[lb, pl.ds(ph, rows, stride=n_ph), :] for lb in range(n_lb)], axis=1)
         for ph in range(n_ph)], axis=0).astype(BF16)
    z5 = jnp.dot(hp, w_ref[:, :S5_WIDTH], preferred_element_type=F32)
    for ph in range(n_ph):
        zz = z5[ph * rows:(ph + 1) * rows]
        if lane_rows > rows:
            zz = jnp.concatenate([zz, jnp.zeros((lane_rows - rows, S5_WIDTH), F32)], axis=0)
        zt = zz.T.reshape(S5_GROUPS, S5_GROUP, lane_rows)
        zt_ref[:, ph, :, :] = zt.astype(BF16)


def _phase_blocks(n_rows, row_blk):
    if row_blk % LANES == 0:
        assert n_rows % row_blk == 0
        return n_rows // row_blk, row_blk, n_rows
    assert row_blk == n_rows
    lane_rows = -(-n_rows // LANES) * LANES
    return 1, lane_rows, lane_rows


def _inproj_call(x3, mod_all, streams, norm1_g, w_in_b, *, n_ph, row_blk):
    rows, t, _ = x3.shape
    n_rb, lane_blk, lane_rows = _phase_blocks(rows, row_blk)
    kern = functools.partial(_inproj_kernel, n_ph=n_ph, rows=row_blk, lane_rows=lane_blk,
                             streams=streams)
    return pl.pallas_call(
        kern,
        out_shape=(
            jax.ShapeDtypeStruct((S5_GROUPS, t, S5_GROUP, lane_rows), BF16),
            jax.ShapeDtypeStruct((rows, t, 2 * GM_WIDTH), F32),
        ),
        grid=(n_rb, t // n_ph),
        in_specs=[
            pl.BlockSpec((row_blk, n_ph, D_MODEL), lambda i, j: (i, j, 0)),
            _const_spec(mod_all.shape),
            _const_spec((1, D_MODEL)),
            _const_spec((D_MODEL, IN_WIDTH)),
        ],
        out_specs=(
            pl.BlockSpec((S5_GROUPS, n_ph, S5_GROUP, lane_blk), lambda i, j: (0, j, 0, i)),
            pl.BlockSpec((row_blk, n_ph, 2 * GM_WIDTH), lambda i, j: (i, j, 0)),
        ),
        scratch_shapes=[pltpu.VMEM((D_MODEL // LANES, row_blk * n_ph, LANES), F32)],
        compiler_params=_cparams(2),
        name="inproj",
    )(x3, mod_all, norm1_g, w_in_b)


def _s5_kernel(*refs, n_sub, lanes, scan, n_cast):
    refs, riders = _split_riders(refs, n_in=5 if scan else 6, n_out=2, n_cast=n_cast)
    if scan:
        zt_ref, wk_ref, qm_ref, tabr_ref, tabi_ref, yt_ref, sf_ref, ybuf, lbuf = refs
        s0_ref = None
    else:
        zt_ref, wk_ref, qm_ref, tabr_ref, tabi_ref, s0_ref, yt_ref, sf_ref, ybuf, lbuf = refs
    groups = range(zt_ref.shape[0])
    n = S5_STATE
    width = SUB * S5_GROUP

    widen = lambda tile: jnp.concatenate([tile] * (lanes // LANES), axis=1)
    sub_pows = []
    for g in groups:
        a1 = (widen(tabr_ref[g]), widen(tabi_ref[g]))
        pows = [a1]
        for _ in range(n_sub - 1):
            pows.append(_cmul(*a1, *pows[-1]))
        sub_pows.append(pows)

    local = []
    for g in groups:
        wk = wk_ref[g]
        lr = li = None
        for j in range(n_sub):
            u = zt_ref[g, j * width:(j + 1) * width, :]
            r = jnp.dot(wk, u, preferred_element_type=F32)
            ybuf[g, j * width:(j + 1) * width, :] = r[:width]
            wr = r[width:width + n]
            wi = r[width + n:]
            if j == 0:
                lr, li = wr, wi
            else:
                tr, ti = _cmul(*sub_pows[g][0], lr, li)
                lr, li = tr + wr, ti + wi
            lbuf[g, j, :n, :] = lr
            lbuf[g, j, n:, :] = li
        local.append((lr, li))

    if scan:
        lane = lax.broadcasted_iota(jnp.int32, (n, lanes), 1)
        xs = list(local)
        ms = [sub_pows[g][n_sub - 1] for g in groups]
        for i in range(int(math.log2(lanes))):
            sh = 1 << i
            for g in groups:
                xr, xi = xs[g]
                rr = jnp.where(lane >= sh, pltpu.roll(xr, sh, 1), 0.0)
                ri = jnp.where(lane >= sh, pltpu.roll(xi, sh, 1), 0.0)
                tr, ti = _cmul(*ms[g], rr, ri)
                xs[g] = (xr + tr, xi + ti)
                ms[g] = _cmul(*ms[g], *ms[g])
        entering = [(jnp.where(lane >= 1, pltpu.roll(xr, 1, 1), 0.0),
                     jnp.where(lane >= 1, pltpu.roll(xi, 1, 1), 0.0)) for xr, xi in xs]
    else:
        entering = [(s0_ref[g, :n, :], s0_ref[g, n:, :]) for g in groups]

    keep = slice(lanes - LANES, lanes)
    for g in groups:
        qm = qm_ref[g]
        sr, si = entering[g]
        for j in range(n_sub):
            if j == 0:
                pr, pi = sr, si
            else:
                tr, ti = _cmul(*sub_pows[g][j - 1], sr, si)
                pr, pi = lbuf[g, j - 1, :n, :] + tr, lbuf[g, j - 1, n:, :] + ti
            sp = jnp.concatenate([pr, pi], axis=0).astype(BF16)
            y = ybuf[g, j * width:(j + 1) * width, :] + jnp.dot(qm, sp, preferred_element_type=F32)
            yt_ref[g, j * width:(j + 1) * width, :] = y.astype(BF16)
        tr, ti = _cmul(*sub_pows[g][n_sub - 1], sr, si)
        sf_ref[g, :n, :] = lbuf[g, n_sub - 1, :n, keep] + tr[:, keep]
        sf_ref[g, n:, :] = lbuf[g, n_sub - 1, n:, keep] + ti[:, keep]
    _run_riders(riders)


def _s5_call(zt, wk, qm, tabr, tabi, s0, *, n_sub, scan, cast_riders=()):
    g, rows, lanes = zt.shape
    n = S5_STATE
    width = SUB * S5_GROUP
    assert lanes % LANES == 0 and lanes & (lanes - 1) == 0
    gps = S5_GROUPS_PER_STEP_LONG if scan else S5_GROUPS_PER_STEP_SHORT
    grp = lambda shape: pl.BlockSpec((gps,) + shape, lambda i: (i, 0, 0))
    in_specs = [grp((rows, lanes)), grp((width + 2 * n, width)), grp((width, 2 * n)),
                grp((n, LANES)), grp((n, LANES))]
    args = [zt, wk, qm, tabr, tabi]
    if not scan:
        in_specs.append(grp((2 * n, lanes)))
        args.append(s0)
    rider_specs, rider_shapes = _cast_rider_specs(cast_riders, (g // gps,))
    kern = functools.partial(_s5_kernel, n_sub=n_sub, lanes=lanes, scan=scan,
                             n_cast=len(cast_riders))
    return pl.pallas_call(
        kern,
        out_shape=(jax.ShapeDtypeStruct((g, rows, lanes), BF16),
                   jax.ShapeDtypeStruct((g, 2 * n, LANES), F32), *rider_shapes),
        grid=(g // gps,),
        in_specs=in_specs + rider_specs,
        out_specs=(grp((rows, lanes)), grp((2 * n, LANES)), *rider_specs),
        scratch_shapes=[pltpu.VMEM((gps, rows, lanes), F32),
                        pltpu.VMEM((gps, n_sub, 2 * n, lanes), F32)],
        compiler_params=_cparams(1),
        name="s5",
    )(*args, *cast_riders)


def _glu_kernel(*refs, n_ph, rows, lane_rows, n_cast):
    (yt_ref, w_ref, b_ref, m_ref), riders = _split_riders(refs, n_in=3, n_out=1, n_cast=n_cast)
    _run_riders(riders)
    gs = []
    for ph in range(n_ph):
        yt = yt_ref[:, ph, :, :].astype(F32).reshape(S5_WIDTH, lane_rows)
        gs.append(_gelu(yt.T[:rows]))
    gy = jnp.concatenate(gs, axis=0)
    gate = jnp.dot(gy.astype(BF16), w_ref[...], preferred_element_type=F32) + b_ref[...]
    m = gy * _sigmoid(gate)
    for ph in range(n_ph):
        _oct_store(m_ref, ph, m[ph * rows:(ph + 1) * rows])


def _glu_call(yt4, w_glu_b, b_glu, *, rows, n_ph, row_blk, cast_riders=()):
    g, t, p, lane_rows = yt4.shape
    n_rb, lane_blk, lane_rows_expected = _phase_blocks(rows, row_blk)
    assert lane_rows == lane_rows_expected
    grid = (n_rb, t // n_ph)
    rider_specs, rider_shapes = _cast_rider_specs(cast_riders, grid)
    kern = functools.partial(_glu_kernel, n_ph=n_ph, rows=row_blk, lane_rows=lane_blk,
                             n_cast=len(cast_riders))
    oct_shape = _oct_shape(rows, t, S5_WIDTH)
    return pl.pallas_call(
        kern,
        out_shape=(jax.ShapeDtypeStruct(oct_shape, F32), *rider_shapes),
        grid=grid,
        in_specs=[
            pl.BlockSpec((g, n_ph, p, lane_blk), lambda i, j: (0, j, 0, i)),
            _const_spec((S5_WIDTH, S5_WIDTH)),
            _const_spec((1, S5_WIDTH)),
            *rider_specs,
        ],
        out_specs=(pl.BlockSpec((row_blk // OCT, oct_shape[1], n_ph, OCT, LANES),
                                lambda i, j: (i, 0, j, 0, 0)), *rider_specs),
        compiler_params=_cparams(2),
        name="glu",
    )(yt4, w_glu_b, b_glu, *cast_riders)


def _main_kernel(x_ref, m_ref, zuv_ref, mod_ref, g2_ref, gf_ref, lng_ref, lnb_ref, gw_ref, gbt_ref,
                 wo_ref, wgu_ref, wd_ref, y_ref, *rest, tm, cl, seq, streams):
    *maybe_v_out_ref, v_ref, ygm_ref, attn_ref, act_ref = rest
    t = tm // OCT
    hd = GM_HEAD_DIM

    attn_ref[...] = jnp.dot(_oct_load(m_ref, t).astype(BF16), wo_ref[:S5_WIDTH, :],
                            preferred_element_type=F32)

    gv = _gelu(zuv_ref[:, GM_WIDTH:])
    cen = gv - jnp.mean(gv, axis=-1, keepdims=True)
    var = jnp.mean(cen * cen, axis=-1, keepdims=True)
    v = cen * lax.rsqrt(var + EPS) * lng_ref[...] + lnb_ref[...]
    for v_out_ref in maybe_v_out_ref:
        v_out_ref[...] = v
    v_ref[...] = v.astype(BF16)

    blk_i = _div_pow2(lax.broadcasted_iota(jnp.int32, (cl, cl), 0), CHUNK)
    blk_j = _div_pow2(lax.broadcasted_iota(jnp.int32, (cl, cl), 1), CHUNK)
    causal = blk_j <= blk_i
    first_head = lax.broadcasted_iota(jnp.int32, (cl, 2 * hd), 1) < hd
    for pr in range(GM_HEADS // 2):
        h0, h1 = 2 * pr, 2 * pr + 1
        wm = jnp.concatenate(
            [jnp.where(causal, gw_ref[h, :cl, :cl], 0.0) for h in (h0, h1)], axis=1).astype(BF16)
        bias = jnp.where(first_head, gbt_ref[:cl, h0:h0 + 1], gbt_ref[:cl, h1:h1 + 1])
        cs = slice(h0 * hd, (h1 + 1) * hd)
        for ci in range(tm // cl):
            rs = slice(ci * cl, (ci + 1) * cl)
            vv = v_ref[rs, cs]
            zero = jnp.zeros_like(vv)
            rhs = jnp.concatenate([jnp.where(first_head, vv, zero), jnp.where(first_head, zero, vv)],
                                  axis=0)
            mixed = jnp.dot(wm, rhs, preferred_element_type=F32) + bias
            ygm_ref[rs, cs] = (_gelu(zuv_ref[rs, cs]) * mixed).astype(BF16)

    attn = attn_ref[...] + jnp.dot(ygm_ref[...], wo_ref[S5_WIDTH:, :], preferred_element_type=F32)

    def mod_rows(idx):
        if len(streams) == 1:
            return _mod_vec(mod_ref, streams[0], idx)
        return jnp.concatenate(
            [jnp.broadcast_to(_mod_vec(mod_ref, s, idx), (seq, D_MODEL)) for s in streams], axis=0)

    gate1, shift2, scale2, gate2 = mod_rows(2), mod_rows(3), mod_rows(4), mod_rows(5)
    x1 = x_ref[...] + gate1 * attn
    h2 = (x1 * _rms_scale(x1) * (g2_ref[...] * (1.0 + scale2)) + shift2).astype(BF16)

    assert D_FF % MXU_DIM == 0
    for lo in range(0, D_FF, MXU_DIM):
        hi = lo + MXU_DIM
        gg = jnp.dot(h2, wgu_ref[:, lo:hi], preferred_element_type=F32)
        up = jnp.dot(h2, wgu_ref[:, D_FF + lo:D_FF + hi], preferred_element_type=F32)
        act_ref[:, lo:hi] = (gg * jax.nn.sigmoid(gg) * up).astype(BF16)
    acc = jnp.dot(act_ref[...], wd_ref[...], preferred_element_type=F32)
    x2 = x1 + gate2 * acc
    y_ref[...] = x2 * _rms_scale(x2) * gf_ref[...]


def _main_sets_kernel(*refs, sets, n_shared):
    shared = refs[:n_shared]
    scratch = refs[-4:]
    step = pl.program_id(0)
    pos = n_shared
    out_pos = n_shared + 3 * len(sets)
    for st in sets:
        ins = refs[pos:pos + 3]
        pos += 3
        n_out = 2 if st["want_v"] else 1
        outs = refs[out_pos:out_pos + n_out]
        out_pos += n_out
        tm = st["tm"]

        @pl.when(jnp.logical_and(step >= st["start"], step < st["start"] + st["n_tiles"]))
        def _(ins=ins, outs=outs, st=st, tm=tm):
            _main_kernel(*ins, *shared, *outs, *[r.at[:tm] for r in scratch],
                         tm=tm, cl=st["cl"], seq=st["seq"], streams=st["streams"])


def _main_call(stream_sets, mod_all, norm2_g, final_g, ln_g, ln_b, gm_w, gm_bt, w_out_b, w_gu_b, w_down_b):
    shared = (mod_all, norm2_g, final_g, ln_g, ln_b, gm_w, gm_bt, w_out_b, w_gu_b, w_down_b)
    set_args, set_in_specs, out_shape, out_specs, statics = [], [], [], [], []
    start = 0
    for st in stream_sets:
        n_tok = st["x2d"].shape[0]
        n_oct, n_lb, t, _, _ = st["m"].shape
        tm = OCT * t
        seq, streams = st["seq"], st["streams"]
        assert n_oct * tm == n_tok
        assert len(streams) == 1 or (n_oct == 1 and len(streams) * seq == tm)

        def tile_index(i, start=start, n_oct=n_oct):
            return (jnp.clip(i - start, 0, n_oct - 1), 0)

        tok = lambda width: pl.BlockSpec((tm, width), tile_index)
        oct_rows = n_lb * t * OCT
        set_args += [st["x2d"], st["m"].reshape(n_oct * oct_rows, LANES), st["zuv"]]
        set_in_specs += [tok(D_MODEL), pl.BlockSpec((oct_rows, LANES), tile_index), tok(2 * GM_WIDTH)]
        out_shape.append(jax.ShapeDtypeStruct((n_tok, D_MODEL), F32))
        out_specs.append(tok(D_MODEL))
        if st["want_v"]:
            out_shape.append(jax.ShapeDtypeStruct((n_tok, GM_WIDTH), F32))
            out_specs.append(tok(GM_WIDTH))
        statics.append(dict(start=start, n_tiles=n_oct, tm=tm, cl=min(GM_CHUNK, seq), seq=seq,
                            streams=streams, want_v=st["want_v"]))
        start += n_oct
    tm_max = max(s["tm"] for s in statics)
    outs = pl.pallas_call(
        functools.partial(_main_sets_kernel, sets=statics, n_shared=len(shared)),
        out_shape=tuple(out_shape),
        grid=(start,),
        in_specs=[_const_spec(a.shape) for a in shared] + set_in_specs,
        out_specs=tuple(out_specs),
        scratch_shapes=[pltpu.VMEM((tm_max, GM_WIDTH), BF16), pltpu.VMEM((tm_max, GM_WIDTH), BF16),
                        pltpu.VMEM((tm_max, D_MODEL), F32), pltpu.VMEM((tm_max, D_FF), BF16)],
        compiler_params=_cparams(1),
        name="main",
    )(*shared, *set_args)
    results, pos = [], 0
    for s in statics:
        n_out = 2 if s["want_v"] else 1
        results.append(tuple(outs[pos:pos + n_out]))
        pos += n_out
    return results


def _mixer_front(x, mod_all, streams, s0, prm, *, t, n_ph, row_blk, want_v, f32_weights=None):
    b, seq, _ = x.shape
    n_chunks = seq // t
    rows = b * n_chunks
    scan = s0 is None
    assert (b == 1) if scan else (n_chunks == 1)
    n_sub = t // SUB

    zt4, zuv = _inproj_call(
        x.reshape(rows, t, D_MODEL), mod_all, streams, prm["norm1_g"], prm["w_in"],
        n_ph=n_ph, row_blk=row_blk)
    lane_rows = zt4.shape[-1]

    if scan:
        s0_l = None
    else:
        s0_l = jnp.transpose(s0, (1, 2, 0))
        s0_l = jnp.pad(s0_l, ((0, 0), (0, 0), (0, lane_rows - rows)))
    pending = dict(f32_weights or {})
    on_glu = {k: pending.pop(k) for k in ("w_gu",) if k in pending}
    yt, sfin, *cast = _s5_call(zt4.reshape(S5_GROUPS, t * S5_GROUP, lane_rows),
                               prm["wk"], prm["qm"], prm["tabr"], prm["tabi"], s0_l,
                               n_sub=n_sub, scan=scan, cast_riders=tuple(pending.values()))
    prm = {**prm, **dict(zip(pending.keys(), cast))}
    m, *cast = _glu_call(yt.reshape(S5_GROUPS, t, S5_GROUP, lane_rows), prm["w_glu"], prm["b_glu"],
                         rows=rows, n_ph=n_ph, row_blk=row_blk, cast_riders=tuple(on_glu.values()))
    prm = {**prm, **dict(zip(on_glu.keys(), cast))}

    n_tok = b * seq
    stream_set = dict(x2d=x.reshape(n_tok, D_MODEL), m=m, zuv=zuv.reshape(n_tok, 2 * GM_WIDTH),
                      streams=streams, seq=seq, want_v=want_v)

    first_kept = lane_rows - LANES
    if scan:
        fin = sfin[:, :, rows - 1 - first_kept][None]
    else:
        assert first_kept == 0
        fin = jnp.transpose(sfin[:, :, :rows], (2, 0, 1))
    return stream_set, fin[..., :S5_STATE], fin[..., S5_STATE:], prm


def kernel(x_prompt, x_sample, state_s5_re, state_s5_im, c_prompt, c_sample, norm1_g, norm2_g, w_ada, b_ada, w_in, s5_lambda_re, s5_lambda_im, s5_log_step, s5_b_re, s5_b_im, s5_c_re, s5_c_im, s5_d, s5_w_glu, s5_b_glu, gm_ln_g, gm_ln_b, gm_w_s, gm_b_s, w_out, ffn_w_gu, ffn_w_down, final_g):
    depth = w_in.shape[0]
    assert depth == 1
    l = 0
    n_p = c_prompt.shape[0]
    n_s = c_sample.shape[0]

    c_all = jnp.concatenate([c_prompt, c_sample], axis=0)
    c_pad = jnp.pad(c_all, ((0, -c_all.shape[0] % SUBLANES), (0, 0)))
    mod_all = _ada_call(c_pad, w_ada[l], b_ada[l][None, :])
    streams_p = tuple(range(n_p))
    streams_s = tuple(range(n_p, n_p + n_s))

    wk, qm, tabr, tabi, w_in_b, w_out_b, w_down_b = _s5_prep_call(
        s5_lambda_re[l], s5_lambda_im[l], s5_log_step[l][None, :], s5_b_re[l], s5_b_im[l],
        s5_c_re[l], s5_c_im[l], s5_d[l], cast_riders=(w_in[l], w_out[l], ffn_w_down[l]))

    prm = dict(
        norm1_g=norm1_g[l][None, :], norm2_g=norm2_g[l][None, :], final_g=final_g[None, :],
        w_in=w_in_b, w_out=w_out_b, w_down=w_down_b, ln_g=gm_ln_g[l][None, :], ln_b=gm_ln_b[l][None, :],
        wk=wk, qm=qm, tabr=tabr, tabi=tabi, b_glu=s5_b_glu[l][None, :],
        gm_w=gm_w_s[l], gm_bt=jnp.transpose(gm_b_s[l]),
    )
    later_weights = dict(w_glu=s5_w_glu[l], w_gu=ffn_w_gu[l])

    set_p, pre, pim, prm = _mixer_front(
        x_prompt, mod_all, streams_p, None, prm, f32_weights=later_weights,
        t=S5_LONG_SUBS * SUB, n_ph=SUBLANES, row_blk=LANES, want_v=False)
    s0 = jnp.concatenate([state_s5_re[l], state_s5_im[l]], axis=-1)
    n_b, seq_s, _ = x_sample.shape
    set_s, sre, sim, _ = _mixer_front(
        x_sample, mod_all, streams_s, s0, prm, t=seq_s, n_ph=seq_s, row_blk=n_b, want_v=True)

    (yp,), (ys, vs) = _main_call(
        [set_p, set_s], mod_all, prm["norm2_g"], prm["final_g"], prm["ln_g"], prm["ln_b"],
        prm["gm_w"], prm["gm_bt"], prm["w_out"], prm["w_gu"], prm["w_down"])
    return (yp.reshape(x_prompt.shape), ys.reshape(x_sample.shape), pre[None], pim[None],
            sre[None], sim[None], vs.reshape(n_b, seq_s, GM_WIDTH)[None])
```

```python
import functools
import math

import jax
import jax.numpy as jnp
from jax import lax
from jax.experimental import pallas as pl
from jax.experimental.pallas import tpu as pltpu

D_MODEL = 1024
S5_WIDTH = 512
S5_GROUP = 16
S5_GROUPS = 32
S5_STATE = 64
GM_WIDTH = 512
GM_CHUNK = 128
GM_HEADS = 8
GM_HEAD_DIM = 64
CHUNK = 64
IN_WIDTH = S5_WIDTH + 2 * GM_WIDTH
D_FF = 2816
EPS = 1e-6

LANES = 128
SUBLANES = 8
BF16_SUBLANES = 16
MXU_DIM = 256
VMEM_LIMIT_BYTES = 56 * 1024 * 1024

SUB = MXU_DIM // S5_GROUP
S5_LONG_SUBS = 4
PREP_GROUPS_PER_STEP = 8
S5_GROUPS_PER_STEP_LONG = 8
S5_GROUPS_PER_STEP_SHORT = 16
ADA_K_BLOCK = 256

F32 = jnp.float32
BF16 = jnp.bfloat16


def _cparams(n_grid_axes):
    return pltpu.CompilerParams(
        dimension_semantics=("arbitrary",) * n_grid_axes,
        vmem_limit_bytes=VMEM_LIMIT_BYTES,
    )


def _const_spec(shape):
    nd = len(shape)
    return pl.BlockSpec(shape, lambda *_: (0,) * nd, pipeline_mode=pl.Buffered(1))


def _rms_scale(x):
    return lax.rsqrt(jnp.mean(x * x, axis=-1, keepdims=True) + EPS)


_GELU_C0 = math.sqrt(2.0 / math.pi)
_GELU_C1 = 0.044715 * _GELU_C0


def _gelu(x):
    hx = 0.5 * x
    return hx + hx * jnp.tanh(x * (_GELU_C0 + _GELU_C1 * (x * x)))


def _sigmoid(x):
    return 0.5 * jnp.tanh(0.5 * x) + 0.5


def _cmul(ar, ai, xr, xi):
    return ar * xr - ai * xi, ar * xi + ai * xr


def _div_pow2(idx, divisor):
    shift = divisor.bit_length() - 1
    assert divisor == 1 << shift
    return lax.shift_right_logical(idx, shift)


def _mod_pow2(idx, divisor):
    assert divisor & (divisor - 1) == 0
    return lax.bitwise_and(idx, divisor - 1)


def _dot_split(x, y):
    x_hi, y_hi = x.astype(BF16), y.astype(BF16)
    x_lo = (x - x_hi.astype(F32)).astype(BF16)
    y_lo = (y - y_hi.astype(F32)).astype(BF16)
    dot = functools.partial(jnp.dot, preferred_element_type=F32)
    return dot(x_hi, y_hi) + (dot(x_lo, y_hi) + dot(x_hi, y_lo))


OCT = SUBLANES


def _oct_shape(n_rows, t, width):
    assert n_rows % OCT == 0 and width % LANES == 0
    return (n_rows // OCT, width // LANES, t, OCT, LANES)


def _oct_store(ref, ph, val):
    n_oct, n_lb = ref.shape[0], ref.shape[1]
    for lb in range(n_lb):
        ref[:, lb, ph, :, :] = val[:, lb * LANES:(lb + 1) * LANES].reshape(n_oct, OCT, LANES)


def _oct_load(ref, t):
    n_lb = ref.shape[0] // (t * OCT)
    chunks = []
    for c in range(OCT):
        chunks.append(jnp.concatenate(
            [ref[pl.ds(lb * t * OCT + c, t, stride=OCT), :] for lb in range(n_lb)], axis=1))
    return jnp.concatenate(chunks, axis=0)


def _cast_rider_specs(arrays, grid):
    n_steps = math.prod(grid)

    def row_block(*idx):
        step = 0
        for i, extent in zip(idx, grid):
            step = step * extent + i
        return (step, 0)

    specs, shapes = [], []
    for a in arrays:
        rows, cols = a.shape
        blk = rows // n_steps
        assert blk * n_steps == rows and blk % BF16_SUBLANES == 0
        specs.append(pl.BlockSpec((blk, cols), row_block))
        shapes.append(jax.ShapeDtypeStruct((rows, cols), BF16))
    return specs, shapes


def _split_riders(refs, n_in, n_out, n_cast):
    ins, refs = refs[:n_in], refs[n_in:]
    cast_in, refs = refs[:n_cast], refs[n_cast:]
    outs, refs = refs[:n_out], refs[n_out:]
    cast_out, scratch = refs[:n_cast], refs[n_cast:]
    return (*ins, *outs, *scratch), list(zip(cast_in, cast_out))


def _run_riders(pairs):
    for src, dst in pairs:
        dst[...] = src[...].astype(BF16)


def _ada_kernel(c_ref, w_ref, b_ref, o_ref):
    @pl.when(pl.program_id(0) == 0)
    def _():
        o_ref[...] = jnp.broadcast_to(b_ref[...], o_ref.shape)

    c = c_ref[...]
    o_ref[...] += _dot_split(c * jax.nn.sigmoid(c), w_ref[...])


def _ada_call(c_pad, w_ada, b_ada):
    rows = c_pad.shape[0]
    n_in, n_out = w_ada.shape
    bk = ADA_K_BLOCK
    return pl.pallas_call(
        _ada_kernel,
        out_shape=jax.ShapeDtypeStruct((rows, n_out), F32),
        grid=(n_in // bk,),
        in_specs=[
            pl.BlockSpec((rows, bk), lambda k: (0, k)),
            pl.BlockSpec((bk, n_out), lambda k: (k, 0)),
            pl.BlockSpec((1, n_out), lambda k: (0, 0)),
        ],
        out_specs=pl.BlockSpec((rows, n_out), lambda k: (0, 0)),
        compiler_params=_cparams(1),
        name="ada",
    )(c_pad, w_ada, b_ada)


def _discretise(lr, li, ls):
    step = jnp.exp(ls)
    mag = jnp.exp(lr * step)
    ar = mag * jnp.cos(li * step)
    ai = mag * jnp.sin(li * step)
    den = lr * lr + li * li
    fr = ((ar - 1.0) * lr + ai * li) / den
    fi = (ai * lr - (ar - 1.0) * li) / den
    return ar, ai, fr, fi


def _selection(rows, cols, row_of_col):
    r = lax.broadcasted_iota(jnp.int32, (rows, cols), 0)
    c = lax.broadcasted_iota(jnp.int32, (rows, cols), 1)
    return jnp.where(r == row_of_col(c), 1.0, 0.0).astype(BF16)


def _place(x, sel):
    hi = x.astype(BF16)
    rest = x - hi.astype(F32)
    mid = rest.astype(BF16)
    lo = (rest - mid.astype(F32)).astype(BF16)
    dot = functools.partial(jnp.dot, preferred_element_type=F32)
    return dot(hi, sel) + (dot(mid, sel) + dot(lo, sel))


def _to_column(row):
    k = row.shape[1]
    r = lax.broadcasted_iota(jnp.int32, (k, k), 0)
    c = lax.broadcasted_iota(jnp.int32, (k, k), 1)
    return jnp.sum(jnp.where(r == c, jnp.broadcast_to(row, (k, k)), 0.0), axis=1, keepdims=True)


def _s5_prep_kernel(*refs, n_cast):
    refs, riders = _split_riders(refs, n_in=8, n_out=4, n_cast=n_cast)
    lam_re_ref, lam_im_ref, ls_ref, d_ref, *grouped = refs
    n, sub, p = S5_STATE, SUB, S5_GROUP
    width = sub * p

    def col_source(c):
        return jnp.where(c < width, (sub - 1) - _div_pow2(c, p), jnp.where(c < 2 * width, sub + 1, sub))

    sels = dict(
        twice=_selection(n, 2 * n, lambda c: _mod_pow2(c, n)),
        tile=_selection(p, width, lambda c: _mod_pow2(c, p)),
        cols=_selection(2 * n, 2 * width + LANES, col_source),
    )
    gps = grouped[0].shape[0]
    pending = [_s5_prep_group(pl.program_id(0) * gps + gi, sels, lam_re_ref, lam_im_ref, ls_ref, d_ref,
                              *[r.at[gi] for r in grouped]) for gi in range(gps)]
    while pending:
        pending = [gen for gen in pending if next(gen, "done") != "done"]
    _run_riders(riders)


def _s5_prep_group(g, sels, lam_re_ref, lam_im_ref, ls_ref, d_ref, b_re_ref, b_im_ref, c_re_ref, c_im_ref,
                   wk_ref, qm_ref, tabr_ref, tabi_ref):
    n, sub, p = S5_STATE, SUB, S5_GROUP
    width = sub * p

    lr_row = lam_re_ref[pl.ds(g, 1), :]
    li_row = lam_im_ref[pl.ds(g, 1), :]
    ls_all = ls_ref[...]
    grp_lane = lax.broadcasted_iota(jnp.int32, ls_all.shape, 1)
    ls = jnp.sum(jnp.where(grp_lane == g, ls_all, 0.0), axis=1, keepdims=True)

    ar8, ai8, fr8, fi8 = _discretise(jnp.broadcast_to(lr_row, (SUBLANES, n)),
                                     jnp.broadcast_to(li_row, (SUBLANES, n)), ls)
    twice = lambda t8: _place(jnp.concatenate([t8] * (p // SUBLANES), axis=0), sels["twice"])
    a2r, a2i = twice(ar8), twice(ai8)
    first = lax.broadcasted_iota(jnp.int32, (p, 2 * n), 1) < n
    c2r = _place(c_re_ref[...], sels["twice"])
    c2i = _place(c_im_ref[...], sels["twice"])
    pr = jnp.ones_like(a2r)
    pi = jnp.zeros_like(a2r)
    ccat = []
    tbl_rows = SUBLANES * (-(-(sub + 2) // SUBLANES))
    tbl_row = lax.broadcasted_iota(jnp.int32, (tbl_rows, 2 * n), 0)
    as_tbl_row = lambda re2, im2: jnp.concatenate([jnp.where(first, re2, im2)] * (tbl_rows // p + 1),
                                                  axis=0)[:tbl_rows]
    tbl = jnp.zeros((tbl_rows, 2 * n), F32)
    for d in range(sub + 1):
        ccat.append(c2r * jnp.where(first, pr, -pi) + c2i * jnp.where(first, -pi, -pr))
        tbl = jnp.where(tbl_row == d, as_tbl_row(pr, pi), tbl)
        pr, pi = _cmul(a2r, a2i, pr, pi)
    tbl = jnp.where(tbl_row == sub + 1, as_tbl_row(twice(fr8), twice(fi8)), tbl)
    yield
    qm_ref[...] = jnp.concatenate(ccat[1:], axis=0).astype(BF16)
    rcat = jnp.concatenate(ccat[:sub], axis=0)

    tbl_t = jnp.concatenate([tbl, jnp.zeros((2 * n - tbl_rows, 2 * n), F32)], axis=0).T
    cols = _place(tbl_t, sels["cols"])
    yield
    apr, fr, a16r = cols[:n, :width], cols[:n, width:2 * width], cols[:n, 2 * width:]
    api, fi, a16i = cols[n:, :width], cols[n:, width:2 * width], cols[n:, 2 * width:]
    btr = _place(b_re_ref[...], sels["tile"])
    bti = _place(b_im_ref[...], sels["tile"])
    bbr, bbi = _cmul(fr, fi, btr, bti)
    pmr, pmi = _cmul(apr, api, bbr, bbi)
    wk_ref[width:width + n, :] = pmr.astype(BF16)
    wk_ref[width + n:, :] = pmi.astype(BF16)
    yield

    bbcat = jnp.concatenate([bbr, bbi], axis=0)
    kt = _dot_split(rcat, bbcat)
    yield
    d_col = _to_column(d_ref[pl.ds(g, 1), :])
    row_p = lax.broadcasted_iota(jnp.int32, (p, width), 0)
    lane_p = lax.broadcasted_iota(jnp.int32, (p, width), 1)
    d_diag = jnp.where(row_p == _mod_pow2(lane_p, p), d_col, 0.0)
    kt = jnp.concatenate([kt[:p] + d_diag, kt[p:]], axis=0)
    col_blk = _div_pow2(lax.broadcasted_iota(jnp.int32, (width, width), 1), p)
    m16 = jnp.zeros((width, width), F32)
    for k in range(sub):
        if k == 0:
            shifted = kt
        else:
            shifted = jnp.concatenate(
                [jnp.zeros((k * p, width), F32), kt[:width - k * p]], axis=0)
        m16 = jnp.where(col_blk == k, shifted, m16)
    wk_ref[:width, :] = m16.astype(BF16)

    tabr_ref[...] = a16r
    tabi_ref[...] = a16i


def _s5_prep_call(lam_re, lam_im, log_step, b_re, b_im, c_re, c_im, d, cast_riders=()):
    g, n, p, sub = S5_GROUPS, S5_STATE, S5_GROUP, SUB
    width = sub * p
    grp = lambda shape: pl.BlockSpec((PREP_GROUPS_PER_STEP,) + shape, lambda i: (i, 0, 0))
    whole = lambda a: pl.BlockSpec(a.shape, lambda i: (0,) * a.ndim)
    n_steps = g // PREP_GROUPS_PER_STEP
    rider_specs, rider_shapes = _cast_rider_specs(cast_riders, (n_steps,))
    return pl.pallas_call(
        functools.partial(_s5_prep_kernel, n_cast=len(cast_riders)),
        out_shape=(
            jax.ShapeDtypeStruct((g, width + 2 * n, width), BF16),
            jax.ShapeDtypeStruct((g, width, 2 * n), BF16),
            jax.ShapeDtypeStruct((g, n, LANES), F32),
            jax.ShapeDtypeStruct((g, n, LANES), F32),
            *rider_shapes,
        ),
        grid=(n_steps,),
        in_specs=[whole(lam_re), whole(lam_im), whole(log_step), whole(d)]
        + [grp((n, p))] * 2 + [grp((p, n))] * 2 + rider_specs,
        out_specs=(grp((width + 2 * n, width)), grp((width, 2 * n)),
                   grp((n, LANES)), grp((n, LANES)), *rider_specs),
        compiler_params=_cparams(1),
        name="s5_prep",
    )(lam_re, lam_im, log_step, d, b_re, b_im, c_re, c_im, *cast_riders)


def _mod_vec(mod_ref, stream, idx):
    return mod_ref[stream:stream + 1, idx * D_MODEL:(idx + 1) * D_MODEL]


def _inproj_kernel(x_ref, mod_ref, g1_ref, w_ref, zt_ref, zuv_ref, hs_ref,
                   *, n_ph, rows, lane_rows, streams):
    m = rows * n_ph
    assert len(streams) in (1, rows)
    shift, scale = (jnp.concatenate([_mod_vec(mod_ref, s, idx) for s in streams], axis=0)
                    for idx in (0, 1))
    gain = (g1_ref[...] * (1.0 + scale))[:, None, :]
    shift = shift[:, None, :]
    x3 = x_ref[...]
    h = (x3 * _rms_scale(x3) * gain + shift).reshape(m, D_MODEL)
    zuv = jnp.dot(h.astype(BF16), w_ref[:, S5_WIDTH:], preferred_element_type=F32)
    zuv_ref[...] = zuv.reshape(rows, n_ph, 2 * GM_WIDTH)

    n_lb = D_MODEL // LANES
    for lb in range(n_lb):
        hs_ref[lb] = h[:, lb * LANES:(lb + 1) * LANES]
    hp = jnp.concatenate(
        [jnp.concatenate([hs_ref[lb, pl.ds(ph, rows, stride=n_ph), :] for lb in range(n_lb)], axis=1)
         for ph in range(n_ph)], axis=0).astype(BF16)
    z5 = jnp.dot(hp, w_ref[:, :S5_WIDTH], preferred_element_type=F32)
    for ph in range(n_ph):
        zz = z5[ph * rows:(ph + 1) * rows]
        if lane_rows > rows:
            zz = jnp.concatenate([zz, jnp.zeros((lane_rows - rows, S5_WIDTH), F32)], axis=0)
        zt = zz.T.reshape(S5_GROUPS, S5_GROUP, lane_rows)
        zt_ref[:, ph, :, :] = zt.astype(BF16)


def _phase_blocks(n_rows, row_blk):
    if row_blk % LANES == 0:
        assert n_rows % row_blk == 0
        return n_rows // row_blk, row_blk, n_rows
    assert row_blk == n_rows
    lane_rows = -(-n_rows // LANES) * LANES
    return 1, lane_rows, lane_rows


def _inproj_call(x3, mod_all, streams, norm1_g, w_in_b, *, n_ph, row_blk):
    rows, t, _ = x3.shape
    n_rb, lane_blk, lane_rows = _phase_blocks(rows, row_blk)
    kern = functools.partial(_inproj_kernel, n_ph=n_ph, rows=row_blk, lane_rows=lane_blk,
                             streams=streams)
    return pl.pallas_call(
        kern,
        out_shape=(
            jax.ShapeDtypeStruct((S5_GROUPS, t, S5_GROUP, lane_rows), BF16),
            jax.ShapeDtypeStruct((rows, t, 2 * GM_WIDTH), F32),
        ),
        grid=(n_rb, t // n_ph),
        in_specs=[
            pl.BlockSpec((row_blk, n_ph, D_MODEL), lambda i, j: (i, j, 0)),
            _const_spec(mod_all.shape),
            _const_spec((1, D_MODEL)),
            _const_spec((D_MODEL, IN_WIDTH)),
        ],
        out_specs=(
            pl.BlockSpec((S5_GROUPS, n_ph, S5_GROUP, lane_blk), lambda i, j: (0, j, 0, i)),
            pl.BlockSpec((row_blk, n_ph, 2 * GM_WIDTH), lambda i, j: (i, j, 0)),
        ),
        scratch_shapes=[pltpu.VMEM((D_MODEL // LANES, row_blk * n_ph, LANES), F32)],
        compiler_params=_cparams(2),
        name="inproj",
    )(x3, mod_all, norm1_g, w_in_b)


def _s5_kernel(*refs, n_sub, lanes, scan, final_rows, n_cast):
    refs, riders = _split_riders(refs, n_in=5 if scan else 6, n_out=2, n_cast=n_cast)
    if scan:
        zt_ref, wk_ref, qm_ref, tabr_ref, tabi_ref, yt_ref, sf_ref, ybuf, lbuf = refs
        s0_ref = None
    else:
        zt_ref, wk_ref, qm_ref, tabr_ref, tabi_ref, s0_ref, yt_ref, sf_ref, ybuf, lbuf = refs
    groups = range(zt_ref.shape[0])
    n = S5_STATE
    width = SUB * S5_GROUP

    widen = lambda tile: jnp.concatenate([tile] * (lanes // LANES), axis=1)
    sub_pows = []
    for g in groups:
        a1 = (widen(tabr_ref[g]), widen(tabi_ref[g]))
        pows = [a1]
        for _ in range(n_sub - 1):
            pows.append(_cmul(*a1, *pows[-1]))
        sub_pows.append(pows)

    local = []
    for g in groups:
        wk = wk_ref[g]
        lr = li = None
        for j in range(n_sub):
            u = zt_ref[g, j * width:(j + 1) * width, :]
            r = jnp.dot(wk, u, preferred_element_type=F32)
            ybuf[g, j * width:(j + 1) * width, :] = r[:width]
            wr = r[width:width + n]
            wi = r[width + n:]
            if j == 0:
                lr, li = wr, wi
            else:
                tr, ti = _cmul(*sub_pows[g][0], lr, li)
                lr, li = tr + wr, ti + wi
            lbuf[g, j, :n, :] = lr
            lbuf[g, j, n:, :] = li
        local.append((lr, li))

    if scan:
        lane = lax.broadcasted_iota(jnp.int32, (n, lanes), 1)
        xs = list(local)
        ms = [sub_pows[g][n_sub - 1] for g in groups]
        for i in range(int(math.log2(lanes))):
            sh = 1 << i
            for g in groups:
                xr, xi = xs[g]
                rr = jnp.where(lane >= sh, pltpu.roll(xr, sh, 1), 0.0)
                ri = jnp.where(lane >= sh, pltpu.roll(xi, sh, 1), 0.0)
                tr, ti = _cmul(*ms[g], rr, ri)
                xs[g] = (xr + tr, xi + ti)
                ms[g] = _cmul(*ms[g], *ms[g])
        entering = [(jnp.where(lane >= 1, pltpu.roll(xr, 1, 1), 0.0),
                     jnp.where(lane >= 1, pltpu.roll(xi, 1, 1), 0.0)) for xr, xi in xs]
    else:
        assert lanes == LANES
        entering = []
        for g in groups:
            s0 = s0_ref[:, g, :]
            s0t = jnp.concatenate([s0, jnp.zeros((LANES - s0.shape[0], 2 * n), F32)], axis=0).T
            entering.append((s0t[:n], s0t[n:]))

    keep = slice(lanes - LANES, lanes)
    for g in groups:
        qm = qm_ref[g]
        sr, si = entering[g]
        for j in range(n_sub):
            if j == 0:
                pr, pi = sr, si
            else:
                tr, ti = _cmul(*sub_pows[g][j - 1], sr, si)
                pr, pi = lbuf[g, j - 1, :n, :] + tr, lbuf[g, j - 1, n:, :] + ti
            sp = jnp.concatenate([pr, pi], axis=0).astype(BF16)
            y = ybuf[g, j * width:(j + 1) * width, :] + jnp.dot(qm, sp, preferred_element_type=F32)
            yt_ref[g, j * width:(j + 1) * width, :] = y.astype(BF16)
        tr, ti = _cmul(*sub_pows[g][n_sub - 1], sr, si)
        ends = jnp.concatenate([lbuf[g, n_sub - 1, :n, keep] + tr[:, keep],
                                lbuf[g, n_sub - 1, n:, keep] + ti[:, keep]], axis=0).T
        sf_ref[:, g, :] = ends[final_rows[0]:final_rows[1]]
    _run_riders(riders)


def _s5_call(zt, wk, qm, tabr, tabi, s0, *, n_streams, n_sub, scan, cast_riders=()):
    g, rows, lanes = zt.shape
    n = S5_STATE
    width = SUB * S5_GROUP
    assert lanes % LANES == 0 and lanes & (lanes - 1) == 0
    gps = S5_GROUPS_PER_STEP_LONG if scan else S5_GROUPS_PER_STEP_SHORT
    grp = lambda shape: pl.BlockSpec((gps,) + shape, lambda i: (i, 0, 0))
    per_stream = pl.BlockSpec((n_streams, gps, 2 * n), lambda i: (0, i, 0))
    in_specs = [grp((rows, lanes)), grp((width + 2 * n, width)), grp((width, 2 * n)),
                grp((n, LANES)), grp((n, LANES))]
    args = [zt, wk, qm, tabr, tabi]
    if scan:
        assert n_streams == 1
        final_rows = (LANES - 1, LANES)
    else:
        in_specs.append(per_stream)
        args.append(s0)
        final_rows = (0, n_streams)
    rider_specs, rider_shapes = _cast_rider_specs(cast_riders, (g // gps,))
    kern = functools.partial(_s5_kernel, n_sub=n_sub, lanes=lanes, scan=scan, final_rows=final_rows,
                             n_cast=len(cast_riders))
    return pl.pallas_call(
        kern,
        out_shape=(jax.ShapeDtypeStruct((g, rows, lanes), BF16),
                   jax.ShapeDtypeStruct((n_streams, g, 2 * n), F32), *rider_shapes),
        grid=(g // gps,),
        in_specs=in_specs + rider_specs,
        out_specs=(grp((rows, lanes)), per_stream, *rider_specs),
        scratch_shapes=[pltpu.VMEM((gps, rows, lanes), F32),
                        pltpu.VMEM((gps, n_sub, 2 * n, lanes), F32)],
        compiler_params=_cparams(1),
        name="s5",
    )(*args, *cast_riders)


def _glu_kernel(*refs, n_ph, rows, lane_rows, n_cast):
    (yt_ref, w_ref, b_ref, m_ref), riders = _split_riders(refs, n_in=3, n_out=1, n_cast=n_cast)
    _run_riders(riders)
    gs = []
    for ph in range(n_ph):
        yt = yt_ref[:, ph, :, :].astype(F32).reshape(S5_WIDTH, lane_rows)
        gs.append(_gelu(yt.T[:rows]))
    gy = jnp.concatenate(gs, axis=0)
    gate = jnp.dot(gy.astype(BF16), w_ref[...], preferred_element_type=F32) + b_ref[...]
    m = gy * _sigmoid(gate)
    for ph in range(n_ph):
        _oct_store(m_ref, ph, m[ph * rows:(ph + 1) * rows])


def _glu_call(yt4, w_glu_b, b_glu, *, rows, n_ph, row_blk, cast_riders=()):
    g, t, p, lane_rows = yt4.shape
    n_rb, lane_blk, lane_rows_expected = _phase_blocks(rows, row_blk)
    assert lane_rows == lane_rows_expected
    grid = (n_rb, t // n_ph)
    rider_specs, rider_shapes = _cast_rider_specs(cast_riders, grid)
    kern = functools.partial(_glu_kernel, n_ph=n_ph, rows=row_blk, lane_rows=lane_blk,
                             n_cast=len(cast_riders))
    oct_shape = _oct_shape(rows, t, S5_WIDTH)
    return pl.pallas_call(
        kern,
        out_shape=(jax.ShapeDtypeStruct(oct_shape, F32), *rider_shapes),
        grid=grid,
        in_specs=[
            pl.BlockSpec((g, n_ph, p, lane_blk), lambda i, j: (0, j, 0, i)),
            _const_spec((S5_WIDTH, S5_WIDTH)),
            _const_spec((1, S5_WIDTH)),
            *rider_specs,
        ],
        out_specs=(pl.BlockSpec((row_blk // OCT, oct_shape[1], n_ph, OCT, LANES),
                                lambda i, j: (i, 0, j, 0, 0)), *rider_specs),
        compiler_params=_cparams(2),
        name="glu",
    )(yt4, w_glu_b, b_glu, *cast_riders)


def _main_kernel(x_ref, m_ref, zuv_ref, mod_ref, g2_ref, gf_ref, lng_ref, lnb_ref, gw_ref, gbt_ref,
                 wo_ref, wgu_ref, wd_ref, y_ref, *rest, tm, cl, seq, streams):
    *maybe_v_out_ref, v_ref, ygm_ref, attn_ref, act_ref = rest
    t = tm // OCT
    hd = GM_HEAD_DIM

    attn_ref[...] = jnp.dot(_oct_load(m_ref, t).astype(BF16), wo_ref[:S5_WIDTH, :],
                            preferred_element_type=F32)

    gv = _gelu(zuv_ref[:, GM_WIDTH:])
    cen = gv - jnp.mean(gv, axis=-1, keepdims=True)
    var = jnp.mean(cen * cen, axis=-1, keepdims=True)
    v = cen * lax.rsqrt(var + EPS) * lng_ref[...] + lnb_ref[...]
    for v_out_ref in maybe_v_out_ref:
        v_out_ref[...] = v
    v_ref[...] = v.astype(BF16)

    blk_i = _div_pow2(lax.broadcasted_iota(jnp.int32, (cl, cl), 0), CHUNK)
    blk_j = _div_pow2(lax.broadcasted_iota(jnp.int32, (cl, cl), 1), CHUNK)
    causal = blk_j <= blk_i
    first_head = lax.broadcasted_iota(jnp.int32, (cl, 2 * hd), 1) < hd
    for pr in range(GM_HEADS // 2):
        h0, h1 = 2 * pr, 2 * pr + 1
        wm = jnp.concatenate(
            [jnp.where(causal, gw_ref[h, :cl, :cl], 0.0) for h in (h0, h1)], axis=1).astype(BF16)
        bias = jnp.where(first_head, gbt_ref[:cl, h0:h0 + 1], gbt_ref[:cl, h1:h1 + 1])
        cs = slice(h0 * hd, (h1 + 1) * hd)
        for ci in range(tm // cl):
            rs = slice(ci * cl, (ci + 1) * cl)
            vv = v_ref[rs, cs]
            zero = jnp.zeros_like(vv)
            rhs = jnp.concatenate([jnp.where(first_head, vv, zero), jnp.where(first_head, zero, vv)],
                                  axis=0)
            mixed = jnp.dot(wm, rhs, preferred_element_type=F32) + bias
            ygm_ref[rs, cs] = (_gelu(zuv_ref[rs, cs]) * mixed).astype(BF16)

    attn = attn_ref[...] + jnp.dot(ygm_ref[...], wo_ref[S5_WIDTH:, :], preferred_element_type=F32)

    def mod_rows(idx):
        if len(streams) == 1:
            return _mod_vec(mod_ref, streams[0], idx)
        return jnp.concatenate(
            [jnp.broadcast_to(_mod_vec(mod_ref, s, idx), (seq, D_MODEL)) for s in streams], axis=0)

    gate1, shift2, scale2, gate2 = mod_rows(2), mod_rows(3), mod_rows(4), mod_rows(5)
    x1 = x_ref[...] + gate1 * attn
    h2 = (x1 * _rms_scale(x1) * (g2_ref[...] * (1.0 + scale2)) + shift2).astype(BF16)

    assert D_FF % MXU_DIM == 0
    for lo in range(0, D_FF, MXU_DIM):
        hi = lo + MXU_DIM
        gg = jnp.dot(h2, wgu_ref[:, lo:hi], preferred_element_type=F32)
        up = jnp.dot(h2, wgu_ref[:, D_FF + lo:D_FF + hi], preferred_element_type=F32)
        act_ref[:, lo:hi] = (gg * jax.nn.sigmoid(gg) * up).astype(BF16)
    acc = jnp.dot(act_ref[...], wd_ref[...], preferred_element_type=F32)
    x2 = x1 + gate2 * acc
    y_ref[...] = x2 * _rms_scale(x2) * gf_ref[...]


def _main_sets_kernel(*refs, sets, n_shared):
    shared = refs[:n_shared]
    scratch = refs[-4:]
    step = pl.program_id(0)
    pos = n_shared
    out_pos = n_shared + 3 * len(sets)
    for st in sets:
        ins = refs[pos:pos + 3]
        pos += 3
        n_out = 2 if st["want_v"] else 1
        outs = refs[out_pos:out_pos + n_out]
        out_pos += n_out
        tm = st["tm"]

        @pl.when(jnp.logical_and(step >= st["start"], step < st["start"] + st["n_tiles"]))
        def _(ins=ins, outs=outs, st=st, tm=tm):
            _main_kernel(*ins, *shared, *outs, *[r.at[:tm] for r in scratch],
                         tm=tm, cl=st["cl"], seq=st["seq"], streams=st["streams"])


def _main_call(stream_sets, mod_all, norm2_g, final_g, ln_g, ln_b, gm_w, gm_bt, w_out_b, w_gu_b, w_down_b):
    shared = (mod_all, norm2_g, final_g, ln_g, ln_b, gm_w, gm_bt, w_out_b, w_gu_b, w_down_b)
    set_args, set_in_specs, out_shape, out_specs, statics = [], [], [], [], []
    start = 0
    for st in stream_sets:
        n_tok = st["x2d"].shape[0]
        n_oct, n_lb, t, _, _ = st["m"].shape
        tm = OCT * t
        seq, streams = st["seq"], st["streams"]
        assert n_oct * tm == n_tok
        assert len(streams) == 1 or (n_oct == 1 and len(streams) * seq == tm)

        def tile_index(i, start=start, n_oct=n_oct):
            return (jnp.clip(i - start, 0, n_oct - 1), 0)

        tok = lambda width: pl.BlockSpec((tm, width), tile_index)
        oct_rows = n_lb * t * OCT
        set_args += [st["x2d"], st["m"].reshape(n_oct * oct_rows, LANES), st["zuv"]]
        set_in_specs += [tok(D_MODEL), pl.BlockSpec((oct_rows, LANES), tile_index), tok(2 * GM_WIDTH)]
        out_shape.append(jax.ShapeDtypeStruct((n_tok, D_MODEL), F32))
        out_specs.append(tok(D_MODEL))
        if st["want_v"]:
            out_shape.append(jax.ShapeDtypeStruct((n_tok, GM_WIDTH), F32))
            out_specs.append(tok(GM_WIDTH))
        statics.append(dict(start=start, n_tiles=n_oct, tm=tm, cl=min(GM_CHUNK, seq), seq=seq,
                            streams=streams, want_v=st["want_v"]))
        start += n_oct
    tm_max = max(s["tm"] for s in statics)
    outs = pl.pallas_call(
        functools.partial(_main_sets_kernel, sets=statics, n_shared=len(shared)),
        out_shape=tuple(out_shape),
        grid=(start,),
        in_specs=[_const_spec(a.shape) for a in shared] + set_in_specs,
        out_specs=tuple(out_specs),
        scratch_shapes=[pltpu.VMEM((tm_max, GM_WIDTH), BF16), pltpu.VMEM((tm_max, GM_WIDTH), BF16),
                        pltpu.VMEM((tm_max, D_MODEL), F32), pltpu.VMEM((tm_max, D_FF), BF16)],
        compiler_params=_cparams(1),
        name="main",
    )(*shared, *set_args)
    results, pos = [], 0
    for s in statics:
        n_out = 2 if s["want_v"] else 1
        results.append(tuple(outs[pos:pos + n_out]))
        pos += n_out
    return results


def _mixer_front(x, mod_all, streams, s0, prm, *, t, n_ph, row_blk, want_v, f32_weights=None):
    b, seq, _ = x.shape
    n_chunks = seq // t
    rows = b * n_chunks
    scan = s0 is None
    assert (b == 1) if scan else (n_chunks == 1)
    n_sub = t // SUB

    zt4, zuv = _inproj_call(
        x.reshape(rows, t, D_MODEL), mod_all, streams, prm["norm1_g"], prm["w_in"],
        n_ph=n_ph, row_blk=row_blk)
    lane_rows = zt4.shape[-1]

    assert not scan or rows == lane_rows
    pending = dict(f32_weights or {})
    on_glu = {k: pending.pop(k) for k in ("w_gu",) if k in pending}
    yt, fin, *cast = _s5_call(zt4.reshape(S5_GROUPS, t * S5_GROUP, lane_rows),
                              prm["wk"], prm["qm"], prm["tabr"], prm["tabi"], s0,
                              n_streams=b, n_sub=n_sub, scan=scan, cast_riders=tuple(pending.values()))
    prm = {**prm, **dict(zip(pending.keys(), cast))}
    m, *cast = _glu_call(yt.reshape(S5_GROUPS, t, S5_GROUP, lane_rows), prm["w_glu"], prm["b_glu"],
                         rows=rows, n_ph=n_ph, row_blk=row_blk, cast_riders=tuple(on_glu.values()))
    prm = {**prm, **dict(zip(on_glu.keys(), cast))}

    n_tok = b * seq
    stream_set = dict(x2d=x.reshape(n_tok, D_MODEL), m=m, zuv=zuv.reshape(n_tok, 2 * GM_WIDTH),
                      streams=streams, seq=seq, want_v=want_v)

    return stream_set, fin[..., :S5_STATE], fin[..., S5_STATE:], prm


def kernel(x_prompt, x_sample, state_s5_re, state_s5_im, c_prompt, c_sample, norm1_g, norm2_g, w_ada, b_ada, w_in, s5_lambda_re, s5_lambda_im, s5_log_step, s5_b_re, s5_b_im, s5_c_re, s5_c_im, s5_d, s5_w_glu, s5_b_glu, gm_ln_g, gm_ln_b, gm_w_s, gm_b_s, w_out, ffn_w_gu, ffn_w_down, final_g):
    depth = w_in.shape[0]
    assert depth == 1
    l = 0
    n_p = c_prompt.shape[0]
    n_s = c_sample.shape[0]

    c_all = jnp.concatenate([c_prompt, c_sample], axis=0)
    c_pad = jnp.pad(c_all, ((0, -c_all.shape[0] % SUBLANES), (0, 0)))
    mod_all = _ada_call(c_pad, w_ada[l], b_ada[l][None, :])
    streams_p = tuple(range(n_p))
    streams_s = tuple(range(n_p, n_p + n_s))

    wk, qm, tabr, tabi, w_in_b, w_out_b, w_down_b = _s5_prep_call(
        s5_lambda_re[l], s5_lambda_im[l], s5_log_step[l][None, :], s5_b_re[l], s5_b_im[l],
        s5_c_re[l], s5_c_im[l], s5_d[l], cast_riders=(w_in[l], w_out[l], ffn_w_down[l]))

    prm = dict(
        norm1_g=norm1_g[l][None, :], norm2_g=norm2_g[l][None, :], final_g=final_g[None, :],
        w_in=w_in_b, w_out=w_out_b, w_down=w_down_b, ln_g=gm_ln_g[l][None, :], ln_b=gm_ln_b[l][None, :],
        wk=wk, qm=qm, tabr=tabr, tabi=tabi, b_glu=s5_b_glu[l][None, :],
        gm_w=gm_w_s[l], gm_bt=jnp.transpose(gm_b_s[l]),
    )
    later_weights = dict(w_glu=s5_w_glu[l], w_gu=ffn_w_gu[l])

    set_p, pre, pim, prm = _mixer_front(
        x_prompt, mod_all, streams_p, None, prm, f32_weights=later_weights,
        t=S5_LONG_SUBS * SUB, n_ph=SUBLANES, row_blk=LANES, want_v=False)
    s0 = jnp.concatenate([state_s5_re[l], state_s5_im[l]], axis=-1)
    n_b, seq_s, _ = x_sample.shape
    set_s, sre, sim, _ = _mixer_front(
        x_sample, mod_all, streams_s, s0, prm, t=seq_s, n_ph=seq_s, row_blk=n_b, want_v=True)

    (yp,), (ys, vs) = _main_call(
        [set_p, set_s], mod_all, prm["norm2_g"], prm["final_g"], prm["ln_g"], prm["ln_b"],
        prm["gm_w"], prm["gm_bt"], prm["w_out"], prm["w_gu"], prm["w_down"])
    return (yp.reshape(x_prompt.shape), ys.reshape(x_sample.shape), pre[None], pim[None],
            sre[None], sim[None], vs.reshape(n_b, seq_s, GM_WIDTH)[None])
```

```python
import functools
import math

import jax
import jax.numpy as jnp
from jax import lax
from jax.experimental import pallas as pl
from jax.experimental.pallas import tpu as pltpu

D_MODEL = 1024
S5_WIDTH = 512
S5_GROUP = 16
S5_GROUPS = 32
S5_STATE = 64
GM_WIDTH = 512
GM_CHUNK = 128
GM_HEADS = 8
GM_HEAD_DIM = 64
CHUNK = 64
IN_WIDTH = S5_WIDTH + 2 * GM_WIDTH
D_FF = 2816
EPS = 1e-6

LANES = 128
SUBLANES = 8
BF16_SUBLANES = 16
MXU_DIM = 256
VMEM_LIMIT_BYTES = 56 * 1024 * 1024

SUB = MXU_DIM // S5_GROUP
S5_LONG_SUBS = 4
PREP_GROUPS_PER_STEP = 8
S5_GROUPS_PER_STEP_LONG = 8
S5_GROUPS_PER_STEP_SHORT = 16
ADA_K_BLOCK = 256

F32 = jnp.float32
BF16 = jnp.bfloat16


def _cparams(n_grid_axes):
    return pltpu.CompilerParams(
        dimension_semantics=("arbitrary",) * n_grid_axes,
        vmem_limit_bytes=VMEM_LIMIT_BYTES,
    )


def _const_spec(shape):
    nd = len(shape)
    return pl.BlockSpec(shape, lambda *_: (0,) * nd, pipeline_mode=pl.Buffered(1))


def _rms_scale(x):
    return lax.rsqrt(jnp.mean(x * x, axis=-1, keepdims=True) + EPS)


_GELU_C0 = math.sqrt(2.0 / math.pi)
_GELU_C1 = 0.044715 * _GELU_C0


def _gelu(x):
    hx = 0.5 * x
    return hx + hx * jnp.tanh(x * (_GELU_C0 + _GELU_C1 * (x * x)))


def _sigmoid(x):
    return 0.5 * jnp.tanh(0.5 * x) + 0.5


def _cmul(ar, ai, xr, xi):
    return ar * xr - ai * xi, ar * xi + ai * xr


def _div_pow2(idx, divisor):
    shift = divisor.bit_length() - 1
    assert divisor == 1 << shift
    return lax.shift_right_logical(idx, shift)


def _mod_pow2(idx, divisor):
    assert divisor & (divisor - 1) == 0
    return lax.bitwise_and(idx, divisor - 1)


def _place_rows(sel, x):
    hi = x.astype(BF16)
    rest = x - hi.astype(F32)
    mid = rest.astype(BF16)
    lo = (rest - mid.astype(F32)).astype(BF16)
    dot = functools.partial(jnp.dot, preferred_element_type=F32)
    return dot(sel, hi) + (dot(sel, mid) + dot(sel, lo))


def _dot_split(x, y):
    x_hi, y_hi = x.astype(BF16), y.astype(BF16)
    x_lo = (x - x_hi.astype(F32)).astype(BF16)
    y_lo = (y - y_hi.astype(F32)).astype(BF16)
    dot = functools.partial(jnp.dot, preferred_element_type=F32)
    return dot(x_hi, y_hi) + (dot(x_lo, y_hi) + dot(x_hi, y_lo))


OCT = SUBLANES


def _oct_shape(n_rows, t, width):
    assert n_rows % OCT == 0 and width % LANES == 0
    return (n_rows // OCT, width // LANES, t, OCT, LANES)


def _oct_store(ref, ph, val):
    n_oct, n_lb = ref.shape[0], ref.shape[1]
    for lb in range(n_lb):
        ref[:, lb, ph, :, :] = val[:, lb * LANES:(lb + 1) * LANES].reshape(n_oct, OCT, LANES)


def _oct_load(ref, t):
    n_lb = ref.shape[0] // (t * OCT)
    chunks = []
    for c in range(OCT):
        chunks.append(jnp.concatenate(
            [ref[pl.ds(lb * t * OCT + c, t, stride=OCT), :] for lb in range(n_lb)], axis=1))
    return jnp.concatenate(chunks, axis=0)


def _cast_rider_specs(arrays, grid):
    n_steps = math.prod(grid)

    def row_block(*idx):
        step = 0
        for i, extent in zip(idx, grid):
            step = step * extent + i
        return (step, 0)

    specs, shapes = [], []
    for a in arrays:
        rows, cols = a.shape
        blk = rows // n_steps
        assert blk * n_steps == rows and blk % BF16_SUBLANES == 0
        specs.append(pl.BlockSpec((blk, cols), row_block))
        shapes.append(jax.ShapeDtypeStruct((rows, cols), BF16))
    return specs, shapes


def _split_riders(refs, n_in, n_out, n_cast):
    ins, refs = refs[:n_in], refs[n_in:]
    cast_in, refs = refs[:n_cast], refs[n_cast:]
    outs, refs = refs[:n_out], refs[n_out:]
    cast_out, scratch = refs[:n_cast], refs[n_cast:]
    return (*ins, *outs, *scratch), list(zip(cast_in, cast_out))


def _run_riders(pairs):
    for src, dst in pairs:
        dst[...] = src[...].astype(BF16)


def _ada_kernel(c_ref, w_ref, b_ref, o_ref):
    @pl.when(pl.program_id(0) == 0)
    def _():
        o_ref[...] = jnp.broadcast_to(b_ref[...], o_ref.shape)

    c = c_ref[...]
    o_ref[...] += _dot_split(c * jax.nn.sigmoid(c), w_ref[...])


def _ada_call(c_pad, w_ada, b_ada):
    rows = c_pad.shape[0]
    n_in, n_out = w_ada.shape
    bk = ADA_K_BLOCK
    return pl.pallas_call(
        _ada_kernel,
        out_shape=jax.ShapeDtypeStruct((rows, n_out), F32),
        grid=(n_in // bk,),
        in_specs=[
            pl.BlockSpec((rows, bk), lambda k: (0, k)),
            pl.BlockSpec((bk, n_out), lambda k: (k, 0)),
            pl.BlockSpec((1, n_out), lambda k: (0, 0)),
        ],
        out_specs=pl.BlockSpec((rows, n_out), lambda k: (0, 0)),
        compiler_params=_cparams(1),
        name="ada",
    )(c_pad, w_ada, b_ada)


def _discretise(lr, li, ls):
    step = jnp.exp(ls)
    mag = jnp.exp(lr * step)
    ar = mag * jnp.cos(li * step)
    ai = mag * jnp.sin(li * step)
    den = lr * lr + li * li
    fr = ((ar - 1.0) * lr + ai * li) / den
    fi = (ai * lr - (ar - 1.0) * li) / den
    return ar, ai, fr, fi


def _selection(rows, cols, row_of_col):
    r = lax.broadcasted_iota(jnp.int32, (rows, cols), 0)
    c = lax.broadcasted_iota(jnp.int32, (rows, cols), 1)
    return jnp.where(r == row_of_col(c), 1.0, 0.0).astype(BF16)


def _place(x, sel):
    hi = x.astype(BF16)
    rest = x - hi.astype(F32)
    mid = rest.astype(BF16)
    lo = (rest - mid.astype(F32)).astype(BF16)
    dot = functools.partial(jnp.dot, preferred_element_type=F32)
    return dot(hi, sel) + (dot(mid, sel) + dot(lo, sel))


def _to_column(row):
    k = row.shape[1]
    r = lax.broadcasted_iota(jnp.int32, (k, k), 0)
    c = lax.broadcasted_iota(jnp.int32, (k, k), 1)
    return jnp.sum(jnp.where(r == c, jnp.broadcast_to(row, (k, k)), 0.0), axis=1, keepdims=True)


def _s5_prep_kernel(*refs, n_cast):
    refs, riders = _split_riders(refs, n_in=7, n_out=4, n_cast=n_cast)
    lam_re_ref, lam_im_ref, ls_ref, d_ref, *grouped = refs
    n, sub, p = S5_STATE, SUB, S5_GROUP
    width = sub * p

    def col_source(c):
        return jnp.where(c < width, (sub - 1) - _div_pow2(c, p), jnp.where(c < 2 * width, sub + 1, sub))

    sels = dict(
        twice=_selection(n, 2 * n, lambda c: _mod_pow2(c, n)),
        tile_rows=_selection(p, width, lambda c: _mod_pow2(c, p)).T,
        cols=_selection(2 * n, 2 * width + LANES, col_source),
    )
    gps = grouped[0].shape[0]
    pending = [_s5_prep_group(pl.program_id(0) * gps + gi, sels, lam_re_ref, lam_im_ref, ls_ref, d_ref,
                              *[r.at[gi] for r in grouped]) for gi in range(gps)]
    while pending:
        pending = [gen for gen in pending if next(gen, "done") != "done"]
    _run_riders(riders)


def _s5_prep_group(g, sels, lam_re_ref, lam_im_ref, ls_ref, d_ref, bt_ref, c_re_ref, c_im_ref,
                   wk_ref, qm_ref, tabr_ref, tabi_ref):
    n, sub, p = S5_STATE, SUB, S5_GROUP
    width = sub * p

    lr_row = lam_re_ref[pl.ds(g, 1), :]
    li_row = lam_im_ref[pl.ds(g, 1), :]
    ls_all = ls_ref[...]
    grp_lane = lax.broadcasted_iota(jnp.int32, ls_all.shape, 1)
    ls = jnp.sum(jnp.where(grp_lane == g, ls_all, 0.0), axis=1, keepdims=True)

    ar8, ai8, fr8, fi8 = _discretise(jnp.broadcast_to(lr_row, (SUBLANES, n)),
                                     jnp.broadcast_to(li_row, (SUBLANES, n)), ls)
    twice = lambda t8: _place(jnp.concatenate([t8] * (p // SUBLANES), axis=0), sels["twice"])
    a2r, a2i = twice(ar8), twice(ai8)
    first = lax.broadcasted_iota(jnp.int32, (p, 2 * n), 1) < n
    c2r = _place(c_re_ref[...], sels["twice"])
    c2i = _place(c_im_ref[...], sels["twice"])
    pr = jnp.ones_like(a2r)
    pi = jnp.zeros_like(a2r)
    ccat = []
    tbl_rows = SUBLANES * (-(-(sub + 2) // SUBLANES))
    tbl_row = lax.broadcasted_iota(jnp.int32, (tbl_rows, 2 * n), 0)
    as_tbl_row = lambda re2, im2: jnp.concatenate([jnp.where(first, re2, im2)] * (tbl_rows // p + 1),
                                                  axis=0)[:tbl_rows]
    tbl = jnp.zeros((tbl_rows, 2 * n), F32)
    for d in range(sub + 1):
        ccat.append(c2r * jnp.where(first, pr, -pi) + c2i * jnp.where(first, -pi, -pr))
        tbl = jnp.where(tbl_row == d, as_tbl_row(pr, pi), tbl)
        pr, pi = _cmul(a2r, a2i, pr, pi)
    tbl = jnp.where(tbl_row == sub + 1, as_tbl_row(twice(fr8), twice(fi8)), tbl)
    yield
    qm_ref[...] = jnp.concatenate(ccat[1:], axis=0).astype(BF16)
    rcat = jnp.concatenate(ccat[:sub], axis=0)

    tbl_t = jnp.concatenate([tbl, jnp.zeros((2 * n - tbl_rows, 2 * n), F32)], axis=0).T
    cols = _place(tbl_t, sels["cols"])
    yield
    apr, fr, a16r = cols[:n, :width], cols[:n, width:2 * width], cols[:n, 2 * width:]
    api, fi, a16i = cols[n:, :width], cols[n:, width:2 * width], cols[n:, 2 * width:]
    b_tiled = _place_rows(sels["tile_rows"], bt_ref[...]).T
    btr, bti = b_tiled[:n], b_tiled[n:]
    bbr, bbi = _cmul(fr, fi, btr, bti)
    pmr, pmi = _cmul(apr, api, bbr, bbi)
    wk_ref[width:width + n, :] = pmr.astype(BF16)
    wk_ref[width + n:, :] = pmi.astype(BF16)
    yield

    bbcat = jnp.concatenate([bbr, bbi], axis=0)
    kt = _dot_split(rcat, bbcat)
    yield
    d_col = _to_column(d_ref[pl.ds(g, 1), :])
    row_p = lax.broadcasted_iota(jnp.int32, (p, width), 0)
    lane_p = lax.broadcasted_iota(jnp.int32, (p, width), 1)
    d_diag = jnp.where(row_p == _mod_pow2(lane_p, p), d_col, 0.0)
    kt = jnp.concatenate([kt[:p] + d_diag, kt[p:]], axis=0)
    col_blk = _div_pow2(lax.broadcasted_iota(jnp.int32, (width, width), 1), p)
    m16 = jnp.zeros((width, width), F32)
    for k in range(sub):
        if k == 0:
            shifted = kt
        else:
            shifted = jnp.concatenate(
                [jnp.zeros((k * p, width), F32), kt[:width - k * p]], axis=0)
        m16 = jnp.where(col_blk == k, shifted, m16)
    wk_ref[:width, :] = m16.astype(BF16)

    tabr_ref[...] = a16r
    tabi_ref[...] = a16i


def _s5_prep_call(lam_re, lam_im, log_step, bt, c_re, c_im, d, cast_riders=()):
    g, n, p, sub = S5_GROUPS, S5_STATE, S5_GROUP, SUB
    width = sub * p
    grp = lambda shape: pl.BlockSpec((PREP_GROUPS_PER_STEP,) + shape, lambda i: (i, 0, 0))
    whole = lambda a: pl.BlockSpec(a.shape, lambda i: (0,) * a.ndim)
    n_steps = g // PREP_GROUPS_PER_STEP
    rider_specs, rider_shapes = _cast_rider_specs(cast_riders, (n_steps,))
    return pl.pallas_call(
        functools.partial(_s5_prep_kernel, n_cast=len(cast_riders)),
        out_shape=(
            jax.ShapeDtypeStruct((g, width + 2 * n, width), BF16),
            jax.ShapeDtypeStruct((g, width, 2 * n), BF16),
            jax.ShapeDtypeStruct((g, n, LANES), F32),
            jax.ShapeDtypeStruct((g, n, LANES), F32),
            *rider_shapes,
        ),
        grid=(n_steps,),
        in_specs=[whole(lam_re), whole(lam_im), whole(log_step), whole(d)]
        + [grp((p, 2 * n))] + [grp((p, n))] * 2 + rider_specs,
        out_specs=(grp((width + 2 * n, width)), grp((width, 2 * n)),
                   grp((n, LANES)), grp((n, LANES)), *rider_specs),
        compiler_params=_cparams(1),
        name="s5_prep",
    )(lam_re, lam_im, log_step, d, bt, c_re, c_im, *cast_riders)


def _mod_vec(mod_ref, stream, idx):
    return mod_ref[stream:stream + 1, idx * D_MODEL:(idx + 1) * D_MODEL]


def _inproj_kernel(x_ref, mod_ref, g1_ref, w_ref, zt_ref, zuv_ref, hs_ref,
                   *, n_ph, rows, lane_rows, streams):
    m = rows * n_ph
    assert len(streams) in (1, rows)
    shift, scale = (jnp.concatenate([_mod_vec(mod_ref, s, idx) for s in streams], axis=0)
                    for idx in (0, 1))
    gain = (g1_ref[...] * (1.0 + scale))[:, None, :]
    shift = shift[:, None, :]
    x3 = x_ref[...]
    h = (x3 * _rms_scale(x3) * gain + shift).reshape(m, D_MODEL)
    zuv = jnp.dot(h.astype(BF16), w_ref[:, S5_WIDTH:], preferred_element_type=F32)
    zuv_ref[...] = zuv.reshape(rows, n_ph, 2 * GM_WIDTH)

    n_lb = D_MODEL // LANES
    for lb in range(n_lb):
        hs_ref[lb] = h[:, lb * LANES:(lb + 1) * LANES]
    hp = jnp.concatenate(
        [jnp.concatenate([hs_ref[lb, pl.ds(ph, rows, stride=n_ph), :] for lb in range(n_lb)], axis=1)
         for ph in range(n_ph)], axis=0).astype(BF16)
    z5 = jnp.dot(hp, w_ref[:, :S5_WIDTH], preferred_element_type=F32)
    for ph in range(n_ph):
        zz = z5[ph * rows:(ph + 1) * rows]
        if lane_rows > rows:
            zz = jnp.concatenate([zz, jnp.zeros((lane_rows - rows, S5_WIDTH), F32)], axis=0)
        zt = zz.T.reshape(S5_GROUPS, S5_GROUP, lane_rows)
        zt_ref[:, ph, :, :] = zt.astype(BF16)


def _phase_blocks(n_rows, row_blk):
    if row_blk % LANES == 0:
        assert n_rows % row_blk == 0
        return n_rows // row_blk, row_blk, n_rows
    assert row_blk == n_rows
    lane_rows = -(-n_rows // LANES) * LANES
    return 1, lane_rows, lane_rows


def _inproj_call(x3, mod_all, streams, norm1_g, w_in_b, *, n_ph, row_blk):
    rows, t, _ = x3.shape
    n_rb, lane_blk, lane_rows = _phase_blocks(rows, row_blk)
    kern = functools.partial(_inproj_kernel, n_ph=n_ph, rows=row_blk, lane_rows=lane_blk,
                             streams=streams)
    return pl.pallas_call(
        kern,
        out_shape=(
            jax.ShapeDtypeStruct((S5_GROUPS, t, S5_GROUP, lane_rows), BF16),
            jax.ShapeDtypeStruct((rows, t, 2 * GM_WIDTH), F32),
        ),
        grid=(n_rb, t // n_ph),
        in_specs=[
            pl.BlockSpec((row_blk, n_ph, D_MODEL), lambda i, j: (i, j, 0)),
            _const_spec(mod_all.shape),
            _const_spec((1, D_MODEL)),
            _const_spec((D_MODEL, IN_WIDTH)),
        ],
        out_specs=(
            pl.BlockSpec((S5_GROUPS, n_ph, S5_GROUP, lane_blk), lambda i, j: (0, j, 0, i)),
            pl.BlockSpec((row_blk, n_ph, 2 * GM_WIDTH), lambda i, j: (i, j, 0)),
        ),
        scratch_shapes=[pltpu.VMEM((D_MODEL // LANES, row_blk * n_ph, LANES), F32)],
        compiler_params=_cparams(2),
        name="inproj",
    )(x3, mod_all, norm1_g, w_in_b)


def _s5_kernel(*refs, n_sub, lanes, scan, final_rows, n_cast):
    refs, riders = _split_riders(refs, n_in=5 if scan else 6, n_out=2, n_cast=n_cast)
    if scan:
        zt_ref, wk_ref, qm_ref, tabr_ref, tabi_ref, yt_ref, sf_ref, ybuf, lbuf = refs
        s0_ref = None
    else:
        zt_ref, wk_ref, qm_ref, tabr_ref, tabi_ref, s0_ref, yt_ref, sf_ref, ybuf, lbuf = refs
    groups = range(zt_ref.shape[0])
    n = S5_STATE
    width = SUB * S5_GROUP

    widen = lambda tile: jnp.concatenate([tile] * (lanes // LANES), axis=1)
    sub_pows = []
    for g in groups:
        a1 = (widen(tabr_ref[g]), widen(tabi_ref[g]))
        pows = [a1]
        for _ in range(n_sub - 1):
            pows.append(_cmul(*a1, *pows[-1]))
        sub_pows.append(pows)

    local = []
    for g in groups:
        wk = wk_ref[g]
        lr = li = None
        for j in range(n_sub):
            u = zt_ref[g, j * width:(j + 1) * width, :]
            r = jnp.dot(wk, u, preferred_element_type=F32)
            ybuf[g, j * width:(j + 1) * width, :] = r[:width]
            wr = r[width:width + n]
            wi = r[width + n:]
            if j == 0:
                lr, li = wr, wi
            else:
                tr, ti = _cmul(*sub_pows[g][0], lr, li)
                lr, li = tr + wr, ti + wi
            lbuf[g, j, :n, :] = lr
            lbuf[g, j, n:, :] = li
        local.append((lr, li))

    if scan:
        lane = lax.broadcasted_iota(jnp.int32, (n, lanes), 1)
        xs = list(local)
        ms = [sub_pows[g][n_sub - 1] for g in groups]
        for i in range(int(math.log2(lanes))):
            sh = 1 << i
            for g in groups:
                xr, xi = xs[g]
                rr = jnp.where(lane >= sh, pltpu.roll(xr, sh, 1), 0.0)
                ri = jnp.where(lane >= sh, pltpu.roll(xi, sh, 1), 0.0)
                tr, ti = _cmul(*ms[g], rr, ri)
                xs[g] = (xr + tr, xi + ti)
                ms[g] = _cmul(*ms[g], *ms[g])
        entering = [(jnp.where(lane >= 1, pltpu.roll(xr, 1, 1), 0.0),
                     jnp.where(lane >= 1, pltpu.roll(xi, 1, 1), 0.0)) for xr, xi in xs]
    else:
        assert lanes == LANES
        entering = []
        for g in groups:
            s0 = s0_ref[:, g, :]
            s0t = jnp.concatenate([s0, jnp.zeros((LANES - s0.shape[0], 2 * n), F32)], axis=0).T
            entering.append((s0t[:n], s0t[n:]))

    keep = slice(lanes - LANES, lanes)
    for g in groups:
        qm = qm_ref[g]
        sr, si = entering[g]
        for j in range(n_sub):
            if j == 0:
                pr, pi = sr, si
            else:
                tr, ti = _cmul(*sub_pows[g][j - 1], sr, si)
                pr, pi = lbuf[g, j - 1, :n, :] + tr, lbuf[g, j - 1, n:, :] + ti
            sp = jnp.concatenate([pr, pi], axis=0).astype(BF16)
            y = ybuf[g, j * width:(j + 1) * width, :] + jnp.dot(qm, sp, preferred_element_type=F32)
            yt_ref[g, j * width:(j + 1) * width, :] = y.astype(BF16)
        tr, ti = _cmul(*sub_pows[g][n_sub - 1], sr, si)
        ends = jnp.concatenate([lbuf[g, n_sub - 1, :n, keep] + tr[:, keep],
                                lbuf[g, n_sub - 1, n:, keep] + ti[:, keep]], axis=0).T
        sf_ref[:, g, :] = ends[final_rows[0]:final_rows[1]]
    _run_riders(riders)


def _s5_call(zt, wk, qm, tabr, tabi, s0, *, n_streams, n_sub, scan, cast_riders=()):
    g, rows, lanes = zt.shape
    n = S5_STATE
    width = SUB * S5_GROUP
    assert lanes % LANES == 0 and lanes & (lanes - 1) == 0
    gps = S5_GROUPS_PER_STEP_LONG if scan else S5_GROUPS_PER_STEP_SHORT
    grp = lambda shape: pl.BlockSpec((gps,) + shape, lambda i: (i, 0, 0))
    per_stream = pl.BlockSpec((n_streams, gps, 2 * n), lambda i: (0, i, 0))
    in_specs = [grp((rows, lanes)), grp((width + 2 * n, width)), grp((width, 2 * n)),
                grp((n, LANES)), grp((n, LANES))]
    args = [zt, wk, qm, tabr, tabi]
    if scan:
        assert n_streams == 1
        final_rows = (LANES - 1, LANES)
    else:
        in_specs.append(per_stream)
        args.append(s0)
        final_rows = (0, n_streams)
    rider_specs, rider_shapes = _cast_rider_specs(cast_riders, (g // gps,))
    kern = functools.partial(_s5_kernel, n_sub=n_sub, lanes=lanes, scan=scan, final_rows=final_rows,
                             n_cast=len(cast_riders))
    return pl.pallas_call(
        kern,
        out_shape=(jax.ShapeDtypeStruct((g, rows, lanes), BF16),
                   jax.ShapeDtypeStruct((n_streams, g, 2 * n), F32), *rider_shapes),
        grid=(g // gps,),
        in_specs=in_specs + rider_specs,
        out_specs=(grp((rows, lanes)), per_stream, *rider_specs),
        scratch_shapes=[pltpu.VMEM((gps, rows, lanes), F32),
                        pltpu.VMEM((gps, n_sub, 2 * n, lanes), F32)],
        compiler_params=_cparams(1),
        name="s5",
    )(*args, *cast_riders)


def _glu_kernel(*refs, n_ph, rows, lane_rows, n_cast):
    (yt_ref, w_ref, b_ref, m_ref), riders = _split_riders(refs, n_in=3, n_out=1, n_cast=n_cast)
    _run_riders(riders)
    gs = []
    for ph in range(n_ph):
        yt = yt_ref[:, ph, :, :].astype(F32).reshape(S5_WIDTH, lane_rows)
        gs.append(_gelu(yt.T[:rows]))
    gy = jnp.concatenate(gs, axis=0)
    gate = jnp.dot(gy.astype(BF16), w_ref[...], preferred_element_type=F32) + b_ref[...]
    m = gy * _sigmoid(gate)
    for ph in range(n_ph):
        _oct_store(m_ref, ph, m[ph * rows:(ph + 1) * rows])


def _glu_call(yt4, w_glu_b, b_glu, *, rows, n_ph, row_blk, cast_riders=()):
    g, t, p, lane_rows = yt4.shape
    n_rb, lane_blk, lane_rows_expected = _phase_blocks(rows, row_blk)
    assert lane_rows == lane_rows_expected
    grid = (n_rb, t // n_ph)
    rider_specs, rider_shapes = _cast_rider_specs(cast_riders, grid)
    kern = functools.partial(_glu_kernel, n_ph=n_ph, rows=row_blk, lane_rows=lane_blk,
                             n_cast=len(cast_riders))
    oct_shape = _oct_shape(rows, t, S5_WIDTH)
    return pl.pallas_call(
        kern,
        out_shape=(jax.ShapeDtypeStruct(oct_shape, F32), *rider_shapes),
        grid=grid,
        in_specs=[
            pl.BlockSpec((g, n_ph, p, lane_blk), lambda i, j: (0, j, 0, i)),
            _const_spec((S5_WIDTH, S5_WIDTH)),
            _const_spec((1, S5_WIDTH)),
            *rider_specs,
        ],
        out_specs=(pl.BlockSpec((row_blk // OCT, oct_shape[1], n_ph, OCT, LANES),
                                lambda i, j: (i, 0, j, 0, 0)), *rider_specs),
        compiler_params=_cparams(2),
        name="glu",
    )(yt4, w_glu_b, b_glu, *cast_riders)


def _main_kernel(x_ref, m_ref, zuv_ref, mod_ref, g2_ref, gf_ref, lng_ref, lnb_ref, gw_ref, gbt_ref,
                 wo_ref, wgu_ref, wd_ref, y_ref, *rest, tm, cl, seq, streams):
    *maybe_v_out_ref, v_ref, ygm_ref, attn_ref, act_ref = rest
    t = tm // OCT
    hd = GM_HEAD_DIM

    attn_ref[...] = jnp.dot(_oct_load(m_ref, t).astype(BF16), wo_ref[:S5_WIDTH, :],
                            preferred_element_type=F32)

    gv = _gelu(zuv_ref[:, GM_WIDTH:])
    cen = gv - jnp.mean(gv, axis=-1, keepdims=True)
    var = jnp.mean(cen * cen, axis=-1, keepdims=True)
    v = cen * lax.rsqrt(var + EPS) * lng_ref[...] + lnb_ref[...]
    for v_out_ref in maybe_v_out_ref:
        v_out_ref[...] = v
    v_ref[...] = v.astype(BF16)

    blk_i = _div_pow2(lax.broadcasted_iota(jnp.int32, (cl, cl), 0), CHUNK)
    blk_j = _div_pow2(lax.broadcasted_iota(jnp.int32, (cl, cl), 1), CHUNK)
    causal = blk_j <= blk_i
    first_head = lax.broadcasted_iota(jnp.int32, (cl, 2 * hd), 1) < hd
    for pr in range(GM_HEADS // 2):
        h0, h1 = 2 * pr, 2 * pr + 1
        wm = jnp.concatenate(
            [jnp.where(causal, gw_ref[h, :cl, :cl], 0.0) for h in (h0, h1)], axis=1).astype(BF16)
        bias = jnp.where(first_head, gbt_ref[:cl, h0:h0 + 1], gbt_ref[:cl, h1:h1 + 1])
        cs = slice(h0 * hd, (h1 + 1) * hd)
        for ci in range(tm // cl):
            rs = slice(ci * cl, (ci + 1) * cl)
            vv = v_ref[rs, cs]
            zero = jnp.zeros_like(vv)
            rhs = jnp.concatenate([jnp.where(first_head, vv, zero), jnp.where(first_head, zero, vv)],
                                  axis=0)
            mixed = jnp.dot(wm, rhs, preferred_element_type=F32) + bias
            ygm_ref[rs, cs] = (_gelu(zuv_ref[rs, cs]) * mixed).astype(BF16)

    attn = attn_ref[...] + jnp.dot(ygm_ref[...], wo_ref[S5_WIDTH:, :], preferred_element_type=F32)

    def mod_rows(idx):
        if len(streams) == 1:
            return _mod_vec(mod_ref, streams[0], idx)
        return jnp.concatenate(
            [jnp.broadcast_to(_mod_vec(mod_ref, s, idx), (seq, D_MODEL)) for s in streams], axis=0)

    gate1, shift2, scale2, gate2 = mod_rows(2), mod_rows(3), mod_rows(4), mod_rows(5)
    x1 = x_ref[...] + gate1 * attn
    h2 = (x1 * _rms_scale(x1) * (g2_ref[...] * (1.0 + scale2)) + shift2).astype(BF16)

    assert D_FF % MXU_DIM == 0
    for lo in range(0, D_FF, MXU_DIM):
        hi = lo + MXU_DIM
        gg = jnp.dot(h2, wgu_ref[:, lo:hi], preferred_element_type=F32)
        up = jnp.dot(h2, wgu_ref[:, D_FF + lo:D_FF + hi], preferred_element_type=F32)
        act_ref[:, lo:hi] = (gg * jax.nn.sigmoid(gg) * up).astype(BF16)
    acc = jnp.dot(act_ref[...], wd_ref[...], preferred_element_type=F32)
    x2 = x1 + gate2 * acc
    y_ref[...] = x2 * _rms_scale(x2) * gf_ref[...]


def _main_sets_kernel(*refs, sets, n_shared):
    shared = refs[:n_shared]
    scratch = refs[-4:]
    step = pl.program_id(0)
    pos = n_shared
    out_pos = n_shared + 3 * len(sets)
    for st in sets:
        ins = refs[pos:pos + 3]
        pos += 3
        n_out = 2 if st["want_v"] else 1
        outs = refs[out_pos:out_pos + n_out]
        out_pos += n_out
        tm = st["tm"]

        @pl.when(jnp.logical_and(step >= st["start"], step < st["start"] + st["n_tiles"]))
        def _(ins=ins, outs=outs, st=st, tm=tm):
            _main_kernel(*ins, *shared, *outs, *[r.at[:tm] for r in scratch],
                         tm=tm, cl=st["cl"], seq=st["seq"], streams=st["streams"])


def _main_call(stream_sets, mod_all, norm2_g, final_g, ln_g, ln_b, gm_w, gm_bt, w_out_b, w_gu_b, w_down_b):
    shared = (mod_all, norm2_g, final_g, ln_g, ln_b, gm_w, gm_bt, w_out_b, w_gu_b, w_down_b)
    set_args, set_in_specs, out_shape, out_specs, statics = [], [], [], [], []
    start = 0
    for st in stream_sets:
        n_tok = st["x2d"].shape[0]
        n_oct, n_lb, t, _, _ = st["m"].shape
        tm = OCT * t
        seq, streams = st["seq"], st["streams"]
        assert n_oct * tm == n_tok
        assert len(streams) == 1 or (n_oct == 1 and len(streams) * seq == tm)

        def tile_index(i, start=start, n_oct=n_oct):
            return (jnp.clip(i - start, 0, n_oct - 1), 0)

        tok = lambda width: pl.BlockSpec((tm, width), tile_index)
        oct_rows = n_lb * t * OCT
        set_args += [st["x2d"], st["m"].reshape(n_oct * oct_rows, LANES), st["zuv"]]
        set_in_specs += [tok(D_MODEL), pl.BlockSpec((oct_rows, LANES), tile_index), tok(2 * GM_WIDTH)]
        out_shape.append(jax.ShapeDtypeStruct((n_tok, D_MODEL), F32))
        out_specs.append(tok(D_MODEL))
        if st["want_v"]:
            out_shape.append(jax.ShapeDtypeStruct((n_tok, GM_WIDTH), F32))
            out_specs.append(tok(GM_WIDTH))
        statics.append(dict(start=start, n_tiles=n_oct, tm=tm, cl=min(GM_CHUNK, seq), seq=seq,
                            streams=streams, want_v=st["want_v"]))
        start += n_oct
    tm_max = max(s["tm"] for s in statics)
    outs = pl.pallas_call(
        functools.partial(_main_sets_kernel, sets=statics, n_shared=len(shared)),
        out_shape=tuple(out_shape),
        grid=(start,),
        in_specs=[_const_spec(a.shape) for a in shared] + set_in_specs,
        out_specs=tuple(out_specs),
        scratch_shapes=[pltpu.VMEM((tm_max, GM_WIDTH), BF16), pltpu.VMEM((tm_max, GM_WIDTH), BF16),
                        pltpu.VMEM((tm_max, D_MODEL), F32), pltpu.VMEM((tm_max, D_FF), BF16)],
        compiler_params=_cparams(1),
        name="main",
    )(*shared, *set_args)
    results, pos = [], 0
    for s in statics:
        n_out = 2 if s["want_v"] else 1
        results.append(tuple(outs[pos:pos + n_out]))
        pos += n_out
    return results


def _mixer_front(x, mod_all, streams, s0, prm, *, t, n_ph, row_blk, want_v, f32_weights=None):
    b, seq, _ = x.shape
    n_chunks = seq // t
    rows = b * n_chunks
    scan = s0 is None
    assert (b == 1) if scan else (n_chunks == 1)
    n_sub = t // SUB

    zt4, zuv = _inproj_call(
        x.reshape(rows, t, D_MODEL), mod_all, streams, prm["norm1_g"], prm["w_in"],
        n_ph=n_ph, row_blk=row_blk)
    lane_rows = zt4.shape[-1]

    assert not scan or rows == lane_rows
    pending = dict(f32_weights or {})
    on_glu = {k: pending.pop(k) for k in ("w_gu",) if k in pending}
    yt, fin, *cast = _s5_call(zt4.reshape(S5_GROUPS, t * S5_GROUP, lane_rows),
                              prm["wk"], prm["qm"], prm["tabr"], prm["tabi"], s0,
                              n_streams=b, n_sub=n_sub, scan=scan, cast_riders=tuple(pending.values()))
    prm = {**prm, **dict(zip(pending.keys(), cast))}
    m, *cast = _glu_call(yt.reshape(S5_GROUPS, t, S5_GROUP, lane_rows), prm["w_glu"], prm["b_glu"],
                         rows=rows, n_ph=n_ph, row_blk=row_blk, cast_riders=tuple(on_glu.values()))
    prm = {**prm, **dict(zip(on_glu.keys(), cast))}

    n_tok = b * seq
    stream_set = dict(x2d=x.reshape(n_tok, D_MODEL), m=m, zuv=zuv.reshape(n_tok, 2 * GM_WIDTH),
                      streams=streams, seq=seq, want_v=want_v)

    return stream_set, fin[..., :S5_STATE], fin[..., S5_STATE:], prm


def kernel(x_prompt, x_sample, state_s5_re, state_s5_im, c_prompt, c_sample, norm1_g, norm2_g, w_ada, b_ada, w_in, s5_lambda_re, s5_lambda_im, s5_log_step, s5_b_re, s5_b_im, s5_c_re, s5_c_im, s5_d, s5_w_glu, s5_b_glu, gm_ln_g, gm_ln_b, gm_w_s, gm_b_s, w_out, ffn_w_gu, ffn_w_down, final_g):
    depth = w_in.shape[0]
    assert depth == 1
    l = 0
    n_p = c_prompt.shape[0]
    n_s = c_sample.shape[0]

    c_all = jnp.concatenate([c_prompt, c_sample], axis=0)
    c_pad = jnp.pad(c_all, ((0, -c_all.shape[0] % SUBLANES), (0, 0)))
    mod_all = _ada_call(c_pad, w_ada[l], b_ada[l][None, :])
    streams_p = tuple(range(n_p))
    streams_s = tuple(range(n_p, n_p + n_s))

    wk, qm, tabr, tabi, w_in_b, w_out_b, w_down_b = _s5_prep_call(
        s5_lambda_re[l], s5_lambda_im[l], s5_log_step[l][None, :],
        jnp.concatenate([jnp.swapaxes(s5_b_re[l], 1, 2), jnp.swapaxes(s5_b_im[l], 1, 2)], axis=-1),
        s5_c_re[l], s5_c_im[l], s5_d[l], cast_riders=(w_in[l], w_out[l], ffn_w_down[l]))

    prm = dict(
        norm1_g=norm1_g[l][None, :], norm2_g=norm2_g[l][None, :], final_g=final_g[None, :],
        w_in=w_in_b, w_out=w_out_b, w_down=w_down_b, ln_g=gm_ln_g[l][None, :], ln_b=gm_ln_b[l][None, :],
        wk=wk, qm=qm, tabr=tabr, tabi=tabi, b_glu=s5_b_glu[l][None, :],
        gm_w=gm_w_s[l], gm_bt=jnp.transpose(gm_b_s[l]),
    )
    later_weights = dict(w_glu=s5_w_glu[l], w_gu=ffn_w_gu[l])

    set_p, pre, pim, prm = _mixer_front(
        x_prompt, mod_all, streams_p, None, prm, f32_weights=later_weights,
        t=S5_LONG_SUBS * SUB, n_ph=SUBLANES, row_blk=LANES, want_v=False)
    s0 = jnp.concatenate([state_s5_re[l], state_s5_im[l]], axis=-1)
    n_b, seq_s, _ = x_sample.shape
    set_s, sre, sim, _ = _mixer_front(
        x_sample, mod_all, streams_s, s0, prm, t=seq_s, n_ph=seq_s, row_blk=n_b, want_v=True)

    (yp,), (ys, vs) = _main_call(
        [set_p, set_s], mod_all, prm["norm2_g"], prm["final_g"], prm["ln_g"], prm["ln_b"],
        prm["gm_w"], prm["gm_bt"], prm["w_out"], prm["w_gu"], prm["w_down"])
    return (yp.reshape(x_prompt.shape), ys.reshape(x_sample.shape), pre[None], pim[None],
            sre[None], sim[None], vs.reshape(n_b, seq_s, GM_WIDTH)[None])
```

```python
import functools
import math

import jax
import jax.numpy as jnp
from jax import lax
from jax.experimental import pallas as pl
from jax.experimental.pallas import tpu as pltpu

D_MODEL = 1024
S5_WIDTH = 512
S5_GROUP = 16
S5_GROUPS = 32
S5_STATE = 64
GM_WIDTH = 512
GM_CHUNK = 128
GM_HEADS = 8
GM_HEAD_DIM = 64
CHUNK = 64
IN_WIDTH = S5_WIDTH + 2 * GM_WIDTH
D_FF = 2816
EPS = 1e-6

LANES = 128
SUBLANES = 8
BF16_SUBLANES = 16
MXU_DIM = 256
VMEM_LIMIT_BYTES = 56 * 1024 * 1024

SUB = MXU_DIM // S5_GROUP
S5_LONG_SUBS = 4
PREP_GROUPS_PER_STEP = 8
S5_GROUPS_PER_STEP_LONG = 8
S5_GROUPS_PER_STEP_SHORT = 16
ADA_K_BLOCK = 256

F32 = jnp.float32
BF16 = jnp.bfloat16


def _cparams(n_grid_axes):
    return pltpu.CompilerParams(
        dimension_semantics=("arbitrary",) * n_grid_axes,
        vmem_limit_bytes=VMEM_LIMIT_BYTES,
    )


def _const_spec(shape):
    nd = len(shape)
    return pl.BlockSpec(shape, lambda *_: (0,) * nd, pipeline_mode=pl.Buffered(1))


def _rms_scale(x):
    return lax.rsqrt(jnp.mean(x * x, axis=-1, keepdims=True) + EPS)


_GELU_C0 = math.sqrt(2.0 / math.pi)
_GELU_C1 = 0.044715 * _GELU_C0


def _gelu(x):
    hx = 0.5 * x
    return hx + hx * jnp.tanh(x * (_GELU_C0 + _GELU_C1 * (x * x)))


def _sigmoid(x):
    return 0.5 * jnp.tanh(0.5 * x) + 0.5


def _cmul(ar, ai, xr, xi):
    return ar * xr - ai * xi, ar * xi + ai * xr


def _div_pow2(idx, divisor):
    shift = divisor.bit_length() - 1
    assert divisor == 1 << shift
    return lax.shift_right_logical(idx, shift)


def _mod_pow2(idx, divisor):
    assert divisor & (divisor - 1) == 0
    return lax.bitwise_and(idx, divisor - 1)


def _place_rows(sel, x):
    hi = x.astype(BF16)
    rest = x - hi.astype(F32)
    mid = rest.astype(BF16)
    lo = (rest - mid.astype(F32)).astype(BF16)
    dot = functools.partial(jnp.dot, preferred_element_type=F32)
    return dot(sel, hi) + (dot(sel, mid) + dot(sel, lo))


def _dot_split(x, y):
    x_hi, y_hi = x.astype(BF16), y.astype(BF16)
    x_lo = (x - x_hi.astype(F32)).astype(BF16)
    y_lo = (y - y_hi.astype(F32)).astype(BF16)
    dot = functools.partial(jnp.dot, preferred_element_type=F32)
    return dot(x_hi, y_hi) + (dot(x_lo, y_hi) + dot(x_hi, y_lo))


OCT = SUBLANES


def _oct_shape(n_rows, t, width):
    assert n_rows % OCT == 0 and width % LANES == 0
    return (n_rows // OCT, width // LANES, t, OCT, LANES)


def _oct_store(ref, ph, val):
    n_oct, n_lb = ref.shape[0], ref.shape[1]
    for lb in range(n_lb):
        ref[:, lb, ph, :, :] = val[:, lb * LANES:(lb + 1) * LANES].reshape(n_oct, OCT, LANES)


def _oct_load(ref, t):
    n_lb = ref.shape[0] // (t * OCT)
    chunks = []
    for c in range(OCT):
        chunks.append(jnp.concatenate(
            [ref[pl.ds(lb * t * OCT + c, t, stride=OCT), :] for lb in range(n_lb)], axis=1))
    return jnp.concatenate(chunks, axis=0)


def _cast_rider_specs(arrays, grid):
    n_steps = math.prod(grid)

    def row_block(*idx):
        step = 0
        for i, extent in zip(idx, grid):
            step = step * extent + i
        return (step, 0)

    specs, shapes = [], []
    for a in arrays:
        rows, cols = a.shape
        blk = rows // n_steps
        assert blk * n_steps == rows and blk % BF16_SUBLANES == 0
        specs.append(pl.BlockSpec((blk, cols), row_block))
        shapes.append(jax.ShapeDtypeStruct((rows, cols), BF16))
    return specs, shapes


def _split_riders(refs, n_in, n_out, n_cast):
    ins, refs = refs[:n_in], refs[n_in:]
    cast_in, refs = refs[:n_cast], refs[n_cast:]
    outs, refs = refs[:n_out], refs[n_out:]
    cast_out, scratch = refs[:n_cast], refs[n_cast:]
    return (*ins, *outs, *scratch), list(zip(cast_in, cast_out))


def _run_riders(pairs):
    for src, dst in pairs:
        dst[...] = src[...].astype(BF16)


def _ada_kernel(c_ref, w_ref, b_ref, o_ref):
    @pl.when(pl.program_id(0) == 0)
    def _():
        o_ref[...] = jnp.broadcast_to(b_ref[...], o_ref.shape)

    c = c_ref[...]
    o_ref[...] += _dot_split(c * jax.nn.sigmoid(c), w_ref[...])


def _ada_call(c_pad, w_ada, b_ada):
    rows = c_pad.shape[0]
    n_in, n_out = w_ada.shape
    bk = ADA_K_BLOCK
    return pl.pallas_call(
        _ada_kernel,
        out_shape=jax.ShapeDtypeStruct((rows, n_out), F32),
        grid=(n_in // bk,),
        in_specs=[
            pl.BlockSpec((rows, bk), lambda k: (0, k)),
            pl.BlockSpec((bk, n_out), lambda k: (k, 0)),
            pl.BlockSpec((1, n_out), lambda k: (0, 0)),
        ],
        out_specs=pl.BlockSpec((rows, n_out), lambda k: (0, 0)),
        compiler_params=_cparams(1),
        name="ada",
    )(c_pad, w_ada, b_ada)


def _discretise(lr, li, ls):
    step = jnp.exp(ls)
    mag = jnp.exp(lr * step)
    ar = mag * jnp.cos(li * step)
    ai = mag * jnp.sin(li * step)
    den = lr * lr + li * li
    fr = ((ar - 1.0) * lr + ai * li) / den
    fi = (ai * lr - (ar - 1.0) * li) / den
    return ar, ai, fr, fi


def _selection(rows, cols, row_of_col):
    r = lax.broadcasted_iota(jnp.int32, (rows, cols), 0)
    c = lax.broadcasted_iota(jnp.int32, (rows, cols), 1)
    return jnp.where(r == row_of_col(c), 1.0, 0.0).astype(BF16)


def _place(x, sel):
    hi = x.astype(BF16)
    rest = x - hi.astype(F32)
    mid = rest.astype(BF16)
    lo = (rest - mid.astype(F32)).astype(BF16)
    dot = functools.partial(jnp.dot, preferred_element_type=F32)
    return dot(hi, sel) + (dot(mid, sel) + dot(lo, sel))


def _to_column(row):
    k = row.shape[1]
    r = lax.broadcasted_iota(jnp.int32, (k, k), 0)
    c = lax.broadcasted_iota(jnp.int32, (k, k), 1)
    return jnp.sum(jnp.where(r == c, jnp.broadcast_to(row, (k, k)), 0.0), axis=1, keepdims=True)


def _s5_prep_kernel(*refs, n_cast):
    refs, riders = _split_riders(refs, n_in=7, n_out=4, n_cast=n_cast)
    lam_re_ref, lam_im_ref, ls_ref, d_ref, *grouped = refs
    n, sub, p = S5_STATE, SUB, S5_GROUP
    width = sub * p

    def col_source(c):
        return jnp.where(c < width, (sub - 1) - _div_pow2(c, p), jnp.where(c < 2 * width, sub + 1, sub))

    sels = dict(
        twice=_selection(n, 2 * n, lambda c: _mod_pow2(c, n)),
        tile_rows=_selection(p, width, lambda c: _mod_pow2(c, p)).T,
        cols=_selection(2 * n, 2 * width + LANES, col_source),
    )
    gps = grouped[0].shape[0]
    pending = [_s5_prep_group(pl.program_id(0) * gps + gi, sels, lam_re_ref, lam_im_ref, ls_ref, d_ref,
                              *[r.at[gi] for r in grouped]) for gi in range(gps)]
    while pending:
        pending = [gen for gen in pending if next(gen, "done") != "done"]
    _run_riders(riders)


def _s5_prep_group(g, sels, lam_re_ref, lam_im_ref, ls_ref, d_ref, bt_ref, c_re_ref, c_im_ref,
                   wk_ref, qm_ref, tabr_ref, tabi_ref):
    n, sub, p = S5_STATE, SUB, S5_GROUP
    width = sub * p

    lr_row = lam_re_ref[pl.ds(g, 1), :]
    li_row = lam_im_ref[pl.ds(g, 1), :]
    ls_all = ls_ref[...]
    grp_lane = lax.broadcasted_iota(jnp.int32, ls_all.shape, 1)
    ls = jnp.sum(jnp.where(grp_lane == g, ls_all, 0.0), axis=1, keepdims=True)

    ar8, ai8, fr8, fi8 = _discretise(jnp.broadcast_to(lr_row, (SUBLANES, n)),
                                     jnp.broadcast_to(li_row, (SUBLANES, n)), ls)
    twice = lambda t8: _place(jnp.concatenate([t8] * (p // SUBLANES), axis=0), sels["twice"])
    a2r, a2i = twice(ar8), twice(ai8)
    first = lax.broadcasted_iota(jnp.int32, (p, 2 * n), 1) < n
    c2r = _place(c_re_ref[...], sels["twice"])
    c2i = _place(c_im_ref[...], sels["twice"])
    pr = jnp.ones_like(a2r)
    pi = jnp.zeros_like(a2r)
    ccat = []
    tbl_rows = SUBLANES * (-(-(sub + 2) // SUBLANES))
    tbl_row = lax.broadcasted_iota(jnp.int32, (tbl_rows, 2 * n), 0)
    as_tbl_row = lambda re2, im2: jnp.concatenate([jnp.where(first, re2, im2)] * (tbl_rows // p + 1),
                                                  axis=0)[:tbl_rows]
    tbl = jnp.zeros((tbl_rows, 2 * n), F32)
    for d in range(sub + 1):
        ccat.append(c2r * jnp.where(first, pr, -pi) + c2i * jnp.where(first, -pi, -pr))
        tbl = jnp.where(tbl_row == d, as_tbl_row(pr, pi), tbl)
        pr, pi = _cmul(a2r, a2i, pr, pi)
    tbl = jnp.where(tbl_row == sub + 1, as_tbl_row(twice(fr8), twice(fi8)), tbl)
    yield
    qm_ref[...] = jnp.concatenate(ccat[1:], axis=0).astype(BF16)
    rcat = jnp.concatenate(ccat[:sub], axis=0)

    tbl_t = jnp.concatenate([tbl, jnp.zeros((2 * n - tbl_rows, 2 * n), F32)], axis=0).T
    cols = _place(tbl_t, sels["cols"])
    yield
    apr, fr, a16r = cols[:n, :width], cols[:n, width:2 * width], cols[:n, 2 * width:]
    api, fi, a16i = cols[n:, :width], cols[n:, width:2 * width], cols[n:, 2 * width:]
    b_tiled = _place_rows(sels["tile_rows"], bt_ref[...]).T
    btr, bti = b_tiled[:n], b_tiled[n:]
    bbr, bbi = _cmul(fr, fi, btr, bti)
    pmr, pmi = _cmul(apr, api, bbr, bbi)
    wk_ref[width:width + n, :] = pmr.astype(BF16)
    wk_ref[width + n:, :] = pmi.astype(BF16)
    yield

    bbcat = jnp.concatenate([bbr, bbi], axis=0)
    kt = _dot_split(rcat, bbcat)
    yield
    d_col = _to_column(d_ref[pl.ds(g, 1), :])
    row_p = lax.broadcasted_iota(jnp.int32, (p, width), 0)
    lane_p = lax.broadcasted_iota(jnp.int32, (p, width), 1)
    d_diag = jnp.where(row_p == _mod_pow2(lane_p, p), d_col, 0.0)
    kt = jnp.concatenate([kt[:p] + d_diag, kt[p:]], axis=0)
    col_blk = _div_pow2(lax.broadcasted_iota(jnp.int32, (width, width), 1), p)
    m16 = jnp.zeros((width, width), F32)
    for k in range(sub):
        if k == 0:
            shifted = kt
        else:
            shifted = jnp.concatenate(
                [jnp.zeros((k * p, width), F32), kt[:width - k * p]], axis=0)
        m16 = jnp.where(col_blk == k, shifted, m16)
    wk_ref[:width, :] = m16.astype(BF16)

    tabr_ref[...] = a16r
    tabi_ref[...] = a16i


def _s5_prep_call(lam_re, lam_im, log_step, bt, c_re, c_im, d, cast_riders=()):
    g, n, p, sub = S5_GROUPS, S5_STATE, S5_GROUP, SUB
    width = sub * p
    grp = lambda shape: pl.BlockSpec((PREP_GROUPS_PER_STEP,) + shape, lambda i: (i, 0, 0))
    whole = lambda a: pl.BlockSpec(a.shape, lambda i: (0,) * a.ndim)
    n_steps = g // PREP_GROUPS_PER_STEP
    rider_specs, rider_shapes = _cast_rider_specs(cast_riders, (n_steps,))
    return pl.pallas_call(
        functools.partial(_s5_prep_kernel, n_cast=len(cast_riders)),
        out_shape=(
            jax.ShapeDtypeStruct((g, width + 2 * n, width), BF16),
            jax.ShapeDtypeStruct((g, width, 2 * n), BF16),
            jax.ShapeDtypeStruct((g, n, LANES), F32),
            jax.ShapeDtypeStruct((g, n, LANES), F32),
            *rider_shapes,
        ),
        grid=(n_steps,),
        in_specs=[whole(lam_re), whole(lam_im), whole(log_step), whole(d)]
        + [grp((p, 2 * n))] + [grp((p, n))] * 2 + rider_specs,
        out_specs=(grp((width + 2 * n, width)), grp((width, 2 * n)),
                   grp((n, LANES)), grp((n, LANES)), *rider_specs),
        compiler_params=_cparams(1),
        name="s5_prep",
    )(lam_re, lam_im, log_step, d, bt, c_re, c_im, *cast_riders)


def _mod_vec(mod_ref, stream, idx):
    return mod_ref[stream:stream + 1, idx * D_MODEL:(idx + 1) * D_MODEL]


def _inproj_kernel(x_ref, mod_ref, g1_ref, w_ref, zt_ref, zuv_ref, hs_ref,
                   *, n_ph, rows, lane_rows, streams):
    m = rows * n_ph
    assert len(streams) in (1, rows)
    shift, scale = (jnp.concatenate([_mod_vec(mod_ref, s, idx) for s in streams], axis=0)
                    for idx in (0, 1))
    gain = (g1_ref[...] * (1.0 + scale))[:, None, :]
    shift = shift[:, None, :]
    x3 = x_ref[...]
    h = (x3 * _rms_scale(x3) * gain + shift).reshape(m, D_MODEL)
    zuv = jnp.dot(h.astype(BF16), w_ref[:, S5_WIDTH:], preferred_element_type=F32)
    zuv_ref[...] = zuv.reshape(rows, n_ph, 2 * GM_WIDTH).astype(BF16)

    n_lb = D_MODEL // LANES
    for lb in range(n_lb):
        hs_ref[lb] = h[:, lb * LANES:(lb + 1) * LANES]
    hp = jnp.concatenate(
        [jnp.concatenate([hs_ref[lb, pl.ds(ph, rows, stride=n_ph), :] for lb in range(n_lb)], axis=1)
         for ph in range(n_ph)], axis=0).astype(BF16)
    z5 = jnp.dot(hp, w_ref[:, :S5_WIDTH], preferred_element_type=F32)
    for ph in range(n_ph):
        zz = z5[ph * rows:(ph + 1) * rows]
        if lane_rows > rows:
            zz = jnp.concatenate([zz, jnp.zeros((lane_rows - rows, S5_WIDTH), F32)], axis=0)
        zt = zz.T.reshape(S5_GROUPS, S5_GROUP, lane_rows)
        zt_ref[:, ph, :, :] = zt.astype(BF16)


def _phase_blocks(n_rows, row_blk):
    if row_blk % LANES == 0:
        assert n_rows % row_blk == 0
        return n_rows // row_blk, row_blk, n_rows
    assert row_blk == n_rows
    lane_rows = -(-n_rows // LANES) * LANES
    return 1, lane_rows, lane_rows


def _inproj_call(x3, mod_all, streams, norm1_g, w_in_b, *, n_ph, row_blk):
    rows, t, _ = x3.shape
    n_rb, lane_blk, lane_rows = _phase_blocks(rows, row_blk)
    kern = functools.partial(_inproj_kernel, n_ph=n_ph, rows=row_blk, lane_rows=lane_blk,
                             streams=streams)
    return pl.pallas_call(
        kern,
        out_shape=(
            jax.ShapeDtypeStruct((S5_GROUPS, t, S5_GROUP, lane_rows), BF16),
            jax.ShapeDtypeStruct((rows, t, 2 * GM_WIDTH), BF16),
        ),
        grid=(n_rb, t // n_ph),
        in_specs=[
            pl.BlockSpec((row_blk, n_ph, D_MODEL), lambda i, j: (i, j, 0)),
            _const_spec(mod_all.shape),
            _const_spec((1, D_MODEL)),
            _const_spec((D_MODEL, IN_WIDTH)),
        ],
        out_specs=(
            pl.BlockSpec((S5_GROUPS, n_ph, S5_GROUP, lane_blk), lambda i, j: (0, j, 0, i)),
            pl.BlockSpec((row_blk, n_ph, 2 * GM_WIDTH), lambda i, j: (i, j, 0)),
        ),
        scratch_shapes=[pltpu.VMEM((D_MODEL // LANES, row_blk * n_ph, LANES), F32)],
        compiler_params=_cparams(2),
        name="inproj",
    )(x3, mod_all, norm1_g, w_in_b)


def _s5_kernel(*refs, n_sub, lanes, scan, final_rows, n_cast):
    refs, riders = _split_riders(refs, n_in=5 if scan else 6, n_out=2, n_cast=n_cast)
    if scan:
        zt_ref, wk_ref, qm_ref, tabr_ref, tabi_ref, yt_ref, sf_ref, ybuf, lbuf = refs
        s0_ref = None
    else:
        zt_ref, wk_ref, qm_ref, tabr_ref, tabi_ref, s0_ref, yt_ref, sf_ref, ybuf, lbuf = refs
    groups = range(zt_ref.shape[0])
    n = S5_STATE
    width = SUB * S5_GROUP

    widen = lambda tile: jnp.concatenate([tile] * (lanes // LANES), axis=1)
    sub_pows = []
    for g in groups:
        a1 = (widen(tabr_ref[g]), widen(tabi_ref[g]))
        pows = [a1]
        for _ in range(n_sub - 1):
            pows.append(_cmul(*a1, *pows[-1]))
        sub_pows.append(pows)

    local = []
    for g in groups:
        wk = wk_ref[g]
        lr = li = None
        for j in range(n_sub):
            u = zt_ref[g, j * width:(j + 1) * width, :]
            r = jnp.dot(wk, u, preferred_element_type=F32)
            ybuf[g, j * width:(j + 1) * width, :] = r[:width]
            wr = r[width:width + n]
            wi = r[width + n:]
            if j == 0:
                lr, li = wr, wi
            else:
                tr, ti = _cmul(*sub_pows[g][0], lr, li)
                lr, li = tr + wr, ti + wi
            lbuf[g, j, :n, :] = lr
            lbuf[g, j, n:, :] = li
        local.append((lr, li))

    if scan:
        lane = lax.broadcasted_iota(jnp.int32, (n, lanes), 1)
        xs = list(local)
        ms = [sub_pows[g][n_sub - 1] for g in groups]
        for i in range(int(math.log2(lanes))):
            sh = 1 << i
            for g in groups:
                xr, xi = xs[g]
                rr = jnp.where(lane >= sh, pltpu.roll(xr, sh, 1), 0.0)
                ri = jnp.where(lane >= sh, pltpu.roll(xi, sh, 1), 0.0)
                tr, ti = _cmul(*ms[g], rr, ri)
                xs[g] = (xr + tr, xi + ti)
                ms[g] = _cmul(*ms[g], *ms[g])
        entering = [(jnp.where(lane >= 1, pltpu.roll(xr, 1, 1), 0.0),
                     jnp.where(lane >= 1, pltpu.roll(xi, 1, 1), 0.0)) for xr, xi in xs]
    else:
        assert lanes == LANES
        entering = []
        for g in groups:
            s0 = s0_ref[:, g, :]
            s0t = jnp.concatenate([s0, jnp.zeros((LANES - s0.shape[0], 2 * n), F32)], axis=0).T
            entering.append((s0t[:n], s0t[n:]))

    keep = slice(lanes - LANES, lanes)
    for g in groups:
        qm = qm_ref[g]
        sr, si = entering[g]
        for j in range(n_sub):
            if j == 0:
                pr, pi = sr, si
            else:
                tr, ti = _cmul(*sub_pows[g][j - 1], sr, si)
                pr, pi = lbuf[g, j - 1, :n, :] + tr, lbuf[g, j - 1, n:, :] + ti
            sp = jnp.concatenate([pr, pi], axis=0).astype(BF16)
            y = ybuf[g, j * width:(j + 1) * width, :] + jnp.dot(qm, sp, preferred_element_type=F32)
            yt_ref[g, j * width:(j + 1) * width, :] = y.astype(BF16)
        tr, ti = _cmul(*sub_pows[g][n_sub - 1], sr, si)
        ends = jnp.concatenate([lbuf[g, n_sub - 1, :n, keep] + tr[:, keep],
                                lbuf[g, n_sub - 1, n:, keep] + ti[:, keep]], axis=0).T
        sf_ref[:, g, :] = ends[final_rows[0]:final_rows[1]]
    _run_riders(riders)


def _s5_call(zt, wk, qm, tabr, tabi, s0, *, n_streams, n_sub, scan, cast_riders=()):
    g, rows, lanes = zt.shape
    n = S5_STATE
    width = SUB * S5_GROUP
    assert lanes % LANES == 0 and lanes & (lanes - 1) == 0
    gps = S5_GROUPS_PER_STEP_LONG if scan else S5_GROUPS_PER_STEP_SHORT
    grp = lambda shape: pl.BlockSpec((gps,) + shape, lambda i: (i, 0, 0))
    per_stream = pl.BlockSpec((n_streams, gps, 2 * n), lambda i: (0, i, 0))
    in_specs = [grp((rows, lanes)), grp((width + 2 * n, width)), grp((width, 2 * n)),
                grp((n, LANES)), grp((n, LANES))]
    args = [zt, wk, qm, tabr, tabi]
    if scan:
        assert n_streams == 1
        final_rows = (LANES - 1, LANES)
    else:
        in_specs.append(per_stream)
        args.append(s0)
        final_rows = (0, n_streams)
    rider_specs, rider_shapes = _cast_rider_specs(cast_riders, (g // gps,))
    kern = functools.partial(_s5_kernel, n_sub=n_sub, lanes=lanes, scan=scan, final_rows=final_rows,
                             n_cast=len(cast_riders))
    return pl.pallas_call(
        kern,
        out_shape=(jax.ShapeDtypeStruct((g, rows, lanes), BF16),
                   jax.ShapeDtypeStruct((n_streams, g, 2 * n), F32), *rider_shapes),
        grid=(g // gps,),
        in_specs=in_specs + rider_specs,
        out_specs=(grp((rows, lanes)), per_stream, *rider_specs),
        scratch_shapes=[pltpu.VMEM((gps, rows, lanes), F32),
                        pltpu.VMEM((gps, n_sub, 2 * n, lanes), F32)],
        compiler_params=_cparams(1),
        name="s5",
    )(*args, *cast_riders)


def _glu_kernel(*refs, n_ph, rows, lane_rows, n_cast):
    (yt_ref, w_ref, b_ref, m_ref), riders = _split_riders(refs, n_in=3, n_out=1, n_cast=n_cast)
    _run_riders(riders)
    gs = []
    for ph in range(n_ph):
        yt = yt_ref[:, ph, :, :].astype(F32).reshape(S5_WIDTH, lane_rows)
        gs.append(_gelu(yt.T[:rows]))
    gy = jnp.concatenate(gs, axis=0)
    gate = jnp.dot(gy.astype(BF16), w_ref[...], preferred_element_type=F32) + b_ref[...]
    m = gy * _sigmoid(gate)
    for ph in range(n_ph):
        _oct_store(m_ref, ph, m[ph * rows:(ph + 1) * rows])


def _glu_call(yt4, w_glu_b, b_glu, *, rows, n_ph, row_blk, cast_riders=()):
    g, t, p, lane_rows = yt4.shape
    n_rb, lane_blk, lane_rows_expected = _phase_blocks(rows, row_blk)
    assert lane_rows == lane_rows_expected
    grid = (n_rb, t // n_ph)
    rider_specs, rider_shapes = _cast_rider_specs(cast_riders, grid)
    kern = functools.partial(_glu_kernel, n_ph=n_ph, rows=row_blk, lane_rows=lane_blk,
                             n_cast=len(cast_riders))
    oct_shape = _oct_shape(rows, t, S5_WIDTH)
    return pl.pallas_call(
        kern,
        out_shape=(jax.ShapeDtypeStruct(oct_shape, F32), *rider_shapes),
        grid=grid,
        in_specs=[
            pl.BlockSpec((g, n_ph, p, lane_blk), lambda i, j: (0, j, 0, i)),
            _const_spec((S5_WIDTH, S5_WIDTH)),
            _const_spec((1, S5_WIDTH)),
            *rider_specs,
        ],
        out_specs=(pl.BlockSpec((row_blk // OCT, oct_shape[1], n_ph, OCT, LANES),
                                lambda i, j: (i, 0, j, 0, 0)), *rider_specs),
        compiler_params=_cparams(2),
        name="glu",
    )(yt4, w_glu_b, b_glu, *cast_riders)


def _main_kernel(x_ref, m_ref, zuv_ref, mod_ref, g2_ref, gf_ref, lng_ref, lnb_ref, gw_ref, gbt_ref,
                 wo_ref, wgu_ref, wd_ref, y_ref, *rest, tm, cl, seq, streams):
    *maybe_v_out_ref, v_ref, ygm_ref, attn_ref, act_ref = rest
    t = tm // OCT
    hd = GM_HEAD_DIM

    attn_ref[...] = jnp.dot(_oct_load(m_ref, t).astype(BF16), wo_ref[:S5_WIDTH, :],
                            preferred_element_type=F32)

    gv = _gelu(zuv_ref[:, GM_WIDTH:].astype(F32))
    cen = gv - jnp.mean(gv, axis=-1, keepdims=True)
    var = jnp.mean(cen * cen, axis=-1, keepdims=True)
    v = cen * lax.rsqrt(var + EPS) * lng_ref[...] + lnb_ref[...]
    for v_out_ref in maybe_v_out_ref:
        v_out_ref[...] = v
    v_ref[...] = v.astype(BF16)

    blk_i = _div_pow2(lax.broadcasted_iota(jnp.int32, (cl, cl), 0), CHUNK)
    blk_j = _div_pow2(lax.broadcasted_iota(jnp.int32, (cl, cl), 1), CHUNK)
    causal = blk_j <= blk_i
    first_head = lax.broadcasted_iota(jnp.int32, (cl, 2 * hd), 1) < hd
    for pr in range(GM_HEADS // 2):
        h0, h1 = 2 * pr, 2 * pr + 1
        wm = jnp.concatenate(
            [jnp.where(causal, gw_ref[h, :cl, :cl], 0.0) for h in (h0, h1)], axis=1).astype(BF16)
        bias = jnp.where(first_head, gbt_ref[:cl, h0:h0 + 1], gbt_ref[:cl, h1:h1 + 1])
        cs = slice(h0 * hd, (h1 + 1) * hd)
        for ci in range(tm // cl):
            rs = slice(ci * cl, (ci + 1) * cl)
            vv = v_ref[rs, cs]
            zero = jnp.zeros_like(vv)
            rhs = jnp.concatenate([jnp.where(first_head, vv, zero), jnp.where(first_head, zero, vv)],
                                  axis=0)
            mixed = jnp.dot(wm, rhs, preferred_element_type=F32) + bias
            ygm_ref[rs, cs] = (_gelu(zuv_ref[rs, cs].astype(F32)) * mixed).astype(BF16)

    attn = attn_ref[...] + jnp.dot(ygm_ref[...], wo_ref[S5_WIDTH:, :], preferred_element_type=F32)

    def mod_rows(idx):
        if len(streams) == 1:
            return _mod_vec(mod_ref, streams[0], idx)
        return jnp.concatenate(
            [jnp.broadcast_to(_mod_vec(mod_ref, s, idx), (seq, D_MODEL)) for s in streams], axis=0)

    gate1, shift2, scale2, gate2 = mod_rows(2), mod_rows(3), mod_rows(4), mod_rows(5)
    x1 = x_ref[...] + gate1 * attn
    h2 = (x1 * _rms_scale(x1) * (g2_ref[...] * (1.0 + scale2)) + shift2).astype(BF16)

    assert D_FF % MXU_DIM == 0
    for lo in range(0, D_FF, MXU_DIM):
        hi = lo + MXU_DIM
        gg = jnp.dot(h2, wgu_ref[:, lo:hi], preferred_element_type=F32)
        up = jnp.dot(h2, wgu_ref[:, D_FF + lo:D_FF + hi], preferred_element_type=F32)
        act_ref[:, lo:hi] = (gg * jax.nn.sigmoid(gg) * up).astype(BF16)
    acc = jnp.dot(act_ref[...], wd_ref[...], preferred_element_type=F32)
    x2 = x1 + gate2 * acc
    y_ref[...] = x2 * _rms_scale(x2) * gf_ref[...]


def _main_sets_kernel(*refs, sets, n_shared):
    shared = refs[:n_shared]
    scratch = refs[-4:]
    step = pl.program_id(0)
    pos = n_shared
    out_pos = n_shared + 3 * len(sets)
    for st in sets:
        ins = refs[pos:pos + 3]
        pos += 3
        n_out = 2 if st["want_v"] else 1
        outs = refs[out_pos:out_pos + n_out]
        out_pos += n_out
        tm = st["tm"]

        @pl.when(jnp.logical_and(step >= st["start"], step < st["start"] + st["n_tiles"]))
        def _(ins=ins, outs=outs, st=st, tm=tm):
            _main_kernel(*ins, *shared, *outs, *[r.at[:tm] for r in scratch],
                         tm=tm, cl=st["cl"], seq=st["seq"], streams=st["streams"])


def _main_call(stream_sets, mod_all, norm2_g, final_g, ln_g, ln_b, gm_w, gm_bt, w_out_b, w_gu_b, w_down_b):
    shared = (mod_all, norm2_g, final_g, ln_g, ln_b, gm_w, gm_bt, w_out_b, w_gu_b, w_down_b)
    set_args, set_in_specs, out_shape, out_specs, statics = [], [], [], [], []
    start = 0
    for st in stream_sets:
        n_tok = st["x2d"].shape[0]
        n_oct, n_lb, t, _, _ = st["m"].shape
        tm = OCT * t
        seq, streams = st["seq"], st["streams"]
        assert n_oct * tm == n_tok
        assert len(streams) == 1 or (n_oct == 1 and len(streams) * seq == tm)

        def tile_index(i, start=start, n_oct=n_oct):
            return (jnp.clip(i - start, 0, n_oct - 1), 0)

        tok = lambda width: pl.BlockSpec((tm, width), tile_index)
        oct_rows = n_lb * t * OCT
        set_args += [st["x2d"], st["m"].reshape(n_oct * oct_rows, LANES), st["zuv"]]
        set_in_specs += [tok(D_MODEL), pl.BlockSpec((oct_rows, LANES), tile_index), tok(2 * GM_WIDTH)]
        out_shape.append(jax.ShapeDtypeStruct((n_tok, D_MODEL), F32))
        out_specs.append(tok(D_MODEL))
        if st["want_v"]:
            out_shape.append(jax.ShapeDtypeStruct((n_tok, GM_WIDTH), F32))
            out_specs.append(tok(GM_WIDTH))
        statics.append(dict(start=start, n_tiles=n_oct, tm=tm, cl=min(GM_CHUNK, seq), seq=seq,
                            streams=streams, want_v=st["want_v"]))
        start += n_oct
    tm_max = max(s["tm"] for s in statics)
    outs = pl.pallas_call(
        functools.partial(_main_sets_kernel, sets=statics, n_shared=len(shared)),
        out_shape=tuple(out_shape),
        grid=(start,),
        in_specs=[_const_spec(a.shape) for a in shared] + set_in_specs,
        out_specs=tuple(out_specs),
        scratch_shapes=[pltpu.VMEM((tm_max, GM_WIDTH), BF16), pltpu.VMEM((tm_max, GM_WIDTH), BF16),
                        pltpu.VMEM((tm_max, D_MODEL), F32), pltpu.VMEM((tm_max, D_FF), BF16)],
        compiler_params=_cparams(1),
        name="main",
    )(*shared, *set_args)
    results, pos = [], 0
    for s in statics:
        n_out = 2 if s["want_v"] else 1
        results.append(tuple(outs[pos:pos + n_out]))
        pos += n_out
    return results


def _mixer_front(x, mod_all, streams, s0, prm, *, t, n_ph, row_blk, want_v, f32_weights=None):
    b, seq, _ = x.shape
    n_chunks = seq // t
    rows = b * n_chunks
    scan = s0 is None
    assert (b == 1) if scan else (n_chunks == 1)
    n_sub = t // SUB

    zt4, zuv = _inproj_call(
        x.reshape(rows, t, D_MODEL), mod_all, streams, prm["norm1_g"], prm["w_in"],
        n_ph=n_ph, row_blk=row_blk)
    lane_rows = zt4.shape[-1]

    assert not scan or rows == lane_rows
    pending = dict(f32_weights or {})
    on_glu = {k: pending.pop(k) for k in ("w_gu",) if k in pending}
    yt, fin, *cast = _s5_call(zt4.reshape(S5_GROUPS, t * S5_GROUP, lane_rows),
                              prm["wk"], prm["qm"], prm["tabr"], prm["tabi"], s0,
                              n_streams=b, n_sub=n_sub, scan=scan, cast_riders=tuple(pending.values()))
    prm = {**prm, **dict(zip(pending.keys(), cast))}
    m, *cast = _glu_call(yt.reshape(S5_GROUPS, t, S5_GROUP, lane_rows), prm["w_glu"], prm["b_glu"],
                         rows=rows, n_ph=n_ph, row_blk=row_blk, cast_riders=tuple(on_glu.values()))
    prm = {**prm, **dict(zip(on_glu.keys(), cast))}

    n_tok = b * seq
    stream_set = dict(x2d=x.reshape(n_tok, D_MODEL), m=m, zuv=zuv.reshape(n_tok, 2 * GM_WIDTH),
                      streams=streams, seq=seq, want_v=want_v)

    return stream_set, fin[..., :S5_STATE], fin[..., S5_STATE:], prm


def kernel(x_prompt, x_sample, state_s5_re, state_s5_im, c_prompt, c_sample, norm1_g, norm2_g, w_ada, b_ada, w_in, s5_lambda_re, s5_lambda_im, s5_log_step, s5_b_re, s5_b_im, s5_c_re, s5_c_im, s5_d, s5_w_glu, s5_b_glu, gm_ln_g, gm_ln_b, gm_w_s, gm_b_s, w_out, ffn_w_gu, ffn_w_down, final_g):
    depth = w_in.shape[0]
    assert depth == 1
    l = 0
    n_p = c_prompt.shape[0]
    n_s = c_sample.shape[0]

    c_all = jnp.concatenate([c_prompt, c_sample], axis=0)
    c_pad = jnp.pad(c_all, ((0, -c_all.shape[0] % SUBLANES), (0, 0)))
    mod_all = _ada_call(c_pad, w_ada[l], b_ada[l][None, :])
    streams_p = tuple(range(n_p))
    streams_s = tuple(range(n_p, n_p + n_s))

    wk, qm, tabr, tabi, w_in_b, w_out_b, w_down_b = _s5_prep_call(
        s5_lambda_re[l], s5_lambda_im[l], s5_log_step[l][None, :],
        jnp.concatenate([jnp.swapaxes(s5_b_re[l], 1, 2), jnp.swapaxes(s5_b_im[l], 1, 2)], axis=-1),
        s5_c_re[l], s5_c_im[l], s5_d[l], cast_riders=(w_in[l], w_out[l], ffn_w_down[l]))

    prm = dict(
        norm1_g=norm1_g[l][None, :], norm2_g=norm2_g[l][None, :], final_g=final_g[None, :],
        w_in=w_in_b, w_out=w_out_b, w_down=w_down_b, ln_g=gm_ln_g[l][None, :], ln_b=gm_ln_b[l][None, :],
        wk=wk, qm=qm, tabr=tabr, tabi=tabi, b_glu=s5_b_glu[l][None, :],
        gm_w=gm_w_s[l], gm_bt=jnp.transpose(gm_b_s[l]),
    )
    later_weights = dict(w_glu=s5_w_glu[l], w_gu=ffn_w_gu[l])

    set_p, pre, pim, prm = _mixer_front(
        x_prompt, mod_all, streams_p, None, prm, f32_weights=later_weights,
        t=S5_LONG_SUBS * SUB, n_ph=SUBLANES, row_blk=LANES, want_v=False)
    s0 = jnp.concatenate([state_s5_re[l], state_s5_im[l]], axis=-1)
    n_b, seq_s, _ = x_sample.shape
    set_s, sre, sim, _ = _mixer_front(
        x_sample, mod_all, streams_s, s0, prm, t=seq_s, n_ph=seq_s, row_blk=n_b, want_v=True)

    (yp,), (ys, vs) = _main_call(
        [set_p, set_s], mod_all, prm["norm2_g"], prm["final_g"], prm["ln_g"], prm["ln_b"],
        prm["gm_w"], prm["gm_bt"], prm["w_out"], prm["w_gu"], prm["w_down"])
    return (yp.reshape(x_prompt.shape), ys.reshape(x_sample.shape), pre[None], pim[None],
            sre[None], sim[None], vs.reshape(n_b, seq_s, GM_WIDTH)[None])
```

```python
import functools
import math

import jax
import jax.numpy as jnp
from jax import lax
from jax.experimental import pallas as pl
from jax.experimental.pallas import tpu as pltpu

D_MODEL = 1024
S5_WIDTH = 512
S5_GROUP = 16
S5_GROUPS = 32
S5_STATE = 64
GM_WIDTH = 512
GM_CHUNK = 128
GM_HEADS = 8
GM_HEAD_DIM = 64
CHUNK = 64
IN_WIDTH = S5_WIDTH + 2 * GM_WIDTH
D_FF = 2816
EPS = 1e-6

LANES = 128
SUBLANES = 8
BF16_SUBLANES = 16
MXU_DIM = 256
VMEM_LIMIT_BYTES = 56 * 1024 * 1024

SUB = MXU_DIM // S5_GROUP
S5_LONG_SUBS = 4
PREP_GROUPS_PER_STEP = 8
S5_GROUPS_PER_STEP_LONG = 8
S5_GROUPS_PER_STEP_SHORT = 16
ADA_K_BLOCK = 256

F32 = jnp.float32
BF16 = jnp.bfloat16


def _cparams(n_grid_axes):
    return pltpu.CompilerParams(
        dimension_semantics=("arbitrary",) * n_grid_axes,
        vmem_limit_bytes=VMEM_LIMIT_BYTES,
    )


def _const_spec(shape):
    nd = len(shape)
    return pl.BlockSpec(shape, lambda *_: (0,) * nd, pipeline_mode=pl.Buffered(1))


def _rms_scale(x):
    return lax.rsqrt(jnp.mean(x * x, axis=-1, keepdims=True) + EPS)


_GELU_C0 = math.sqrt(2.0 / math.pi)
_GELU_C1 = 0.044715 * _GELU_C0


def _gelu(x):
    hx = 0.5 * x
    return hx + hx * jnp.tanh(x * (_GELU_C0 + _GELU_C1 * (x * x)))


def _sigmoid(x):
    return 0.5 * jnp.tanh(0.5 * x) + 0.5


def _cmul(ar, ai, xr, xi):
    return ar * xr - ai * xi, ar * xi + ai * xr


def _div_pow2(idx, divisor):
    shift = divisor.bit_length() - 1
    assert divisor == 1 << shift
    return lax.shift_right_logical(idx, shift)


def _mod_pow2(idx, divisor):
    assert divisor & (divisor - 1) == 0
    return lax.bitwise_and(idx, divisor - 1)


def _place_rows(sel, x):
    hi = x.astype(BF16)
    rest = x - hi.astype(F32)
    mid = rest.astype(BF16)
    lo = (rest - mid.astype(F32)).astype(BF16)
    dot = functools.partial(jnp.dot, preferred_element_type=F32)
    return dot(sel, hi) + (dot(sel, mid) + dot(sel, lo))


def _dot_split(x, y):
    x_hi, y_hi = x.astype(BF16), y.astype(BF16)
    x_lo = (x - x_hi.astype(F32)).astype(BF16)
    y_lo = (y - y_hi.astype(F32)).astype(BF16)
    dot = functools.partial(jnp.dot, preferred_element_type=F32)
    return dot(x_hi, y_hi) + (dot(x_lo, y_hi) + dot(x_hi, y_lo))


OCT = SUBLANES


def _oct_shape(n_rows, t, width):
    assert n_rows % OCT == 0 and width % LANES == 0
    return (n_rows // OCT, width // LANES, t, OCT, LANES)


def _oct_store(ref, ph, val):
    n_oct, n_lb = ref.shape[0], ref.shape[1]
    for lb in range(n_lb):
        ref[:, lb, ph, :, :] = val[:, lb * LANES:(lb + 1) * LANES].reshape(n_oct, OCT, LANES)


def _oct_load(ref, t):
    n_lb = ref.shape[0] // (t * OCT)
    chunks = []
    for c in range(OCT):
        chunks.append(jnp.concatenate(
            [ref[pl.ds(lb * t * OCT + c, t, stride=OCT), :] for lb in range(n_lb)], axis=1))
    return jnp.concatenate(chunks, axis=0)


def _cast_rider_specs(arrays, grid):
    n_steps = math.prod(grid)

    def row_block(*idx):
        step = 0
        for i, extent in zip(idx, grid):
            step = step * extent + i
        return (step, 0)

    specs, shapes = [], []
    for a in arrays:
        rows, cols = a.shape
        blk = rows // n_steps
        assert blk * n_steps == rows and blk % BF16_SUBLANES == 0
        specs.append(pl.BlockSpec((blk, cols), row_block))
        shapes.append(jax.ShapeDtypeStruct((rows, cols), BF16))
    return specs, shapes


def _split_riders(refs, n_in, n_out, n_cast):
    ins, refs = refs[:n_in], refs[n_in:]
    cast_in, refs = refs[:n_cast], refs[n_cast:]
    outs, refs = refs[:n_out], refs[n_out:]
    cast_out, scratch = refs[:n_cast], refs[n_cast:]
    return (*ins, *outs, *scratch), list(zip(cast_in, cast_out))


def _run_riders(pairs):
    for src, dst in pairs:
        dst[...] = src[...].astype(BF16)


def _ada_kernel(c_ref, w_ref, b_ref, o_ref):
    @pl.when(pl.program_id(0) == 0)
    def _():
        o_ref[...] = jnp.broadcast_to(b_ref[...], o_ref.shape)

    c = c_ref[...]
    o_ref[...] += _dot_split(c * jax.nn.sigmoid(c), w_ref[...])


def _ada_call(c_pad, w_ada, b_ada):
    rows = c_pad.shape[0]
    n_in, n_out = w_ada.shape
    bk = ADA_K_BLOCK
    return pl.pallas_call(
        _ada_kernel,
        out_shape=jax.ShapeDtypeStruct((rows, n_out), F32),
        grid=(n_in // bk,),
        in_specs=[
            pl.BlockSpec((rows, bk), lambda k: (0, k)),
            pl.BlockSpec((bk, n_out), lambda k: (k, 0)),
            pl.BlockSpec((1, n_out), lambda k: (0, 0)),
        ],
        out_specs=pl.BlockSpec((rows, n_out), lambda k: (0, 0)),
        compiler_params=_cparams(1),
        name="ada",
    )(c_pad, w_ada, b_ada)


def _discretise(lr, li, ls):
    step = jnp.exp(ls)
    mag = jnp.exp(lr * step)
    ar = mag * jnp.cos(li * step)
    ai = mag * jnp.sin(li * step)
    den = lr * lr + li * li
    fr = ((ar - 1.0) * lr + ai * li) / den
    fi = (ai * lr - (ar - 1.0) * li) / den
    return ar, ai, fr, fi


def _selection(rows, cols, row_of_col):
    r = lax.broadcasted_iota(jnp.int32, (rows, cols), 0)
    c = lax.broadcasted_iota(jnp.int32, (rows, cols), 1)
    return jnp.where(r == row_of_col(c), 1.0, 0.0).astype(BF16)


def _place(x, sel):
    hi = x.astype(BF16)
    rest = x - hi.astype(F32)
    mid = rest.astype(BF16)
    lo = (rest - mid.astype(F32)).astype(BF16)
    dot = functools.partial(jnp.dot, preferred_element_type=F32)
    return dot(hi, sel) + (dot(mid, sel) + dot(lo, sel))


def _to_column(row):
    k = row.shape[1]
    r = lax.broadcasted_iota(jnp.int32, (k, k), 0)
    c = lax.broadcasted_iota(jnp.int32, (k, k), 1)
    return jnp.sum(jnp.where(r == c, jnp.broadcast_to(row, (k, k)), 0.0), axis=1, keepdims=True)


def _s5_prep_kernel(*refs, n_cast):
    refs, riders = _split_riders(refs, n_in=7, n_out=4, n_cast=n_cast)
    lam_re_ref, lam_im_ref, ls_ref, d_ref, *grouped = refs
    n, sub, p = S5_STATE, SUB, S5_GROUP
    width = sub * p

    def col_source(c):
        return jnp.where(c < width, (sub - 1) - _div_pow2(c, p), jnp.where(c < 2 * width, sub + 1, sub))

    sels = dict(
        twice=_selection(n, 2 * n, lambda c: _mod_pow2(c, n)),
        tile_rows=_selection(p, width, lambda c: _mod_pow2(c, p)).T,
        cols=_selection(2 * n, 2 * width + LANES, col_source),
    )
    gps = grouped[0].shape[0]
    pending = [_s5_prep_group(pl.program_id(0) * gps + gi, sels, lam_re_ref, lam_im_ref, ls_ref, d_ref,
                              *[r.at[gi] for r in grouped]) for gi in range(gps)]
    while pending:
        pending = [gen for gen in pending if next(gen, "done") != "done"]
    _run_riders(riders)


def _s5_prep_group(g, sels, lam_re_ref, lam_im_ref, ls_ref, d_ref, bt_ref, c_re_ref, c_im_ref,
                   wk_ref, qm_ref, tabr_ref, tabi_ref):
    n, sub, p = S5_STATE, SUB, S5_GROUP
    width = sub * p

    lr_row = lam_re_ref[pl.ds(g, 1), :]
    li_row = lam_im_ref[pl.ds(g, 1), :]
    ls_all = ls_ref[...]
    grp_lane = lax.broadcasted_iota(jnp.int32, ls_all.shape, 1)
    ls = jnp.sum(jnp.where(grp_lane == g, ls_all, 0.0), axis=1, keepdims=True)

    ar8, ai8, fr8, fi8 = _discretise(jnp.broadcast_to(lr_row, (SUBLANES, n)),
                                     jnp.broadcast_to(li_row, (SUBLANES, n)), ls)
    twice = lambda t8: _place(jnp.concatenate([t8] * (p // SUBLANES), axis=0), sels["twice"])
    a2r, a2i = twice(ar8), twice(ai8)
    first = lax.broadcasted_iota(jnp.int32, (p, 2 * n), 1) < n
    c2r = _place(c_re_ref[...], sels["twice"])
    c2i = _place(c_im_ref[...], sels["twice"])
    pr = jnp.ones_like(a2r)
    pi = jnp.zeros_like(a2r)
    ccat = []
    tbl_rows = SUBLANES * (-(-(sub + 2) // SUBLANES))
    tbl_row = lax.broadcasted_iota(jnp.int32, (tbl_rows, 2 * n), 0)
    as_tbl_row = lambda re2, im2: jnp.concatenate([jnp.where(first, re2, im2)] * (tbl_rows // p + 1),
                                                  axis=0)[:tbl_rows]
    tbl = jnp.zeros((tbl_rows, 2 * n), F32)
    for d in range(sub + 1):
        ccat.append(c2r * jnp.where(first, pr, -pi) + c2i * jnp.where(first, -pi, -pr))
        tbl = jnp.where(tbl_row == d, as_tbl_row(pr, pi), tbl)
        pr, pi = _cmul(a2r, a2i, pr, pi)
    tbl = jnp.where(tbl_row == sub + 1, as_tbl_row(twice(fr8), twice(fi8)), tbl)
    yield
    qm_ref[...] = jnp.concatenate(ccat[1:], axis=0).astype(BF16)
    rcat = jnp.concatenate(ccat[:sub], axis=0)

    tbl_t = jnp.concatenate([tbl, jnp.zeros((2 * n - tbl_rows, 2 * n), F32)], axis=0).T
    cols = _place(tbl_t, sels["cols"])
    yield
    apr, fr, a16r = cols[:n, :width], cols[:n, width:2 * width], cols[:n, 2 * width:]
    api, fi, a16i = cols[n:, :width], cols[n:, width:2 * width], cols[n:, 2 * width:]
    b_tiled = _place_rows(sels["tile_rows"], bt_ref[...]).T
    btr, bti = b_tiled[:n], b_tiled[n:]
    bbr, bbi = _cmul(fr, fi, btr, bti)
    pmr, pmi = _cmul(apr, api, bbr, bbi)
    wk_ref[width:width + n, :] = pmr.astype(BF16)
    wk_ref[width + n:, :] = pmi.astype(BF16)
    yield

    bbcat = jnp.concatenate([bbr, bbi], axis=0)
    kt = _dot_split(rcat, bbcat)
    yield
    d_col = _to_column(d_ref[pl.ds(g, 1), :])
    row_p = lax.broadcasted_iota(jnp.int32, (p, width), 0)
    lane_p = lax.broadcasted_iota(jnp.int32, (p, width), 1)
    d_diag = jnp.where(row_p == _mod_pow2(lane_p, p), d_col, 0.0)
    kt = jnp.concatenate([kt[:p] + d_diag, kt[p:]], axis=0)
    col_blk = _div_pow2(lax.broadcasted_iota(jnp.int32, (width, width), 1), p)
    m16 = jnp.zeros((width, width), F32)
    for k in range(sub):
        if k == 0:
            shifted = kt
        else:
            shifted = jnp.concatenate(
                [jnp.zeros((k * p, width), F32), kt[:width - k * p]], axis=0)
        m16 = jnp.where(col_blk == k, shifted, m16)
    wk_ref[:width, :] = m16.astype(BF16)

    tabr_ref[...] = a16r
    tabi_ref[...] = a16i


def _s5_prep_call(lam_re, lam_im, log_step, bt, c_re, c_im, d, cast_riders=()):
    g, n, p, sub = S5_GROUPS, S5_STATE, S5_GROUP, SUB
    width = sub * p
    grp = lambda shape: pl.BlockSpec((PREP_GROUPS_PER_STEP,) + shape, lambda i: (i, 0, 0))
    whole = lambda a: pl.BlockSpec(a.shape, lambda i: (0,) * a.ndim)
    n_steps = g // PREP_GROUPS_PER_STEP
    rider_specs, rider_shapes = _cast_rider_specs(cast_riders, (n_steps,))
    return pl.pallas_call(
        functools.partial(_s5_prep_kernel, n_cast=len(cast_riders)),
        out_shape=(
            jax.ShapeDtypeStruct((g, width + 2 * n, width), BF16),
            jax.ShapeDtypeStruct((g, width, 2 * n), BF16),
            jax.ShapeDtypeStruct((g, n, LANES), F32),
            jax.ShapeDtypeStruct((g, n, LANES), F32),
            *rider_shapes,
        ),
        grid=(n_steps,),
        in_specs=[whole(lam_re), whole(lam_im), whole(log_step), whole(d)]
        + [grp((p, 2 * n))] + [grp((p, n))] * 2 + rider_specs,
        out_specs=(grp((width + 2 * n, width)), grp((width, 2 * n)),
                   grp((n, LANES)), grp((n, LANES)), *rider_specs),
        compiler_params=_cparams(1),
        name="s5_prep",
    )(lam_re, lam_im, log_step, d, bt, c_re, c_im, *cast_riders)


def _mod_vec(mod_ref, stream, idx):
    return mod_ref[stream:stream + 1, idx * D_MODEL:(idx + 1) * D_MODEL]


def _inproj_kernel(x_ref, mod_ref, g1_ref, w_ref, zt_ref, zuv_ref, hs_ref,
                   *, n_ph, rows, lane_rows, streams):
    m = rows * n_ph
    assert len(streams) in (1, rows)
    shift, scale = (jnp.concatenate([_mod_vec(mod_ref, s, idx) for s in streams], axis=0)
                    for idx in (0, 1))
    gain = (g1_ref[...] * (1.0 + scale))[:, None, :]
    shift = shift[:, None, :]
    x3 = x_ref[...]
    h = (x3 * _rms_scale(x3) * gain + shift).reshape(m, D_MODEL)
    zuv = jnp.dot(h.astype(BF16), w_ref[:, S5_WIDTH:], preferred_element_type=F32)
    zuv_ref[...] = zuv.reshape(rows, n_ph, 2 * GM_WIDTH).astype(BF16)

    n_lb = D_MODEL // LANES
    for lb in range(n_lb):
        hs_ref[lb] = h[:, lb * LANES:(lb + 1) * LANES]
    hp = jnp.concatenate(
        [jnp.concatenate([hs_ref[lb, pl.ds(ph, rows, stride=n_ph), :] for lb in range(n_lb)], axis=1)
         for ph in range(n_ph)], axis=0).astype(BF16)
    z5 = jnp.dot(hp, w_ref[:, :S5_WIDTH], preferred_element_type=F32)
    for ph in range(n_ph):
        zz = z5[ph * rows:(ph + 1) * rows]
        if lane_rows > rows:
            zz = jnp.concatenate([zz, jnp.zeros((lane_rows - rows, S5_WIDTH), F32)], axis=0)
        zt = zz.T.reshape(S5_GROUPS, S5_GROUP, lane_rows)
        zt_ref[:, ph, :, :] = zt.astype(BF16)


def _phase_blocks(n_rows, row_blk):
    if row_blk % LANES == 0:
        assert n_rows % row_blk == 0
        return n_rows // row_blk, row_blk, n_rows
    assert row_blk == n_rows
    lane_rows = -(-n_rows // LANES) * LANES
    return 1, lane_rows, lane_rows


def _inproj_call(x3, mod_all, streams, norm1_g, w_in_b, *, n_ph, row_blk):
    rows, t, _ = x3.shape
    n_rb, lane_blk, lane_rows = _phase_blocks(rows, row_blk)
    kern = functools.partial(_inproj_kernel, n_ph=n_ph, rows=row_blk, lane_rows=lane_blk,
                             streams=streams)
    return pl.pallas_call(
        kern,
        out_shape=(
            jax.ShapeDtypeStruct((S5_GROUPS, t, S5_GROUP, lane_rows), BF16),
            jax.ShapeDtypeStruct((rows, t, 2 * GM_WIDTH), BF16),
        ),
        grid=(n_rb, t // n_ph),
        in_specs=[
            pl.BlockSpec((row_blk, n_ph, D_MODEL), lambda i, j: (i, j, 0)),
            _const_spec(mod_all.shape),
            _const_spec((1, D_MODEL)),
            _const_spec((D_MODEL, IN_WIDTH)),
        ],
        out_specs=(
            pl.BlockSpec((S5_GROUPS, n_ph, S5_GROUP, lane_blk), lambda i, j: (0, j, 0, i)),
            pl.BlockSpec((row_blk, n_ph, 2 * GM_WIDTH), lambda i, j: (i, j, 0)),
        ),
        scratch_shapes=[pltpu.VMEM((D_MODEL // LANES, row_blk * n_ph, LANES), F32)],
        compiler_params=_cparams(2),
        name="inproj",
    )(x3, mod_all, norm1_g, w_in_b)


def _s5_kernel(*refs, n_sub, lanes, scan, final_rows, n_cast):
    refs, riders = _split_riders(refs, n_in=5 if scan else 6, n_out=2, n_cast=n_cast)
    if scan:
        zt_ref, wk_ref, qm_ref, tabr_ref, tabi_ref, yt_ref, sf_ref, ybuf, lbuf = refs
        s0_ref = None
    else:
        zt_ref, wk_ref, qm_ref, tabr_ref, tabi_ref, s0_ref, yt_ref, sf_ref, ybuf, lbuf = refs
    groups = range(zt_ref.shape[0])
    n = S5_STATE
    width = SUB * S5_GROUP

    widen = lambda tile: jnp.concatenate([tile] * (lanes // LANES), axis=1)
    sub_pows = []
    for g in groups:
        a1 = (widen(tabr_ref[g]), widen(tabi_ref[g]))
        pows = [a1]
        for _ in range(n_sub - 1):
            pows.append(_cmul(*a1, *pows[-1]))
        sub_pows.append(pows)

    local = []
    for g in groups:
        wk = wk_ref[g]
        lr = li = None
        for j in range(n_sub):
            u = zt_ref[g, j * width:(j + 1) * width, :]
            r = jnp.dot(wk, u, preferred_element_type=F32)
            ybuf[g, j * width:(j + 1) * width, :] = r[:width]
            wr = r[width:width + n]
            wi = r[width + n:]
            if j == 0:
                lr, li = wr, wi
            else:
                tr, ti = _cmul(*sub_pows[g][0], lr, li)
                lr, li = tr + wr, ti + wi
            lbuf[g, j, :n, :] = lr
            lbuf[g, j, n:, :] = li
        local.append((lr, li))

    if scan:
        lane = lax.broadcasted_iota(jnp.int32, (n, lanes), 1)
        xs = list(local)
        ms = [sub_pows[g][n_sub - 1] for g in groups]
        for i in range(int(math.log2(lanes))):
            sh = 1 << i
            for g in groups:
                xr, xi = xs[g]
                rr = jnp.where(lane >= sh, pltpu.roll(xr, sh, 1), 0.0)
                ri = jnp.where(lane >= sh, pltpu.roll(xi, sh, 1), 0.0)
                tr, ti = _cmul(*ms[g], rr, ri)
                xs[g] = (xr + tr, xi + ti)
                ms[g] = _cmul(*ms[g], *ms[g])
        entering = [(jnp.where(lane >= 1, pltpu.roll(xr, 1, 1), 0.0),
                     jnp.where(lane >= 1, pltpu.roll(xi, 1, 1), 0.0)) for xr, xi in xs]
    else:
        assert lanes == LANES
        entering = []
        for g in groups:
            s0 = s0_ref[:, g, :]
            s0t = jnp.concatenate([s0, jnp.zeros((LANES - s0.shape[0], 2 * n), F32)], axis=0).T
            entering.append((s0t[:n], s0t[n:]))

    keep = slice(lanes - LANES, lanes)
    for g in groups:
        qm = qm_ref[g]
        sr, si = entering[g]
        for j in range(n_sub):
            if j == 0:
                pr, pi = sr, si
            else:
                tr, ti = _cmul(*sub_pows[g][j - 1], sr, si)
                pr, pi = lbuf[g, j - 1, :n, :] + tr, lbuf[g, j - 1, n:, :] + ti
            sp = jnp.concatenate([pr, pi], axis=0).astype(BF16)
            y = ybuf[g, j * width:(j + 1) * width, :] + jnp.dot(qm, sp, preferred_element_type=F32)
            yt_ref[g, j * width:(j + 1) * width, :] = y.astype(BF16)
        tr, ti = _cmul(*sub_pows[g][n_sub - 1], sr, si)
        ends = jnp.concatenate([lbuf[g, n_sub - 1, :n, keep] + tr[:, keep],
                                lbuf[g, n_sub - 1, n:, keep] + ti[:, keep]], axis=0).T
        sf_ref[:, g, :] = ends[final_rows[0]:final_rows[1]]
    _run_riders(riders)


def _s5_call(zt, wk, qm, tabr, tabi, s0, *, n_streams, n_sub, scan, cast_riders=()):
    g, rows, lanes = zt.shape
    n = S5_STATE
    width = SUB * S5_GROUP
    assert lanes % LANES == 0 and lanes & (lanes - 1) == 0
    gps = S5_GROUPS_PER_STEP_LONG if scan else S5_GROUPS_PER_STEP_SHORT
    grp = lambda shape: pl.BlockSpec((gps,) + shape, lambda i: (i, 0, 0))
    per_stream = pl.BlockSpec((n_streams, gps, 2 * n), lambda i: (0, i, 0))
    in_specs = [grp((rows, lanes)), grp((width + 2 * n, width)), grp((width, 2 * n)),
                grp((n, LANES)), grp((n, LANES))]
    args = [zt, wk, qm, tabr, tabi]
    if scan:
        assert n_streams == 1
        final_rows = (LANES - 1, LANES)
    else:
        in_specs.append(per_stream)
        args.append(s0)
        final_rows = (0, n_streams)
    rider_specs, rider_shapes = _cast_rider_specs(cast_riders, (g // gps,))
    kern = functools.partial(_s5_kernel, n_sub=n_sub, lanes=lanes, scan=scan, final_rows=final_rows,
                             n_cast=len(cast_riders))
    return pl.pallas_call(
        kern,
        out_shape=(jax.ShapeDtypeStruct((g, rows, lanes), BF16),
                   jax.ShapeDtypeStruct((n_streams, g, 2 * n), F32), *rider_shapes),
        grid=(g // gps,),
        in_specs=in_specs + rider_specs,
        out_specs=(grp((rows, lanes)), per_stream, *rider_specs),
        scratch_shapes=[pltpu.VMEM((gps, rows, lanes), F32),
                        pltpu.VMEM((gps, n_sub, 2 * n, lanes), F32)],
        compiler_params=_cparams(1),
        name="s5",
    )(*args, *cast_riders)


def _glu_kernel(*refs, n_ph, rows, lane_rows, n_cast):
    (yt_ref, w_ref, b_ref, m_ref), riders = _split_riders(refs, n_in=3, n_out=1, n_cast=n_cast)
    _run_riders(riders)
    gs = []
    for ph in range(n_ph):
        yt = yt_ref[:, ph, :, :].astype(F32).reshape(S5_WIDTH, lane_rows)
        gs.append(_gelu(yt.T[:rows]))
    gy = jnp.concatenate(gs, axis=0)
    gate = jnp.dot(gy.astype(BF16), w_ref[...], preferred_element_type=F32) + b_ref[...]
    m = gy * _sigmoid(gate)
    for ph in range(n_ph):
        _oct_store(m_ref, ph, m[ph * rows:(ph + 1) * rows])


def _glu_call(yt4, w_glu_b, b_glu, *, rows, n_ph, row_blk, cast_riders=()):
    g, t, p, lane_rows = yt4.shape
    n_rb, lane_blk, lane_rows_expected = _phase_blocks(rows, row_blk)
    assert lane_rows == lane_rows_expected
    grid = (n_rb, t // n_ph)
    rider_specs, rider_shapes = _cast_rider_specs(cast_riders, grid)
    kern = functools.partial(_glu_kernel, n_ph=n_ph, rows=row_blk, lane_rows=lane_blk,
                             n_cast=len(cast_riders))
    oct_shape = _oct_shape(rows, t, S5_WIDTH)
    return pl.pallas_call(
        kern,
        out_shape=(jax.ShapeDtypeStruct(oct_shape, F32), *rider_shapes),
        grid=grid,
        in_specs=[
            pl.BlockSpec((g, n_ph, p, lane_blk), lambda i, j: (0, j, 0, i)),
            _const_spec((S5_WIDTH, S5_WIDTH)),
            _const_spec((1, S5_WIDTH)),
            *rider_specs,
        ],
        out_specs=(pl.BlockSpec((row_blk // OCT, oct_shape[1], n_ph, OCT, LANES),
                                lambda i, j: (i, 0, j, 0, 0)), *rider_specs),
        compiler_params=_cparams(2),
        name="glu",
    )(yt4, w_glu_b, b_glu, *cast_riders)


def _mod_rows(mod_ref, streams, seq, idx):
    if len(streams) == 1:
        return _mod_vec(mod_ref, streams[0], idx)
    return jnp.concatenate(
        [jnp.broadcast_to(_mod_vec(mod_ref, s, idx), (seq, D_MODEL)) for s in streams], axis=0)


def _interleave(*gens):
    gens = list(gens)
    while gens:
        for g in list(gens):
            try:
                next(g)
            except StopIteration:
                gens.remove(g)


def _main_front(x_ref, m_ref, zuv_ref, v_out_refs, mod_ref, g2_ref, lng_ref, lnb_ref, gw_ref, gbt_ref,
                wo_ref, x1_ref, h2_ref, v_ref, ygm_ref, *, tm, cl, seq, streams):
    t = tm // OCT
    hd = GM_HEAD_DIM

    gv = _gelu(zuv_ref[:, GM_WIDTH:].astype(F32))
    cen = gv - jnp.mean(gv, axis=-1, keepdims=True)
    var = jnp.mean(cen * cen, axis=-1, keepdims=True)
    v = cen * lax.rsqrt(var + EPS) * lng_ref[...] + lnb_ref[...]
    for v_out_ref in v_out_refs:
        v_out_ref[...] = v
    v_ref[...] = v.astype(BF16)
    yield

    blk_i = _div_pow2(lax.broadcasted_iota(jnp.int32, (cl, cl), 0), CHUNK)
    blk_j = _div_pow2(lax.broadcasted_iota(jnp.int32, (cl, cl), 1), CHUNK)
    causal = blk_j <= blk_i
    first_head = lax.broadcasted_iota(jnp.int32, (cl, 2 * hd), 1) < hd
    for pr in range(GM_HEADS // 2):
        h0, h1 = 2 * pr, 2 * pr + 1
        wm = jnp.concatenate(
            [jnp.where(causal, gw_ref[h, :cl, :cl], 0.0) for h in (h0, h1)], axis=1).astype(BF16)
        bias = jnp.where(first_head, gbt_ref[:cl, h0:h0 + 1], gbt_ref[:cl, h1:h1 + 1])
        cs = slice(h0 * hd, (h1 + 1) * hd)
        for ci in range(tm // cl):
            rs = slice(ci * cl, (ci + 1) * cl)
            vv = v_ref[rs, cs]
            zero = jnp.zeros_like(vv)
            rhs = jnp.concatenate([jnp.where(first_head, vv, zero), jnp.where(first_head, zero, vv)],
                                  axis=0)
            mixed = jnp.dot(wm, rhs, preferred_element_type=F32) + bias
            ygm_ref[rs, cs] = (_gelu(zuv_ref[rs, cs].astype(F32)) * mixed).astype(BF16)
        yield

    x1_ref[...] = jnp.dot(_oct_load(m_ref, t).astype(BF16), wo_ref[:S5_WIDTH, :],
                          preferred_element_type=F32)
    yield
    attn = x1_ref[...] + jnp.dot(ygm_ref[...], wo_ref[S5_WIDTH:, :], preferred_element_type=F32)
    gate1, shift2, scale2 = (_mod_rows(mod_ref, streams, seq, idx) for idx in (2, 3, 4))
    x1 = x_ref[...] + gate1 * attn
    x1_ref[...] = x1
    h2_ref[...] = (x1 * _rms_scale(x1) * (g2_ref[...] * (1.0 + scale2)) + shift2).astype(BF16)


def _main_back(y_ref, mod_ref, gf_ref, wgu_ref, wd_ref, x1_ref, h2_ref, act_ref, *, seq, streams):
    assert D_FF % MXU_DIM == 0 and D_MODEL % MXU_DIM == 0
    for lo in range(0, D_FF, MXU_DIM):
        hi = lo + MXU_DIM
        gg = jnp.dot(h2_ref[...], wgu_ref[:, lo:hi], preferred_element_type=F32)
        up = jnp.dot(h2_ref[...], wgu_ref[:, D_FF + lo:D_FF + hi], preferred_element_type=F32)
        act_ref[:, lo:hi] = (gg * jax.nn.sigmoid(gg) * up).astype(BF16)
        yield
    gate2 = _mod_rows(mod_ref, streams, seq, 5)
    sumsq = None
    for lo in range(0, D_MODEL, MXU_DIM):
        cs = slice(lo, lo + MXU_DIM)
        acc = jnp.dot(act_ref[...], wd_ref[:, cs], preferred_element_type=F32)
        x2 = x1_ref[:, cs] + gate2[:, cs] * acc
        part = jnp.sum(x2 * x2, axis=-1, keepdims=True)
        sumsq = part if sumsq is None else sumsq + part
        y_ref[:, cs] = x2
        yield
    y_ref[...] = y_ref[...] * lax.rsqrt(sumsq * (1.0 / D_MODEL) + EPS) * gf_ref[...]


def _main_sets_kernel(*refs, sets, plan, n_shared):
    mod_ref, g2_ref, gf_ref, lng_ref, lnb_ref, gw_ref, gbt_ref, wo_ref, wgu_ref, wd_ref = refs[:n_shared]
    x1_ref, h2_ref, v_ref, ygm_ref, act_ref = refs[-5:]
    slot_rows = x1_ref.shape[0] // 2
    step = pl.program_id(0)
    set_refs, pos, out_pos = [], n_shared, n_shared + 3 * len(sets)
    for st in sets:
        n_out = 2 if st["want_v"] else 1
        set_refs.append((refs[pos:pos + 3], refs[out_pos:out_pos + n_out]))
        pos += 3
        out_pos += n_out

    def slot(parity, tm):
        first = pl.multiple_of(parity * slot_rows, slot_rows)
        return x1_ref.at[pl.ds(first, tm)], h2_ref.at[pl.ds(first, tm)]

    for lo, hi, front, back in plan:

        @pl.when(jnp.logical_and(step >= lo, step < hi))
        def _(front=front, back=back):
            parity = lax.rem(step, 2)
            gens = []
            if back is not None:
                st, (_, outs) = sets[back], set_refs[back]
                gens.append(_main_back(outs[0], mod_ref, gf_ref, wgu_ref, wd_ref,
                                       *slot(1 - parity, st["tm"]), act_ref.at[:st["tm"]],
                                       seq=st["seq"], streams=st["streams"]))
            if front is not None:
                st, (ins, outs) = sets[front], set_refs[front]
                tm = st["tm"]
                gens.append(_main_front(*ins, outs[1:], mod_ref, g2_ref, lng_ref, lnb_ref, gw_ref, gbt_ref,
                                        wo_ref, *slot(parity, tm), v_ref.at[:tm], ygm_ref.at[:tm],
                                        tm=tm, cl=st["cl"], seq=st["seq"], streams=st["streams"]))
            _interleave(*gens)


def _main_call(stream_sets, mod_all, norm2_g, final_g, ln_g, ln_b, gm_w, gm_bt, w_out_b, w_gu_b, w_down_b):
    shared = (mod_all, norm2_g, final_g, ln_g, ln_b, gm_w, gm_bt, w_out_b, w_gu_b, w_down_b)
    set_args, set_in_specs, out_shape, out_specs, statics = [], [], [], [], []
    start = 0
    for st in stream_sets:
        n_tok = st["x2d"].shape[0]
        n_oct, n_lb, t, _, _ = st["m"].shape
        tm = OCT * t
        seq, streams = st["seq"], st["streams"]
        assert n_oct * tm == n_tok
        assert len(streams) == 1 or (n_oct == 1 and len(streams) * seq == tm)

        def front_index(i, start=start, n_oct=n_oct):
            return (jnp.clip(i - start, 0, n_oct - 1), 0)

        def back_index(i, start=start, n_oct=n_oct):
            return (jnp.clip(i - 1 - start, 0, n_oct - 1), 0)

        mode = dict(pipeline_mode=pl.Buffered(1)) if n_oct == 1 else {}
        oct_rows = n_lb * t * OCT
        set_args += [st["x2d"], st["m"].reshape(n_oct * oct_rows, LANES), st["zuv"]]
        set_in_specs += [pl.BlockSpec((tm, D_MODEL), front_index, **mode),
                         pl.BlockSpec((oct_rows, LANES), front_index, **mode),
                         pl.BlockSpec((tm, 2 * GM_WIDTH), front_index, **mode)]
        out_shape.append(jax.ShapeDtypeStruct((n_tok, D_MODEL), F32))
        out_specs.append(pl.BlockSpec((tm, D_MODEL), back_index))
        if st["want_v"]:
            out_shape.append(jax.ShapeDtypeStruct((n_tok, GM_WIDTH), F32))
            out_specs.append(pl.BlockSpec((tm, GM_WIDTH), front_index))
        statics.append(dict(tm=tm, cl=min(GM_CHUNK, seq), seq=seq, streams=streams, want_v=st["want_v"]))
        start += n_oct
    tile_set = [k for k, st in enumerate(stream_sets) for _ in range(st["m"].shape[0])]
    plan = []
    for s in range(start + 1):
        pair = (tile_set[s] if s < start else None, tile_set[s - 1] if s >= 1 else None)
        if plan and plan[-1][2:] == pair:
            plan[-1] = (plan[-1][0], s + 1, *pair)
        else:
            plan.append((s, s + 1, *pair))
    tm_max = max(s["tm"] for s in statics)
    outs = pl.pallas_call(
        functools.partial(_main_sets_kernel, sets=statics, plan=plan, n_shared=len(shared)),
        out_shape=tuple(out_shape),
        grid=(start + 1,),
        in_specs=[_const_spec(a.shape) for a in shared] + set_in_specs,
        out_specs=tuple(out_specs),
        scratch_shapes=[pltpu.VMEM((2 * tm_max, D_MODEL), F32), pltpu.VMEM((2 * tm_max, D_MODEL), BF16),
                        pltpu.VMEM((tm_max, GM_WIDTH), BF16), pltpu.VMEM((tm_max, GM_WIDTH), BF16),
                        pltpu.VMEM((tm_max, D_FF), BF16)],
        compiler_params=_cparams(1),
        name="main",
    )(*shared, *set_args)
    results, pos = [], 0
    for s in statics:
        n_out = 2 if s["want_v"] else 1
        results.append(tuple(outs[pos:pos + n_out]))
        pos += n_out
    return results


def _mixer_front(x, mod_all, streams, s0, prm, *, t, n_ph, row_blk, want_v, f32_weights=None):
    b, seq, _ = x.shape
    n_chunks = seq // t
    rows = b * n_chunks
    scan = s0 is None
    assert (b == 1) if scan else (n_chunks == 1)
    n_sub = t // SUB

    zt4, zuv = _inproj_call(
        x.reshape(rows, t, D_MODEL), mod_all, streams, prm["norm1_g"], prm["w_in"],
        n_ph=n_ph, row_blk=row_blk)
    lane_rows = zt4.shape[-1]

    assert not scan or rows == lane_rows
    pending = dict(f32_weights or {})
    on_glu = {k: pending.pop(k) for k in ("w_gu",) if k in pending}
    yt, fin, *cast = _s5_call(zt4.reshape(S5_GROUPS, t * S5_GROUP, lane_rows),
                              prm["wk"], prm["qm"], prm["tabr"], prm["tabi"], s0,
                              n_streams=b, n_sub=n_sub, scan=scan, cast_riders=tuple(pending.values()))
    prm = {**prm, **dict(zip(pending.keys(), cast))}
    m, *cast = _glu_call(yt.reshape(S5_GROUPS, t, S5_GROUP, lane_rows), prm["w_glu"], prm["b_glu"],
                         rows=rows, n_ph=n_ph, row_blk=row_blk, cast_riders=tuple(on_glu.values()))
    prm = {**prm, **dict(zip(on_glu.keys(), cast))}

    n_tok = b * seq
    stream_set = dict(x2d=x.reshape(n_tok, D_MODEL), m=m, zuv=zuv.reshape(n_tok, 2 * GM_WIDTH),
                      streams=streams, seq=seq, want_v=want_v)

    return stream_set, fin[..., :S5_STATE], fin[..., S5_STATE:], prm


def kernel(x_prompt, x_sample, state_s5_re, state_s5_im, c_prompt, c_sample, norm1_g, norm2_g, w_ada, b_ada, w_in, s5_lambda_re, s5_lambda_im, s5_log_step, s5_b_re, s5_b_im, s5_c_re, s5_c_im, s5_d, s5_w_glu, s5_b_glu, gm_ln_g, gm_ln_b, gm_w_s, gm_b_s, w_out, ffn_w_gu, ffn_w_down, final_g):
    depth = w_in.shape[0]
    assert depth == 1
    l = 0
    n_p = c_prompt.shape[0]
    n_s = c_sample.shape[0]

    c_all = jnp.concatenate([c_prompt, c_sample], axis=0)
    c_pad = jnp.pad(c_all, ((0, -c_all.shape[0] % SUBLANES), (0, 0)))
    mod_all = _ada_call(c_pad, w_ada[l], b_ada[l][None, :])
    streams_p = tuple(range(n_p))
    streams_s = tuple(range(n_p, n_p + n_s))

    wk, qm, tabr, tabi, w_in_b, w_out_b, w_down_b = _s5_prep_call(
        s5_lambda_re[l], s5_lambda_im[l], s5_log_step[l][None, :],
        jnp.concatenate([jnp.swapaxes(s5_b_re[l], 1, 2), jnp.swapaxes(s5_b_im[l], 1, 2)], axis=-1),
        s5_c_re[l], s5_c_im[l], s5_d[l], cast_riders=(w_in[l], w_out[l], ffn_w_down[l]))

    prm = dict(
        norm1_g=norm1_g[l][None, :], norm2_g=norm2_g[l][None, :], final_g=final_g[None, :],
        w_in=w_in_b, w_out=w_out_b, w_down=w_down_b, ln_g=gm_ln_g[l][None, :], ln_b=gm_ln_b[l][None, :],
        wk=wk, qm=qm, tabr=tabr, tabi=tabi, b_glu=s5_b_glu[l][None, :],
        gm_w=gm_w_s[l], gm_bt=jnp.transpose(gm_b_s[l]),
    )
    later_weights = dict(w_glu=s5_w_glu[l], w_gu=ffn_w_gu[l])

    set_p, pre, pim, prm = _mixer_front(
        x_prompt, mod_all, streams_p, None, prm, f32_weights=later_weights,
        t=S5_LONG_SUBS * SUB, n_ph=SUBLANES, row_blk=LANES, want_v=False)
    s0 = jnp.concatenate([state_s5_re[l], state_s5_im[l]], axis=-1)
    n_b, seq_s, _ = x_sample.shape
    set_s, sre, sim, _ = _mixer_front(
        x_sample, mod_all, streams_s, s0, prm, t=seq_s, n_ph=seq_s, row_blk=n_b, want_v=True)

    (yp,), (ys, vs) = _main_call(
        [set_p, set_s], mod_all, prm["norm2_g"], prm["final_g"], prm["ln_g"], prm["ln_b"],
        prm["gm_w"], prm["gm_bt"], prm["w_out"], prm["w_gu"], prm["w_down"])
    return (yp.reshape(x_prompt.shape), ys.reshape(x_sample.shape), pre[None], pim[None],
            sre[None], sim[None], vs.reshape(n_b, seq_s, GM_WIDTH)[None])
```

```python
import functools
import math

import jax
import jax.numpy as jnp
from jax import lax
from jax.experimental import pallas as pl
from jax.experimental.pallas import tpu as pltpu

D_MODEL = 1024
S5_WIDTH = 512
S5_GROUP = 16
S5_GROUPS = 32
S5_STATE = 64
GM_WIDTH = 512
GM_CHUNK = 128
GM_HEADS = 8
GM_HEAD_DIM = 64
CHUNK = 64
IN_WIDTH = S5_WIDTH + 2 * GM_WIDTH
D_FF = 2816
EPS = 1e-6

LANES = 128
SUBLANES = 8
BF16_SUBLANES = 16
MXU_DIM = 256
VMEM_LIMIT_BYTES = 56 * 1024 * 1024

SUB = MXU_DIM // S5_GROUP
S5_LONG_SUBS = 4
PREP_GROUPS_PER_STEP = 8
S5_GROUPS_PER_STEP_LONG = 8
S5_GROUPS_PER_STEP_SHORT = 16
ADA_K_BLOCK = 256

F32 = jnp.float32
BF16 = jnp.bfloat16


def _cparams(n_grid_axes):
    return pltpu.CompilerParams(
        dimension_semantics=("arbitrary",) * n_grid_axes,
        vmem_limit_bytes=VMEM_LIMIT_BYTES,
    )


def _const_spec(shape):
    nd = len(shape)
    return pl.BlockSpec(shape, lambda *_: (0,) * nd, pipeline_mode=pl.Buffered(1))


def _rms_scale(x):
    return lax.rsqrt(jnp.mean(x * x, axis=-1, keepdims=True) + EPS)


_GELU_C0 = math.sqrt(2.0 / math.pi)
_GELU_C1 = 0.044715 * _GELU_C0


def _gelu(x):
    hx = 0.5 * x
    return hx + hx * jnp.tanh(x * (_GELU_C0 + _GELU_C1 * (x * x)))


def _sigmoid(x):
    return 0.5 * jnp.tanh(0.5 * x) + 0.5


def _cmul(ar, ai, xr, xi):
    return ar * xr - ai * xi, ar * xi + ai * xr


def _div_pow2(idx, divisor):
    shift = divisor.bit_length() - 1
    assert divisor == 1 << shift
    return lax.shift_right_logical(idx, shift)


def _mod_pow2(idx, divisor):
    assert divisor & (divisor - 1) == 0
    return lax.bitwise_and(idx, divisor - 1)


def _place_rows(sel, x):
    hi = x.astype(BF16)
    rest = x - hi.astype(F32)
    mid = rest.astype(BF16)
    lo = (rest - mid.astype(F32)).astype(BF16)
    dot = functools.partial(jnp.dot, preferred_element_type=F32)
    return dot(sel, hi) + (dot(sel, mid) + dot(sel, lo))


def _dot_split(x, y):
    x_hi, y_hi = x.astype(BF16), y.astype(BF16)
    x_lo = (x - x_hi.astype(F32)).astype(BF16)
    y_lo = (y - y_hi.astype(F32)).astype(BF16)
    dot = functools.partial(jnp.dot, preferred_element_type=F32)
    return dot(x_hi, y_hi) + (dot(x_lo, y_hi) + dot(x_hi, y_lo))


OCT = SUBLANES


def _oct_shape(n_rows, t, width):
    assert n_rows % OCT == 0 and width % LANES == 0
    return (n_rows // OCT, width // LANES, t, OCT, LANES)


def _oct_store(ref, ph, val):
    n_oct, n_lb = ref.shape[0], ref.shape[1]
    for lb in range(n_lb):
        ref[:, lb, ph, :, :] = val[:, lb * LANES:(lb + 1) * LANES].reshape(n_oct, OCT, LANES)


def _oct_load(ref, t):
    n_lb = ref.shape[0] // (t * OCT)
    chunks = []
    for c in range(OCT):
        chunks.append(jnp.concatenate(
            [ref[pl.ds(lb * t * OCT + c, t, stride=OCT), :] for lb in range(n_lb)], axis=1))
    return jnp.concatenate(chunks, axis=0)


def _cast_rider_specs(arrays, grid):
    n_steps = math.prod(grid)

    def row_block(*idx):
        step = 0
        for i, extent in zip(idx, grid):
            step = step * extent + i
        return (step, 0)

    specs, shapes = [], []
    for a in arrays:
        rows, cols = a.shape
        blk = rows // n_steps
        assert blk * n_steps == rows and blk % BF16_SUBLANES == 0
        specs.append(pl.BlockSpec((blk, cols), row_block))
        shapes.append(jax.ShapeDtypeStruct((rows, cols), BF16))
    return specs, shapes


def _split_riders(refs, n_in, n_out, n_cast):
    ins, refs = refs[:n_in], refs[n_in:]
    cast_in, refs = refs[:n_cast], refs[n_cast:]
    outs, refs = refs[:n_out], refs[n_out:]
    cast_out, scratch = refs[:n_cast], refs[n_cast:]
    return (*ins, *outs, *scratch), list(zip(cast_in, cast_out))


def _run_riders(pairs):
    for src, dst in pairs:
        dst[...] = src[...].astype(BF16)


def _ada_kernel(c_ref, w_ref, b_ref, o_ref):
    @pl.when(pl.program_id(0) == 0)
    def _():
        o_ref[...] = jnp.broadcast_to(b_ref[...], o_ref.shape)

    c = c_ref[...]
    o_ref[...] += _dot_split(c * jax.nn.sigmoid(c), w_ref[...])


def _ada_call(c_pad, w_ada, b_ada):
    rows = c_pad.shape[0]
    n_in, n_out = w_ada.shape
    bk = ADA_K_BLOCK
    return pl.pallas_call(
        _ada_kernel,
        out_shape=jax.ShapeDtypeStruct((rows, n_out), F32),
        grid=(n_in // bk,),
        in_specs=[
            pl.BlockSpec((rows, bk), lambda k: (0, k)),
            pl.BlockSpec((bk, n_out), lambda k: (k, 0)),
            pl.BlockSpec((1, n_out), lambda k: (0, 0)),
        ],
        out_specs=pl.BlockSpec((rows, n_out), lambda k: (0, 0)),
        compiler_params=_cparams(1),
        name="ada",
    )(c_pad, w_ada, b_ada)


def _discretise(lr, li, ls):
    step = jnp.exp(ls)
    mag = jnp.exp(lr * step)
    ar = mag * jnp.cos(li * step)
    ai = mag * jnp.sin(li * step)
    den = lr * lr + li * li
    fr = ((ar - 1.0) * lr + ai * li) / den
    fi = (ai * lr - (ar - 1.0) * li) / den
    return ar, ai, fr, fi


def _selection(rows, cols, row_of_col):
    r = lax.broadcasted_iota(jnp.int32, (rows, cols), 0)
    c = lax.broadcasted_iota(jnp.int32, (rows, cols), 1)
    return jnp.where(r == row_of_col(c), 1.0, 0.0).astype(BF16)


def _place(x, sel):
    hi = x.astype(BF16)
    rest = x - hi.astype(F32)
    mid = rest.astype(BF16)
    lo = (rest - mid.astype(F32)).astype(BF16)
    dot = functools.partial(jnp.dot, preferred_element_type=F32)
    return dot(hi, sel) + (dot(mid, sel) + dot(lo, sel))


def _to_column(row):
    k = row.shape[1]
    r = lax.broadcasted_iota(jnp.int32, (k, k), 0)
    c = lax.broadcasted_iota(jnp.int32, (k, k), 1)
    return jnp.sum(jnp.where(r == c, jnp.broadcast_to(row, (k, k)), 0.0), axis=1, keepdims=True)


def _s5_prep_kernel(*refs, n_cast):
    refs, riders = _split_riders(refs, n_in=7, n_out=4, n_cast=n_cast)
    lam_re_ref, lam_im_ref, ls_ref, d_ref, *grouped = refs
    n, sub, p = S5_STATE, SUB, S5_GROUP
    width = sub * p

    def col_source(c):
        return jnp.where(c < width, (sub - 1) - _div_pow2(c, p), jnp.where(c < 2 * width, sub + 1, sub))

    sels = dict(
        twice=_selection(n, 2 * n, lambda c: _mod_pow2(c, n)),
        tile_rows=_selection(p, width, lambda c: _mod_pow2(c, p)).T,
        cols=_selection(2 * n, 2 * width + LANES, col_source),
    )
    gps = grouped[0].shape[0]
    pending = [_s5_prep_group(pl.program_id(0) * gps + gi, sels, lam_re_ref, lam_im_ref, ls_ref, d_ref,
                              *[r.at[gi] for r in grouped]) for gi in range(gps)]
    while pending:
        pending = [gen for gen in pending if next(gen, "done") != "done"]
    _run_riders(riders)


def _s5_prep_group(g, sels, lam_re_ref, lam_im_ref, ls_ref, d_ref, bt_ref, c_re_ref, c_im_ref,
                   wk_ref, qm_ref, tabr_ref, tabi_ref):
    n, sub, p = S5_STATE, SUB, S5_GROUP
    width = sub * p

    lr_row = lam_re_ref[pl.ds(g, 1), :]
    li_row = lam_im_ref[pl.ds(g, 1), :]
    ls_all = ls_ref[...]
    grp_lane = lax.broadcasted_iota(jnp.int32, ls_all.shape, 1)
    ls = jnp.sum(jnp.where(grp_lane == g, ls_all, 0.0), axis=1, keepdims=True)

    ar8, ai8, fr8, fi8 = _discretise(jnp.broadcast_to(lr_row, (SUBLANES, n)),
                                     jnp.broadcast_to(li_row, (SUBLANES, n)), ls)
    twice = lambda t8: _place(jnp.concatenate([t8] * (p // SUBLANES), axis=0), sels["twice"])
    a2r, a2i = twice(ar8), twice(ai8)
    first = lax.broadcasted_iota(jnp.int32, (p, 2 * n), 1) < n
    c2r = _place(c_re_ref[...], sels["twice"])
    c2i = _place(c_im_ref[...], sels["twice"])
    pr = jnp.ones_like(a2r)
    pi = jnp.zeros_like(a2r)
    ccat = []
    tbl_rows = SUBLANES * (-(-(sub + 2) // SUBLANES))
    tbl_row = lax.broadcasted_iota(jnp.int32, (tbl_rows, 2 * n), 0)
    as_tbl_row = lambda re2, im2: jnp.concatenate([jnp.where(first, re2, im2)] * (tbl_rows // p + 1),
                                                  axis=0)[:tbl_rows]
    tbl = jnp.zeros((tbl_rows, 2 * n), F32)
    for d in range(sub + 1):
        ccat.append(c2r * jnp.where(first, pr, -pi) + c2i * jnp.where(first, -pi, -pr))
        tbl = jnp.where(tbl_row == d, as_tbl_row(pr, pi), tbl)
        pr, pi = _cmul(a2r, a2i, pr, pi)
    tbl = jnp.where(tbl_row == sub + 1, as_tbl_row(twice(fr8), twice(fi8)), tbl)
    yield
    qm_ref[...] = jnp.concatenate(ccat[1:], axis=0).astype(BF16)
    rcat = jnp.concatenate(ccat[:sub], axis=0)

    tbl_t = jnp.concatenate([tbl, jnp.zeros((2 * n - tbl_rows, 2 * n), F32)], axis=0).T
    cols = _place(tbl_t, sels["cols"])
    yield
    apr, fr, a16r = cols[:n, :width], cols[:n, width:2 * width], cols[:n, 2 * width:]
    api, fi, a16i = cols[n:, :width], cols[n:, width:2 * width], cols[n:, 2 * width:]
    b_tiled = _place_rows(sels["tile_rows"], bt_ref[...]).T
    btr, bti = b_tiled[:n], b_tiled[n:]
    bbr, bbi = _cmul(fr, fi, btr, bti)
    pmr, pmi = _cmul(apr, api, bbr, bbi)
    wk_ref[width:width + n, :] = pmr.astype(BF16)
    wk_ref[width + n:, :] = pmi.astype(BF16)
    yield

    bbcat = jnp.concatenate([bbr, bbi], axis=0)
    kt = _dot_split(rcat, bbcat)
    yield
    d_col = _to_column(d_ref[pl.ds(g, 1), :])
    row_p = lax.broadcasted_iota(jnp.int32, (p, width), 0)
    lane_p = lax.broadcasted_iota(jnp.int32, (p, width), 1)
    d_diag = jnp.where(row_p == _mod_pow2(lane_p, p), d_col, 0.0)
    kt = jnp.concatenate([kt[:p] + d_diag, kt[p:]], axis=0)
    col_blk = _div_pow2(lax.broadcasted_iota(jnp.int32, (width, width), 1), p)
    m16 = jnp.zeros((width, width), F32)
    for k in range(sub):
        if k == 0:
            shifted = kt
        else:
            shifted = jnp.concatenate(
                [jnp.zeros((k * p, width), F32), kt[:width - k * p]], axis=0)
        m16 = jnp.where(col_blk == k, shifted, m16)
    wk_ref[:width, :] = m16.astype(BF16)

    tabr_ref[...] = a16r
    tabi_ref[...] = a16i


def _s5_prep_call(lam_re, lam_im, log_step, bt, c_re, c_im, d, cast_riders=()):
    g, n, p, sub = S5_GROUPS, S5_STATE, S5_GROUP, SUB
    width = sub * p
    grp = lambda shape: pl.BlockSpec((PREP_GROUPS_PER_STEP,) + shape, lambda i: (i, 0, 0))
    whole = lambda a: pl.BlockSpec(a.shape, lambda i: (0,) * a.ndim)
    n_steps = g // PREP_GROUPS_PER_STEP
    rider_specs, rider_shapes = _cast_rider_specs(cast_riders, (n_steps,))
    return pl.pallas_call(
        functools.partial(_s5_prep_kernel, n_cast=len(cast_riders)),
        out_shape=(
            jax.ShapeDtypeStruct((g, width + 2 * n, width), BF16),
            jax.ShapeDtypeStruct((g, width, 2 * n), BF16),
            jax.ShapeDtypeStruct((g, n, LANES), F32),
            jax.ShapeDtypeStruct((g, n, LANES), F32),
            *rider_shapes,
        ),
        grid=(n_steps,),
        in_specs=[whole(lam_re), whole(lam_im), whole(log_step), whole(d)]
        + [grp((p, 2 * n))] + [grp((p, n))] * 2 + rider_specs,
        out_specs=(grp((width + 2 * n, width)), grp((width, 2 * n)),
                   grp((n, LANES)), grp((n, LANES)), *rider_specs),
        compiler_params=_cparams(1),
        name="s5_prep",
    )(lam_re, lam_im, log_step, d, bt, c_re, c_im, *cast_riders)


def _mod_vec(mod_ref, stream, idx):
    return mod_ref[stream:stream + 1, idx * D_MODEL:(idx + 1) * D_MODEL]


def _inproj_kernel(x_ref, mod_ref, g1_ref, w_ref, zt_ref, zuv_ref, hs_ref,
                   *, n_ph, rows, lane_rows, streams):
    m = rows * n_ph
    assert len(streams) in (1, rows)
    shift, scale = (jnp.concatenate([_mod_vec(mod_ref, s, idx) for s in streams], axis=0)
                    for idx in (0, 1))
    gain = (g1_ref[...] * (1.0 + scale))[:, None, :]
    shift = shift[:, None, :]
    x3 = x_ref[...]
    h = (x3 * _rms_scale(x3) * gain + shift).reshape(m, D_MODEL)
    zuv = jnp.dot(h.astype(BF16), w_ref[:, S5_WIDTH:], preferred_element_type=F32)
    zuv_ref[...] = zuv.reshape(rows, n_ph, 2 * GM_WIDTH).astype(BF16)

    n_lb = D_MODEL // LANES
    for lb in range(n_lb):
        hs_ref[lb] = h[:, lb * LANES:(lb + 1) * LANES]
    hp = jnp.concatenate(
        [jnp.concatenate([hs_ref[lb, pl.ds(ph, rows, stride=n_ph), :] for lb in range(n_lb)], axis=1)
         for ph in range(n_ph)], axis=0).astype(BF16)
    z5 = jnp.dot(hp, w_ref[:, :S5_WIDTH], preferred_element_type=F32)
    for ph in range(n_ph):
        zz = z5[ph * rows:(ph + 1) * rows]
        if lane_rows > rows:
            zz = jnp.concatenate([zz, jnp.zeros((lane_rows - rows, S5_WIDTH), F32)], axis=0)
        zt = zz.T.reshape(S5_GROUPS, S5_GROUP, lane_rows)
        zt_ref[:, ph, :, :] = zt.astype(BF16)


def _phase_blocks(n_rows, row_blk):
    if row_blk % LANES == 0:
        assert n_rows % row_blk == 0
        return n_rows // row_blk, row_blk, n_rows
    assert row_blk == n_rows
    lane_rows = -(-n_rows // LANES) * LANES
    return 1, lane_rows, lane_rows


def _inproj_call(x3, mod_all, streams, norm1_g, w_in_b, *, n_ph, row_blk):
    rows, t, _ = x3.shape
    n_rb, lane_blk, lane_rows = _phase_blocks(rows, row_blk)
    kern = functools.partial(_inproj_kernel, n_ph=n_ph, rows=row_blk, lane_rows=lane_blk,
                             streams=streams)
    return pl.pallas_call(
        kern,
        out_shape=(
            jax.ShapeDtypeStruct((S5_GROUPS, t, S5_GROUP, lane_rows), BF16),
            jax.ShapeDtypeStruct((rows, t, 2 * GM_WIDTH), BF16),
        ),
        grid=(n_rb, t // n_ph),
        in_specs=[
            pl.BlockSpec((row_blk, n_ph, D_MODEL), lambda i, j: (i, j, 0)),
            _const_spec(mod_all.shape),
            _const_spec((1, D_MODEL)),
            _const_spec((D_MODEL, IN_WIDTH)),
        ],
        out_specs=(
            pl.BlockSpec((S5_GROUPS, n_ph, S5_GROUP, lane_blk), lambda i, j: (0, j, 0, i)),
            pl.BlockSpec((row_blk, n_ph, 2 * GM_WIDTH), lambda i, j: (i, j, 0)),
        ),
        scratch_shapes=[pltpu.VMEM((D_MODEL // LANES, row_blk * n_ph, LANES), F32)],
        compiler_params=_cparams(2),
        name="inproj",
    )(x3, mod_all, norm1_g, w_in_b)


def _s5_kernel(*refs, n_sub, lanes, scan, final_rows, n_cast):
    refs, riders = _split_riders(refs, n_in=5 if scan else 6, n_out=2, n_cast=n_cast)
    if scan:
        zt_ref, wk_ref, qm_ref, tabr_ref, tabi_ref, yt_ref, sf_ref, ybuf, lbuf = refs
        s0_ref = None
    else:
        zt_ref, wk_ref, qm_ref, tabr_ref, tabi_ref, s0_ref, yt_ref, sf_ref, ybuf, lbuf = refs
    groups = range(zt_ref.shape[0])
    n = S5_STATE
    width = SUB * S5_GROUP

    widen = lambda tile: jnp.concatenate([tile] * (lanes // LANES), axis=1)
    sub_pows = []
    for g in groups:
        a1 = (widen(tabr_ref[g]), widen(tabi_ref[g]))
        pows = [a1]
        for _ in range(n_sub - 1):
            pows.append(_cmul(*a1, *pows[-1]))
        sub_pows.append(pows)

    local = []
    for g in groups:
        wk = wk_ref[g]
        lr = li = None
        for j in range(n_sub):
            u = zt_ref[g, j * width:(j + 1) * width, :]
            r = jnp.dot(wk, u, preferred_element_type=F32)
            ybuf[g, j * width:(j + 1) * width, :] = r[:width]
            wr = r[width:width + n]
            wi = r[width + n:]
            if j == 0:
                lr, li = wr, wi
            else:
                tr, ti = _cmul(*sub_pows[g][0], lr, li)
                lr, li = tr + wr, ti + wi
            lbuf[g, j, :n, :] = lr
            lbuf[g, j, n:, :] = li
        local.append((lr, li))

    if scan:
        lane = lax.broadcasted_iota(jnp.int32, (n, lanes), 1)
        xs = list(local)
        ms = [sub_pows[g][n_sub - 1] for g in groups]
        for i in range(int(math.log2(lanes))):
            sh = 1 << i
            for g in groups:
                xr, xi = xs[g]
                rr = jnp.where(lane >= sh, pltpu.roll(xr, sh, 1), 0.0)
                ri = jnp.where(lane >= sh, pltpu.roll(xi, sh, 1), 0.0)
                tr, ti = _cmul(*ms[g], rr, ri)
                xs[g] = (xr + tr, xi + ti)
                ms[g] = _cmul(*ms[g], *ms[g])
        entering = [(jnp.where(lane >= 1, pltpu.roll(xr, 1, 1), 0.0),
                     jnp.where(lane >= 1, pltpu.roll(xi, 1, 1), 0.0)) for xr, xi in xs]
    else:
        assert lanes == LANES
        entering = []
        for g in groups:
            s0 = s0_ref[:, g, :]
            s0t = jnp.concatenate([s0, jnp.zeros((LANES - s0.shape[0], 2 * n), F32)], axis=0).T
            entering.append((s0t[:n], s0t[n:]))

    keep = slice(lanes - LANES, lanes)
    for g in groups:
        qm = qm_ref[g]
        sr, si = entering[g]
        for j in range(n_sub):
            if j == 0:
                pr, pi = sr, si
            else:
                tr, ti = _cmul(*sub_pows[g][j - 1], sr, si)
                pr, pi = lbuf[g, j - 1, :n, :] + tr, lbuf[g, j - 1, n:, :] + ti
            sp = jnp.concatenate([pr, pi], axis=0).astype(BF16)
            y = ybuf[g, j * width:(j + 1) * width, :] + jnp.dot(qm, sp, preferred_element_type=F32)
            yt_ref[g, j * width:(j + 1) * width, :] = y.astype(BF16)
        tr, ti = _cmul(*sub_pows[g][n_sub - 1], sr, si)
        ends = jnp.concatenate([lbuf[g, n_sub - 1, :n, keep] + tr[:, keep],
                                lbuf[g, n_sub - 1, n:, keep] + ti[:, keep]], axis=0).T
        sf_ref[:, g, :] = ends[final_rows[0]:final_rows[1]]
    _run_riders(riders)


def _s5_call(zt, wk, qm, tabr, tabi, s0, *, n_streams, n_sub, scan, cast_riders=()):
    g, rows, lanes = zt.shape
    n = S5_STATE
    width = SUB * S5_GROUP
    assert lanes % LANES == 0 and lanes & (lanes - 1) == 0
    gps = S5_GROUPS_PER_STEP_LONG if scan else S5_GROUPS_PER_STEP_SHORT
    grp = lambda shape: pl.BlockSpec((gps,) + shape, lambda i: (i, 0, 0))
    per_stream = pl.BlockSpec((n_streams, gps, 2 * n), lambda i: (0, i, 0))
    in_specs = [grp((rows, lanes)), grp((width + 2 * n, width)), grp((width, 2 * n)),
                grp((n, LANES)), grp((n, LANES))]
    args = [zt, wk, qm, tabr, tabi]
    if scan:
        assert n_streams == 1
        final_rows = (LANES - 1, LANES)
    else:
        in_specs.append(per_stream)
        args.append(s0)
        final_rows = (0, n_streams)
    rider_specs, rider_shapes = _cast_rider_specs(cast_riders, (g // gps,))
    kern = functools.partial(_s5_kernel, n_sub=n_sub, lanes=lanes, scan=scan, final_rows=final_rows,
                             n_cast=len(cast_riders))
    return pl.pallas_call(
        kern,
        out_shape=(jax.ShapeDtypeStruct((g, rows, lanes), BF16),
                   jax.ShapeDtypeStruct((n_streams, g, 2 * n), F32), *rider_shapes),
        grid=(g // gps,),
        in_specs=in_specs + rider_specs,
        out_specs=(grp((rows, lanes)), per_stream, *rider_specs),
        scratch_shapes=[pltpu.VMEM((gps, rows, lanes), F32),
                        pltpu.VMEM((gps, n_sub, 2 * n, lanes), F32)],
        compiler_params=_cparams(1),
        name="s5",
    )(*args, *cast_riders)


def _glu_kernel(*refs, n_ph, rows, lane_rows, n_cast):
    (yt_ref, w_ref, b_ref, m_ref), riders = _split_riders(refs, n_in=3, n_out=1, n_cast=n_cast)
    _run_riders(riders)
    gs = []
    for ph in range(n_ph):
        yt = yt_ref[:, ph, :, :].astype(F32).reshape(S5_WIDTH, lane_rows)
        gs.append(_gelu(yt.T[:rows]))
    gy = jnp.concatenate(gs, axis=0)
    gate = jnp.dot(gy.astype(BF16), w_ref[...], preferred_element_type=F32) + b_ref[...]
    m = gy * _sigmoid(gate)
    for ph in range(n_ph):
        _oct_store(m_ref, ph, m[ph * rows:(ph + 1) * rows])


def _glu_call(yt4, w_glu_b, b_glu, *, rows, n_ph, row_blk, cast_riders=()):
    g, t, p, lane_rows = yt4.shape
    n_rb, lane_blk, lane_rows_expected = _phase_blocks(rows, row_blk)
    assert lane_rows == lane_rows_expected
    grid = (n_rb, t // n_ph)
    rider_specs, rider_shapes = _cast_rider_specs(cast_riders, grid)
    kern = functools.partial(_glu_kernel, n_ph=n_ph, rows=row_blk, lane_rows=lane_blk,
                             n_cast=len(cast_riders))
    oct_shape = _oct_shape(rows, t, S5_WIDTH)
    return pl.pallas_call(
        kern,
        out_shape=(jax.ShapeDtypeStruct(oct_shape, F32), *rider_shapes),
        grid=grid,
        in_specs=[
            pl.BlockSpec((g, n_ph, p, lane_blk), lambda i, j: (0, j, 0, i)),
            _const_spec((S5_WIDTH, S5_WIDTH)),
            _const_spec((1, S5_WIDTH)),
            *rider_specs,
        ],
        out_specs=(pl.BlockSpec((row_blk // OCT, oct_shape[1], n_ph, OCT, LANES),
                                lambda i, j: (i, 0, j, 0, 0)), *rider_specs),
        compiler_params=_cparams(2),
        name="glu",
    )(yt4, w_glu_b, b_glu, *cast_riders)


def _mod_rows(mod_ref, streams, seq, idx):
    if len(streams) == 1:
        return _mod_vec(mod_ref, streams[0], idx)
    return jnp.concatenate(
        [jnp.broadcast_to(_mod_vec(mod_ref, s, idx), (seq, D_MODEL)) for s in streams], axis=0)


def _interleave(*gens):
    gens = list(gens)
    while gens:
        for g in list(gens):
            try:
                next(g)
            except StopIteration:
                gens.remove(g)


def _main_front(x_ref, m_ref, zuv_ref, v_out_refs, mod_ref, g2_ref, lng_ref, lnb_ref, gw_ref, gbt_ref,
                wo_ref, x1_ref, h2_ref, v_ref, ygm_ref, *, tm, cl, seq, streams):
    t = tm // OCT
    hd = GM_HEAD_DIM

    gv = _gelu(zuv_ref[:, GM_WIDTH:].astype(F32))
    cen = gv - jnp.mean(gv, axis=-1, keepdims=True)
    var = jnp.mean(cen * cen, axis=-1, keepdims=True)
    v = cen * lax.rsqrt(var + EPS) * lng_ref[...] + lnb_ref[...]
    for v_out_ref in v_out_refs:
        v_out_ref[...] = v
    v_ref[...] = v.astype(BF16)
    yield

    blk_i = _div_pow2(lax.broadcasted_iota(jnp.int32, (cl, cl), 0), CHUNK)
    blk_j = _div_pow2(lax.broadcasted_iota(jnp.int32, (cl, cl), 1), CHUNK)
    causal = blk_j <= blk_i
    first_head = lax.broadcasted_iota(jnp.int32, (cl, 2 * hd), 1) < hd
    for pr in range(GM_HEADS // 2):
        h0, h1 = 2 * pr, 2 * pr + 1
        wm = jnp.concatenate(
            [jnp.where(causal, gw_ref[h, :cl, :cl], 0.0) for h in (h0, h1)], axis=1).astype(BF16)
        bias = jnp.where(first_head, gbt_ref[:cl, h0:h0 + 1], gbt_ref[:cl, h1:h1 + 1])
        cs = slice(h0 * hd, (h1 + 1) * hd)
        for ci in range(tm // cl):
            rs = slice(ci * cl, (ci + 1) * cl)
            vv = v_ref[rs, cs]
            zero = jnp.zeros_like(vv)
            rhs = jnp.concatenate([jnp.where(first_head, vv, zero), jnp.where(first_head, zero, vv)],
                                  axis=0)
            mixed = jnp.dot(wm, rhs, preferred_element_type=F32) + bias
            ygm_ref[rs, cs] = (_gelu(zuv_ref[rs, cs].astype(F32)) * mixed).astype(BF16)
        yield

    x1_ref[...] = jnp.dot(_oct_load(m_ref, t).astype(BF16), wo_ref[:S5_WIDTH, :],
                          preferred_element_type=F32)
    yield
    attn = x1_ref[...] + jnp.dot(ygm_ref[...], wo_ref[S5_WIDTH:, :], preferred_element_type=F32)
    gate1, shift2, scale2 = (_mod_rows(mod_ref, streams, seq, idx) for idx in (2, 3, 4))
    x1 = x_ref[...] + gate1 * attn
    x1_ref[...] = x1
    h2_ref[...] = (x1 * _rms_scale(x1) * (g2_ref[...] * (1.0 + scale2)) + shift2).astype(BF16)


def _main_back(y_ref, mod_ref, gf_ref, wgu_ref, wd_ref, x1_ref, h2_ref, act_ref, *, seq, streams):
    assert D_FF % MXU_DIM == 0 and D_MODEL % MXU_DIM == 0
    for lo in range(0, D_FF, MXU_DIM):
        hi = lo + MXU_DIM
        gg = jnp.dot(h2_ref[...], wgu_ref[:, lo:hi], preferred_element_type=F32)
        up = jnp.dot(h2_ref[...], wgu_ref[:, D_FF + lo:D_FF + hi], preferred_element_type=F32)
        act_ref[:, lo:hi] = (gg * jax.nn.sigmoid(gg) * up).astype(BF16)
        yield
    gate2 = _mod_rows(mod_ref, streams, seq, 5)
    sumsq = None
    for lo in range(0, D_MODEL, MXU_DIM):
        cs = slice(lo, lo + MXU_DIM)
        acc = jnp.dot(act_ref[...], wd_ref[:, cs], preferred_element_type=F32)
        x2 = x1_ref[:, cs] + gate2[:, cs] * acc
        part = jnp.sum(x2 * x2, axis=-1, keepdims=True)
        sumsq = part if sumsq is None else sumsq + part
        y_ref[:, cs] = x2
        yield
    y_ref[...] = y_ref[...] * lax.rsqrt(sumsq * (1.0 / D_MODEL) + EPS) * gf_ref[...]


def _main_sets_kernel(*refs, sets, plan, n_shared):
    mod_ref, g2_ref, gf_ref, lng_ref, lnb_ref, gw_ref, gbt_ref, wo_ref, wgu_ref, wd_ref = refs[:n_shared]
    x1_ref, h2_ref, v_ref, ygm_ref, act_ref = refs[-5:]
    slot_rows = x1_ref.shape[0] // 2
    step = pl.program_id(0)
    set_refs, pos, out_pos = [], n_shared, n_shared + 3 * len(sets)
    for st in sets:
        n_out = 2 if st["want_v"] else 1
        set_refs.append((refs[pos:pos + 3], refs[out_pos:out_pos + n_out]))
        pos += 3
        out_pos += n_out

    def slot(parity, tm):
        return (x1_ref.at[parity * slot_rows:parity * slot_rows + tm],
                h2_ref.at[parity * slot_rows:parity * slot_rows + tm])

    paths = [(lo, hi, front, back, parity) for lo, hi, front, back in plan
             for parity in sorted({s % 2 for s in range(lo, min(hi, lo + 2))})]
    for lo, hi, front, back, parity in paths:
        in_range = jnp.logical_and(step >= lo, step < hi)

        @pl.when(jnp.logical_and(in_range, lax.rem(step, 2) == parity))
        def _(front=front, back=back, parity=parity):
            gens = []
            if back is not None:
                st, (_, outs) = sets[back], set_refs[back]
                gens.append(_main_back(outs[0], mod_ref, gf_ref, wgu_ref, wd_ref,
                                       *slot(1 - parity, st["tm"]), act_ref.at[:st["tm"]],
                                       seq=st["seq"], streams=st["streams"]))
            if front is not None:
                st, (ins, outs) = sets[front], set_refs[front]
                tm = st["tm"]
                gens.append(_main_front(*ins, outs[1:], mod_ref, g2_ref, lng_ref, lnb_ref, gw_ref, gbt_ref,
                                        wo_ref, *slot(parity, tm), v_ref.at[:tm], ygm_ref.at[:tm],
                                        tm=tm, cl=st["cl"], seq=st["seq"], streams=st["streams"]))
            _interleave(*gens)


def _main_call(stream_sets, mod_all, norm2_g, final_g, ln_g, ln_b, gm_w, gm_bt, w_out_b, w_gu_b, w_down_b):
    shared = (mod_all, norm2_g, final_g, ln_g, ln_b, gm_w, gm_bt, w_out_b, w_gu_b, w_down_b)
    set_args, set_in_specs, out_shape, out_specs, statics = [], [], [], [], []
    start = 0
    for st in stream_sets:
        n_tok = st["x2d"].shape[0]
        n_oct, n_lb, t, _, _ = st["m"].shape
        tm = OCT * t
        seq, streams = st["seq"], st["streams"]
        assert n_oct * tm == n_tok
        assert len(streams) == 1 or (n_oct == 1 and len(streams) * seq == tm)

        def front_index(i, start=start, n_oct=n_oct):
            return (jnp.clip(i - start, 0, n_oct - 1), 0)

        def back_index(i, start=start, n_oct=n_oct):
            return (jnp.clip(i - 1 - start, 0, n_oct - 1), 0)

        mode = dict(pipeline_mode=pl.Buffered(1)) if n_oct == 1 else {}
        oct_rows = n_lb * t * OCT
        set_args += [st["x2d"], st["m"].reshape(n_oct * oct_rows, LANES), st["zuv"]]
        set_in_specs += [pl.BlockSpec((tm, D_MODEL), front_index, **mode),
                         pl.BlockSpec((oct_rows, LANES), front_index, **mode),
                         pl.BlockSpec((tm, 2 * GM_WIDTH), front_index, **mode)]
        out_shape.append(jax.ShapeDtypeStruct((n_tok, D_MODEL), F32))
        out_specs.append(pl.BlockSpec((tm, D_MODEL), back_index))
        if st["want_v"]:
            out_shape.append(jax.ShapeDtypeStruct((n_tok, GM_WIDTH), F32))
            out_specs.append(pl.BlockSpec((tm, GM_WIDTH), front_index))
        statics.append(dict(tm=tm, cl=min(GM_CHUNK, seq), seq=seq, streams=streams, want_v=st["want_v"]))
        start += n_oct
    tile_set = [k for k, st in enumerate(stream_sets) for _ in range(st["m"].shape[0])]
    plan = []
    for s in range(start + 1):
        pair = (tile_set[s] if s < start else None, tile_set[s - 1] if s >= 1 else None)
        if plan and plan[-1][2:] == pair:
            plan[-1] = (plan[-1][0], s + 1, *pair)
        else:
            plan.append((s, s + 1, *pair))
    tm_max = max(s["tm"] for s in statics)
    outs = pl.pallas_call(
        functools.partial(_main_sets_kernel, sets=statics, plan=plan, n_shared=len(shared)),
        out_shape=tuple(out_shape),
        grid=(start + 1,),
        in_specs=[_const_spec(a.shape) for a in shared] + set_in_specs,
        out_specs=tuple(out_specs),
        scratch_shapes=[pltpu.VMEM((2 * tm_max, D_MODEL), F32), pltpu.VMEM((2 * tm_max, D_MODEL), BF16),
                        pltpu.VMEM((tm_max, GM_WIDTH), BF16), pltpu.VMEM((tm_max, GM_WIDTH), BF16),
                        pltpu.VMEM((tm_max, D_FF), BF16)],
        compiler_params=_cparams(1),
        name="main",
    )(*shared, *set_args)
    results, pos = [], 0
    for s in statics:
        n_out = 2 if s["want_v"] else 1
        results.append(tuple(outs[pos:pos + n_out]))
        pos += n_out
    return results


def _mixer_front(x, mod_all, streams, s0, prm, *, t, n_ph, row_blk, want_v, f32_weights=None):
    b, seq, _ = x.shape
    n_chunks = seq // t
    rows = b * n_chunks
    scan = s0 is None
    assert (b == 1) if scan else (n_chunks == 1)
    n_sub = t // SUB

    zt4, zuv = _inproj_call(
        x.reshape(rows, t, D_MODEL), mod_all, streams, prm["norm1_g"], prm["w_in"],
        n_ph=n_ph, row_blk=row_blk)
    lane_rows = zt4.shape[-1]

    assert not scan or rows == lane_rows
    pending = dict(f32_weights or {})
    on_glu = {k: pending.pop(k) for k in ("w_gu",) if k in pending}
    yt, fin, *cast = _s5_call(zt4.reshape(S5_GROUPS, t * S5_GROUP, lane_rows),
                              prm["wk"], prm["qm"], prm["tabr"], prm["tabi"], s0,
                              n_streams=b, n_sub=n_sub, scan=scan, cast_riders=tuple(pending.values()))
    prm = {**prm, **dict(zip(pending.keys(), cast))}
    m, *cast = _glu_call(yt.reshape(S5_GROUPS, t, S5_GROUP, lane_rows), prm["w_glu"], prm["b_glu"],
                         rows=rows, n_ph=n_ph, row_blk=row_blk, cast_riders=tuple(on_glu.values()))
    prm = {**prm, **dict(zip(on_glu.keys(), cast))}

    n_tok = b * seq
    stream_set = dict(x2d=x.reshape(n_tok, D_MODEL), m=m, zuv=zuv.reshape(n_tok, 2 * GM_WIDTH),
                      streams=streams, seq=seq, want_v=want_v)

    return stream_set, fin[..., :S5_STATE], fin[..., S5_STATE:], prm


def kernel(x_prompt, x_sample, state_s5_re, state_s5_im, c_prompt, c_sample, norm1_g, norm2_g, w_ada, b_ada, w_in, s5_lambda_re, s5_lambda_im, s5_log_step, s5_b_re, s5_b_im, s5_c_re, s5_c_im, s5_d, s5_w_glu, s5_b_glu, gm_ln_g, gm_ln_b, gm_w_s, gm_b_s, w_out, ffn_w_gu, ffn_w_down, final_g):
    depth = w_in.shape[0]
    assert depth == 1
    l = 0
    n_p = c_prompt.shape[0]
    n_s = c_sample.shape[0]

    c_all = jnp.concatenate([c_prompt, c_sample], axis=0)
    c_pad = jnp.pad(c_all, ((0, -c_all.shape[0] % SUBLANES), (0, 0)))
    mod_all = _ada_call(c_pad, w_ada[l], b_ada[l][None, :])
    streams_p = tuple(range(n_p))
    streams_s = tuple(range(n_p, n_p + n_s))

    wk, qm, tabr, tabi, w_in_b, w_out_b, w_down_b = _s5_prep_call(
        s5_lambda_re[l], s5_lambda_im[l], s5_log_step[l][None, :],
        jnp.concatenate([jnp.swapaxes(s5_b_re[l], 1, 2), jnp.swapaxes(s5_b_im[l], 1, 2)], axis=-1),
        s5_c_re[l], s5_c_im[l], s5_d[l], cast_riders=(w_in[l], w_out[l], ffn_w_down[l]))

    prm = dict(
        norm1_g=norm1_g[l][None, :], norm2_g=norm2_g[l][None, :], final_g=final_g[None, :],
        w_in=w_in_b, w_out=w_out_b, w_down=w_down_b, ln_g=gm_ln_g[l][None, :], ln_b=gm_ln_b[l][None, :],
        wk=wk, qm=qm, tabr=tabr, tabi=tabi, b_glu=s5_b_glu[l][None, :],
        gm_w=gm_w_s[l], gm_bt=jnp.transpose(gm_b_s[l]),
    )
    later_weights = dict(w_glu=s5_w_glu[l], w_gu=ffn_w_gu[l])

    set_p, pre, pim, prm = _mixer_front(
        x_prompt, mod_all, streams_p, None, prm, f32_weights=later_weights,
        t=S5_LONG_SUBS * SUB, n_ph=SUBLANES, row_blk=LANES, want_v=False)
    s0 = jnp.concatenate([state_s5_re[l], state_s5_im[l]], axis=-1)
    n_b, seq_s, _ = x_sample.shape
    set_s, sre, sim, _ = _mixer_front(
        x_sample, mod_all, streams_s, s0, prm, t=seq_s, n_ph=seq_s, row_blk=n_b, want_v=True)

    (yp,), (ys, vs) = _main_call(
        [set_p, set_s], mod_all, prm["norm2_g"], prm["final_g"], prm["ln_g"], prm["ln_b"],
        prm["gm_w"], prm["gm_bt"], prm["w_out"], prm["w_gu"], prm["w_down"])
    return (yp.reshape(x_prompt.shape), ys.reshape(x_sample.shape), pre[None], pim[None],
            sre[None], sim[None], vs.reshape(n_b, seq_s, GM_WIDTH)[None])
```

```python
import functools
import math

import jax
import jax.numpy as jnp
from jax import lax
from jax.experimental import pallas as pl
from jax.experimental.pallas import tpu as pltpu

D_MODEL = 1024
S5_WIDTH = 512
S5_GROUP = 16
S5_GROUPS = 32
S5_STATE = 64
GM_WIDTH = 512
GM_CHUNK = 128
GM_HEADS = 8
GM_HEAD_DIM = 64
CHUNK = 64
IN_WIDTH = S5_WIDTH + 2 * GM_WIDTH
D_FF = 2816
EPS = 1e-6

LANES = 128
SUBLANES = 8
BF16_SUBLANES = 16
MXU_DIM = 256
VMEM_LIMIT_BYTES = 56 * 1024 * 1024

SUB = MXU_DIM // S5_GROUP
S5_LONG_SUBS = 4
PREP_GROUPS_PER_STEP = 8
S5_GROUPS_PER_STEP_LONG = 8
S5_GROUPS_PER_STEP_SHORT = 16

F32 = jnp.float32
BF16 = jnp.bfloat16


def _cparams(n_grid_axes):
    return pltpu.CompilerParams(
        dimension_semantics=("arbitrary",) * n_grid_axes,
        vmem_limit_bytes=VMEM_LIMIT_BYTES,
    )


def _const_spec(shape):
    nd = len(shape)
    return pl.BlockSpec(shape, lambda *_: (0,) * nd, pipeline_mode=pl.Buffered(1))


def _rms_scale(x):
    return lax.rsqrt(jnp.mean(x * x, axis=-1, keepdims=True) + EPS)


_GELU_C0 = math.sqrt(2.0 / math.pi)
_GELU_C1 = 0.044715 * _GELU_C0


def _gelu(x):
    hx = 0.5 * x
    return hx + hx * jnp.tanh(x * (_GELU_C0 + _GELU_C1 * (x * x)))


def _sigmoid(x):
    return 0.5 * jnp.tanh(0.5 * x) + 0.5


def _cmul(ar, ai, xr, xi):
    return ar * xr - ai * xi, ar * xi + ai * xr


def _div_pow2(idx, divisor):
    shift = divisor.bit_length() - 1
    assert divisor == 1 << shift
    return lax.shift_right_logical(idx, shift)


def _mod_pow2(idx, divisor):
    assert divisor & (divisor - 1) == 0
    return lax.bitwise_and(idx, divisor - 1)


def _place_rows(sel, x):
    hi = x.astype(BF16)
    rest = x - hi.astype(F32)
    mid = rest.astype(BF16)
    lo = (rest - mid.astype(F32)).astype(BF16)
    dot = functools.partial(jnp.dot, preferred_element_type=F32)
    return dot(sel, hi) + (dot(sel, mid) + dot(sel, lo))


def _dot_split(x, y):
    x_hi, y_hi = x.astype(BF16), y.astype(BF16)
    x_lo = (x - x_hi.astype(F32)).astype(BF16)
    y_lo = (y - y_hi.astype(F32)).astype(BF16)
    dot = functools.partial(jnp.dot, preferred_element_type=F32)
    return dot(x_hi, y_hi) + (dot(x_lo, y_hi) + dot(x_hi, y_lo))


OCT = SUBLANES


def _oct_shape(n_rows, t, width):
    assert n_rows % OCT == 0 and width % LANES == 0
    return (n_rows // OCT, width // LANES, t, OCT, LANES)


def _oct_store(ref, ph, val):
    n_oct, n_lb = ref.shape[0], ref.shape[1]
    for lb in range(n_lb):
        ref[:, lb, ph, :, :] = val[:, lb * LANES:(lb + 1) * LANES].reshape(n_oct, OCT, LANES)


def _oct_load(ref, t):
    n_lb = ref.shape[0] // (t * OCT)
    chunks = []
    for c in range(OCT):
        chunks.append(jnp.concatenate(
            [ref[pl.ds(lb * t * OCT + c, t, stride=OCT), :] for lb in range(n_lb)], axis=1))
    return jnp.concatenate(chunks, axis=0)


def _cast_rider_specs(arrays, grid):
    n_steps = math.prod(grid)

    def row_block(*idx):
        step = 0
        for i, extent in zip(idx, grid):
            step = step * extent + i
        return (step, 0)

    specs, shapes = [], []
    for a in arrays:
        rows, cols = a.shape
        blk = rows // n_steps
        assert blk * n_steps == rows and blk % BF16_SUBLANES == 0
        specs.append(pl.BlockSpec((blk, cols), row_block))
        shapes.append(jax.ShapeDtypeStruct((rows, cols), BF16))
    return specs, shapes


def _split_riders(refs, n_in, n_out, n_cast):
    ins, refs = refs[:n_in], refs[n_in:]
    cast_in, refs = refs[:n_cast], refs[n_cast:]
    outs, refs = refs[:n_out], refs[n_out:]
    cast_out, scratch = refs[:n_cast], refs[n_cast:]
    return (*ins, *outs, *scratch), list(zip(cast_in, cast_out))


def _run_riders(pairs):
    for src, dst in pairs:
        dst[...] = src[...].astype(BF16)


def _ada_step(c_ref, w_ref, b_ref, o_ref):
    @pl.when(pl.program_id(0) == 0)
    def _():
        o_ref[...] = jnp.broadcast_to(b_ref[...], o_ref.shape)

    c = c_ref[...]
    o_ref[...] += _dot_split(c * jax.nn.sigmoid(c), w_ref[...])


def _ada_specs(c_pad, w_ada, n_steps):
    rows = c_pad.shape[0]
    n_in, n_out = w_ada.shape
    bk = n_in // n_steps
    assert bk * n_steps == n_in and bk % LANES == 0
    in_specs = [
        pl.BlockSpec((rows, bk), lambda k: (0, k)),
        pl.BlockSpec((bk, n_out), lambda k: (k, 0)),
        pl.BlockSpec((1, n_out), lambda k: (0, 0)),
    ]
    return in_specs, pl.BlockSpec((rows, n_out), lambda k: (0, 0)), jax.ShapeDtypeStruct((rows, n_out), F32)


def _discretise(lr, li, ls):
    step = jnp.exp(ls)
    mag = jnp.exp(lr * step)
    ar = mag * jnp.cos(li * step)
    ai = mag * jnp.sin(li * step)
    den = lr * lr + li * li
    fr = ((ar - 1.0) * lr + ai * li) / den
    fi = (ai * lr - (ar - 1.0) * li) / den
    return ar, ai, fr, fi


def _selection(rows, cols, row_of_col):
    r = lax.broadcasted_iota(jnp.int32, (rows, cols), 0)
    c = lax.broadcasted_iota(jnp.int32, (rows, cols), 1)
    return jnp.where(r == row_of_col(c), 1.0, 0.0).astype(BF16)


def _place(x, sel):
    hi = x.astype(BF16)
    rest = x - hi.astype(F32)
    mid = rest.astype(BF16)
    lo = (rest - mid.astype(F32)).astype(BF16)
    dot = functools.partial(jnp.dot, preferred_element_type=F32)
    return dot(hi, sel) + (dot(mid, sel) + dot(lo, sel))


def _to_column(row):
    k = row.shape[1]
    r = lax.broadcasted_iota(jnp.int32, (k, k), 0)
    c = lax.broadcasted_iota(jnp.int32, (k, k), 1)
    return jnp.sum(jnp.where(r == c, jnp.broadcast_to(row, (k, k)), 0.0), axis=1, keepdims=True)


def _s5_prep_kernel(*refs, n_cast):
    refs, riders = _split_riders(refs, n_in=10, n_out=5, n_cast=n_cast)
    c_ref, w_ada_ref, b_ada_ref, lam_re_ref, lam_im_ref, ls_ref, d_ref, *grouped, mod_ref = refs
    n, sub, p = S5_STATE, SUB, S5_GROUP
    width = sub * p

    def col_source(c):
        return jnp.where(c < width, (sub - 1) - _div_pow2(c, p), jnp.where(c < 2 * width, sub + 1, sub))

    sels = dict(
        twice=_selection(n, 2 * n, lambda c: _mod_pow2(c, n)),
        tile_rows=_selection(p, width, lambda c: _mod_pow2(c, p)).T,
        cols=_selection(2 * n, 2 * width + LANES, col_source),
    )
    gps = grouped[0].shape[0]
    pending = [_s5_prep_group(pl.program_id(0) * gps + gi, sels, lam_re_ref, lam_im_ref, ls_ref, d_ref,
                              *[r.at[gi] for r in grouped]) for gi in range(gps)]
    while pending:
        pending = [gen for gen in pending if next(gen, "done") != "done"]
    _ada_step(c_ref, w_ada_ref, b_ada_ref, mod_ref)
    _run_riders(riders)


def _s5_prep_group(g, sels, lam_re_ref, lam_im_ref, ls_ref, d_ref, bt_ref, c_re_ref, c_im_ref,
                   wk_ref, qm_ref, tabr_ref, tabi_ref):
    n, sub, p = S5_STATE, SUB, S5_GROUP
    width = sub * p

    lr_row = lam_re_ref[pl.ds(g, 1), :]
    li_row = lam_im_ref[pl.ds(g, 1), :]
    ls_all = ls_ref[...]
    grp_lane = lax.broadcasted_iota(jnp.int32, ls_all.shape, 1)
    ls = jnp.sum(jnp.where(grp_lane == g, ls_all, 0.0), axis=1, keepdims=True)

    ar8, ai8, fr8, fi8 = _discretise(jnp.broadcast_to(lr_row, (SUBLANES, n)),
                                     jnp.broadcast_to(li_row, (SUBLANES, n)), ls)
    twice = lambda t8: _place(jnp.concatenate([t8] * (p // SUBLANES), axis=0), sels["twice"])
    a2r, a2i = twice(ar8), twice(ai8)
    first = lax.broadcasted_iota(jnp.int32, (p, 2 * n), 1) < n
    c2r = _place(c_re_ref[...], sels["twice"])
    c2i = _place(c_im_ref[...], sels["twice"])
    pr = jnp.ones_like(a2r)
    pi = jnp.zeros_like(a2r)
    ccat = []
    tbl_rows = SUBLANES * (-(-(sub + 2) // SUBLANES))
    tbl_row = lax.broadcasted_iota(jnp.int32, (tbl_rows, 2 * n), 0)
    as_tbl_row = lambda re2, im2: jnp.concatenate([jnp.where(first, re2, im2)] * (tbl_rows // p + 1),
                                                  axis=0)[:tbl_rows]
    tbl = jnp.zeros((tbl_rows, 2 * n), F32)
    for d in range(sub + 1):
        ccat.append(c2r * jnp.where(first, pr, -pi) + c2i * jnp.where(first, -pi, -pr))
        tbl = jnp.where(tbl_row == d, as_tbl_row(pr, pi), tbl)
        pr, pi = _cmul(a2r, a2i, pr, pi)
    tbl = jnp.where(tbl_row == sub + 1, as_tbl_row(twice(fr8), twice(fi8)), tbl)
    yield
    qm_ref[...] = jnp.concatenate(ccat[1:], axis=0).astype(BF16)
    rcat = jnp.concatenate(ccat[:sub], axis=0)

    tbl_t = jnp.concatenate([tbl, jnp.zeros((2 * n - tbl_rows, 2 * n), F32)], axis=0).T
    cols = _place(tbl_t, sels["cols"])
    yield
    apr, fr, a16r = cols[:n, :width], cols[:n, width:2 * width], cols[:n, 2 * width:]
    api, fi, a16i = cols[n:, :width], cols[n:, width:2 * width], cols[n:, 2 * width:]
    b_tiled = _place_rows(sels["tile_rows"], bt_ref[...]).T
    btr, bti = b_tiled[:n], b_tiled[n:]
    bbr, bbi = _cmul(fr, fi, btr, bti)
    pmr, pmi = _cmul(apr, api, bbr, bbi)
    wk_ref[width:width + n, :] = pmr.astype(BF16)
    wk_ref[width + n:, :] = pmi.astype(BF16)
    yield

    bbcat = jnp.concatenate([bbr, bbi], axis=0)
    kt = _dot_split(rcat, bbcat)
    yield
    d_col = _to_column(d_ref[pl.ds(g, 1), :])
    row_p = lax.broadcasted_iota(jnp.int32, (p, width), 0)
    lane_p = lax.broadcasted_iota(jnp.int32, (p, width), 1)
    d_diag = jnp.where(row_p == _mod_pow2(lane_p, p), d_col, 0.0)
    kt = jnp.concatenate([kt[:p] + d_diag, kt[p:]], axis=0)
    col_blk = _div_pow2(lax.broadcasted_iota(jnp.int32, (width, width), 1), p)
    m16 = jnp.zeros((width, width), F32)
    for k in range(sub):
        if k == 0:
            shifted = kt
        else:
            shifted = jnp.concatenate(
                [jnp.zeros((k * p, width), F32), kt[:width - k * p]], axis=0)
        m16 = jnp.where(col_blk == k, shifted, m16)
    wk_ref[:width, :] = m16.astype(BF16)

    tabr_ref[...] = a16r
    tabi_ref[...] = a16i


def _s5_prep_call(c_pad, w_ada, b_ada, lam_re, lam_im, log_step, bt, c_re, c_im, d, cast_riders=()):
    g, n, p, sub = S5_GROUPS, S5_STATE, S5_GROUP, SUB
    width = sub * p
    grp = lambda shape: pl.BlockSpec((PREP_GROUPS_PER_STEP,) + shape, lambda i: (i, 0, 0))
    whole = lambda a: pl.BlockSpec(a.shape, lambda i: (0,) * a.ndim)
    n_steps = g // PREP_GROUPS_PER_STEP
    rider_specs, rider_shapes = _cast_rider_specs(cast_riders, (n_steps,))
    ada_in_specs, ada_out_spec, ada_shape = _ada_specs(c_pad, w_ada, n_steps)
    return pl.pallas_call(
        functools.partial(_s5_prep_kernel, n_cast=len(cast_riders)),
        out_shape=(
            jax.ShapeDtypeStruct((g, width + 2 * n, width), BF16),
            jax.ShapeDtypeStruct((g, width, 2 * n), BF16),
            jax.ShapeDtypeStruct((g, n, LANES), F32),
            jax.ShapeDtypeStruct((g, n, LANES), F32),
            ada_shape,
            *rider_shapes,
        ),
        grid=(n_steps,),
        in_specs=ada_in_specs + [whole(lam_re), whole(lam_im), whole(log_step), whole(d)]
        + [grp((p, 2 * n))] + [grp((p, n))] * 2 + rider_specs,
        out_specs=(grp((width + 2 * n, width)), grp((width, 2 * n)),
                   grp((n, LANES)), grp((n, LANES)), ada_out_spec, *rider_specs),
        compiler_params=_cparams(1),
        name="prep",
    )(c_pad, w_ada, b_ada, lam_re, lam_im, log_step, d, bt, c_re, c_im, *cast_riders)


def _mod_vec(mod_ref, stream, idx):
    return mod_ref[stream:stream + 1, idx * D_MODEL:(idx + 1) * D_MODEL]


def _inproj_kernel(x_ref, mod_ref, g1_ref, w_ref, zt_ref, zuv_ref, hs_ref,
                   *, n_ph, rows, lane_rows, streams):
    m = rows * n_ph
    assert len(streams) in (1, rows)
    shift, scale = (jnp.concatenate([_mod_vec(mod_ref, s, idx) for s in streams], axis=0)
                    for idx in (0, 1))
    gain = (g1_ref[...] * (1.0 + scale))[:, None, :]
    shift = shift[:, None, :]
    x3 = x_ref[...]
    h = (x3 * _rms_scale(x3) * gain + shift).reshape(m, D_MODEL)
    zuv = jnp.dot(h.astype(BF16), w_ref[:, S5_WIDTH:], preferred_element_type=F32)
    zuv_ref[...] = zuv.reshape(rows, n_ph, 2 * GM_WIDTH).astype(BF16)

    n_lb = D_MODEL // LANES
    for lb in range(n_lb):
        hs_ref[lb] = h[:, lb * LANES:(lb + 1) * LANES]
    hp = jnp.concatenate(
        [jnp.concatenate([hs_ref[lb, pl.ds(ph, rows, stride=n_ph), :] for lb in range(n_lb)], axis=1)
         for ph in range(n_ph)], axis=0).astype(BF16)
    z5 = jnp.dot(hp, w_ref[:, :S5_WIDTH], preferred_element_type=F32)
    for ph in range(n_ph):
        zz = z5[ph * rows:(ph + 1) * rows]
        if lane_rows > rows:
            zz = jnp.concatenate([zz, jnp.zeros((lane_rows - rows, S5_WIDTH), F32)], axis=0)
        zt = zz.T.reshape(S5_GROUPS, S5_GROUP, lane_rows)
        zt_ref[:, ph, :, :] = zt.astype(BF16)


def _phase_blocks(n_rows, row_blk):
    if row_blk % LANES == 0:
        assert n_rows % row_blk == 0
        return n_rows // row_blk, row_blk, n_rows
    assert row_blk == n_rows
    lane_rows = -(-n_rows // LANES) * LANES
    return 1, lane_rows, lane_rows


def _inproj_call(x3, mod_all, streams, norm1_g, w_in_b, *, n_ph, row_blk):
    rows, t, _ = x3.shape
    n_rb, lane_blk, lane_rows = _phase_blocks(rows, row_blk)
    kern = functools.partial(_inproj_kernel, n_ph=n_ph, rows=row_blk, lane_rows=lane_blk,
                             streams=streams)
    return pl.pallas_call(
        kern,
        out_shape=(
            jax.ShapeDtypeStruct((S5_GROUPS, t, S5_GROUP, lane_rows), BF16),
            jax.ShapeDtypeStruct((rows, t, 2 * GM_WIDTH), BF16),
        ),
        grid=(n_rb, t // n_ph),
        in_specs=[
            pl.BlockSpec((row_blk, n_ph, D_MODEL), lambda i, j: (i, j, 0)),
            _const_spec(mod_all.shape),
            _const_spec((1, D_MODEL)),
            _const_spec((D_MODEL, IN_WIDTH)),
        ],
        out_specs=(
            pl.BlockSpec((S5_GROUPS, n_ph, S5_GROUP, lane_blk), lambda i, j: (0, j, 0, i)),
            pl.BlockSpec((row_blk, n_ph, 2 * GM_WIDTH), lambda i, j: (i, j, 0)),
        ),
        scratch_shapes=[pltpu.VMEM((D_MODEL // LANES, row_blk * n_ph, LANES), F32)],
        compiler_params=_cparams(2),
        name="inproj",
    )(x3, mod_all, norm1_g, w_in_b)


def _s5_kernel(*refs, n_sub, lanes, scan, final_rows, n_cast):
    refs, riders = _split_riders(refs, n_in=5 if scan else 6, n_out=2, n_cast=n_cast)
    if scan:
        zt_ref, wk_ref, qm_ref, tabr_ref, tabi_ref, yt_ref, sf_ref, ybuf, lbuf = refs
        s0_ref = None
    else:
        zt_ref, wk_ref, qm_ref, tabr_ref, tabi_ref, s0_ref, yt_ref, sf_ref, ybuf, lbuf = refs
    groups = range(zt_ref.shape[0])
    n = S5_STATE
    width = SUB * S5_GROUP

    widen = lambda tile: jnp.concatenate([tile] * (lanes // LANES), axis=1)
    sub_pows = []
    for g in groups:
        a1 = (widen(tabr_ref[g]), widen(tabi_ref[g]))
        pows = [a1]
        for _ in range(n_sub - 1):
            pows.append(_cmul(*a1, *pows[-1]))
        sub_pows.append(pows)

    local = []
    for g in groups:
        wk = wk_ref[g]
        lr = li = None
        for j in range(n_sub):
            u = zt_ref[g, j * width:(j + 1) * width, :]
            r = jnp.dot(wk, u, preferred_element_type=F32)
            ybuf[g, j * width:(j + 1) * width, :] = r[:width]
            wr = r[width:width + n]
            wi = r[width + n:]
            if j == 0:
                lr, li = wr, wi
            else:
                tr, ti = _cmul(*sub_pows[g][0], lr, li)
                lr, li = tr + wr, ti + wi
            lbuf[g, j, :n, :] = lr
            lbuf[g, j, n:, :] = li
        local.append((lr, li))

    if scan:
        lane = lax.broadcasted_iota(jnp.int32, (n, lanes), 1)
        xs = list(local)
        ms = [sub_pows[g][n_sub - 1] for g in groups]
        for i in range(int(math.log2(lanes))):
            sh = 1 << i
            for g in groups:
                xr, xi = xs[g]
                rr = jnp.where(lane >= sh, pltpu.roll(xr, sh, 1), 0.0)
                ri = jnp.where(lane >= sh, pltpu.roll(xi, sh, 1), 0.0)
                tr, ti = _cmul(*ms[g], rr, ri)
                xs[g] = (xr + tr, xi + ti)
                ms[g] = _cmul(*ms[g], *ms[g])
        entering = [(jnp.where(lane >= 1, pltpu.roll(xr, 1, 1), 0.0),
                     jnp.where(lane >= 1, pltpu.roll(xi, 1, 1), 0.0)) for xr, xi in xs]
    else:
        assert lanes == LANES
        entering = []
        for g in groups:
            s0 = s0_ref[:, g, :]
            s0t = jnp.concatenate([s0, jnp.zeros((LANES - s0.shape[0], 2 * n), F32)], axis=0).T
            entering.append((s0t[:n], s0t[n:]))

    keep = slice(lanes - LANES, lanes)
    for g in groups:
        qm = qm_ref[g]
        sr, si = entering[g]
        for j in range(n_sub):
            if j == 0:
                pr, pi = sr, si
            else:
                tr, ti = _cmul(*sub_pows[g][j - 1], sr, si)
                pr, pi = lbuf[g, j - 1, :n, :] + tr, lbuf[g, j - 1, n:, :] + ti
            sp = jnp.concatenate([pr, pi], axis=0).astype(BF16)
            y = ybuf[g, j * width:(j + 1) * width, :] + jnp.dot(qm, sp, preferred_element_type=F32)
            yt_ref[g, j * width:(j + 1) * width, :] = y.astype(BF16)
        tr, ti = _cmul(*sub_pows[g][n_sub - 1], sr, si)
        ends = jnp.concatenate([lbuf[g, n_sub - 1, :n, keep] + tr[:, keep],
                                lbuf[g, n_sub - 1, n:, keep] + ti[:, keep]], axis=0).T
        sf_ref[:, g, :] = ends[final_rows[0]:final_rows[1]]
    _run_riders(riders)


def _s5_call(zt, wk, qm, tabr, tabi, s0, *, n_streams, n_sub, scan, cast_riders=()):
    g, rows, lanes = zt.shape
    n = S5_STATE
    width = SUB * S5_GROUP
    assert lanes % LANES == 0 and lanes & (lanes - 1) == 0
    gps = S5_GROUPS_PER_STEP_LONG if scan else S5_GROUPS_PER_STEP_SHORT
    grp = lambda shape: pl.BlockSpec((gps,) + shape, lambda i: (i, 0, 0))
    per_stream = pl.BlockSpec((n_streams, gps, 2 * n), lambda i: (0, i, 0))
    in_specs = [grp((rows, lanes)), grp((width + 2 * n, width)), grp((width, 2 * n)),
                grp((n, LANES)), grp((n, LANES))]
    args = [zt, wk, qm, tabr, tabi]
    if scan:
        assert n_streams == 1
        final_rows = (LANES - 1, LANES)
    else:
        in_specs.append(per_stream)
        args.append(s0)
        final_rows = (0, n_streams)
    rider_specs, rider_shapes = _cast_rider_specs(cast_riders, (g // gps,))
    kern = functools.partial(_s5_kernel, n_sub=n_sub, lanes=lanes, scan=scan, final_rows=final_rows,
                             n_cast=len(cast_riders))
    return pl.pallas_call(
        kern,
        out_shape=(jax.ShapeDtypeStruct((g, rows, lanes), BF16),
                   jax.ShapeDtypeStruct((n_streams, g, 2 * n), F32), *rider_shapes),
        grid=(g // gps,),
        in_specs=in_specs + rider_specs,
        out_specs=(grp((rows, lanes)), per_stream, *rider_specs),
        scratch_shapes=[pltpu.VMEM((gps, rows, lanes), F32),
                        pltpu.VMEM((gps, n_sub, 2 * n, lanes), F32)],
        compiler_params=_cparams(1),
        name="s5",
    )(*args, *cast_riders)


def _glu_kernel(*refs, n_ph, rows, lane_rows, n_cast):
    (yt_ref, w_ref, b_ref, m_ref), riders = _split_riders(refs, n_in=3, n_out=1, n_cast=n_cast)
    _run_riders(riders)
    gs = []
    for ph in range(n_ph):
        yt = yt_ref[:, ph, :, :].astype(F32).reshape(S5_WIDTH, lane_rows)
        gs.append(_gelu(yt.T[:rows]))
    gy = jnp.concatenate(gs, axis=0)
    gate = jnp.dot(gy.astype(BF16), w_ref[...], preferred_element_type=F32) + b_ref[...]
    m = gy * _sigmoid(gate)
    for ph in range(n_ph):
        _oct_store(m_ref, ph, m[ph * rows:(ph + 1) * rows])


def _glu_call(yt4, w_glu_b, b_glu, *, rows, n_ph, row_blk, cast_riders=()):
    g, t, p, lane_rows = yt4.shape
    n_rb, lane_blk, lane_rows_expected = _phase_blocks(rows, row_blk)
    assert lane_rows == lane_rows_expected
    grid = (n_rb, t // n_ph)
    rider_specs, rider_shapes = _cast_rider_specs(cast_riders, grid)
    kern = functools.partial(_glu_kernel, n_ph=n_ph, rows=row_blk, lane_rows=lane_blk,
                             n_cast=len(cast_riders))
    oct_shape = _oct_shape(rows, t, S5_WIDTH)
    return pl.pallas_call(
        kern,
        out_shape=(jax.ShapeDtypeStruct(oct_shape, F32), *rider_shapes),
        grid=grid,
        in_specs=[
            pl.BlockSpec((g, n_ph, p, lane_blk), lambda i, j: (0, j, 0, i)),
            _const_spec((S5_WIDTH, S5_WIDTH)),
            _const_spec((1, S5_WIDTH)),
            *rider_specs,
        ],
        out_specs=(pl.BlockSpec((row_blk // OCT, oct_shape[1], n_ph, OCT, LANES),
                                lambda i, j: (i, 0, j, 0, 0)), *rider_specs),
        compiler_params=_cparams(2),
        name="glu",
    )(yt4, w_glu_b, b_glu, *cast_riders)


def _main_kernel(x_ref, m_ref, zuv_ref, mod_ref, g2_ref, gf_ref, lng_ref, lnb_ref, gw_ref, gbt_ref,
                 wo_ref, wgu_ref, wd_ref, y_ref, *rest, tm, cl, seq, streams):
    *maybe_v_out_ref, v_ref, ygm_ref, attn_ref, act_ref = rest
    t = tm // OCT
    hd = GM_HEAD_DIM

    attn_ref[...] = jnp.dot(_oct_load(m_ref, t).astype(BF16), wo_ref[:S5_WIDTH, :],
                            preferred_element_type=F32)

    gv = _gelu(zuv_ref[:, GM_WIDTH:].astype(F32))
    cen = gv - jnp.mean(gv, axis=-1, keepdims=True)
    var = jnp.mean(cen * cen, axis=-1, keepdims=True)
    v = cen * lax.rsqrt(var + EPS) * lng_ref[...] + lnb_ref[...]
    for v_out_ref in maybe_v_out_ref:
        v_out_ref[...] = v
    v_ref[...] = v.astype(BF16)

    blk_i = _div_pow2(lax.broadcasted_iota(jnp.int32, (cl, cl), 0), CHUNK)
    blk_j = _div_pow2(lax.broadcasted_iota(jnp.int32, (cl, cl), 1), CHUNK)
    causal = blk_j <= blk_i
    first_head = lax.broadcasted_iota(jnp.int32, (cl, 2 * hd), 1) < hd
    for pr in range(GM_HEADS // 2):
        h0, h1 = 2 * pr, 2 * pr + 1
        wm = jnp.concatenate(
            [jnp.where(causal, gw_ref[h, :cl, :cl], 0.0) for h in (h0, h1)], axis=1).astype(BF16)
        bias = jnp.where(first_head, gbt_ref[:cl, h0:h0 + 1], gbt_ref[:cl, h1:h1 + 1])
        cs = slice(h0 * hd, (h1 + 1) * hd)
        for ci in range(tm // cl):
            rs = slice(ci * cl, (ci + 1) * cl)
            vv = v_ref[rs, cs]
            zero = jnp.zeros_like(vv)
            rhs = jnp.concatenate([jnp.where(first_head, vv, zero), jnp.where(first_head, zero, vv)],
                                  axis=0)
            mixed = jnp.dot(wm, rhs, preferred_element_type=F32) + bias
            ygm_ref[rs, cs] = (_gelu(zuv_ref[rs, cs].astype(F32)) * mixed).astype(BF16)

    attn = attn_ref[...] + jnp.dot(ygm_ref[...], wo_ref[S5_WIDTH:, :], preferred_element_type=F32)

    def mod_rows(idx):
        if len(streams) == 1:
            return _mod_vec(mod_ref, streams[0], idx)
        return jnp.concatenate(
            [jnp.broadcast_to(_mod_vec(mod_ref, s, idx), (seq, D_MODEL)) for s in streams], axis=0)

    gate1, shift2, scale2, gate2 = mod_rows(2), mod_rows(3), mod_rows(4), mod_rows(5)
    x1 = x_ref[...] + gate1 * attn
    h2 = (x1 * _rms_scale(x1) * (g2_ref[...] * (1.0 + scale2)) + shift2).astype(BF16)

    assert D_FF % MXU_DIM == 0
    for lo in range(0, D_FF, MXU_DIM):
        hi = lo + MXU_DIM
        gg = jnp.dot(h2, wgu_ref[:, lo:hi], preferred_element_type=F32)
        up = jnp.dot(h2, wgu_ref[:, D_FF + lo:D_FF + hi], preferred_element_type=F32)
        act_ref[:, lo:hi] = (gg * jax.nn.sigmoid(gg) * up).astype(BF16)
    acc = jnp.dot(act_ref[...], wd_ref[...], preferred_element_type=F32)
    x2 = x1 + gate2 * acc
    y_ref[...] = x2 * _rms_scale(x2) * gf_ref[...]


def _main_sets_kernel(*refs, sets, n_shared):
    shared = refs[:n_shared]
    scratch = refs[-4:]
    step = pl.program_id(0)
    pos = n_shared
    out_pos = n_shared + 3 * len(sets)
    for st in sets:
        ins = refs[pos:pos + 3]
        pos += 3
        n_out = 2 if st["want_v"] else 1
        outs = refs[out_pos:out_pos + n_out]
        out_pos += n_out
        tm = st["tm"]

        @pl.when(jnp.logical_and(step >= st["start"], step < st["start"] + st["n_tiles"]))
        def _(ins=ins, outs=outs, st=st, tm=tm):
            _main_kernel(*ins, *shared, *outs, *[r.at[:tm] for r in scratch],
                         tm=tm, cl=st["cl"], seq=st["seq"], streams=st["streams"])


def _main_call(stream_sets, mod_all, norm2_g, final_g, ln_g, ln_b, gm_w, gm_bt, w_out_b, w_gu_b, w_down_b):
    shared = (mod_all, norm2_g, final_g, ln_g, ln_b, gm_w, gm_bt, w_out_b, w_gu_b, w_down_b)
    set_args, set_in_specs, out_shape, out_specs, statics = [], [], [], [], []
    start = 0
    for st in stream_sets:
        n_tok = st["x2d"].shape[0]
        n_oct, n_lb, t, _, _ = st["m"].shape
        tm = OCT * t
        seq, streams = st["seq"], st["streams"]
        assert n_oct * tm == n_tok
        assert len(streams) == 1 or (n_oct == 1 and len(streams) * seq == tm)

        def tile_index(i, start=start, n_oct=n_oct):
            return (jnp.clip(i - start, 0, n_oct - 1), 0)

        tok = lambda width: pl.BlockSpec((tm, width), tile_index)
        oct_rows = n_lb * t * OCT
        set_args += [st["x2d"], st["m"].reshape(n_oct * oct_rows, LANES), st["zuv"]]
        set_in_specs += [tok(D_MODEL), pl.BlockSpec((oct_rows, LANES), tile_index), tok(2 * GM_WIDTH)]
        out_shape.append(jax.ShapeDtypeStruct((n_tok, D_MODEL), F32))
        out_specs.append(tok(D_MODEL))
        if st["want_v"]:
            out_shape.append(jax.ShapeDtypeStruct((n_tok, GM_WIDTH), F32))
            out_specs.append(tok(GM_WIDTH))
        statics.append(dict(start=start, n_tiles=n_oct, tm=tm, cl=min(GM_CHUNK, seq), seq=seq,
                            streams=streams, want_v=st["want_v"]))
        start += n_oct
    tm_max = max(s["tm"] for s in statics)
    outs = pl.pallas_call(
        functools.partial(_main_sets_kernel, sets=statics, n_shared=len(shared)),
        out_shape=tuple(out_shape),
        grid=(start,),
        in_specs=[_const_spec(a.shape) for a in shared] + set_in_specs,
        out_specs=tuple(out_specs),
        scratch_shapes=[pltpu.VMEM((tm_max, GM_WIDTH), BF16), pltpu.VMEM((tm_max, GM_WIDTH), BF16),
                        pltpu.VMEM((tm_max, D_MODEL), F32), pltpu.VMEM((tm_max, D_FF), BF16)],
        compiler_params=_cparams(1),
        name="main",
    )(*shared, *set_args)
    results, pos = [], 0
    for s in statics:
        n_out = 2 if s["want_v"] else 1
        results.append(tuple(outs[pos:pos + n_out]))
        pos += n_out
    return results


def _mixer_front(x, mod_all, streams, s0, prm, *, t, n_ph, row_blk, want_v, f32_weights=None):
    b, seq, _ = x.shape
    n_chunks = seq // t
    rows = b * n_chunks
    scan = s0 is None
    assert (b == 1) if scan else (n_chunks == 1)
    n_sub = t // SUB

    zt4, zuv = _inproj_call(
        x.reshape(rows, t, D_MODEL), mod_all, streams, prm["norm1_g"], prm["w_in"],
        n_ph=n_ph, row_blk=row_blk)
    lane_rows = zt4.shape[-1]

    assert not scan or rows == lane_rows
    pending = dict(f32_weights or {})
    on_glu = {k: pending.pop(k) for k in ("w_gu",) if k in pending}
    yt, fin, *cast = _s5_call(zt4.reshape(S5_GROUPS, t * S5_GROUP, lane_rows),
                              prm["wk"], prm["qm"], prm["tabr"], prm["tabi"], s0,
                              n_streams=b, n_sub=n_sub, scan=scan, cast_riders=tuple(pending.values()))
    prm = {**prm, **dict(zip(pending.keys(), cast))}
    m, *cast = _glu_call(yt.reshape(S5_GROUPS, t, S5_GROUP, lane_rows), prm["w_glu"], prm["b_glu"],
                         rows=rows, n_ph=n_ph, row_blk=row_blk, cast_riders=tuple(on_glu.values()))
    prm = {**prm, **dict(zip(on_glu.keys(), cast))}

    n_tok = b * seq
    stream_set = dict(x2d=x.reshape(n_tok, D_MODEL), m=m, zuv=zuv.reshape(n_tok, 2 * GM_WIDTH),
                      streams=streams, seq=seq, want_v=want_v)

    return stream_set, fin[..., :S5_STATE], fin[..., S5_STATE:], prm


def kernel(x_prompt, x_sample, state_s5_re, state_s5_im, c_prompt, c_sample, norm1_g, norm2_g, w_ada, b_ada, w_in, s5_lambda_re, s5_lambda_im, s5_log_step, s5_b_re, s5_b_im, s5_c_re, s5_c_im, s5_d, s5_w_glu, s5_b_glu, gm_ln_g, gm_ln_b, gm_w_s, gm_b_s, w_out, ffn_w_gu, ffn_w_down, final_g):
    depth = w_in.shape[0]
    assert depth == 1
    l = 0
    n_p = c_prompt.shape[0]
    n_s = c_sample.shape[0]

    c_all = jnp.concatenate([c_prompt, c_sample], axis=0)
    c_pad = jnp.pad(c_all, ((0, -c_all.shape[0] % SUBLANES), (0, 0)))
    streams_p = tuple(range(n_p))
    streams_s = tuple(range(n_p, n_p + n_s))

    wk, qm, tabr, tabi, mod_all, w_in_b, w_out_b, w_down_b = _s5_prep_call(
        c_pad, w_ada[l], b_ada[l][None, :], s5_lambda_re[l], s5_lambda_im[l], s5_log_step[l][None, :],
        jnp.concatenate([jnp.swapaxes(s5_b_re[l], 1, 2), jnp.swapaxes(s5_b_im[l], 1, 2)], axis=-1),
        s5_c_re[l], s5_c_im[l], s5_d[l], cast_riders=(w_in[l], w_out[l], ffn_w_down[l]))

    prm = dict(
        norm1_g=norm1_g[l][None, :], norm2_g=norm2_g[l][None, :], final_g=final_g[None, :],
        w_in=w_in_b, w_out=w_out_b, w_down=w_down_b, ln_g=gm_ln_g[l][None, :], ln_b=gm_ln_b[l][None, :],
        wk=wk, qm=qm, tabr=tabr, tabi=tabi, b_glu=s5_b_glu[l][None, :],
        gm_w=gm_w_s[l], gm_bt=jnp.transpose(gm_b_s[l]),
    )
    later_weights = dict(w_glu=s5_w_glu[l], w_gu=ffn_w_gu[l])

    set_p, pre, pim, prm = _mixer_front(
        x_prompt, mod_all, streams_p, None, prm, f32_weights=later_weights,
        t=S5_LONG_SUBS * SUB, n_ph=SUBLANES, row_blk=LANES, want_v=False)
    s0 = jnp.concatenate([state_s5_re[l], state_s5_im[l]], axis=-1)
    n_b, seq_s, _ = x_sample.shape
    set_s, sre, sim, _ = _mixer_front(
        x_sample, mod_all, streams_s, s0, prm, t=seq_s, n_ph=seq_s, row_blk=n_b, want_v=True)

    (yp,), (ys, vs) = _main_call(
        [set_p, set_s], mod_all, prm["norm2_g"], prm["final_g"], prm["ln_g"], prm["ln_b"],
        prm["gm_w"], prm["gm_bt"], prm["w_out"], prm["w_gu"], prm["w_down"])
    return (yp.reshape(x_prompt.shape), ys.reshape(x_sample.shape), pre[None], pim[None],
            sre[None], sim[None], vs.reshape(n_b, seq_s, GM_WIDTH)[None])
```

```python
import functools
import math

import jax
import jax.numpy as jnp
from jax import lax
from jax.experimental import pallas as pl
from jax.experimental.pallas import tpu as pltpu

D_MODEL = 1024
S5_WIDTH = 512
S5_GROUP = 16
S5_GROUPS = 32
S5_STATE = 64
GM_WIDTH = 512
GM_CHUNK = 128
GM_HEADS = 8
GM_HEAD_DIM = 64
CHUNK = 64
IN_WIDTH = S5_WIDTH + 2 * GM_WIDTH
D_FF = 2816
EPS = 1e-6

LANES = 128
SUBLANES = 8
BF16_SUBLANES = 16
MXU_DIM = 256
VMEM_LIMIT_BYTES = 56 * 1024 * 1024

SUB = MXU_DIM // S5_GROUP
S5_LONG_SUBS = 4
PREP_GROUPS_PER_STEP = 8
S5_GROUPS_PER_STEP_LONG = 8
S5_GROUPS_PER_STEP_SHORT = 16

F32 = jnp.float32
BF16 = jnp.bfloat16


def _cparams(n_grid_axes):
    return pltpu.CompilerParams(
        dimension_semantics=("arbitrary",) * n_grid_axes,
        vmem_limit_bytes=VMEM_LIMIT_BYTES,
    )


def _const_spec(shape):
    nd = len(shape)
    return pl.BlockSpec(shape, lambda *_: (0,) * nd, pipeline_mode=pl.Buffered(1))


def _rms_scale(x):
    return lax.rsqrt(jnp.mean(x * x, axis=-1, keepdims=True) + EPS)


_GELU_C0 = math.sqrt(2.0 / math.pi)
_GELU_C1 = 0.044715 * _GELU_C0


def _gelu(x):
    hx = 0.5 * x
    return hx + hx * jnp.tanh(x * (_GELU_C0 + _GELU_C1 * (x * x)))


def _sigmoid(x):
    return 0.5 * jnp.tanh(0.5 * x) + 0.5


def _cmul(ar, ai, xr, xi):
    return ar * xr - ai * xi, ar * xi + ai * xr


def _div_pow2(idx, divisor):
    shift = divisor.bit_length() - 1
    assert divisor == 1 << shift
    return lax.shift_right_logical(idx, shift)


def _mod_pow2(idx, divisor):
    assert divisor & (divisor - 1) == 0
    return lax.bitwise_and(idx, divisor - 1)


def _place_rows(sel, x):
    hi = x.astype(BF16)
    rest = x - hi.astype(F32)
    mid = rest.astype(BF16)
    lo = (rest - mid.astype(F32)).astype(BF16)
    dot = functools.partial(jnp.dot, preferred_element_type=F32)
    return dot(sel, hi) + (dot(sel, mid) + dot(sel, lo))


def _dot_split(x, y):
    x_hi, y_hi = x.astype(BF16), y.astype(BF16)
    x_lo = (x - x_hi.astype(F32)).astype(BF16)
    y_lo = (y - y_hi.astype(F32)).astype(BF16)
    dot = functools.partial(jnp.dot, preferred_element_type=F32)
    return dot(x_hi, y_hi) + (dot(x_lo, y_hi) + dot(x_hi, y_lo))


OCT = SUBLANES


def _oct_shape(n_rows, t, width):
    assert n_rows % OCT == 0 and width % LANES == 0
    return (n_rows // OCT, width // LANES, t, OCT, LANES)


def _oct_store(ref, ph, val):
    n_oct, n_lb = ref.shape[0], ref.shape[1]
    for lb in range(n_lb):
        ref[:, lb, ph, :, :] = val[:, lb * LANES:(lb + 1) * LANES].reshape(n_oct, OCT, LANES)


def _oct_load(ref, t):
    n_lb = ref.shape[0] // (t * OCT)
    chunks = []
    for c in range(OCT):
        chunks.append(jnp.concatenate(
            [ref[pl.ds(lb * t * OCT + c, t, stride=OCT), :] for lb in range(n_lb)], axis=1))
    return jnp.concatenate(chunks, axis=0)


def _cast_rider_specs(arrays, grid):
    n_steps = math.prod(grid)

    def row_block(*idx):
        step = 0
        for i, extent in zip(idx, grid):
            step = step * extent + i
        return (step, 0)

    in_specs, out_specs, shapes, sources = [], [], [], []
    for a in arrays:
        a, part, n_parts = a if isinstance(a, tuple) else (a, 0, 1)
        rows, cols = a.shape[0], a.shape[1] // n_parts
        blk = rows // n_steps
        assert blk * n_steps == rows and blk % BF16_SUBLANES == 0
        assert cols * n_parts == a.shape[1] and cols % LANES == 0
        in_specs.append(pl.BlockSpec((blk, cols), lambda *idx, part=part: (row_block(*idx)[0], part)))
        out_specs.append(pl.BlockSpec((blk, cols), row_block))
        shapes.append(jax.ShapeDtypeStruct((rows, cols), BF16))
        sources.append(a)
    return in_specs, out_specs, shapes, sources


def _split_riders(refs, n_in, n_out, n_cast):
    ins, refs = refs[:n_in], refs[n_in:]
    cast_in, refs = refs[:n_cast], refs[n_cast:]
    outs, refs = refs[:n_out], refs[n_out:]
    cast_out, scratch = refs[:n_cast], refs[n_cast:]
    return (*ins, *outs, *scratch), list(zip(cast_in, cast_out))


def _run_riders(pairs):
    for src, dst in pairs:
        dst[...] = src[...].astype(BF16)


def _ada_step(c_ref, w_ref, b_ref, o_ref):
    @pl.when(pl.program_id(0) == 0)
    def _():
        o_ref[...] = jnp.broadcast_to(b_ref[...], o_ref.shape)

    c = c_ref[...]
    o_ref[...] += _dot_split(c * jax.nn.sigmoid(c), w_ref[...])


def _ada_specs(c_pad, w_ada, n_steps):
    rows = c_pad.shape[0]
    n_in, n_out = w_ada.shape
    bk = n_in // n_steps
    assert bk * n_steps == n_in and bk % LANES == 0
    in_specs = [
        pl.BlockSpec((rows, bk), lambda k: (0, k)),
        pl.BlockSpec((bk, n_out), lambda k: (k, 0)),
        pl.BlockSpec((1, n_out), lambda k: (0, 0)),
    ]
    return in_specs, pl.BlockSpec((rows, n_out), lambda k: (0, 0)), jax.ShapeDtypeStruct((rows, n_out), F32)


def _discretise(lr, li, ls):
    step = jnp.exp(ls)
    mag = jnp.exp(lr * step)
    ar = mag * jnp.cos(li * step)
    ai = mag * jnp.sin(li * step)
    den = lr * lr + li * li
    fr = ((ar - 1.0) * lr + ai * li) / den
    fi = (ai * lr - (ar - 1.0) * li) / den
    return ar, ai, fr, fi


def _selection(rows, cols, row_of_col):
    r = lax.broadcasted_iota(jnp.int32, (rows, cols), 0)
    c = lax.broadcasted_iota(jnp.int32, (rows, cols), 1)
    return jnp.where(r == row_of_col(c), 1.0, 0.0).astype(BF16)


def _place(x, sel):
    hi = x.astype(BF16)
    rest = x - hi.astype(F32)
    mid = rest.astype(BF16)
    lo = (rest - mid.astype(F32)).astype(BF16)
    dot = functools.partial(jnp.dot, preferred_element_type=F32)
    return dot(hi, sel) + (dot(mid, sel) + dot(lo, sel))


def _to_column(row):
    k = row.shape[1]
    r = lax.broadcasted_iota(jnp.int32, (k, k), 0)
    c = lax.broadcasted_iota(jnp.int32, (k, k), 1)
    return jnp.sum(jnp.where(r == c, jnp.broadcast_to(row, (k, k)), 0.0), axis=1, keepdims=True)


def _s5_prep_kernel(*refs, n_cast):
    refs, riders = _split_riders(refs, n_in=10, n_out=5, n_cast=n_cast)
    c_ref, w_ada_ref, b_ada_ref, lam_re_ref, lam_im_ref, ls_ref, d_ref, *grouped, mod_ref = refs
    n, sub, p = S5_STATE, SUB, S5_GROUP
    width = sub * p

    def col_source(c):
        return jnp.where(c < width, (sub - 1) - _div_pow2(c, p), jnp.where(c < 2 * width, sub + 1, sub))

    sels = dict(
        twice=_selection(n, 2 * n, lambda c: _mod_pow2(c, n)),
        tile_rows=_selection(p, width, lambda c: _mod_pow2(c, p)).T,
        cols=_selection(2 * n, 2 * width + LANES, col_source),
    )
    gps = grouped[0].shape[0]
    pending = [_s5_prep_group(pl.program_id(0) * gps + gi, sels, lam_re_ref, lam_im_ref, ls_ref, d_ref,
                              *[r.at[gi] for r in grouped]) for gi in range(gps)]
    while pending:
        pending = [gen for gen in pending if next(gen, "done") != "done"]
    _ada_step(c_ref, w_ada_ref, b_ada_ref, mod_ref)
    _run_riders(riders)


def _s5_prep_group(g, sels, lam_re_ref, lam_im_ref, ls_ref, d_ref, bt_ref, c_re_ref, c_im_ref,
                   wk_ref, qm_ref, tabr_ref, tabi_ref):
    n, sub, p = S5_STATE, SUB, S5_GROUP
    width = sub * p

    lr_row = lam_re_ref[pl.ds(g, 1), :]
    li_row = lam_im_ref[pl.ds(g, 1), :]
    ls_all = ls_ref[...]
    grp_lane = lax.broadcasted_iota(jnp.int32, ls_all.shape, 1)
    ls = jnp.sum(jnp.where(grp_lane == g, ls_all, 0.0), axis=1, keepdims=True)

    ar8, ai8, fr8, fi8 = _discretise(jnp.broadcast_to(lr_row, (SUBLANES, n)),
                                     jnp.broadcast_to(li_row, (SUBLANES, n)), ls)
    twice = lambda t8: _place(jnp.concatenate([t8] * (p // SUBLANES), axis=0), sels["twice"])
    a2r, a2i = twice(ar8), twice(ai8)
    first = lax.broadcasted_iota(jnp.int32, (p, 2 * n), 1) < n
    c2r = _place(c_re_ref[...], sels["twice"])
    c2i = _place(c_im_ref[...], sels["twice"])
    pr = jnp.ones_like(a2r)
    pi = jnp.zeros_like(a2r)
    ccat = []
    tbl_rows = SUBLANES * (-(-(sub + 2) // SUBLANES))
    tbl_row = lax.broadcasted_iota(jnp.int32, (tbl_rows, 2 * n), 0)
    as_tbl_row = lambda re2, im2: jnp.concatenate([jnp.where(first, re2, im2)] * (tbl_rows // p + 1),
                                                  axis=0)[:tbl_rows]
    tbl = jnp.zeros((tbl_rows, 2 * n), F32)
    for d in range(sub + 1):
        ccat.append(c2r * jnp.where(first, pr, -pi) + c2i * jnp.where(first, -pi, -pr))
        tbl = jnp.where(tbl_row == d, as_tbl_row(pr, pi), tbl)
        pr, pi = _cmul(a2r, a2i, pr, pi)
    tbl = jnp.where(tbl_row == sub + 1, as_tbl_row(twice(fr8), twice(fi8)), tbl)
    yield
    qm_ref[...] = jnp.concatenate(ccat[1:], axis=0).astype(BF16)
    rcat = jnp.concatenate(ccat[:sub], axis=0)

    tbl_t = jnp.concatenate([tbl, jnp.zeros((2 * n - tbl_rows, 2 * n), F32)], axis=0).T
    cols = _place(tbl_t, sels["cols"])
    yield
    apr, fr, a16r = cols[:n, :width], cols[:n, width:2 * width], cols[:n, 2 * width:]
    api, fi, a16i = cols[n:, :width], cols[n:, width:2 * width], cols[n:, 2 * width:]
    b_tiled = _place_rows(sels["tile_rows"], bt_ref[...]).T
    btr, bti = b_tiled[:n], b_tiled[n:]
    bbr, bbi = _cmul(fr, fi, btr, bti)
    pmr, pmi = _cmul(apr, api, bbr, bbi)
    wk_ref[width:width + n, :] = pmr.astype(BF16)
    wk_ref[width + n:, :] = pmi.astype(BF16)
    yield

    bbcat = jnp.concatenate([bbr, bbi], axis=0)
    kt = _dot_split(rcat, bbcat)
    yield
    d_col = _to_column(d_ref[pl.ds(g, 1), :])
    row_p = lax.broadcasted_iota(jnp.int32, (p, width), 0)
    lane_p = lax.broadcasted_iota(jnp.int32, (p, width), 1)
    d_diag = jnp.where(row_p == _mod_pow2(lane_p, p), d_col, 0.0)
    kt = jnp.concatenate([kt[:p] + d_diag, kt[p:]], axis=0)
    col_blk = _div_pow2(lax.broadcasted_iota(jnp.int32, (width, width), 1), p)
    m16 = jnp.zeros((width, width), F32)
    for k in range(sub):
        if k == 0:
            shifted = kt
        else:
            shifted = jnp.concatenate(
                [jnp.zeros((k * p, width), F32), kt[:width - k * p]], axis=0)
        m16 = jnp.where(col_blk == k, shifted, m16)
    wk_ref[:width, :] = m16.astype(BF16)

    tabr_ref[...] = a16r
    tabi_ref[...] = a16i


def _s5_prep_call(c_pad, w_ada, b_ada, lam_re, lam_im, log_step, bt, c_re, c_im, d, cast_riders=()):
    g, n, p, sub = S5_GROUPS, S5_STATE, S5_GROUP, SUB
    width = sub * p
    grp = lambda shape: pl.BlockSpec((PREP_GROUPS_PER_STEP,) + shape, lambda i: (i, 0, 0))
    whole = lambda a: pl.BlockSpec(a.shape, lambda i: (0,) * a.ndim)
    n_steps = g // PREP_GROUPS_PER_STEP
    rider_in_specs, rider_specs, rider_shapes, rider_args = _cast_rider_specs(cast_riders, (n_steps,))
    ada_in_specs, ada_out_spec, ada_shape = _ada_specs(c_pad, w_ada, n_steps)
    return pl.pallas_call(
        functools.partial(_s5_prep_kernel, n_cast=len(cast_riders)),
        out_shape=(
            jax.ShapeDtypeStruct((g, width + 2 * n, width), BF16),
            jax.ShapeDtypeStruct((g, width, 2 * n), BF16),
            jax.ShapeDtypeStruct((g, n, LANES), F32),
            jax.ShapeDtypeStruct((g, n, LANES), F32),
            ada_shape,
            *rider_shapes,
        ),
        grid=(n_steps,),
        in_specs=ada_in_specs + [whole(lam_re), whole(lam_im), whole(log_step), whole(d)]
        + [grp((p, 2 * n))] + [grp((p, n))] * 2 + rider_in_specs,
        out_specs=(grp((width + 2 * n, width)), grp((width, 2 * n)),
                   grp((n, LANES)), grp((n, LANES)), ada_out_spec, *rider_specs),
        compiler_params=_cparams(1),
        name="prep",
    )(c_pad, w_ada, b_ada, lam_re, lam_im, log_step, d, bt, c_re, c_im, *rider_args)


def _mod_vec(mod_ref, stream, idx):
    return mod_ref[stream:stream + 1, idx * D_MODEL:(idx + 1) * D_MODEL]


def _inproj_kernel(x_ref, mod_ref, g1_ref, w_ref, zt_ref, zuv_ref, hs_ref,
                   *, n_ph, rows, lane_rows, streams):
    m = rows * n_ph
    assert len(streams) in (1, rows)
    shift, scale = (jnp.concatenate([_mod_vec(mod_ref, s, idx) for s in streams], axis=0)
                    for idx in (0, 1))
    gain = (g1_ref[...] * (1.0 + scale))[:, None, :]
    shift = shift[:, None, :]
    x3 = x_ref[...]
    h = (x3 * _rms_scale(x3) * gain + shift).reshape(m, D_MODEL)
    zuv = jnp.dot(h.astype(BF16), w_ref[:, S5_WIDTH:], preferred_element_type=F32)
    zuv_ref[...] = zuv.reshape(rows, n_ph, 2 * GM_WIDTH).astype(BF16)

    n_lb = D_MODEL // LANES
    for lb in range(n_lb):
        hs_ref[lb] = h[:, lb * LANES:(lb + 1) * LANES]
    hp = jnp.concatenate(
        [jnp.concatenate([hs_ref[lb, pl.ds(ph, rows, stride=n_ph), :] for lb in range(n_lb)], axis=1)
         for ph in range(n_ph)], axis=0).astype(BF16)
    z5 = jnp.dot(hp, w_ref[:, :S5_WIDTH], preferred_element_type=F32)
    for ph in range(n_ph):
        zz = z5[ph * rows:(ph + 1) * rows]
        if lane_rows > rows:
            zz = jnp.concatenate([zz, jnp.zeros((lane_rows - rows, S5_WIDTH), F32)], axis=0)
        zt = zz.T.reshape(S5_GROUPS, S5_GROUP, lane_rows)
        zt_ref[:, ph, :, :] = zt.astype(BF16)


def _phase_blocks(n_rows, row_blk):
    if row_blk % LANES == 0:
        assert n_rows % row_blk == 0
        return n_rows // row_blk, row_blk, n_rows
    assert row_blk == n_rows
    lane_rows = -(-n_rows // LANES) * LANES
    return 1, lane_rows, lane_rows


def _inproj_call(x3, mod_all, streams, norm1_g, w_in_b, *, n_ph, row_blk):
    rows, t, _ = x3.shape
    n_rb, lane_blk, lane_rows = _phase_blocks(rows, row_blk)
    kern = functools.partial(_inproj_kernel, n_ph=n_ph, rows=row_blk, lane_rows=lane_blk,
                             streams=streams)
    return pl.pallas_call(
        kern,
        out_shape=(
            jax.ShapeDtypeStruct((S5_GROUPS, t, S5_GROUP, lane_rows), BF16),
            jax.ShapeDtypeStruct((rows, t, 2 * GM_WIDTH), BF16),
        ),
        grid=(n_rb, t // n_ph),
        in_specs=[
            pl.BlockSpec((row_blk, n_ph, D_MODEL), lambda i, j: (i, j, 0)),
            _const_spec(mod_all.shape),
            _const_spec((1, D_MODEL)),
            _const_spec((D_MODEL, IN_WIDTH)),
        ],
        out_specs=(
            pl.BlockSpec((S5_GROUPS, n_ph, S5_GROUP, lane_blk), lambda i, j: (0, j, 0, i)),
            pl.BlockSpec((row_blk, n_ph, 2 * GM_WIDTH), lambda i, j: (i, j, 0)),
        ),
        scratch_shapes=[pltpu.VMEM((D_MODEL // LANES, row_blk * n_ph, LANES), F32)],
        compiler_params=_cparams(2),
        name="inproj",
    )(x3, mod_all, norm1_g, w_in_b)


def _s5_kernel(*refs, n_sub, lanes, scan, final_rows, n_cast):
    refs, riders = _split_riders(refs, n_in=5 if scan else 6, n_out=2, n_cast=n_cast)
    if scan:
        zt_ref, wk_ref, qm_ref, tabr_ref, tabi_ref, yt_ref, sf_ref, ybuf, lbuf = refs
        s0_ref = None
    else:
        zt_ref, wk_ref, qm_ref, tabr_ref, tabi_ref, s0_ref, yt_ref, sf_ref, ybuf, lbuf = refs
    groups = range(zt_ref.shape[0])
    n = S5_STATE
    width = SUB * S5_GROUP

    widen = lambda tile: jnp.concatenate([tile] * (lanes // LANES), axis=1)
    sub_pows = []
    for g in groups:
        a1 = (widen(tabr_ref[g]), widen(tabi_ref[g]))
        pows = [a1]
        for _ in range(n_sub - 1):
            pows.append(_cmul(*a1, *pows[-1]))
        sub_pows.append(pows)

    local = []
    for g in groups:
        wk = wk_ref[g]
        lr = li = None
        for j in range(n_sub):
            u = zt_ref[g, j * width:(j + 1) * width, :]
            r = jnp.dot(wk, u, preferred_element_type=F32)
            ybuf[g, j * width:(j + 1) * width, :] = r[:width]
            wr = r[width:width + n]
            wi = r[width + n:]
            if j == 0:
                lr, li = wr, wi
            else:
                tr, ti = _cmul(*sub_pows[g][0], lr, li)
                lr, li = tr + wr, ti + wi
            lbuf[g, j, :n, :] = lr
            lbuf[g, j, n:, :] = li
        local.append((lr, li))

    if scan:
        lane = lax.broadcasted_iota(jnp.int32, (n, lanes), 1)
        xs = list(local)
        ms = [sub_pows[g][n_sub - 1] for g in groups]
        for i in range(int(math.log2(lanes))):
            sh = 1 << i
            for g in groups:
                xr, xi = xs[g]
                rr = jnp.where(lane >= sh, pltpu.roll(xr, sh, 1), 0.0)
                ri = jnp.where(lane >= sh, pltpu.roll(xi, sh, 1), 0.0)
                tr, ti = _cmul(*ms[g], rr, ri)
                xs[g] = (xr + tr, xi + ti)
                ms[g] = _cmul(*ms[g], *ms[g])
        entering = [(jnp.where(lane >= 1, pltpu.roll(xr, 1, 1), 0.0),
                     jnp.where(lane >= 1, pltpu.roll(xi, 1, 1), 0.0)) for xr, xi in xs]
    else:
        assert lanes == LANES
        entering = []
        for g in groups:
            s0 = s0_ref[:, g, :]
            s0t = jnp.concatenate([s0, jnp.zeros((LANES - s0.shape[0], 2 * n), F32)], axis=0).T
            entering.append((s0t[:n], s0t[n:]))

    keep = slice(lanes - LANES, lanes)
    for g in groups:
        qm = qm_ref[g]
        sr, si = entering[g]
        for j in range(n_sub):
            if j == 0:
                pr, pi = sr, si
            else:
                tr, ti = _cmul(*sub_pows[g][j - 1], sr, si)
                pr, pi = lbuf[g, j - 1, :n, :] + tr, lbuf[g, j - 1, n:, :] + ti
            sp = jnp.concatenate([pr, pi], axis=0).astype(BF16)
            y = ybuf[g, j * width:(j + 1) * width, :] + jnp.dot(qm, sp, preferred_element_type=F32)
            yt_ref[g, j * width:(j + 1) * width, :] = y.astype(BF16)
        tr, ti = _cmul(*sub_pows[g][n_sub - 1], sr, si)
        ends = jnp.concatenate([lbuf[g, n_sub - 1, :n, keep] + tr[:, keep],
                                lbuf[g, n_sub - 1, n:, keep] + ti[:, keep]], axis=0).T
        sf_ref[:, g, :] = ends[final_rows[0]:final_rows[1]]
    _run_riders(riders)


def _s5_call(zt, wk, qm, tabr, tabi, s0, *, n_streams, n_sub, scan, cast_riders=()):
    g, rows, lanes = zt.shape
    n = S5_STATE
    width = SUB * S5_GROUP
    assert lanes % LANES == 0 and lanes & (lanes - 1) == 0
    gps = S5_GROUPS_PER_STEP_LONG if scan else S5_GROUPS_PER_STEP_SHORT
    grp = lambda shape: pl.BlockSpec((gps,) + shape, lambda i: (i, 0, 0))
    per_stream = pl.BlockSpec((n_streams, gps, 2 * n), lambda i: (0, i, 0))
    in_specs = [grp((rows, lanes)), grp((width + 2 * n, width)), grp((width, 2 * n)),
                grp((n, LANES)), grp((n, LANES))]
    args = [zt, wk, qm, tabr, tabi]
    if scan:
        assert n_streams == 1
        final_rows = (LANES - 1, LANES)
    else:
        in_specs.append(per_stream)
        args.append(s0)
        final_rows = (0, n_streams)
    rider_in_specs, rider_specs, rider_shapes, rider_args = _cast_rider_specs(cast_riders, (g // gps,))
    kern = functools.partial(_s5_kernel, n_sub=n_sub, lanes=lanes, scan=scan, final_rows=final_rows,
                             n_cast=len(cast_riders))
    return pl.pallas_call(
        kern,
        out_shape=(jax.ShapeDtypeStruct((g, rows, lanes), BF16),
                   jax.ShapeDtypeStruct((n_streams, g, 2 * n), F32), *rider_shapes),
        grid=(g // gps,),
        in_specs=in_specs + rider_in_specs,
        out_specs=(grp((rows, lanes)), per_stream, *rider_specs),
        scratch_shapes=[pltpu.VMEM((gps, rows, lanes), F32),
                        pltpu.VMEM((gps, n_sub, 2 * n, lanes), F32)],
        compiler_params=_cparams(1),
        name="s5",
    )(*args, *rider_args)


def _glu_kernel(*refs, n_ph, rows, lane_rows, n_cast):
    (yt_ref, w_ref, b_ref, m_ref), riders = _split_riders(refs, n_in=3, n_out=1, n_cast=n_cast)
    _run_riders(riders)
    gs = []
    for ph in range(n_ph):
        yt = yt_ref[:, ph, :, :].astype(F32).reshape(S5_WIDTH, lane_rows)
        gs.append(_gelu(yt.T[:rows]))
    gy = jnp.concatenate(gs, axis=0)
    gate = jnp.dot(gy.astype(BF16), w_ref[...], preferred_element_type=F32) + b_ref[...]
    m = gy * _sigmoid(gate)
    for ph in range(n_ph):
        _oct_store(m_ref, ph, m[ph * rows:(ph + 1) * rows])


def _glu_call(yt4, w_glu_b, b_glu, *, rows, n_ph, row_blk, cast_riders=()):
    g, t, p, lane_rows = yt4.shape
    n_rb, lane_blk, lane_rows_expected = _phase_blocks(rows, row_blk)
    assert lane_rows == lane_rows_expected
    grid = (n_rb, t // n_ph)
    rider_in_specs, rider_specs, rider_shapes, rider_args = _cast_rider_specs(cast_riders, grid)
    kern = functools.partial(_glu_kernel, n_ph=n_ph, rows=row_blk, lane_rows=lane_blk,
                             n_cast=len(cast_riders))
    oct_shape = _oct_shape(rows, t, S5_WIDTH)
    return pl.pallas_call(
        kern,
        out_shape=(jax.ShapeDtypeStruct(oct_shape, F32), *rider_shapes),
        grid=grid,
        in_specs=[
            pl.BlockSpec((g, n_ph, p, lane_blk), lambda i, j: (0, j, 0, i)),
            _const_spec((S5_WIDTH, S5_WIDTH)),
            _const_spec((1, S5_WIDTH)),
            *rider_in_specs,
        ],
        out_specs=(pl.BlockSpec((row_blk // OCT, oct_shape[1], n_ph, OCT, LANES),
                                lambda i, j: (i, 0, j, 0, 0)), *rider_specs),
        compiler_params=_cparams(2),
        name="glu",
    )(yt4, w_glu_b, b_glu, *rider_args)


def _main_kernel(x_ref, m_ref, zuv_ref, mod_ref, g2_ref, gf_ref, lng_ref, lnb_ref, gw_ref, gbt_ref,
                 wo_ref, wg_ref, wu_ref, wd_ref, y_ref, *rest, tm, cl, seq, streams):
    *maybe_v_out_ref, v_ref, ygm_ref, attn_ref, act_ref = rest
    t = tm // OCT
    hd = GM_HEAD_DIM

    attn_ref[...] = jnp.dot(_oct_load(m_ref, t).astype(BF16), wo_ref[:S5_WIDTH, :],
                            preferred_element_type=F32)

    gv = _gelu(zuv_ref[:, GM_WIDTH:].astype(F32))
    cen = gv - jnp.mean(gv, axis=-1, keepdims=True)
    var = jnp.mean(cen * cen, axis=-1, keepdims=True)
    v = cen * lax.rsqrt(var + EPS) * lng_ref[...] + lnb_ref[...]
    for v_out_ref in maybe_v_out_ref:
        v_out_ref[...] = v
    v_ref[...] = v.astype(BF16)

    blk_i = _div_pow2(lax.broadcasted_iota(jnp.int32, (cl, cl), 0), CHUNK)
    blk_j = _div_pow2(lax.broadcasted_iota(jnp.int32, (cl, cl), 1), CHUNK)
    causal = blk_j <= blk_i
    first_head = lax.broadcasted_iota(jnp.int32, (cl, 2 * hd), 1) < hd
    for pr in range(GM_HEADS // 2):
        h0, h1 = 2 * pr, 2 * pr + 1
        wm = jnp.concatenate(
            [jnp.where(causal, gw_ref[h, :cl, :cl], 0.0) for h in (h0, h1)], axis=1).astype(BF16)
        bias = jnp.where(first_head, gbt_ref[:cl, h0:h0 + 1], gbt_ref[:cl, h1:h1 + 1])
        cs = slice(h0 * hd, (h1 + 1) * hd)
        for ci in range(tm // cl):
            rs = slice(ci * cl, (ci + 1) * cl)
            vv = v_ref[rs, cs]
            zero = jnp.zeros_like(vv)
            rhs = jnp.concatenate([jnp.where(first_head, vv, zero), jnp.where(first_head, zero, vv)],
                                  axis=0)
            mixed = jnp.dot(wm, rhs, preferred_element_type=F32) + bias
            ygm_ref[rs, cs] = (_gelu(zuv_ref[rs, cs].astype(F32)) * mixed).astype(BF16)

    attn = attn_ref[...] + jnp.dot(ygm_ref[...], wo_ref[S5_WIDTH:, :], preferred_element_type=F32)

    def mod_rows(idx):
        if len(streams) == 1:
            return _mod_vec(mod_ref, streams[0], idx)
        return jnp.concatenate(
            [jnp.broadcast_to(_mod_vec(mod_ref, s, idx), (seq, D_MODEL)) for s in streams], axis=0)

    gate1, shift2, scale2, gate2 = mod_rows(2), mod_rows(3), mod_rows(4), mod_rows(5)
    x1 = x_ref[...] + gate1 * attn
    h2 = (x1 * _rms_scale(x1) * (g2_ref[...] * (1.0 + scale2)) + shift2).astype(BF16)

    assert D_FF % MXU_DIM == 0
    for lo in range(0, D_FF, MXU_DIM):
        hi = lo + MXU_DIM
        gg = jnp.dot(h2, wg_ref[:, lo:hi], preferred_element_type=F32)
        up = jnp.dot(h2, wu_ref[:, lo:hi], preferred_element_type=F32)
        act_ref[:, lo:hi] = (gg * jax.nn.sigmoid(gg) * up).astype(BF16)
    acc = jnp.dot(act_ref[...], wd_ref[...], preferred_element_type=F32)
    x2 = x1 + gate2 * acc
    y_ref[...] = x2 * _rms_scale(x2) * gf_ref[...]


def _main_sets_kernel(*refs, sets, n_shared):
    shared = refs[:n_shared]
    scratch = refs[-4:]
    step = pl.program_id(0)
    pos = n_shared
    out_pos = n_shared + 3 * len(sets)
    for st in sets:
        ins = refs[pos:pos + 3]
        pos += 3
        n_out = 2 if st["want_v"] else 1
        outs = refs[out_pos:out_pos + n_out]
        out_pos += n_out
        tm = st["tm"]

        @pl.when(jnp.logical_and(step >= st["start"], step < st["start"] + st["n_tiles"]))
        def _(ins=ins, outs=outs, st=st, tm=tm):
            _main_kernel(*ins, *shared, *outs, *[r.at[:tm] for r in scratch],
                         tm=tm, cl=st["cl"], seq=st["seq"], streams=st["streams"])


def _main_call(stream_sets, mod_all, norm2_g, final_g, ln_g, ln_b, gm_w, gm_bt, w_out_b, w_gate_b, w_up_b,
               w_down_b):
    shared = (mod_all, norm2_g, final_g, ln_g, ln_b, gm_w, gm_bt, w_out_b, w_gate_b, w_up_b, w_down_b)
    set_args, set_in_specs, out_shape, out_specs, statics = [], [], [], [], []
    start = 0
    for st in stream_sets:
        n_tok = st["x2d"].shape[0]
        n_oct, n_lb, t, _, _ = st["m"].shape
        tm = OCT * t
        seq, streams = st["seq"], st["streams"]
        assert n_oct * tm == n_tok
        assert len(streams) == 1 or (n_oct == 1 and len(streams) * seq == tm)

        def tile_index(i, start=start, n_oct=n_oct):
            return (jnp.clip(i - start, 0, n_oct - 1), 0)

        tok = lambda width: pl.BlockSpec((tm, width), tile_index)
        oct_rows = n_lb * t * OCT
        set_args += [st["x2d"], st["m"].reshape(n_oct * oct_rows, LANES), st["zuv"]]
        set_in_specs += [tok(D_MODEL), pl.BlockSpec((oct_rows, LANES), tile_index), tok(2 * GM_WIDTH)]
        out_shape.append(jax.ShapeDtypeStruct((n_tok, D_MODEL), F32))
        out_specs.append(tok(D_MODEL))
        if st["want_v"]:
            out_shape.append(jax.ShapeDtypeStruct((n_tok, GM_WIDTH), F32))
            out_specs.append(tok(GM_WIDTH))
        statics.append(dict(start=start, n_tiles=n_oct, tm=tm, cl=min(GM_CHUNK, seq), seq=seq,
                            streams=streams, want_v=st["want_v"]))
        start += n_oct
    tm_max = max(s["tm"] for s in statics)
    outs = pl.pallas_call(
        functools.partial(_main_sets_kernel, sets=statics, n_shared=len(shared)),
        out_shape=tuple(out_shape),
        grid=(start,),
        in_specs=[_const_spec(a.shape) for a in shared] + set_in_specs,
        out_specs=tuple(out_specs),
        scratch_shapes=[pltpu.VMEM((tm_max, GM_WIDTH), BF16), pltpu.VMEM((tm_max, GM_WIDTH), BF16),
                        pltpu.VMEM((tm_max, D_MODEL), F32), pltpu.VMEM((tm_max, D_FF), BF16)],
        compiler_params=_cparams(1),
        name="main",
    )(*shared, *set_args)
    results, pos = [], 0
    for s in statics:
        n_out = 2 if s["want_v"] else 1
        results.append(tuple(outs[pos:pos + n_out]))
        pos += n_out
    return results


def _mixer_front(x, mod_all, streams, s0, prm, *, t, n_ph, row_blk, want_v, f32_weights=None):
    b, seq, _ = x.shape
    n_chunks = seq // t
    rows = b * n_chunks
    scan = s0 is None
    assert (b == 1) if scan else (n_chunks == 1)
    n_sub = t // SUB

    zt4, zuv = _inproj_call(
        x.reshape(rows, t, D_MODEL), mod_all, streams, prm["norm1_g"], prm["w_in"],
        n_ph=n_ph, row_blk=row_blk)
    lane_rows = zt4.shape[-1]

    assert not scan or rows == lane_rows
    pending = dict(f32_weights or {})
    on_glu = {k: pending.pop(k) for k in ("w_up",) if k in pending}
    yt, fin, *cast = _s5_call(zt4.reshape(S5_GROUPS, t * S5_GROUP, lane_rows),
                              prm["wk"], prm["qm"], prm["tabr"], prm["tabi"], s0,
                              n_streams=b, n_sub=n_sub, scan=scan, cast_riders=tuple(pending.values()))
    prm = {**prm, **dict(zip(pending.keys(), cast))}
    m, *cast = _glu_call(yt.reshape(S5_GROUPS, t, S5_GROUP, lane_rows), prm["w_glu"], prm["b_glu"],
                         rows=rows, n_ph=n_ph, row_blk=row_blk, cast_riders=tuple(on_glu.values()))
    prm = {**prm, **dict(zip(on_glu.keys(), cast))}

    n_tok = b * seq
    stream_set = dict(x2d=x.reshape(n_tok, D_MODEL), m=m, zuv=zuv.reshape(n_tok, 2 * GM_WIDTH),
                      streams=streams, seq=seq, want_v=want_v)

    return stream_set, fin[..., :S5_STATE], fin[..., S5_STATE:], prm


def kernel(x_prompt, x_sample, state_s5_re, state_s5_im, c_prompt, c_sample, norm1_g, norm2_g, w_ada, b_ada, w_in, s5_lambda_re, s5_lambda_im, s5_log_step, s5_b_re, s5_b_im, s5_c_re, s5_c_im, s5_d, s5_w_glu, s5_b_glu, gm_ln_g, gm_ln_b, gm_w_s, gm_b_s, w_out, ffn_w_gu, ffn_w_down, final_g):
    depth = w_in.shape[0]
    assert depth == 1
    l = 0
    n_p = c_prompt.shape[0]
    n_s = c_sample.shape[0]

    c_all = jnp.concatenate([c_prompt, c_sample], axis=0)
    c_pad = jnp.pad(c_all, ((0, -c_all.shape[0] % SUBLANES), (0, 0)))
    streams_p = tuple(range(n_p))
    streams_s = tuple(range(n_p, n_p + n_s))

    wk, qm, tabr, tabi, mod_all, w_in_b, w_out_b, w_down_b = _s5_prep_call(
        c_pad, w_ada[l], b_ada[l][None, :], s5_lambda_re[l], s5_lambda_im[l], s5_log_step[l][None, :],
        jnp.concatenate([jnp.swapaxes(s5_b_re[l], 1, 2), jnp.swapaxes(s5_b_im[l], 1, 2)], axis=-1),
        s5_c_re[l], s5_c_im[l], s5_d[l], cast_riders=(w_in[l], w_out[l], ffn_w_down[l]))

    prm = dict(
        norm1_g=norm1_g[l][None, :], norm2_g=norm2_g[l][None, :], final_g=final_g[None, :],
        w_in=w_in_b, w_out=w_out_b, w_down=w_down_b, ln_g=gm_ln_g[l][None, :], ln_b=gm_ln_b[l][None, :],
        wk=wk, qm=qm, tabr=tabr, tabi=tabi, b_glu=s5_b_glu[l][None, :],
        gm_w=gm_w_s[l], gm_bt=jnp.transpose(gm_b_s[l]),
    )
    later_weights = dict(w_glu=s5_w_glu[l], w_gate=(ffn_w_gu[l], 0, 2), w_up=(ffn_w_gu[l], 1, 2))

    set_p, pre, pim, prm = _mixer_front(
        x_prompt, mod_all, streams_p, None, prm, f32_weights=later_weights,
        t=S5_LONG_SUBS * SUB, n_ph=SUBLANES, row_blk=LANES, want_v=False)
    s0 = jnp.concatenate([state_s5_re[l], state_s5_im[l]], axis=-1)
    n_b, seq_s, _ = x_sample.shape
    set_s, sre, sim, _ = _mixer_front(
        x_sample, mod_all, streams_s, s0, prm, t=seq_s, n_ph=seq_s, row_blk=n_b, want_v=True)

    (yp,), (ys, vs) = _main_call(
        [set_p, set_s], mod_all, prm["norm2_g"], prm["final_g"], prm["ln_g"], prm["ln_b"],
        prm["gm_w"], prm["gm_bt"], prm["w_out"], prm["w_gate"], prm["w_up"], prm["w_down"])
    return (yp.reshape(x_prompt.shape), ys.reshape(x_sample.shape), pre[None], pim[None],
            sre[None], sim[None], vs.reshape(n_b, seq_s, GM_WIDTH)[None])
```

```python
import functools
import math

import jax
import jax.numpy as jnp
from jax import lax
from jax.experimental import pallas as pl
from jax.experimental.pallas import tpu as pltpu

D_MODEL = 1024
S5_WIDTH = 512
S5_GROUP = 16
S5_GROUPS = 32
S5_STATE = 64
GM_WIDTH = 512
GM_CHUNK = 128
GM_HEADS = 8
GM_HEAD_DIM = 64
CHUNK = 64
IN_WIDTH = S5_WIDTH + 2 * GM_WIDTH
D_FF = 2816
EPS = 1e-6

LANES = 128
SUBLANES = 8
BF16_SUBLANES = 16
MXU_DIM = 256
VMEM_LIMIT_BYTES = 56 * 1024 * 1024

SUB = MXU_DIM // S5_GROUP
S5_LONG_SUBS = 4
PREP_GROUPS_PER_STEP = 8
S5_GROUPS_PER_STEP_LONG = 8
S5_GROUPS_PER_STEP_SHORT = 8

F32 = jnp.float32
BF16 = jnp.bfloat16


def _cparams(n_grid_axes):
    return pltpu.CompilerParams(
        dimension_semantics=("arbitrary",) * n_grid_axes,
        vmem_limit_bytes=VMEM_LIMIT_BYTES,
    )


def _const_spec(shape):
    nd = len(shape)
    return pl.BlockSpec(shape, lambda *_: (0,) * nd, pipeline_mode=pl.Buffered(1))


def _rms_scale(x):
    return lax.rsqrt(jnp.mean(x * x, axis=-1, keepdims=True) + EPS)


_GELU_C0 = math.sqrt(2.0 / math.pi)
_GELU_C1 = 0.044715 * _GELU_C0


def _gelu(x):
    hx = 0.5 * x
    return hx + hx * jnp.tanh(x * (_GELU_C0 + _GELU_C1 * (x * x)))


def _sigmoid(x):
    return 0.5 * jnp.tanh(0.5 * x) + 0.5


def _cmul(ar, ai, xr, xi):
    return ar * xr - ai * xi, ar * xi + ai * xr


def _div_pow2(idx, divisor):
    shift = divisor.bit_length() - 1
    assert divisor == 1 << shift
    return lax.shift_right_logical(idx, shift)


def _mod_pow2(idx, divisor):
    assert divisor & (divisor - 1) == 0
    return lax.bitwise_and(idx, divisor - 1)


def _place_rows(sel, x):
    hi = x.astype(BF16)
    rest = x - hi.astype(F32)
    mid = rest.astype(BF16)
    lo = (rest - mid.astype(F32)).astype(BF16)
    dot = functools.partial(jnp.dot, preferred_element_type=F32)
    return dot(sel, hi) + (dot(sel, mid) + dot(sel, lo))


def _dot_split(x, y):
    x_hi, y_hi = x.astype(BF16), y.astype(BF16)
    x_lo = (x - x_hi.astype(F32)).astype(BF16)
    y_lo = (y - y_hi.astype(F32)).astype(BF16)
    dot = functools.partial(jnp.dot, preferred_element_type=F32)
    return dot(x_hi, y_hi) + (dot(x_lo, y_hi) + dot(x_hi, y_lo))


OCT = SUBLANES


def _oct_shape(n_rows, t, width):
    assert n_rows % OCT == 0 and width % LANES == 0 and t % 2 == 0
    return (n_rows // OCT, width // LANES, t // 2, 2 * OCT, LANES)


def _oct_store(ref, pair, val_even, val_odd):
    n_oct, n_lb = ref.shape[0], ref.shape[1]
    for lb in range(n_lb):
        lanes = slice(lb * LANES, (lb + 1) * LANES)
        both = jnp.concatenate([val_even[:, lanes].reshape(n_oct, OCT, LANES),
                                val_odd[:, lanes].reshape(n_oct, OCT, LANES)], axis=1)
        ref[:, lb, pair, :, :] = both.astype(BF16)


def _oct_load(ref, t):
    n_lb = ref.shape[0] // (t * OCT)
    chunks = []
    for c in range(OCT):
        chunks.append(jnp.concatenate(
            [ref[pl.ds(lb * t * OCT + c, t, stride=OCT), :] for lb in range(n_lb)], axis=1))
    return jnp.concatenate(chunks, axis=0)


def _cast_rider_specs(arrays, grid):
    n_steps = math.prod(grid)

    def row_block(*idx):
        step = 0
        for i, extent in zip(idx, grid):
            step = step * extent + i
        return (step, 0)

    in_specs, out_specs, shapes, sources = [], [], [], []
    for a in arrays:
        a, part, n_parts = a if isinstance(a, tuple) else (a, 0, 1)
        rows, cols = a.shape[0], a.shape[1] // n_parts
        blk = rows // n_steps
        assert blk * n_steps == rows and blk % BF16_SUBLANES == 0
        assert cols * n_parts == a.shape[1] and cols % LANES == 0
        in_specs.append(pl.BlockSpec((blk, cols), lambda *idx, part=part: (row_block(*idx)[0], part)))
        out_specs.append(pl.BlockSpec((blk, cols), row_block))
        shapes.append(jax.ShapeDtypeStruct((rows, cols), BF16))
        sources.append(a)
    return in_specs, out_specs, shapes, sources


def _split_riders(refs, n_in, n_out, n_cast):
    ins, refs = refs[:n_in], refs[n_in:]
    cast_in, refs = refs[:n_cast], refs[n_cast:]
    outs, refs = refs[:n_out], refs[n_out:]
    cast_out, scratch = refs[:n_cast], refs[n_cast:]
    return (*ins, *outs, *scratch), list(zip(cast_in, cast_out))


def _run_riders(pairs):
    for src, dst in pairs:
        dst[...] = src[...].astype(BF16)


def _ada_step(c_ref, w_ref, b_ref, o_ref):
    @pl.when(pl.program_id(0) == 0)
    def _():
        o_ref[...] = jnp.broadcast_to(b_ref[...], o_ref.shape)

    c = c_ref[...]
    o_ref[...] += _dot_split(c * jax.nn.sigmoid(c), w_ref[...])


def _ada_specs(c_pad, w_ada, n_steps):
    rows = c_pad.shape[0]
    n_in, n_out = w_ada.shape
    bk = n_in // n_steps
    assert bk * n_steps == n_in and bk % LANES == 0
    in_specs = [
        pl.BlockSpec((rows, bk), lambda k: (0, k)),
        pl.BlockSpec((bk, n_out), lambda k: (k, 0)),
        pl.BlockSpec((1, n_out), lambda k: (0, 0)),
    ]
    return in_specs, pl.BlockSpec((rows, n_out), lambda k: (0, 0)), jax.ShapeDtypeStruct((rows, n_out), F32)


def _discretise(lr, li, ls):
    step = jnp.exp(ls)
    mag = jnp.exp(lr * step)
    ar = mag * jnp.cos(li * step)
    ai = mag * jnp.sin(li * step)
    den = lr * lr + li * li
    fr = ((ar - 1.0) * lr + ai * li) / den
    fi = (ai * lr - (ar - 1.0) * li) / den
    return ar, ai, fr, fi


def _selection(rows, cols, row_of_col):
    r = lax.broadcasted_iota(jnp.int32, (rows, cols), 0)
    c = lax.broadcasted_iota(jnp.int32, (rows, cols), 1)
    return jnp.where(r == row_of_col(c), 1.0, 0.0).astype(BF16)


def _place(x, sel):
    hi = x.astype(BF16)
    rest = x - hi.astype(F32)
    mid = rest.astype(BF16)
    lo = (rest - mid.astype(F32)).astype(BF16)
    dot = functools.partial(jnp.dot, preferred_element_type=F32)
    return dot(hi, sel) + (dot(mid, sel) + dot(lo, sel))


def _to_column(row):
    k = row.shape[1]
    r = lax.broadcasted_iota(jnp.int32, (k, k), 0)
    c = lax.broadcasted_iota(jnp.int32, (k, k), 1)
    return jnp.sum(jnp.where(r == c, jnp.broadcast_to(row, (k, k)), 0.0), axis=1, keepdims=True)


def _s5_prep_kernel(*refs, n_cast):
    refs, riders = _split_riders(refs, n_in=10, n_out=5, n_cast=n_cast)
    c_ref, w_ada_ref, b_ada_ref, lam_re_ref, lam_im_ref, ls_ref, d_ref, *grouped, mod_ref = refs
    n, sub, p = S5_STATE, SUB, S5_GROUP
    width = sub * p

    def col_source(c):
        return jnp.where(c < width, (sub - 1) - _div_pow2(c, p), jnp.where(c < 2 * width, sub + 1, sub))

    sels = dict(
        twice=_selection(n, 2 * n, lambda c: _mod_pow2(c, n)),
        tile_rows=_selection(p, width, lambda c: _mod_pow2(c, p)).T,
        cols=_selection(2 * n, 2 * width + LANES, col_source),
    )
    gps = grouped[0].shape[0]
    pending = [_s5_prep_group(pl.program_id(0) * gps + gi, sels, lam_re_ref, lam_im_ref, ls_ref, d_ref,
                              *[r.at[gi] for r in grouped]) for gi in range(gps)]
    while pending:
        pending = [gen for gen in pending if next(gen, "done") != "done"]
    _ada_step(c_ref, w_ada_ref, b_ada_ref, mod_ref)
    _run_riders(riders)


def _s5_prep_group(g, sels, lam_re_ref, lam_im_ref, ls_ref, d_ref, bt_ref, c_re_ref, c_im_ref,
                   wk_ref, qm_ref, tabr_ref, tabi_ref):
    n, sub, p = S5_STATE, SUB, S5_GROUP
    width = sub * p

    lr_row = lam_re_ref[pl.ds(g, 1), :]
    li_row = lam_im_ref[pl.ds(g, 1), :]
    ls_all = ls_ref[...]
    grp_lane = lax.broadcasted_iota(jnp.int32, ls_all.shape, 1)
    ls = jnp.sum(jnp.where(grp_lane == g, ls_all, 0.0), axis=1, keepdims=True)

    ar8, ai8, fr8, fi8 = _discretise(jnp.broadcast_to(lr_row, (SUBLANES, n)),
                                     jnp.broadcast_to(li_row, (SUBLANES, n)), ls)
    twice = lambda t8: _place(jnp.concatenate([t8] * (p // SUBLANES), axis=0), sels["twice"])
    a2r, a2i = twice(ar8), twice(ai8)
    first = lax.broadcasted_iota(jnp.int32, (p, 2 * n), 1) < n
    c2r = _place(c_re_ref[...], sels["twice"])
    c2i = _place(c_im_ref[...], sels["twice"])
    pr = jnp.ones_like(a2r)
    pi = jnp.zeros_like(a2r)
    ccat = []
    tbl_rows = SUBLANES * (-(-(sub + 2) // SUBLANES))
    tbl_row = lax.broadcasted_iota(jnp.int32, (tbl_rows, 2 * n), 0)
    as_tbl_row = lambda re2, im2: jnp.concatenate([jnp.where(first, re2, im2)] * (tbl_rows // p + 1),
                                                  axis=0)[:tbl_rows]
    tbl = jnp.zeros((tbl_rows, 2 * n), F32)
    for d in range(sub + 1):
        ccat.append(c2r * jnp.where(first, pr, -pi) + c2i * jnp.where(first, -pi, -pr))
        tbl = jnp.where(tbl_row == d, as_tbl_row(pr, pi), tbl)
        pr, pi = _cmul(a2r, a2i, pr, pi)
    tbl = jnp.where(tbl_row == sub + 1, as_tbl_row(twice(fr8), twice(fi8)), tbl)
    yield
    qm_ref[...] = jnp.concatenate(ccat[1:], axis=0).astype(BF16)
    rcat = jnp.concatenate(ccat[:sub], axis=0)

    tbl_t = jnp.concatenate([tbl, jnp.zeros((2 * n - tbl_rows, 2 * n), F32)], axis=0).T
    cols = _place(tbl_t, sels["cols"])
    yield
    apr, fr, a16r = cols[:n, :width], cols[:n, width:2 * width], cols[:n, 2 * width:]
    api, fi, a16i = cols[n:, :width], cols[n:, width:2 * width], cols[n:, 2 * width:]
    b_tiled = _place_rows(sels["tile_rows"], bt_ref[...]).T
    btr, bti = b_tiled[:n], b_tiled[n:]
    bbr, bbi = _cmul(fr, fi, btr, bti)
    pmr, pmi = _cmul(apr, api, bbr, bbi)
    wk_ref[width:width + n, :] = pmr.astype(BF16)
    wk_ref[width + n:, :] = pmi.astype(BF16)
    yield

    bbcat = jnp.concatenate([bbr, bbi], axis=0)
    kt = _dot_split(rcat, bbcat)
    yield
    d_col = _to_column(d_ref[pl.ds(g, 1), :])
    row_p = lax.broadcasted_iota(jnp.int32, (p, width), 0)
    lane_p = lax.broadcasted_iota(jnp.int32, (p, width), 1)
    d_diag = jnp.where(row_p == _mod_pow2(lane_p, p), d_col, 0.0)
    kt = jnp.concatenate([kt[:p] + d_diag, kt[p:]], axis=0)
    col_blk = _div_pow2(lax.broadcasted_iota(jnp.int32, (width, width), 1), p)
    m16 = jnp.zeros((width, width), F32)
    for k in range(sub):
        if k == 0:
            shifted = kt
        else:
            shifted = jnp.concatenate(
                [jnp.zeros((k * p, width), F32), kt[:width - k * p]], axis=0)
        m16 = jnp.where(col_blk == k, shifted, m16)
    wk_ref[:width, :] = m16.astype(BF16)

    tabr_ref[...] = a16r
    tabi_ref[...] = a16i


def _s5_prep_call(c_pad, w_ada, b_ada, lam_re, lam_im, log_step, bt, c_re, c_im, d, cast_riders=()):
    g, n, p, sub = S5_GROUPS, S5_STATE, S5_GROUP, SUB
    width = sub * p
    grp = lambda shape: pl.BlockSpec((PREP_GROUPS_PER_STEP,) + shape, lambda i: (i, 0, 0))
    whole = lambda a: pl.BlockSpec(a.shape, lambda i: (0,) * a.ndim)
    n_steps = g // PREP_GROUPS_PER_STEP
    rider_in_specs, rider_specs, rider_shapes, rider_args = _cast_rider_specs(cast_riders, (n_steps,))
    ada_in_specs, ada_out_spec, ada_shape = _ada_specs(c_pad, w_ada, n_steps)
    return pl.pallas_call(
        functools.partial(_s5_prep_kernel, n_cast=len(cast_riders)),
        out_shape=(
            jax.ShapeDtypeStruct((g, width + 2 * n, width), BF16),
            jax.ShapeDtypeStruct((g, width, 2 * n), BF16),
            jax.ShapeDtypeStruct((g, n, LANES), F32),
            jax.ShapeDtypeStruct((g, n, LANES), F32),
            ada_shape,
            *rider_shapes,
        ),
        grid=(n_steps,),
        in_specs=ada_in_specs + [whole(lam_re), whole(lam_im), whole(log_step), whole(d)]
        + [grp((p, 2 * n))] + [grp((p, n))] * 2 + rider_in_specs,
        out_specs=(grp((width + 2 * n, width)), grp((width, 2 * n)),
                   grp((n, LANES)), grp((n, LANES)), ada_out_spec, *rider_specs),
        compiler_params=_cparams(1),
        name="prep",
    )(c_pad, w_ada, b_ada, lam_re, lam_im, log_step, d, bt, c_re, c_im, *rider_args)


def _mod_vec(mod_ref, stream, idx):
    return mod_ref[stream:stream + 1, idx * D_MODEL:(idx + 1) * D_MODEL]


def _inproj_kernel(x_ref, mod_ref, g1_ref, w_ref, zt_ref, zuv_ref, hs_ref,
                   *, n_ph, rows, lane_rows, streams):
    m = rows * n_ph
    assert len(streams) in (1, rows)
    shift, scale = (jnp.concatenate([_mod_vec(mod_ref, s, idx) for s in streams], axis=0)
                    for idx in (0, 1))
    gain = (g1_ref[...] * (1.0 + scale))[:, None, :]
    shift = shift[:, None, :]
    x3 = x_ref[...]
    h = (x3 * _rms_scale(x3) * gain + shift).reshape(m, D_MODEL)
    zuv = jnp.dot(h.astype(BF16), w_ref[:, S5_WIDTH:], preferred_element_type=F32)
    zuv_ref[...] = zuv.reshape(rows, n_ph, 2 * GM_WIDTH).astype(BF16)

    n_lb = D_MODEL // LANES
    for lb in range(n_lb):
        hs_ref[lb] = h[:, lb * LANES:(lb + 1) * LANES]
    hp = jnp.concatenate(
        [jnp.concatenate([hs_ref[lb, pl.ds(ph, rows, stride=n_ph), :] for lb in range(n_lb)], axis=1)
         for ph in range(n_ph)], axis=0).astype(BF16)
    z5 = jnp.dot(hp, w_ref[:, :S5_WIDTH], preferred_element_type=F32)
    for ph in range(n_ph):
        zz = z5[ph * rows:(ph + 1) * rows]
        if lane_rows > rows:
            zz = jnp.concatenate([zz, jnp.zeros((lane_rows - rows, S5_WIDTH), F32)], axis=0)
        zt = zz.T.reshape(S5_GROUPS, S5_GROUP, lane_rows)
        zt_ref[:, ph, :, :] = zt.astype(BF16)


def _phase_blocks(n_rows, row_blk):
    if row_blk % LANES == 0:
        assert n_rows % row_blk == 0
        return n_rows // row_blk, row_blk, n_rows
    assert row_blk == n_rows
    lane_rows = -(-n_rows // LANES) * LANES
    return 1, lane_rows, lane_rows


def _inproj_call(x3, mod_all, streams, norm1_g, w_in_b, *, n_ph, row_blk):
    rows, t, _ = x3.shape
    n_rb, lane_blk, lane_rows = _phase_blocks(rows, row_blk)
    kern = functools.partial(_inproj_kernel, n_ph=n_ph, rows=row_blk, lane_rows=lane_blk,
                             streams=streams)
    return pl.pallas_call(
        kern,
        out_shape=(
            jax.ShapeDtypeStruct((S5_GROUPS, t, S5_GROUP, lane_rows), BF16),
            jax.ShapeDtypeStruct((rows, t, 2 * GM_WIDTH), BF16),
        ),
        grid=(n_rb, t // n_ph),
        in_specs=[
            pl.BlockSpec((row_blk, n_ph, D_MODEL), lambda i, j: (i, j, 0)),
            _const_spec(mod_all.shape),
            _const_spec((1, D_MODEL)),
            _const_spec((D_MODEL, IN_WIDTH)),
        ],
        out_specs=(
            pl.BlockSpec((S5_GROUPS, n_ph, S5_GROUP, lane_blk), lambda i, j: (0, j, 0, i)),
            pl.BlockSpec((row_blk, n_ph, 2 * GM_WIDTH), lambda i, j: (i, j, 0)),
        ),
        scratch_shapes=[pltpu.VMEM((D_MODEL // LANES, row_blk * n_ph, LANES), F32)],
        compiler_params=_cparams(2),
        name="inproj",
    )(x3, mod_all, norm1_g, w_in_b)


def _s5_kernel(*refs, n_sub, lanes, scan, final_rows, n_cast):
    refs, riders = _split_riders(refs, n_in=5 if scan else 6, n_out=2, n_cast=n_cast)
    if scan:
        zt_ref, wk_ref, qm_ref, tabr_ref, tabi_ref, yt_ref, sf_ref, ybuf, lbuf = refs
        s0_ref = None
    else:
        zt_ref, wk_ref, qm_ref, tabr_ref, tabi_ref, s0_ref, yt_ref, sf_ref, ybuf, lbuf = refs
    groups = range(zt_ref.shape[0])
    n = S5_STATE
    width = SUB * S5_GROUP

    widen = lambda tile: jnp.concatenate([tile] * (lanes // LANES), axis=1)
    sub_pows = []
    for g in groups:
        a1 = (widen(tabr_ref[g]), widen(tabi_ref[g]))
        pows = [a1]
        for _ in range(n_sub - 1):
            pows.append(_cmul(*a1, *pows[-1]))
        sub_pows.append(pows)

    local = []
    for g in groups:
        wk = wk_ref[g]
        lr = li = None
        for j in range(n_sub):
            u = zt_ref[g, j * width:(j + 1) * width, :]
            r = jnp.dot(wk, u, preferred_element_type=F32)
            ybuf[g, j * width:(j + 1) * width, :] = r[:width]
            wr = r[width:width + n]
            wi = r[width + n:]
            if j == 0:
                lr, li = wr, wi
            else:
                tr, ti = _cmul(*sub_pows[g][0], lr, li)
                lr, li = tr + wr, ti + wi
            lbuf[g, j, :n, :] = lr
            lbuf[g, j, n:, :] = li
        local.append((lr, li))

    if scan:
        lane = lax.broadcasted_iota(jnp.int32, (n, lanes), 1)
        xs = list(local)
        ms = [sub_pows[g][n_sub - 1] for g in groups]
        for i in range(int(math.log2(lanes))):
            sh = 1 << i
            for g in groups:
                xr, xi = xs[g]
                rr = jnp.where(lane >= sh, pltpu.roll(xr, sh, 1), 0.0)
                ri = jnp.where(lane >= sh, pltpu.roll(xi, sh, 1), 0.0)
                tr, ti = _cmul(*ms[g], rr, ri)
                xs[g] = (xr + tr, xi + ti)
                ms[g] = _cmul(*ms[g], *ms[g])
        entering = [(jnp.where(lane >= 1, pltpu.roll(xr, 1, 1), 0.0),
                     jnp.where(lane >= 1, pltpu.roll(xi, 1, 1), 0.0)) for xr, xi in xs]
    else:
        assert lanes == LANES
        entering = []
        for g in groups:
            s0 = s0_ref[:, g, :]
            s0t = jnp.concatenate([s0, jnp.zeros((LANES - s0.shape[0], 2 * n), F32)], axis=0).T
            entering.append((s0t[:n], s0t[n:]))

    keep = slice(lanes - LANES, lanes)
    for g in groups:
        qm = qm_ref[g]
        sr, si = entering[g]
        for j in range(n_sub):
            if j == 0:
                pr, pi = sr, si
            else:
                tr, ti = _cmul(*sub_pows[g][j - 1], sr, si)
                pr, pi = lbuf[g, j - 1, :n, :] + tr, lbuf[g, j - 1, n:, :] + ti
            sp = jnp.concatenate([pr, pi], axis=0).astype(BF16)
            y = ybuf[g, j * width:(j + 1) * width, :] + jnp.dot(qm, sp, preferred_element_type=F32)
            yt_ref[g, j * width:(j + 1) * width, :] = y.astype(BF16)
        tr, ti = _cmul(*sub_pows[g][n_sub - 1], sr, si)
        ends = jnp.concatenate([lbuf[g, n_sub - 1, :n, keep] + tr[:, keep],
                                lbuf[g, n_sub - 1, n:, keep] + ti[:, keep]], axis=0).T
        sf_ref[:, g, :] = ends[final_rows[0]:final_rows[1]]
    _run_riders(riders)


def _s5_call(zt, wk, qm, tabr, tabi, s0, *, n_streams, n_sub, scan, cast_riders=()):
    g, rows, lanes = zt.shape
    n = S5_STATE
    width = SUB * S5_GROUP
    assert lanes % LANES == 0 and lanes & (lanes - 1) == 0
    gps = S5_GROUPS_PER_STEP_LONG if scan else S5_GROUPS_PER_STEP_SHORT
    grp = lambda shape: pl.BlockSpec((gps,) + shape, lambda i: (i, 0, 0))
    per_stream = pl.BlockSpec((n_streams, gps, 2 * n), lambda i: (0, i, 0))
    in_specs = [grp((rows, lanes)), grp((width + 2 * n, width)), grp((width, 2 * n)),
                grp((n, LANES)), grp((n, LANES))]
    args = [zt, wk, qm, tabr, tabi]
    if scan:
        assert n_streams == 1
        final_rows = (LANES - 1, LANES)
    else:
        in_specs.append(per_stream)
        args.append(s0)
        final_rows = (0, n_streams)
    rider_in_specs, rider_specs, rider_shapes, rider_args = _cast_rider_specs(cast_riders, (g // gps,))
    kern = functools.partial(_s5_kernel, n_sub=n_sub, lanes=lanes, scan=scan, final_rows=final_rows,
                             n_cast=len(cast_riders))
    return pl.pallas_call(
        kern,
        out_shape=(jax.ShapeDtypeStruct((g, rows, lanes), BF16),
                   jax.ShapeDtypeStruct((n_streams, g, 2 * n), F32), *rider_shapes),
        grid=(g // gps,),
        in_specs=in_specs + rider_in_specs,
        out_specs=(grp((rows, lanes)), per_stream, *rider_specs),
        scratch_shapes=[pltpu.VMEM((gps, rows, lanes), F32),
                        pltpu.VMEM((gps, n_sub, 2 * n, lanes), F32)],
        compiler_params=_cparams(1),
        name="s5",
    )(*args, *rider_args)


def _glu_kernel(*refs, n_ph, rows, lane_rows, n_cast):
    (yt_ref, w_ref, b_ref, m_ref), riders = _split_riders(refs, n_in=3, n_out=1, n_cast=n_cast)
    _run_riders(riders)
    gs = []
    for ph in range(n_ph):
        yt = yt_ref[:, ph, :, :].astype(F32).reshape(S5_WIDTH, lane_rows)
        gs.append(_gelu(yt.T[:rows]))
    gy = jnp.concatenate(gs, axis=0)
    gate = jnp.dot(gy.astype(BF16), w_ref[...], preferred_element_type=F32) + b_ref[...]
    m = gy * _sigmoid(gate)
    for pair in range(n_ph // 2):
        lo = 2 * pair * rows
        _oct_store(m_ref, pair, m[lo:lo + rows], m[lo + rows:lo + 2 * rows])


def _glu_call(yt4, w_glu_b, b_glu, *, rows, n_ph, row_blk, cast_riders=()):
    g, t, p, lane_rows = yt4.shape
    n_rb, lane_blk, lane_rows_expected = _phase_blocks(rows, row_blk)
    assert lane_rows == lane_rows_expected
    grid = (n_rb, t // n_ph)
    rider_in_specs, rider_specs, rider_shapes, rider_args = _cast_rider_specs(cast_riders, grid)
    kern = functools.partial(_glu_kernel, n_ph=n_ph, rows=row_blk, lane_rows=lane_blk,
                             n_cast=len(cast_riders))
    assert n_ph % 2 == 0
    oct_shape = _oct_shape(rows, t, S5_WIDTH)
    return pl.pallas_call(
        kern,
        out_shape=(jax.ShapeDtypeStruct(oct_shape, BF16), *rider_shapes),
        grid=grid,
        in_specs=[
            pl.BlockSpec((g, n_ph, p, lane_blk), lambda i, j: (0, j, 0, i)),
            _const_spec((S5_WIDTH, S5_WIDTH)),
            _const_spec((1, S5_WIDTH)),
            *rider_in_specs,
        ],
        out_specs=(pl.BlockSpec((row_blk // OCT, oct_shape[1], n_ph // 2, 2 * OCT, LANES),
                                lambda i, j: (i, 0, j, 0, 0)), *rider_specs),
        compiler_params=_cparams(2),
        name="glu",
    )(yt4, w_glu_b, b_glu, *rider_args)


def _main_kernel(x_ref, m_ref, zuv_ref, mod_ref, g2_ref, gf_ref, lng_ref, lnb_ref, gw_ref, gbt_ref,
                 wo_ref, wg_ref, wu_ref, wd_ref, y_ref, *rest, tm, cl, seq, streams):
    *maybe_v_out_ref, v_ref, ygm_ref, attn_ref, act_ref, stage_ref = rest
    t = tm // OCT
    hd = GM_HEAD_DIM

    stage_ref[...] = m_ref[...].astype(F32)
    attn_ref[...] = jnp.dot(_oct_load(stage_ref, t).astype(BF16), wo_ref[:S5_WIDTH, :],
                            preferred_element_type=F32)

    gv = _gelu(zuv_ref[:, GM_WIDTH:].astype(F32))
    cen = gv - jnp.mean(gv, axis=-1, keepdims=True)
    var = jnp.mean(cen * cen, axis=-1, keepdims=True)
    v = cen * lax.rsqrt(var + EPS) * lng_ref[...] + lnb_ref[...]
    for v_out_ref in maybe_v_out_ref:
        v_out_ref[...] = v
    v_ref[...] = v.astype(BF16)

    blk_i = _div_pow2(lax.broadcasted_iota(jnp.int32, (cl, cl), 0), CHUNK)
    blk_j = _div_pow2(lax.broadcasted_iota(jnp.int32, (cl, cl), 1), CHUNK)
    causal = blk_j <= blk_i
    first_head = lax.broadcasted_iota(jnp.int32, (cl, 2 * hd), 1) < hd
    for pr in range(GM_HEADS // 2):
        h0, h1 = 2 * pr, 2 * pr + 1
        wm = jnp.concatenate(
            [jnp.where(causal, gw_ref[h, :cl, :cl], 0.0) for h in (h0, h1)], axis=1).astype(BF16)
        bias = jnp.where(first_head, gbt_ref[:cl, h0:h0 + 1], gbt_ref[:cl, h1:h1 + 1])
        cs = slice(h0 * hd, (h1 + 1) * hd)
        for ci in range(tm // cl):
            rs = slice(ci * cl, (ci + 1) * cl)
            vv = v_ref[rs, cs]
            zero = jnp.zeros_like(vv)
            rhs = jnp.concatenate([jnp.where(first_head, vv, zero), jnp.where(first_head, zero, vv)],
                                  axis=0)
            mixed = jnp.dot(wm, rhs, preferred_element_type=F32) + bias
            ygm_ref[rs, cs] = (_gelu(zuv_ref[rs, cs].astype(F32)) * mixed).astype(BF16)

    attn = attn_ref[...] + jnp.dot(ygm_ref[...], wo_ref[S5_WIDTH:, :], preferred_element_type=F32)

    def mod_rows(idx):
        if len(streams) == 1:
            return _mod_vec(mod_ref, streams[0], idx)
        return jnp.concatenate(
            [jnp.broadcast_to(_mod_vec(mod_ref, s, idx), (seq, D_MODEL)) for s in streams], axis=0)

    gate1, shift2, scale2, gate2 = mod_rows(2), mod_rows(3), mod_rows(4), mod_rows(5)
    x1 = x_ref[...] + gate1 * attn
    h2 = (x1 * _rms_scale(x1) * (g2_ref[...] * (1.0 + scale2)) + shift2).astype(BF16)

    assert D_FF % MXU_DIM == 0
    for lo in range(0, D_FF, MXU_DIM):
        hi = lo + MXU_DIM
        gg = jnp.dot(h2, wg_ref[:, lo:hi], preferred_element_type=F32)
        up = jnp.dot(h2, wu_ref[:, lo:hi], preferred_element_type=F32)
        act_ref[:, lo:hi] = (gg * jax.nn.sigmoid(gg) * up).astype(BF16)
    acc = jnp.dot(act_ref[...], wd_ref[...], preferred_element_type=F32)
    x2 = x1 + gate2 * acc
    y_ref[...] = x2 * _rms_scale(x2) * gf_ref[...]


def _main_sets_kernel(*refs, sets, n_shared):
    shared = refs[:n_shared]
    *scratch, stage_ref = refs[-5:]
    step = pl.program_id(0)
    pos = n_shared
    out_pos = n_shared + 3 * len(sets)
    for st in sets:
        ins = refs[pos:pos + 3]
        pos += 3
        n_out = 2 if st["want_v"] else 1
        outs = refs[out_pos:out_pos + n_out]
        out_pos += n_out
        tm = st["tm"]

        @pl.when(jnp.logical_and(step >= st["start"], step < st["start"] + st["n_tiles"]))
        def _(ins=ins, outs=outs, st=st, tm=tm):
            _main_kernel(*ins, *shared, *outs, *[r.at[:tm] for r in scratch],
                         stage_ref.at[:ins[1].shape[0]],
                         tm=tm, cl=st["cl"], seq=st["seq"], streams=st["streams"])


def _main_call(stream_sets, mod_all, norm2_g, final_g, ln_g, ln_b, gm_w, gm_bt, w_out_b, w_gate_b, w_up_b,
               w_down_b):
    shared = (mod_all, norm2_g, final_g, ln_g, ln_b, gm_w, gm_bt, w_out_b, w_gate_b, w_up_b, w_down_b)
    set_args, set_in_specs, out_shape, out_specs, statics = [], [], [], [], []
    start = 0
    for st in stream_sets:
        n_tok = st["x2d"].shape[0]
        n_oct, n_lb, half_t, _, _ = st["m"].shape
        t = 2 * half_t
        tm = OCT * t
        seq, streams = st["seq"], st["streams"]
        assert n_oct * tm == n_tok
        assert len(streams) == 1 or (n_oct == 1 and len(streams) * seq == tm)

        def tile_index(i, start=start, n_oct=n_oct):
            return (jnp.clip(i - start, 0, n_oct - 1), 0)

        tok = lambda width: pl.BlockSpec((tm, width), tile_index)
        oct_rows = n_lb * t * OCT
        set_args += [st["x2d"], st["m"].reshape(n_oct * oct_rows, LANES), st["zuv"]]
        set_in_specs += [tok(D_MODEL), pl.BlockSpec((oct_rows, LANES), tile_index), tok(2 * GM_WIDTH)]
        out_shape.append(jax.ShapeDtypeStruct((n_tok, D_MODEL), F32))
        out_specs.append(tok(D_MODEL))
        if st["want_v"]:
            out_shape.append(jax.ShapeDtypeStruct((n_tok, GM_WIDTH), F32))
            out_specs.append(tok(GM_WIDTH))
        statics.append(dict(start=start, n_tiles=n_oct, tm=tm, cl=min(GM_CHUNK, seq), seq=seq,
                            streams=streams, want_v=st["want_v"]))
        start += n_oct
    tm_max = max(s["tm"] for s in statics)
    outs = pl.pallas_call(
        functools.partial(_main_sets_kernel, sets=statics, n_shared=len(shared)),
        out_shape=tuple(out_shape),
        grid=(start,),
        in_specs=[_const_spec(a.shape) for a in shared] + set_in_specs,
        out_specs=tuple(out_specs),
        scratch_shapes=[pltpu.VMEM((tm_max, GM_WIDTH), BF16), pltpu.VMEM((tm_max, GM_WIDTH), BF16),
                        pltpu.VMEM((tm_max, D_MODEL), F32), pltpu.VMEM((tm_max, D_FF), BF16),
                        pltpu.VMEM((tm_max * (S5_WIDTH // LANES), LANES), F32)],
        compiler_params=_cparams(1),
        name="main",
    )(*shared, *set_args)
    results, pos = [], 0
    for s in statics:
        n_out = 2 if s["want_v"] else 1
        results.append(tuple(outs[pos:pos + n_out]))
        pos += n_out
    return results


def _mixer_front(x, mod_all, streams, s0, prm, *, t, n_ph, row_blk, want_v, f32_weights=None,
                 glu_blocks=None):
    b, seq, _ = x.shape
    n_chunks = seq // t
    rows = b * n_chunks
    scan = s0 is None
    assert (b == 1) if scan else (n_chunks == 1)
    n_sub = t // SUB

    zt4, zuv = _inproj_call(
        x.reshape(rows, t, D_MODEL), mod_all, streams, prm["norm1_g"], prm["w_in"],
        n_ph=n_ph, row_blk=row_blk)
    lane_rows = zt4.shape[-1]

    assert not scan or rows == lane_rows
    pending = dict(f32_weights or {})
    on_glu = {k: pending.pop(k) for k in ("w_up",) if k in pending}
    yt, fin, *cast = _s5_call(zt4.reshape(S5_GROUPS, t * S5_GROUP, lane_rows),
                              prm["wk"], prm["qm"], prm["tabr"], prm["tabi"], s0,
                              n_streams=b, n_sub=n_sub, scan=scan, cast_riders=tuple(pending.values()))
    prm = {**prm, **dict(zip(pending.keys(), cast))}
    glu_n_ph, glu_row_blk = glu_blocks or (n_ph, row_blk)
    m, *cast = _glu_call(yt.reshape(S5_GROUPS, t, S5_GROUP, lane_rows), prm["w_glu"], prm["b_glu"],
                         rows=rows, n_ph=glu_n_ph, row_blk=glu_row_blk, cast_riders=tuple(on_glu.values()))
    prm = {**prm, **dict(zip(on_glu.keys(), cast))}

    n_tok = b * seq
    stream_set = dict(x2d=x.reshape(n_tok, D_MODEL), m=m, zuv=zuv.reshape(n_tok, 2 * GM_WIDTH),
                      streams=streams, seq=seq, want_v=want_v)

    return stream_set, fin[..., :S5_STATE], fin[..., S5_STATE:], prm


def kernel(x_prompt, x_sample, state_s5_re, state_s5_im, c_prompt, c_sample, norm1_g, norm2_g, w_ada, b_ada, w_in, s5_lambda_re, s5_lambda_im, s5_log_step, s5_b_re, s5_b_im, s5_c_re, s5_c_im, s5_d, s5_w_glu, s5_b_glu, gm_ln_g, gm_ln_b, gm_w_s, gm_b_s, w_out, ffn_w_gu, ffn_w_down, final_g):
    depth = w_in.shape[0]
    assert depth == 1
    l = 0
    n_p = c_prompt.shape[0]
    n_s = c_sample.shape[0]

    c_all = jnp.concatenate([c_prompt, c_sample], axis=0)
    c_pad = jnp.pad(c_all, ((0, -c_all.shape[0] % SUBLANES), (0, 0)))
    streams_p = tuple(range(n_p))
    streams_s = tuple(range(n_p, n_p + n_s))

    wk, qm, tabr, tabi, mod_all, w_in_b, w_out_b, w_down_b = _s5_prep_call(
        c_pad, w_ada[l], b_ada[l][None, :], s5_lambda_re[l], s5_lambda_im[l], s5_log_step[l][None, :],
        jnp.concatenate([jnp.swapaxes(s5_b_re[l], 1, 2), jnp.swapaxes(s5_b_im[l], 1, 2)], axis=-1),
        s5_c_re[l], s5_c_im[l], s5_d[l], cast_riders=(w_in[l], w_out[l], ffn_w_down[l]))

    prm = dict(
        norm1_g=norm1_g[l][None, :], norm2_g=norm2_g[l][None, :], final_g=final_g[None, :],
        w_in=w_in_b, w_out=w_out_b, w_down=w_down_b, ln_g=gm_ln_g[l][None, :], ln_b=gm_ln_b[l][None, :],
        wk=wk, qm=qm, tabr=tabr, tabi=tabi, b_glu=s5_b_glu[l][None, :],
        gm_w=gm_w_s[l], gm_bt=jnp.transpose(gm_b_s[l]),
    )
    later_weights = dict(w_glu=s5_w_glu[l], w_gate=(ffn_w_gu[l], 0, 2), w_up=(ffn_w_gu[l], 1, 2))

    set_p, pre, pim, prm = _mixer_front(
        x_prompt, mod_all, streams_p, None, prm, f32_weights=later_weights,
        t=S5_LONG_SUBS * SUB, n_ph=SUBLANES, row_blk=LANES, want_v=False,
        glu_blocks=(SUBLANES // 2, 2 * LANES))
    s0 = jnp.concatenate([state_s5_re[l], state_s5_im[l]], axis=-1)
    n_b, seq_s, _ = x_sample.shape
    set_s, sre, sim, _ = _mixer_front(
        x_sample, mod_all, streams_s, s0, prm, t=seq_s, n_ph=seq_s, row_blk=n_b, want_v=True)

    (yp,), (ys, vs) = _main_call(
        [set_p, set_s], mod_all, prm["norm2_g"], prm["final_g"], prm["ln_g"], prm["ln_b"],
        prm["gm_w"], prm["gm_bt"], prm["w_out"], prm["w_gate"], prm["w_up"], prm["w_down"])
    return (yp.reshape(x_prompt.shape), ys.reshape(x_sample.shape), pre[None], pim[None],
            sre[None], sim[None], vs.reshape(n_b, seq_s, GM_WIDTH)[None])
```

```python
import functools
import math

import jax
import jax.numpy as jnp
from jax import lax
from jax.experimental import pallas as pl
from jax.experimental.pallas import tpu as pltpu

D_MODEL = 1024
S5_WIDTH = 512
S5_GROUP = 16
S5_GROUPS = 32
S5_STATE = 64
GM_WIDTH = 512
GM_CHUNK = 128
GM_HEADS = 8
GM_HEAD_DIM = 64
CHUNK = 64
IN_WIDTH = S5_WIDTH + 2 * GM_WIDTH
D_FF = 2816
EPS = 1e-6

LANES = 128
SUBLANES = 8
BF16_SUBLANES = 16
MXU_DIM = 256
VMEM_LIMIT_BYTES = 56 * 1024 * 1024

SUB = MXU_DIM // S5_GROUP
S5_LONG_SUBS = 4
PREP_GROUPS_PER_STEP = 8
S5_GROUPS_PER_STEP_LONG = 8
S5_GROUPS_PER_STEP_SHORT = 8

F32 = jnp.float32
BF16 = jnp.bfloat16


def _cparams(n_grid_axes):
    return pltpu.CompilerParams(
        dimension_semantics=("arbitrary",) * n_grid_axes,
        vmem_limit_bytes=VMEM_LIMIT_BYTES,
    )


def _const_spec(shape):
    nd = len(shape)
    return pl.BlockSpec(shape, lambda *_: (0,) * nd, pipeline_mode=pl.Buffered(1))


def _rms_scale(x):
    return lax.rsqrt(jnp.mean(x * x, axis=-1, keepdims=True) + EPS)


_GELU_C0 = math.sqrt(2.0 / math.pi)
_GELU_C1 = 0.044715 * _GELU_C0


def _gelu(x):
    hx = 0.5 * x
    return hx + hx * jnp.tanh(x * (_GELU_C0 + _GELU_C1 * (x * x)))


def _sigmoid(x):
    return 0.5 * jnp.tanh(0.5 * x) + 0.5


def _cmul(ar, ai, xr, xi):
    return ar * xr - ai * xi, ar * xi + ai * xr


def _div_pow2(idx, divisor):
    shift = divisor.bit_length() - 1
    assert divisor == 1 << shift
    return lax.shift_right_logical(idx, shift)


def _mod_pow2(idx, divisor):
    assert divisor & (divisor - 1) == 0
    return lax.bitwise_and(idx, divisor - 1)


def _place_rows(sel, x):
    hi = x.astype(BF16)
    rest = x - hi.astype(F32)
    mid = rest.astype(BF16)
    lo = (rest - mid.astype(F32)).astype(BF16)
    dot = functools.partial(jnp.dot, preferred_element_type=F32)
    return dot(sel, hi) + (dot(sel, mid) + dot(sel, lo))


def _dot_split(x, y):
    x_hi, y_hi = x.astype(BF16), y.astype(BF16)
    x_lo = (x - x_hi.astype(F32)).astype(BF16)
    y_lo = (y - y_hi.astype(F32)).astype(BF16)
    dot = functools.partial(jnp.dot, preferred_element_type=F32)
    return dot(x_hi, y_hi) + (dot(x_lo, y_hi) + dot(x_hi, y_lo))


OCT = SUBLANES


def _oct_shape(n_rows, t, width):
    assert n_rows % OCT == 0 and width % LANES == 0 and t % 2 == 0
    return (n_rows // OCT, width // LANES, t // 2, 2 * OCT, LANES)


def _oct_store(ref, pair, val_even, val_odd):
    n_oct, n_lb = ref.shape[0], ref.shape[1]
    for lb in range(n_lb):
        lanes = slice(lb * LANES, (lb + 1) * LANES)
        both = jnp.concatenate([val_even[:, lanes].reshape(n_oct, OCT, LANES),
                                val_odd[:, lanes].reshape(n_oct, OCT, LANES)], axis=1)
        ref[:, lb, pair, :, :] = both.astype(BF16)


def _oct_load(ref, t):
    n_lb = ref.shape[0] // (t * OCT)
    chunks = []
    for c in range(OCT):
        chunks.append(jnp.concatenate(
            [ref[pl.ds(lb * t * OCT + c, t, stride=OCT), :] for lb in range(n_lb)], axis=1))
    return jnp.concatenate(chunks, axis=0)


def _cast_rider_specs(arrays, grid):
    n_steps = math.prod(grid)

    def row_block(*idx):
        step = 0
        for i, extent in zip(idx, grid):
            step = step * extent + i
        return (step, 0)

    in_specs, out_specs, shapes, sources = [], [], [], []
    for a in arrays:
        a, part, n_parts = a if isinstance(a, tuple) else (a, 0, 1)
        rows, cols = a.shape[0], a.shape[1] // n_parts
        blk = rows // n_steps
        assert blk * n_steps == rows and blk % BF16_SUBLANES == 0
        assert cols * n_parts == a.shape[1] and cols % LANES == 0
        in_specs.append(pl.BlockSpec((blk, cols), lambda *idx, part=part: (row_block(*idx)[0], part)))
        out_specs.append(pl.BlockSpec((blk, cols), row_block))
        shapes.append(jax.ShapeDtypeStruct((rows, cols), BF16))
        sources.append(a)
    return in_specs, out_specs, shapes, sources


def _split_riders(refs, n_in, n_out, n_cast):
    ins, refs = refs[:n_in], refs[n_in:]
    cast_in, refs = refs[:n_cast], refs[n_cast:]
    outs, refs = refs[:n_out], refs[n_out:]
    cast_out, scratch = refs[:n_cast], refs[n_cast:]
    return (*ins, *outs, *scratch), list(zip(cast_in, cast_out))


def _run_riders(pairs):
    for src, dst in pairs:
        dst[...] = src[...].astype(BF16)


def _ada_step(c_ref, w_ref, b_ref, o_ref):
    @pl.when(pl.program_id(0) == 0)
    def _():
        o_ref[...] = jnp.broadcast_to(b_ref[...], o_ref.shape)

    c = c_ref[...]
    o_ref[...] += _dot_split(c * jax.nn.sigmoid(c), w_ref[...])


def _ada_specs(c_pad, w_ada, n_steps):
    rows = c_pad.shape[0]
    n_in, n_out = w_ada.shape
    bk = n_in // n_steps
    assert bk * n_steps == n_in and bk % LANES == 0
    in_specs = [
        pl.BlockSpec((rows, bk), lambda k: (0, k)),
        pl.BlockSpec((bk, n_out), lambda k: (k, 0)),
        pl.BlockSpec((1, n_out), lambda k: (0, 0)),
    ]
    return in_specs, pl.BlockSpec((rows, n_out), lambda k: (0, 0)), jax.ShapeDtypeStruct((rows, n_out), F32)


def _discretise(lr, li, ls):
    step = jnp.exp(ls)
    mag = jnp.exp(lr * step)
    ar = mag * jnp.cos(li * step)
    ai = mag * jnp.sin(li * step)
    den = lr * lr + li * li
    fr = ((ar - 1.0) * lr + ai * li) / den
    fi = (ai * lr - (ar - 1.0) * li) / den
    return ar, ai, fr, fi


def _selection(rows, cols, row_of_col):
    r = lax.broadcasted_iota(jnp.int32, (rows, cols), 0)
    c = lax.broadcasted_iota(jnp.int32, (rows, cols), 1)
    return jnp.where(r == row_of_col(c), 1.0, 0.0).astype(BF16)


def _place(x, sel):
    hi = x.astype(BF16)
    rest = x - hi.astype(F32)
    mid = rest.astype(BF16)
    lo = (rest - mid.astype(F32)).astype(BF16)
    dot = functools.partial(jnp.dot, preferred_element_type=F32)
    return dot(hi, sel) + (dot(mid, sel) + dot(lo, sel))


def _to_column(row):
    k = row.shape[1]
    r = lax.broadcasted_iota(jnp.int32, (k, k), 0)
    c = lax.broadcasted_iota(jnp.int32, (k, k), 1)
    return jnp.sum(jnp.where(r == c, jnp.broadcast_to(row, (k, k)), 0.0), axis=1, keepdims=True)


def _s5_prep_kernel(*refs, n_cast):
    refs, riders = _split_riders(refs, n_in=10, n_out=5, n_cast=n_cast)
    c_ref, w_ada_ref, b_ada_ref, lam_re_ref, lam_im_ref, ls_ref, d_ref, *grouped, mod_ref = refs
    n, sub, p = S5_STATE, SUB, S5_GROUP
    width = sub * p

    def col_source(c):
        return jnp.where(c < width, (sub - 1) - _div_pow2(c, p), jnp.where(c < 2 * width, sub + 1, sub))

    sels = dict(
        twice=_selection(n, 2 * n, lambda c: _mod_pow2(c, n)),
        tile_rows=_selection(p, width, lambda c: _mod_pow2(c, p)).T,
        cols=_selection(2 * n, 2 * width + LANES, col_source),
    )
    gps = grouped[0].shape[0]
    pending = [_s5_prep_group(pl.program_id(0) * gps + gi, sels, lam_re_ref, lam_im_ref, ls_ref, d_ref,
                              *[r.at[gi] for r in grouped]) for gi in range(gps)]
    while pending:
        pending = [gen for gen in pending if next(gen, "done") != "done"]
    _ada_step(c_ref, w_ada_ref, b_ada_ref, mod_ref)
    _run_riders(riders)


def _s5_prep_group(g, sels, lam_re_ref, lam_im_ref, ls_ref, d_ref, bt_ref, c_re_ref, c_im_ref,
                   wk_ref, qm_ref, tabr_ref, tabi_ref):
    n, sub, p = S5_STATE, SUB, S5_GROUP
    width = sub * p

    lr_row = lam_re_ref[pl.ds(g, 1), :]
    li_row = lam_im_ref[pl.ds(g, 1), :]
    ls_all = ls_ref[...]
    grp_lane = lax.broadcasted_iota(jnp.int32, ls_all.shape, 1)
    ls = jnp.sum(jnp.where(grp_lane == g, ls_all, 0.0), axis=1, keepdims=True)

    ar8, ai8, fr8, fi8 = _discretise(jnp.broadcast_to(lr_row, (SUBLANES, n)),
                                     jnp.broadcast_to(li_row, (SUBLANES, n)), ls)
    twice = lambda t8: _place(jnp.concatenate([t8] * (p // SUBLANES), axis=0), sels["twice"])
    a2r, a2i = twice(ar8), twice(ai8)
    first = lax.broadcasted_iota(jnp.int32, (p, 2 * n), 1) < n
    c2r = _place(c_re_ref[...], sels["twice"])
    c2i = _place(c_im_ref[...], sels["twice"])
    pr = jnp.ones_like(a2r)
    pi = jnp.zeros_like(a2r)
    ccat = []
    tbl_rows = SUBLANES * (-(-(sub + 2) // SUBLANES))
    tbl_row = lax.broadcasted_iota(jnp.int32, (tbl_rows, 2 * n), 0)
    as_tbl_row = lambda re2, im2: jnp.concatenate([jnp.where(first, re2, im2)] * (tbl_rows // p + 1),
                                                  axis=0)[:tbl_rows]
    tbl = jnp.zeros((tbl_rows, 2 * n), F32)
    for d in range(sub + 1):
        ccat.append(c2r * jnp.where(first, pr, -pi) + c2i * jnp.where(first, -pi, -pr))
        tbl = jnp.where(tbl_row == d, as_tbl_row(pr, pi), tbl)
        pr, pi = _cmul(a2r, a2i, pr, pi)
    tbl = jnp.where(tbl_row == sub + 1, as_tbl_row(twice(fr8), twice(fi8)), tbl)
    yield
    qm_ref[...] = jnp.concatenate(ccat[1:], axis=0).astype(BF16)
    rcat = jnp.concatenate(ccat[:sub], axis=0)

    tbl_t = jnp.concatenate([tbl, jnp.zeros((2 * n - tbl_rows, 2 * n), F32)], axis=0).T
    cols = _place(tbl_t, sels["cols"])
    yield
    apr, fr, a16r = cols[:n, :width], cols[:n, width:2 * width], cols[:n, 2 * width:]
    api, fi, a16i = cols[n:, :width], cols[n:, width:2 * width], cols[n:, 2 * width:]
    b_tiled = _place_rows(sels["tile_rows"], bt_ref[...]).T
    btr, bti = b_tiled[:n], b_tiled[n:]
    bbr, bbi = _cmul(fr, fi, btr, bti)
    pmr, pmi = _cmul(apr, api, bbr, bbi)
    wk_ref[width:width + n, :] = pmr.astype(BF16)
    wk_ref[width + n:, :] = pmi.astype(BF16)
    yield

    bbcat = jnp.concatenate([bbr, bbi], axis=0)
    kt = _dot_split(rcat, bbcat)
    yield
    d_col = _to_column(d_ref[pl.ds(g, 1), :])
    row_p = lax.broadcasted_iota(jnp.int32, (p, width), 0)
    lane_p = lax.broadcasted_iota(jnp.int32, (p, width), 1)
    d_diag = jnp.where(row_p == _mod_pow2(lane_p, p), d_col, 0.0)
    kt = jnp.concatenate([kt[:p] + d_diag, kt[p:]], axis=0)
    col_blk = _div_pow2(lax.broadcasted_iota(jnp.int32, (width, width), 1), p)
    m16 = jnp.zeros((width, width), F32)
    for k in range(sub):
        if k == 0:
            shifted = kt
        else:
            shifted = jnp.concatenate(
                [jnp.zeros((k * p, width), F32), kt[:width - k * p]], axis=0)
        m16 = jnp.where(col_blk == k, shifted, m16)
    wk_ref[:width, :] = m16.astype(BF16)

    tabr_ref[...] = a16r
    tabi_ref[...] = a16i


def _s5_prep_call(c_pad, w_ada, b_ada, lam_re, lam_im, log_step, bt, c_re, c_im, d, cast_riders=()):
    g, n, p, sub = S5_GROUPS, S5_STATE, S5_GROUP, SUB
    width = sub * p
    grp = lambda shape: pl.BlockSpec((PREP_GROUPS_PER_STEP,) + shape, lambda i: (i, 0, 0))
    whole = lambda a: pl.BlockSpec(a.shape, lambda i: (0,) * a.ndim)
    n_steps = g // PREP_GROUPS_PER_STEP
    rider_in_specs, rider_specs, rider_shapes, rider_args = _cast_rider_specs(cast_riders, (n_steps,))
    ada_in_specs, ada_out_spec, ada_shape = _ada_specs(c_pad, w_ada, n_steps)
    return pl.pallas_call(
        functools.partial(_s5_prep_kernel, n_cast=len(cast_riders)),
        out_shape=(
            jax.ShapeDtypeStruct((g, width + 2 * n, width), BF16),
            jax.ShapeDtypeStruct((g, width, 2 * n), BF16),
            jax.ShapeDtypeStruct((g, n, LANES), F32),
            jax.ShapeDtypeStruct((g, n, LANES), F32),
            ada_shape,
            *rider_shapes,
        ),
        grid=(n_steps,),
        in_specs=ada_in_specs + [whole(lam_re), whole(lam_im), whole(log_step), whole(d)]
        + [grp((p, 2 * n))] + [grp((p, n))] * 2 + rider_in_specs,
        out_specs=(grp((width + 2 * n, width)), grp((width, 2 * n)),
                   grp((n, LANES)), grp((n, LANES)), ada_out_spec, *rider_specs),
        compiler_params=_cparams(1),
        name="prep",
    )(c_pad, w_ada, b_ada, lam_re, lam_im, log_step, d, bt, c_re, c_im, *rider_args)


def _mod_vec(mod_ref, stream, idx):
    return mod_ref[stream:stream + 1, idx * D_MODEL:(idx + 1) * D_MODEL]


def _inproj_kernel(x_ref, mod_ref, g1_ref, w_ref, zt_ref, zuv_ref, hs_ref,
                   *, n_ph, rows, lane_rows, streams):
    m = rows * n_ph
    assert len(streams) in (1, rows)
    shift, scale = (jnp.concatenate([_mod_vec(mod_ref, s, idx) for s in streams], axis=0)
                    for idx in (0, 1))
    gain = (g1_ref[...] * (1.0 + scale))[:, None, :]
    shift = shift[:, None, :]
    x3 = x_ref[...]
    h = (x3 * _rms_scale(x3) * gain + shift).reshape(m, D_MODEL)
    zuv = jnp.dot(h.astype(BF16), w_ref[:, S5_WIDTH:], preferred_element_type=F32)
    zuv_ref[...] = _gelu(zuv).reshape(rows, n_ph, 2 * GM_WIDTH).astype(BF16)

    n_lb = D_MODEL // LANES
    for lb in range(n_lb):
        hs_ref[lb] = h[:, lb * LANES:(lb + 1) * LANES]
    hp = jnp.concatenate(
        [jnp.concatenate([hs_ref[lb, pl.ds(ph, rows, stride=n_ph), :] for lb in range(n_lb)], axis=1)
         for ph in range(n_ph)], axis=0).astype(BF16)
    z5 = jnp.dot(hp, w_ref[:, :S5_WIDTH], preferred_element_type=F32)
    for ph in range(n_ph):
        zz = z5[ph * rows:(ph + 1) * rows]
        if lane_rows > rows:
            zz = jnp.concatenate([zz, jnp.zeros((lane_rows - rows, S5_WIDTH), F32)], axis=0)
        zt = zz.T.reshape(S5_GROUPS, S5_GROUP, lane_rows)
        zt_ref[:, ph, :, :] = zt.astype(BF16)


def _phase_blocks(n_rows, row_blk):
    if row_blk % LANES == 0:
        assert n_rows % row_blk == 0
        return n_rows // row_blk, row_blk, n_rows
    assert row_blk == n_rows
    lane_rows = -(-n_rows // LANES) * LANES
    return 1, lane_rows, lane_rows


def _inproj_call(x3, mod_all, streams, norm1_g, w_in_b, *, n_ph, row_blk):
    rows, t, _ = x3.shape
    n_rb, lane_blk, lane_rows = _phase_blocks(rows, row_blk)
    kern = functools.partial(_inproj_kernel, n_ph=n_ph, rows=row_blk, lane_rows=lane_blk,
                             streams=streams)
    return pl.pallas_call(
        kern,
        out_shape=(
            jax.ShapeDtypeStruct((S5_GROUPS, t, S5_GROUP, lane_rows), BF16),
            jax.ShapeDtypeStruct((rows, t, 2 * GM_WIDTH), BF16),
        ),
        grid=(n_rb, t // n_ph),
        in_specs=[
            pl.BlockSpec((row_blk, n_ph, D_MODEL), lambda i, j: (i, j, 0)),
            _const_spec(mod_all.shape),
            _const_spec((1, D_MODEL)),
            _const_spec((D_MODEL, IN_WIDTH)),
        ],
        out_specs=(
            pl.BlockSpec((S5_GROUPS, n_ph, S5_GROUP, lane_blk), lambda i, j: (0, j, 0, i)),
            pl.BlockSpec((row_blk, n_ph, 2 * GM_WIDTH), lambda i, j: (i, j, 0)),
        ),
        scratch_shapes=[pltpu.VMEM((D_MODEL // LANES, row_blk * n_ph, LANES), F32)],
        compiler_params=_cparams(2),
        name="inproj",
    )(x3, mod_all, norm1_g, w_in_b)


def _s5_kernel(*refs, n_sub, lanes, scan, final_rows, n_cast):
    refs, riders = _split_riders(refs, n_in=5 if scan else 6, n_out=2, n_cast=n_cast)
    if scan:
        zt_ref, wk_ref, qm_ref, tabr_ref, tabi_ref, yt_ref, sf_ref, ybuf, lbuf = refs
        s0_ref = None
    else:
        zt_ref, wk_ref, qm_ref, tabr_ref, tabi_ref, s0_ref, yt_ref, sf_ref, ybuf, lbuf = refs
    groups = range(zt_ref.shape[0])
    n = S5_STATE
    width = SUB * S5_GROUP

    widen = lambda tile: jnp.concatenate([tile] * (lanes // LANES), axis=1)
    sub_pows = []
    for g in groups:
        a1 = (widen(tabr_ref[g]), widen(tabi_ref[g]))
        pows = [a1]
        for _ in range(n_sub - 1):
            pows.append(_cmul(*a1, *pows[-1]))
        sub_pows.append(pows)

    local = []
    for g in groups:
        wk = wk_ref[g]
        lr = li = None
        for j in range(n_sub):
            u = zt_ref[g, j * width:(j + 1) * width, :]
            r = jnp.dot(wk, u, preferred_element_type=F32)
            ybuf[g, j * width:(j + 1) * width, :] = r[:width]
            wr = r[width:width + n]
            wi = r[width + n:]
            if j == 0:
                lr, li = wr, wi
            else:
                tr, ti = _cmul(*sub_pows[g][0], lr, li)
                lr, li = tr + wr, ti + wi
            lbuf[g, j, :n, :] = lr
            lbuf[g, j, n:, :] = li
        local.append((lr, li))

    if scan:
        lane = lax.broadcasted_iota(jnp.int32, (n, lanes), 1)
        xs = list(local)
        ms = [sub_pows[g][n_sub - 1] for g in groups]
        for i in range(int(math.log2(lanes))):
            sh = 1 << i
            for g in groups:
                xr, xi = xs[g]
                rr = jnp.where(lane >= sh, pltpu.roll(xr, sh, 1), 0.0)
                ri = jnp.where(lane >= sh, pltpu.roll(xi, sh, 1), 0.0)
                tr, ti = _cmul(*ms[g], rr, ri)
                xs[g] = (xr + tr, xi + ti)
                ms[g] = _cmul(*ms[g], *ms[g])
        entering = [(jnp.where(lane >= 1, pltpu.roll(xr, 1, 1), 0.0),
                     jnp.where(lane >= 1, pltpu.roll(xi, 1, 1), 0.0)) for xr, xi in xs]
    else:
        assert lanes == LANES
        entering = []
        for g in groups:
            s0 = s0_ref[:, g, :]
            s0t = jnp.concatenate([s0, jnp.zeros((LANES - s0.shape[0], 2 * n), F32)], axis=0).T
            entering.append((s0t[:n], s0t[n:]))

    keep = slice(lanes - LANES, lanes)
    for g in groups:
        qm = qm_ref[g]
        sr, si = entering[g]
        for j in range(n_sub):
            if j == 0:
                pr, pi = sr, si
            else:
                tr, ti = _cmul(*sub_pows[g][j - 1], sr, si)
                pr, pi = lbuf[g, j - 1, :n, :] + tr, lbuf[g, j - 1, n:, :] + ti
            sp = jnp.concatenate([pr, pi], axis=0).astype(BF16)
            y = ybuf[g, j * width:(j + 1) * width, :] + jnp.dot(qm, sp, preferred_element_type=F32)
            yt_ref[g, j * width:(j + 1) * width, :] = y.astype(BF16)
        tr, ti = _cmul(*sub_pows[g][n_sub - 1], sr, si)
        ends = jnp.concatenate([lbuf[g, n_sub - 1, :n, keep] + tr[:, keep],
                                lbuf[g, n_sub - 1, n:, keep] + ti[:, keep]], axis=0).T
        sf_ref[:, g, :] = ends[final_rows[0]:final_rows[1]]
    _run_riders(riders)


def _s5_call(zt, wk, qm, tabr, tabi, s0, *, n_streams, n_sub, scan, cast_riders=()):
    g, rows, lanes = zt.shape
    n = S5_STATE
    width = SUB * S5_GROUP
    assert lanes % LANES == 0 and lanes & (lanes - 1) == 0
    gps = S5_GROUPS_PER_STEP_LONG if scan else S5_GROUPS_PER_STEP_SHORT
    grp = lambda shape: pl.BlockSpec((gps,) + shape, lambda i: (i, 0, 0))
    per_stream = pl.BlockSpec((n_streams, gps, 2 * n), lambda i: (0, i, 0))
    in_specs = [grp((rows, lanes)), grp((width + 2 * n, width)), grp((width, 2 * n)),
                grp((n, LANES)), grp((n, LANES))]
    args = [zt, wk, qm, tabr, tabi]
    if scan:
        assert n_streams == 1
        final_rows = (LANES - 1, LANES)
    else:
        in_specs.append(per_stream)
        args.append(s0)
        final_rows = (0, n_streams)
    rider_in_specs, rider_specs, rider_shapes, rider_args = _cast_rider_specs(cast_riders, (g // gps,))
    kern = functools.partial(_s5_kernel, n_sub=n_sub, lanes=lanes, scan=scan, final_rows=final_rows,
                             n_cast=len(cast_riders))
    return pl.pallas_call(
        kern,
        out_shape=(jax.ShapeDtypeStruct((g, rows, lanes), BF16),
                   jax.ShapeDtypeStruct((n_streams, g, 2 * n), F32), *rider_shapes),
        grid=(g // gps,),
        in_specs=in_specs + rider_in_specs,
        out_specs=(grp((rows, lanes)), per_stream, *rider_specs),
        scratch_shapes=[pltpu.VMEM((gps, rows, lanes), F32),
                        pltpu.VMEM((gps, n_sub, 2 * n, lanes), F32)],
        compiler_params=_cparams(1),
        name="s5",
    )(*args, *rider_args)


def _glu_kernel(*refs, n_ph, rows, lane_rows, n_cast):
    (yt_ref, w_ref, b_ref, m_ref), riders = _split_riders(refs, n_in=3, n_out=1, n_cast=n_cast)
    _run_riders(riders)
    gs = []
    for ph in range(n_ph):
        yt = yt_ref[:, ph, :, :].astype(F32).reshape(S5_WIDTH, lane_rows)
        gs.append(_gelu(yt.T[:rows]))
    gy = jnp.concatenate(gs, axis=0)
    gate = jnp.dot(gy.astype(BF16), w_ref[...], preferred_element_type=F32) + b_ref[...]
    m = gy * _sigmoid(gate)
    for pair in range(n_ph // 2):
        lo = 2 * pair * rows
        _oct_store(m_ref, pair, m[lo:lo + rows], m[lo + rows:lo + 2 * rows])


def _glu_call(yt4, w_glu_b, b_glu, *, rows, n_ph, row_blk, cast_riders=()):
    g, t, p, lane_rows = yt4.shape
    n_rb, lane_blk, lane_rows_expected = _phase_blocks(rows, row_blk)
    assert lane_rows == lane_rows_expected
    grid = (n_rb, t // n_ph)
    rider_in_specs, rider_specs, rider_shapes, rider_args = _cast_rider_specs(cast_riders, grid)
    kern = functools.partial(_glu_kernel, n_ph=n_ph, rows=row_blk, lane_rows=lane_blk,
                             n_cast=len(cast_riders))
    assert n_ph % 2 == 0
    oct_shape = _oct_shape(rows, t, S5_WIDTH)
    return pl.pallas_call(
        kern,
        out_shape=(jax.ShapeDtypeStruct(oct_shape, BF16), *rider_shapes),
        grid=grid,
        in_specs=[
            pl.BlockSpec((g, n_ph, p, lane_blk), lambda i, j: (0, j, 0, i)),
            _const_spec((S5_WIDTH, S5_WIDTH)),
            _const_spec((1, S5_WIDTH)),
            *rider_in_specs,
        ],
        out_specs=(pl.BlockSpec((row_blk // OCT, oct_shape[1], n_ph // 2, 2 * OCT, LANES),
                                lambda i, j: (i, 0, j, 0, 0)), *rider_specs),
        compiler_params=_cparams(2),
        name="glu",
    )(yt4, w_glu_b, b_glu, *rider_args)


def _main_kernel(x_ref, m_ref, zuv_ref, mod_ref, g2_ref, gf_ref, lng_ref, lnb_ref, gw_ref, gbt_ref,
                 wo_ref, wg_ref, wu_ref, wd_ref, y_ref, *rest, tm, cl, seq, streams):
    *maybe_v_out_ref, v_ref, ygm_ref, attn_ref, act_ref, stage_ref = rest
    t = tm // OCT
    hd = GM_HEAD_DIM

    stage_ref[...] = m_ref[...].astype(F32)
    attn_ref[...] = jnp.dot(_oct_load(stage_ref, t).astype(BF16), wo_ref[:S5_WIDTH, :],
                            preferred_element_type=F32)

    gv = zuv_ref[:, GM_WIDTH:].astype(F32)
    cen = gv - jnp.mean(gv, axis=-1, keepdims=True)
    var = jnp.mean(cen * cen, axis=-1, keepdims=True)
    v = cen * lax.rsqrt(var + EPS) * lng_ref[...] + lnb_ref[...]
    for v_out_ref in maybe_v_out_ref:
        v_out_ref[...] = v
    v_ref[...] = v.astype(BF16)

    blk_i = _div_pow2(lax.broadcasted_iota(jnp.int32, (cl, cl), 0), CHUNK)
    blk_j = _div_pow2(lax.broadcasted_iota(jnp.int32, (cl, cl), 1), CHUNK)
    causal = blk_j <= blk_i
    first_head = lax.broadcasted_iota(jnp.int32, (cl, 2 * hd), 1) < hd
    for pr in range(GM_HEADS // 2):
        h0, h1 = 2 * pr, 2 * pr + 1
        wm = jnp.concatenate(
            [jnp.where(causal, gw_ref[h, :cl, :cl], 0.0) for h in (h0, h1)], axis=1).astype(BF16)
        bias = jnp.where(first_head, gbt_ref[:cl, h0:h0 + 1], gbt_ref[:cl, h1:h1 + 1])
        cs = slice(h0 * hd, (h1 + 1) * hd)
        for ci in range(tm // cl):
            rs = slice(ci * cl, (ci + 1) * cl)
            vv = v_ref[rs, cs]
            zero = jnp.zeros_like(vv)
            rhs = jnp.concatenate([jnp.where(first_head, vv, zero), jnp.where(first_head, zero, vv)],
                                  axis=0)
            mixed = jnp.dot(wm, rhs, preferred_element_type=F32) + bias
            ygm_ref[rs, cs] = (zuv_ref[rs, cs].astype(F32) * mixed).astype(BF16)

    attn = attn_ref[...] + jnp.dot(ygm_ref[...], wo_ref[S5_WIDTH:, :], preferred_element_type=F32)

    def mod_rows(idx):
        if len(streams) == 1:
            return _mod_vec(mod_ref, streams[0], idx)
        return jnp.concatenate(
            [jnp.broadcast_to(_mod_vec(mod_ref, s, idx), (seq, D_MODEL)) for s in streams], axis=0)

    gate1, shift2, scale2, gate2 = mod_rows(2), mod_rows(3), mod_rows(4), mod_rows(5)
    x1 = x_ref[...] + gate1 * attn
    h2 = (x1 * _rms_scale(x1) * (g2_ref[...] * (1.0 + scale2)) + shift2).astype(BF16)

    assert D_FF % MXU_DIM == 0
    for lo in range(0, D_FF, MXU_DIM):
        hi = lo + MXU_DIM
        gg = jnp.dot(h2, wg_ref[:, lo:hi], preferred_element_type=F32)
        up = jnp.dot(h2, wu_ref[:, lo:hi], preferred_element_type=F32)
        act_ref[:, lo:hi] = (gg * jax.nn.sigmoid(gg) * up).astype(BF16)
    acc = jnp.dot(act_ref[...], wd_ref[...], preferred_element_type=F32)
    x2 = x1 + gate2 * acc
    y_ref[...] = x2 * _rms_scale(x2) * gf_ref[...]


def _main_sets_kernel(*refs, sets, n_shared):
    shared = refs[:n_shared]
    *scratch, stage_ref = refs[-5:]
    step = pl.program_id(0)
    pos = n_shared
    out_pos = n_shared + 3 * len(sets)
    for st in sets:
        ins = refs[pos:pos + 3]
        pos += 3
        n_out = 2 if st["want_v"] else 1
        outs = refs[out_pos:out_pos + n_out]
        out_pos += n_out
        tm = st["tm"]

        @pl.when(jnp.logical_and(step >= st["start"], step < st["start"] + st["n_tiles"]))
        def _(ins=ins, outs=outs, st=st, tm=tm):
            _main_kernel(*ins, *shared, *outs, *[r.at[:tm] for r in scratch],
                         stage_ref.at[:ins[1].shape[0]],
                         tm=tm, cl=st["cl"], seq=st["seq"], streams=st["streams"])


def _main_call(stream_sets, mod_all, norm2_g, final_g, ln_g, ln_b, gm_w, gm_bt, w_out_b, w_gate_b, w_up_b,
               w_down_b):
    shared = (mod_all, norm2_g, final_g, ln_g, ln_b, gm_w, gm_bt, w_out_b, w_gate_b, w_up_b, w_down_b)
    set_args, set_in_specs, out_shape, out_specs, statics = [], [], [], [], []
    start = 0
    for st in stream_sets:
        n_tok = st["x2d"].shape[0]
        n_oct, n_lb, half_t, _, _ = st["m"].shape
        t = 2 * half_t
        tm = OCT * t
        seq, streams = st["seq"], st["streams"]
        assert n_oct * tm == n_tok
        assert len(streams) == 1 or (n_oct == 1 and len(streams) * seq == tm)

        def tile_index(i, start=start, n_oct=n_oct):
            return (jnp.clip(i - start, 0, n_oct - 1), 0)

        tok = lambda width: pl.BlockSpec((tm, width), tile_index)
        oct_rows = n_lb * t * OCT
        set_args += [st["x2d"], st["m"].reshape(n_oct * oct_rows, LANES), st["zuv"]]
        set_in_specs += [tok(D_MODEL), pl.BlockSpec((oct_rows, LANES), tile_index), tok(2 * GM_WIDTH)]
        out_shape.append(jax.ShapeDtypeStruct((n_tok, D_MODEL), F32))
        out_specs.append(tok(D_MODEL))
        if st["want_v"]:
            out_shape.append(jax.ShapeDtypeStruct((n_tok, GM_WIDTH), F32))
            out_specs.append(tok(GM_WIDTH))
        statics.append(dict(start=start, n_tiles=n_oct, tm=tm, cl=min(GM_CHUNK, seq), seq=seq,
                            streams=streams, want_v=st["want_v"]))
        start += n_oct
    tm_max = max(s["tm"] for s in statics)
    outs = pl.pallas_call(
        functools.partial(_main_sets_kernel, sets=statics, n_shared=len(shared)),
        out_shape=tuple(out_shape),
        grid=(start,),
        in_specs=[_const_spec(a.shape) for a in shared] + set_in_specs,
        out_specs=tuple(out_specs),
        scratch_shapes=[pltpu.VMEM((tm_max, GM_WIDTH), BF16), pltpu.VMEM((tm_max, GM_WIDTH), BF16),
                        pltpu.VMEM((tm_max, D_MODEL), F32), pltpu.VMEM((tm_max, D_FF), BF16),
                        pltpu.VMEM((tm_max * (S5_WIDTH // LANES), LANES), F32)],
        compiler_params=_cparams(1),
        name="main",
    )(*shared, *set_args)
    results, pos = [], 0
    for s in statics:
        n_out = 2 if s["want_v"] else 1
        results.append(tuple(outs[pos:pos + n_out]))
        pos += n_out
    return results


def _mixer_front(x, mod_all, streams, s0, prm, *, t, n_ph, row_blk, want_v, f32_weights=None,
                 glu_blocks=None):
    b, seq, _ = x.shape
    n_chunks = seq // t
    rows = b * n_chunks
    scan = s0 is None
    assert (b == 1) if scan else (n_chunks == 1)
    n_sub = t // SUB

    zt4, zuv = _inproj_call(
        x.reshape(rows, t, D_MODEL), mod_all, streams, prm["norm1_g"], prm["w_in"],
        n_ph=n_ph, row_blk=row_blk)
    lane_rows = zt4.shape[-1]

    assert not scan or rows == lane_rows
    pending = dict(f32_weights or {})
    on_glu = {k: pending.pop(k) for k in ("w_up",) if k in pending}
    yt, fin, *cast = _s5_call(zt4.reshape(S5_GROUPS, t * S5_GROUP, lane_rows),
                              prm["wk"], prm["qm"], prm["tabr"], prm["tabi"], s0,
                              n_streams=b, n_sub=n_sub, scan=scan, cast_riders=tuple(pending.values()))
    prm = {**prm, **dict(zip(pending.keys(), cast))}
    glu_n_ph, glu_row_blk = glu_blocks or (n_ph, row_blk)
    m, *cast = _glu_call(yt.reshape(S5_GROUPS, t, S5_GROUP, lane_rows), prm["w_glu"], prm["b_glu"],
                         rows=rows, n_ph=glu_n_ph, row_blk=glu_row_blk, cast_riders=tuple(on_glu.values()))
    prm = {**prm, **dict(zip(on_glu.keys(), cast))}

    n_tok = b * seq
    stream_set = dict(x2d=x.reshape(n_tok, D_MODEL), m=m, zuv=zuv.reshape(n_tok, 2 * GM_WIDTH),
                      streams=streams, seq=seq, want_v=want_v)

    return stream_set, fin[..., :S5_STATE], fin[..., S5_STATE:], prm


def kernel(x_prompt, x_sample, state_s5_re, state_s5_im, c_prompt, c_sample, norm1_g, norm2_g, w_ada, b_ada, w_in, s5_lambda_re, s5_lambda_im, s5_log_step, s5_b_re, s5_b_im, s5_c_re, s5_c_im, s5_d, s5_w_glu, s5_b_glu, gm_ln_g, gm_ln_b, gm_w_s, gm_b_s, w_out, ffn_w_gu, ffn_w_down, final_g):
    depth = w_in.shape[0]
    assert depth == 1
    l = 0
    n_p = c_prompt.shape[0]
    n_s = c_sample.shape[0]

    c_all = jnp.concatenate([c_prompt, c_sample], axis=0)
    c_pad = jnp.pad(c_all, ((0, -c_all.shape[0] % SUBLANES), (0, 0)))
    streams_p = tuple(range(n_p))
    streams_s = tuple(range(n_p, n_p + n_s))

    wk, qm, tabr, tabi, mod_all, w_in_b, w_out_b, w_down_b = _s5_prep_call(
        c_pad, w_ada[l], b_ada[l][None, :], s5_lambda_re[l], s5_lambda_im[l], s5_log_step[l][None, :],
        jnp.concatenate([jnp.swapaxes(s5_b_re[l], 1, 2), jnp.swapaxes(s5_b_im[l], 1, 2)], axis=-1),
        s5_c_re[l], s5_c_im[l], s5_d[l], cast_riders=(w_in[l], w_out[l], ffn_w_down[l]))

    prm = dict(
        norm1_g=norm1_g[l][None, :], norm2_g=norm2_g[l][None, :], final_g=final_g[None, :],
        w_in=w_in_b, w_out=w_out_b, w_down=w_down_b, ln_g=gm_ln_g[l][None, :], ln_b=gm_ln_b[l][None, :],
        wk=wk, qm=qm, tabr=tabr, tabi=tabi, b_glu=s5_b_glu[l][None, :],
        gm_w=gm_w_s[l], gm_bt=jnp.transpose(gm_b_s[l]),
    )
    later_weights = dict(w_glu=s5_w_glu[l], w_gate=(ffn_w_gu[l], 0, 2), w_up=(ffn_w_gu[l], 1, 2))

    set_p, pre, pim, prm = _mixer_front(
        x_prompt, mod_all, streams_p, None, prm, f32_weights=later_weights,
        t=S5_LONG_SUBS * SUB, n_ph=SUBLANES, row_blk=LANES, want_v=False,
        glu_blocks=(SUBLANES // 2, 2 * LANES))
    s0 = jnp.concatenate([state_s5_re[l], state_s5_im[l]], axis=-1)
    n_b, seq_s, _ = x_sample.shape
    set_s, sre, sim, _ = _mixer_front(
        x_sample, mod_all, streams_s, s0, prm, t=seq_s, n_ph=seq_s, row_blk=n_b, want_v=True)

    (yp,), (ys, vs) = _main_call(
        [set_p, set_s], mod_all, prm["norm2_g"], prm["final_g"], prm["ln_g"], prm["ln_b"],
        prm["gm_w"], prm["gm_bt"], prm["w_out"], prm["w_gate"], prm["w_up"], prm["w_down"])
    return (yp.reshape(x_prompt.shape), ys.reshape(x_sample.shape), pre[None], pim[None],
            sre[None], sim[None], vs.reshape(n_b, seq_s, GM_WIDTH)[None])
```

```python
import functools
import math

import jax
import jax.numpy as jnp
from jax import lax
from jax.experimental import pallas as pl
from jax.experimental.pallas import tpu as pltpu

D_MODEL = 1024
S5_WIDTH = 512
S5_GROUP = 16
S5_GROUPS = 32
S5_STATE = 64
GM_WIDTH = 512
GM_CHUNK = 128
GM_HEADS = 8
GM_HEAD_DIM = 64
CHUNK = 64
IN_WIDTH = S5_WIDTH + 2 * GM_WIDTH
D_FF = 2816
EPS = 1e-6

LANES = 128
SUBLANES = 8
BF16_SUBLANES = 16
MXU_DIM = 256
VMEM_LIMIT_BYTES = 56 * 1024 * 1024

SUB = MXU_DIM // S5_GROUP
S5_LONG_SUBS = 4
PREP_GROUPS_PER_STEP = 8
S5_GROUPS_PER_STEP_LONG = 8
S5_GROUPS_PER_STEP_SHORT = 8

F32 = jnp.float32
BF16 = jnp.bfloat16


def _cparams(n_grid_axes):
    return pltpu.CompilerParams(
        dimension_semantics=("arbitrary",) * n_grid_axes,
        vmem_limit_bytes=VMEM_LIMIT_BYTES,
    )


def _const_spec(shape):
    nd = len(shape)
    return pl.BlockSpec(shape, lambda *_: (0,) * nd, pipeline_mode=pl.Buffered(1))


def _rms_scale(x):
    return lax.rsqrt(jnp.mean(x * x, axis=-1, keepdims=True) + EPS)


_GELU_C0 = math.sqrt(2.0 / math.pi)
_GELU_C1 = 0.044715 * _GELU_C0


def _gelu(x):
    hx = 0.5 * x
    return hx + hx * jnp.tanh(x * (_GELU_C0 + _GELU_C1 * (x * x)))


def _sigmoid(x):
    return 0.5 * jnp.tanh(0.5 * x) + 0.5


def _cmul(ar, ai, xr, xi):
    return ar * xr - ai * xi, ar * xi + ai * xr


def _div_pow2(idx, divisor):
    shift = divisor.bit_length() - 1
    assert divisor == 1 << shift
    return lax.shift_right_logical(idx, shift)


def _mod_pow2(idx, divisor):
    assert divisor & (divisor - 1) == 0
    return lax.bitwise_and(idx, divisor - 1)


def _place_rows(sel, x):
    hi = x.astype(BF16)
    rest = x - hi.astype(F32)
    mid = rest.astype(BF16)
    lo = (rest - mid.astype(F32)).astype(BF16)
    dot = functools.partial(jnp.dot, preferred_element_type=F32)
    return dot(sel, hi) + (dot(sel, mid) + dot(sel, lo))


def _dot_split(x, y):
    x_hi, y_hi = x.astype(BF16), y.astype(BF16)
    x_lo = (x - x_hi.astype(F32)).astype(BF16)
    y_lo = (y - y_hi.astype(F32)).astype(BF16)
    dot = functools.partial(jnp.dot, preferred_element_type=F32)
    return dot(x_hi, y_hi) + (dot(x_lo, y_hi) + dot(x_hi, y_lo))


OCT = SUBLANES


def _oct_shape(n_rows, t, width):
    assert n_rows % OCT == 0 and width % LANES == 0 and t % 2 == 0
    return (n_rows // OCT, width // LANES, t // 2, 2 * OCT, LANES)


def _oct_store(ref, pair, val_even, val_odd):
    n_oct, n_lb = ref.shape[0], ref.shape[1]
    for lb in range(n_lb):
        lanes = slice(lb * LANES, (lb + 1) * LANES)
        both = jnp.concatenate([val_even[:, lanes].reshape(n_oct, OCT, LANES),
                                val_odd[:, lanes].reshape(n_oct, OCT, LANES)], axis=1)
        ref[:, lb, pair, :, :] = both.astype(BF16)


def _oct_load(ref, t):
    n_lb = ref.shape[0] // (t * OCT)
    chunks = []
    for c in range(OCT):
        chunks.append(jnp.concatenate(
            [ref[pl.ds(lb * t * OCT + c, t, stride=OCT), :] for lb in range(n_lb)], axis=1))
    return jnp.concatenate(chunks, axis=0)


def _cast_rider_specs(arrays, grid):
    n_steps = math.prod(grid)

    def row_block(*idx):
        step = 0
        for i, extent in zip(idx, grid):
            step = step * extent + i
        return (step, 0)

    in_specs, out_specs, shapes, sources = [], [], [], []
    for a in arrays:
        a, part, n_parts = a if isinstance(a, tuple) else (a, 0, 1)
        rows, cols = a.shape[0], a.shape[1] // n_parts
        blk = rows // n_steps
        assert blk * n_steps == rows and blk % BF16_SUBLANES == 0
        assert cols * n_parts == a.shape[1] and cols % LANES == 0
        in_specs.append(pl.BlockSpec((blk, cols), lambda *idx, part=part: (row_block(*idx)[0], part)))
        out_specs.append(pl.BlockSpec((blk, cols), row_block))
        shapes.append(jax.ShapeDtypeStruct((rows, cols), BF16))
        sources.append(a)
    return in_specs, out_specs, shapes, sources


def _split_riders(refs, n_in, n_out, n_cast):
    ins, refs = refs[:n_in], refs[n_in:]
    cast_in, refs = refs[:n_cast], refs[n_cast:]
    outs, refs = refs[:n_out], refs[n_out:]
    cast_out, scratch = refs[:n_cast], refs[n_cast:]
    return (*ins, *outs, *scratch), list(zip(cast_in, cast_out))


def _run_riders(pairs):
    for src, dst in pairs:
        dst[...] = src[...].astype(BF16)


def _ada_step(c_ref, w_ref, b_ref, o_ref):
    @pl.when(pl.program_id(0) == 0)
    def _():
        o_ref[...] = jnp.broadcast_to(b_ref[...], o_ref.shape)

    c = c_ref[...]
    o_ref[...] += _dot_split(c * jax.nn.sigmoid(c), w_ref[...])


def _ada_specs(c_pad, w_ada, n_steps):
    rows = c_pad.shape[0]
    n_in, n_out = w_ada.shape
    bk = n_in // n_steps
    assert bk * n_steps == n_in and bk % LANES == 0
    in_specs = [
        pl.BlockSpec((rows, bk), lambda k: (0, k)),
        pl.BlockSpec((bk, n_out), lambda k: (k, 0)),
        pl.BlockSpec((1, n_out), lambda k: (0, 0)),
    ]
    return in_specs, pl.BlockSpec((rows, n_out), lambda k: (0, 0)), jax.ShapeDtypeStruct((rows, n_out), F32)


def _discretise(lr, li, ls):
    step = jnp.exp(ls)
    mag = jnp.exp(lr * step)
    ar = mag * jnp.cos(li * step)
    ai = mag * jnp.sin(li * step)
    den = lr * lr + li * li
    fr = ((ar - 1.0) * lr + ai * li) / den
    fi = (ai * lr - (ar - 1.0) * li) / den
    return ar, ai, fr, fi


def _selection(rows, cols, row_of_col):
    r = lax.broadcasted_iota(jnp.int32, (rows, cols), 0)
    c = lax.broadcasted_iota(jnp.int32, (rows, cols), 1)
    return jnp.where(r == row_of_col(c), 1.0, 0.0).astype(BF16)


def _place(x, sel):
    hi = x.astype(BF16)
    rest = x - hi.astype(F32)
    mid = rest.astype(BF16)
    lo = (rest - mid.astype(F32)).astype(BF16)
    dot = functools.partial(jnp.dot, preferred_element_type=F32)
    return dot(hi, sel) + (dot(mid, sel) + dot(lo, sel))


def _to_column(row):
    k = row.shape[1]
    r = lax.broadcasted_iota(jnp.int32, (k, k), 0)
    c = lax.broadcasted_iota(jnp.int32, (k, k), 1)
    return jnp.sum(jnp.where(r == c, jnp.broadcast_to(row, (k, k)), 0.0), axis=1, keepdims=True)


def _s5_prep_kernel(*refs, n_cast):
    refs, riders = _split_riders(refs, n_in=10, n_out=5, n_cast=n_cast)
    c_ref, w_ada_ref, b_ada_ref, lam_re_ref, lam_im_ref, ls_ref, d_ref, *grouped, mod_ref = refs
    n, sub, p = S5_STATE, SUB, S5_GROUP
    width = sub * p

    def col_source(c):
        return jnp.where(c < width, (sub - 1) - _div_pow2(c, p), jnp.where(c < 2 * width, sub + 1, sub))

    sels = dict(
        twice=_selection(n, 2 * n, lambda c: _mod_pow2(c, n)),
        tile_rows=_selection(p, width, lambda c: _mod_pow2(c, p)).T,
        cols=_selection(2 * n, 2 * width + LANES, col_source),
    )
    gps = grouped[0].shape[0]
    pending = [_s5_prep_group(pl.program_id(0) * gps + gi, sels, lam_re_ref, lam_im_ref, ls_ref, d_ref,
                              *[r.at[gi] for r in grouped]) for gi in range(gps)]
    while pending:
        pending = [gen for gen in pending if next(gen, "done") != "done"]
    _ada_step(c_ref, w_ada_ref, b_ada_ref, mod_ref)
    _run_riders(riders)


def _s5_prep_group(g, sels, lam_re_ref, lam_im_ref, ls_ref, d_ref, bt_ref, c_re_ref, c_im_ref,
                   wk_ref, qm_ref, tabr_ref, tabi_ref):
    n, sub, p = S5_STATE, SUB, S5_GROUP
    width = sub * p

    lr_row = lam_re_ref[pl.ds(g, 1), :]
    li_row = lam_im_ref[pl.ds(g, 1), :]
    ls_all = ls_ref[...]
    grp_lane = lax.broadcasted_iota(jnp.int32, ls_all.shape, 1)
    ls = jnp.sum(jnp.where(grp_lane == g, ls_all, 0.0), axis=1, keepdims=True)

    ar8, ai8, fr8, fi8 = _discretise(jnp.broadcast_to(lr_row, (SUBLANES, n)),
                                     jnp.broadcast_to(li_row, (SUBLANES, n)), ls)
    twice = lambda t8: _place(jnp.concatenate([t8] * (p // SUBLANES), axis=0), sels["twice"])
    a2r, a2i = twice(ar8), twice(ai8)
    first = lax.broadcasted_iota(jnp.int32, (p, 2 * n), 1) < n
    c2r = _place(c_re_ref[...], sels["twice"])
    c2i = _place(c_im_ref[...], sels["twice"])
    pr = jnp.ones_like(a2r)
    pi = jnp.zeros_like(a2r)
    ccat = []
    tbl_rows = SUBLANES * (-(-(sub + 2) // SUBLANES))
    tbl_row = lax.broadcasted_iota(jnp.int32, (tbl_rows, 2 * n), 0)
    as_tbl_row = lambda re2, im2: jnp.concatenate([jnp.where(first, re2, im2)] * (tbl_rows // p + 1),
                                                  axis=0)[:tbl_rows]
    tbl = jnp.zeros((tbl_rows, 2 * n), F32)
    for d in range(sub + 1):
        ccat.append(c2r * jnp.where(first, pr, -pi) + c2i * jnp.where(first, -pi, -pr))
        tbl = jnp.where(tbl_row == d, as_tbl_row(pr, pi), tbl)
        pr, pi = _cmul(a2r, a2i, pr, pi)
    tbl = jnp.where(tbl_row == sub + 1, as_tbl_row(twice(fr8), twice(fi8)), tbl)
    yield
    qm_ref[...] = jnp.concatenate(ccat[1:], axis=0).astype(BF16)
    rcat = jnp.concatenate(ccat[:sub], axis=0)

    tbl_t = jnp.concatenate([tbl, jnp.zeros((2 * n - tbl_rows, 2 * n), F32)], axis=0).T
    cols = _place(tbl_t, sels["cols"])
    yield
    apr, fr, a16r = cols[:n, :width], cols[:n, width:2 * width], cols[:n, 2 * width:]
    api, fi, a16i = cols[n:, :width], cols[n:, width:2 * width], cols[n:, 2 * width:]
    b_tiled = _place_rows(sels["tile_rows"], bt_ref[...]).T
    btr, bti = b_tiled[:n], b_tiled[n:]
    bbr, bbi = _cmul(fr, fi, btr, bti)
    pmr, pmi = _cmul(apr, api, bbr, bbi)
    wk_ref[width:width + n, :] = pmr.astype(BF16)
    wk_ref[width + n:, :] = pmi.astype(BF16)
    yield

    bbcat = jnp.concatenate([bbr, bbi], axis=0)
    kt = _dot_split(rcat, bbcat)
    yield
    d_col = _to_column(d_ref[pl.ds(g, 1), :])
    row_p = lax.broadcasted_iota(jnp.int32, (p, width), 0)
    lane_p = lax.broadcasted_iota(jnp.int32, (p, width), 1)
    d_diag = jnp.where(row_p == _mod_pow2(lane_p, p), d_col, 0.0)
    kt = jnp.concatenate([kt[:p] + d_diag, kt[p:]], axis=0)
    col_blk = _div_pow2(lax.broadcasted_iota(jnp.int32, (width, width), 1), p)
    m16 = jnp.zeros((width, width), F32)
    for k in range(sub):
        if k == 0:
            shifted = kt
        else:
            shifted = jnp.concatenate(
                [jnp.zeros((k * p, width), F32), kt[:width - k * p]], axis=0)
        m16 = jnp.where(col_blk == k, shifted, m16)
    wk_ref[:width, :] = m16.astype(BF16)

    tabr_ref[...] = a16r
    tabi_ref[...] = a16i


def _s5_prep_call(c_pad, w_ada, b_ada, lam_re, lam_im, log_step, bt, c_re, c_im, d, cast_riders=()):
    g, n, p, sub = S5_GROUPS, S5_STATE, S5_GROUP, SUB
    width = sub * p
    grp = lambda shape: pl.BlockSpec((PREP_GROUPS_PER_STEP,) + shape, lambda i: (i, 0, 0))
    whole = lambda a: pl.BlockSpec(a.shape, lambda i: (0,) * a.ndim)
    n_steps = g // PREP_GROUPS_PER_STEP
    rider_in_specs, rider_specs, rider_shapes, rider_args = _cast_rider_specs(cast_riders, (n_steps,))
    ada_in_specs, ada_out_spec, ada_shape = _ada_specs(c_pad, w_ada, n_steps)
    return pl.pallas_call(
        functools.partial(_s5_prep_kernel, n_cast=len(cast_riders)),
        out_shape=(
            jax.ShapeDtypeStruct((g, width + 2 * n, width), BF16),
            jax.ShapeDtypeStruct((g, width, 2 * n), BF16),
            jax.ShapeDtypeStruct((g, n, LANES), F32),
            jax.ShapeDtypeStruct((g, n, LANES), F32),
            ada_shape,
            *rider_shapes,
        ),
        grid=(n_steps,),
        in_specs=ada_in_specs + [whole(lam_re), whole(lam_im), whole(log_step), whole(d)]
        + [grp((p, 2 * n))] + [grp((p, n))] * 2 + rider_in_specs,
        out_specs=(grp((width + 2 * n, width)), grp((width, 2 * n)),
                   grp((n, LANES)), grp((n, LANES)), ada_out_spec, *rider_specs),
        compiler_params=_cparams(1),
        name="prep",
    )(c_pad, w_ada, b_ada, lam_re, lam_im, log_step, d, bt, c_re, c_im, *rider_args)


def _mod_vec(mod_ref, stream, idx):
    return mod_ref[stream:stream + 1, idx * D_MODEL:(idx + 1) * D_MODEL]


def _inproj_kernel(x_ref, mod_ref, g1_ref, w_ref, zt_ref, zuv_ref, hs_ref,
                   *, n_ph, rows, lane_rows, streams):
    m = rows * n_ph
    assert len(streams) in (1, rows)
    shift, scale = (jnp.concatenate([_mod_vec(mod_ref, s, idx) for s in streams], axis=0)
                    for idx in (0, 1))
    gain = (g1_ref[...] * (1.0 + scale))[:, None, :]
    shift = shift[:, None, :]
    x3 = x_ref[...]
    h = (x3 * _rms_scale(x3) * gain + shift).reshape(m, D_MODEL)
    zuv = jnp.dot(h.astype(BF16), w_ref[:, S5_WIDTH:], preferred_element_type=F32)
    zuv_ref[...] = _gelu(zuv).reshape(rows, n_ph, 2 * GM_WIDTH).astype(BF16)

    n_lb = D_MODEL // LANES
    for lb in range(n_lb):
        hs_ref[lb] = h[:, lb * LANES:(lb + 1) * LANES]
    hp = jnp.concatenate(
        [jnp.concatenate([hs_ref[lb, pl.ds(ph, rows, stride=n_ph), :] for lb in range(n_lb)], axis=1)
         for ph in range(n_ph)], axis=0).astype(BF16)
    z5 = jnp.dot(hp, w_ref[:, :S5_WIDTH], preferred_element_type=F32)
    for ph in range(n_ph):
        zz = z5[ph * rows:(ph + 1) * rows]
        if lane_rows > rows:
            zz = jnp.concatenate([zz, jnp.zeros((lane_rows - rows, S5_WIDTH), F32)], axis=0)
        zt = zz.T.reshape(S5_GROUPS, S5_GROUP, lane_rows)
        zt_ref[:, ph, :, :] = zt.astype(BF16)


def _phase_blocks(n_rows, row_blk):
    if row_blk % LANES == 0:
        assert n_rows % row_blk == 0
        return n_rows // row_blk, row_blk, n_rows
    assert row_blk == n_rows
    lane_rows = -(-n_rows // LANES) * LANES
    return 1, lane_rows, lane_rows


def _inproj_call(x3, mod_all, streams, norm1_g, w_in_b, *, n_ph, row_blk):
    rows, t, _ = x3.shape
    n_rb, lane_blk, lane_rows = _phase_blocks(rows, row_blk)
    kern = functools.partial(_inproj_kernel, n_ph=n_ph, rows=row_blk, lane_rows=lane_blk,
                             streams=streams)
    return pl.pallas_call(
        kern,
        out_shape=(
            jax.ShapeDtypeStruct((S5_GROUPS, t, S5_GROUP, lane_rows), BF16),
            jax.ShapeDtypeStruct((rows, t, 2 * GM_WIDTH), BF16),
        ),
        grid=(n_rb, t // n_ph),
        in_specs=[
            pl.BlockSpec((row_blk, n_ph, D_MODEL), lambda i, j: (i, j, 0)),
            _const_spec(mod_all.shape),
            _const_spec((1, D_MODEL)),
            _const_spec((D_MODEL, IN_WIDTH)),
        ],
        out_specs=(
            pl.BlockSpec((S5_GROUPS, n_ph, S5_GROUP, lane_blk), lambda i, j: (0, j, 0, i)),
            pl.BlockSpec((row_blk, n_ph, 2 * GM_WIDTH), lambda i, j: (i, j, 0)),
        ),
        scratch_shapes=[pltpu.VMEM((D_MODEL // LANES, row_blk * n_ph, LANES), F32)],
        compiler_params=_cparams(2),
        name="inproj",
    )(x3, mod_all, norm1_g, w_in_b)


def _s5_kernel(*refs, n_sub, lanes, scan, final_rows, n_cast):
    refs, riders = _split_riders(refs, n_in=5 if scan else 6, n_out=2, n_cast=n_cast)
    if scan:
        zt_ref, wk_ref, qm_ref, tabr_ref, tabi_ref, yt_ref, sf_ref, ybuf, lbuf = refs
        s0_ref = None
    else:
        zt_ref, wk_ref, qm_ref, tabr_ref, tabi_ref, s0_ref, yt_ref, sf_ref, ybuf, lbuf = refs
    groups = range(zt_ref.shape[0])
    n = S5_STATE
    width = SUB * S5_GROUP

    widen = lambda tile: jnp.concatenate([tile] * (lanes // LANES), axis=1)
    sub_pows = []
    for g in groups:
        a1 = (widen(tabr_ref[g]), widen(tabi_ref[g]))
        pows = [a1]
        for _ in range(n_sub - 1):
            pows.append(_cmul(*a1, *pows[-1]))
        sub_pows.append(pows)

    local = []
    for g in groups:
        wk = wk_ref[g]
        lr = li = None
        for j in range(n_sub):
            u = zt_ref[g, j * width:(j + 1) * width, :]
            r = jnp.dot(wk, u, preferred_element_type=F32)
            ybuf[g, j * width:(j + 1) * width, :] = r[:width]
            wr = r[width:width + n]
            wi = r[width + n:]
            if j == 0:
                lr, li = wr, wi
            else:
                tr, ti = _cmul(*sub_pows[g][0], lr, li)
                lr, li = tr + wr, ti + wi
            lbuf[g, j, :n, :] = lr
            lbuf[g, j, n:, :] = li
        local.append((lr, li))

    if scan:
        lane = lax.broadcasted_iota(jnp.int32, (n, lanes), 1)
        xs = list(local)
        ms = [sub_pows[g][n_sub - 1] for g in groups]
        for i in range(int(math.log2(lanes))):
            sh = 1 << i
            for g in groups:
                xr, xi = xs[g]
                rr = jnp.where(lane >= sh, pltpu.roll(xr, sh, 1), 0.0)
                ri = jnp.where(lane >= sh, pltpu.roll(xi, sh, 1), 0.0)
                tr, ti = _cmul(*ms[g], rr, ri)
                xs[g] = (xr + tr, xi + ti)
                ms[g] = _cmul(*ms[g], *ms[g])
        entering = [(jnp.where(lane >= 1, pltpu.roll(xr, 1, 1), 0.0),
                     jnp.where(lane >= 1, pltpu.roll(xi, 1, 1), 0.0)) for xr, xi in xs]
    else:
        assert lanes == LANES
        entering = []
        for g in groups:
            s0 = s0_ref[:, g, :]
            s0t = jnp.concatenate([s0, jnp.zeros((LANES - s0.shape[0], 2 * n), F32)], axis=0).T
            entering.append((s0t[:n], s0t[n:]))

    keep = slice(lanes - LANES, lanes)
    for g in groups:
        qm = qm_ref[g]
        sr, si = entering[g]
        for j in range(n_sub):
            if j == 0:
                pr, pi = sr, si
            else:
                tr, ti = _cmul(*sub_pows[g][j - 1], sr, si)
                pr, pi = lbuf[g, j - 1, :n, :] + tr, lbuf[g, j - 1, n:, :] + ti
            sp = jnp.concatenate([pr, pi], axis=0).astype(BF16)
            y = ybuf[g, j * width:(j + 1) * width, :] + jnp.dot(qm, sp, preferred_element_type=F32)
            yt_ref[g, j * width:(j + 1) * width, :] = y.astype(BF16)
        tr, ti = _cmul(*sub_pows[g][n_sub - 1], sr, si)
        ends = jnp.concatenate([lbuf[g, n_sub - 1, :n, keep] + tr[:, keep],
                                lbuf[g, n_sub - 1, n:, keep] + ti[:, keep]], axis=0).T
        sf_ref[:, g, :] = ends[final_rows[0]:final_rows[1]]
    _run_riders(riders)


def _s5_call(zt, wk, qm, tabr, tabi, s0, *, n_streams, n_sub, scan, cast_riders=()):
    g, rows, lanes = zt.shape
    n = S5_STATE
    width = SUB * S5_GROUP
    assert lanes % LANES == 0 and lanes & (lanes - 1) == 0
    gps = S5_GROUPS_PER_STEP_LONG if scan else S5_GROUPS_PER_STEP_SHORT
    grp = lambda shape: pl.BlockSpec((gps,) + shape, lambda i: (i, 0, 0))
    per_stream = pl.BlockSpec((n_streams, gps, 2 * n), lambda i: (0, i, 0))
    in_specs = [grp((rows, lanes)), grp((width + 2 * n, width)), grp((width, 2 * n)),
                grp((n, LANES)), grp((n, LANES))]
    args = [zt, wk, qm, tabr, tabi]
    if scan:
        assert n_streams == 1
        final_rows = (LANES - 1, LANES)
    else:
        in_specs.append(per_stream)
        args.append(s0)
        final_rows = (0, n_streams)
    rider_in_specs, rider_specs, rider_shapes, rider_args = _cast_rider_specs(cast_riders, (g // gps,))
    kern = functools.partial(_s5_kernel, n_sub=n_sub, lanes=lanes, scan=scan, final_rows=final_rows,
                             n_cast=len(cast_riders))
    return pl.pallas_call(
        kern,
        out_shape=(jax.ShapeDtypeStruct((g, rows, lanes), BF16),
                   jax.ShapeDtypeStruct((n_streams, g, 2 * n), F32), *rider_shapes),
        grid=(g // gps,),
        in_specs=in_specs + rider_in_specs,
        out_specs=(grp((rows, lanes)), per_stream, *rider_specs),
        scratch_shapes=[pltpu.VMEM((gps, rows, lanes), F32),
                        pltpu.VMEM((gps, n_sub, 2 * n, lanes), F32)],
        compiler_params=_cparams(1),
        name="s5",
    )(*args, *rider_args)


def _glu_kernel(*refs, n_ph, rows, lane_rows, n_cast):
    (yt_ref, w_ref, b_ref, m_ref), riders = _split_riders(refs, n_in=3, n_out=1, n_cast=n_cast)
    _run_riders(riders)
    gs = []
    for ph in range(n_ph):
        yt = yt_ref[:, ph, :, :].astype(F32).reshape(S5_WIDTH, lane_rows)
        gs.append(_gelu(yt.T[:rows]))
    gy = jnp.concatenate(gs, axis=0)
    gate = jnp.dot(gy.astype(BF16), w_ref[...], preferred_element_type=F32) + b_ref[...]
    m = gy * _sigmoid(gate)
    for pair in range(n_ph // 2):
        lo = 2 * pair * rows
        _oct_store(m_ref, pair, m[lo:lo + rows], m[lo + rows:lo + 2 * rows])


def _glu_call(yt4, w_glu_b, b_glu, *, rows, n_ph, row_blk, cast_riders=()):
    g, t, p, lane_rows = yt4.shape
    n_rb, lane_blk, lane_rows_expected = _phase_blocks(rows, row_blk)
    assert lane_rows == lane_rows_expected
    grid = (n_rb, t // n_ph)
    rider_in_specs, rider_specs, rider_shapes, rider_args = _cast_rider_specs(cast_riders, grid)
    kern = functools.partial(_glu_kernel, n_ph=n_ph, rows=row_blk, lane_rows=lane_blk,
                             n_cast=len(cast_riders))
    assert n_ph % 2 == 0
    oct_shape = _oct_shape(rows, t, S5_WIDTH)
    return pl.pallas_call(
        kern,
        out_shape=(jax.ShapeDtypeStruct(oct_shape, BF16), *rider_shapes),
        grid=grid,
        in_specs=[
            pl.BlockSpec((g, n_ph, p, lane_blk), lambda i, j: (0, j, 0, i)),
            _const_spec((S5_WIDTH, S5_WIDTH)),
            _const_spec((1, S5_WIDTH)),
            *rider_in_specs,
        ],
        out_specs=(pl.BlockSpec((row_blk // OCT, oct_shape[1], n_ph // 2, 2 * OCT, LANES),
                                lambda i, j: (i, 0, j, 0, 0)), *rider_specs),
        compiler_params=_cparams(2),
        name="glu",
    )(yt4, w_glu_b, b_glu, *rider_args)


def _main_kernel(x_ref, m_ref, zuv_ref, mod_ref, g2_ref, gf_ref, lng_ref, lnb_ref, gw_ref, gbt_ref,
                 wo_ref, wg_ref, wu_ref, wd_ref, y_ref, *rest, tm, cl, seq, streams,
                 before_ffn_up=None, before_ffn_down=None):
    *maybe_v_out_ref, v_ref, ygm_ref, attn_ref, act_ref, stage_ref = rest
    t = tm // OCT
    hd = GM_HEAD_DIM

    stage_ref[...] = m_ref[...].astype(F32)
    attn_ref[...] = jnp.dot(_oct_load(stage_ref, t).astype(BF16), wo_ref[:S5_WIDTH, :],
                            preferred_element_type=F32)

    gv = zuv_ref[:, GM_WIDTH:].astype(F32)
    cen = gv - jnp.mean(gv, axis=-1, keepdims=True)
    var = jnp.mean(cen * cen, axis=-1, keepdims=True)
    v = cen * lax.rsqrt(var + EPS) * lng_ref[...] + lnb_ref[...]
    for v_out_ref in maybe_v_out_ref:
        v_out_ref[...] = v
    v_ref[...] = v.astype(BF16)

    blk_i = _div_pow2(lax.broadcasted_iota(jnp.int32, (cl, cl), 0), CHUNK)
    blk_j = _div_pow2(lax.broadcasted_iota(jnp.int32, (cl, cl), 1), CHUNK)
    causal = blk_j <= blk_i
    first_head = lax.broadcasted_iota(jnp.int32, (cl, 2 * hd), 1) < hd
    for pr in range(GM_HEADS // 2):
        h0, h1 = 2 * pr, 2 * pr + 1
        wm = jnp.concatenate(
            [jnp.where(causal, gw_ref[h, :cl, :cl], 0.0) for h in (h0, h1)], axis=1).astype(BF16)
        bias = jnp.where(first_head, gbt_ref[:cl, h0:h0 + 1], gbt_ref[:cl, h1:h1 + 1])
        cs = slice(h0 * hd, (h1 + 1) * hd)
        for ci in range(tm // cl):
            rs = slice(ci * cl, (ci + 1) * cl)
            vv = v_ref[rs, cs]
            zero = jnp.zeros_like(vv)
            rhs = jnp.concatenate([jnp.where(first_head, vv, zero), jnp.where(first_head, zero, vv)],
                                  axis=0)
            mixed = jnp.dot(wm, rhs, preferred_element_type=F32) + bias
            ygm_ref[rs, cs] = (zuv_ref[rs, cs].astype(F32) * mixed).astype(BF16)

    attn = attn_ref[...] + jnp.dot(ygm_ref[...], wo_ref[S5_WIDTH:, :], preferred_element_type=F32)

    def mod_rows(idx):
        if len(streams) == 1:
            return _mod_vec(mod_ref, streams[0], idx)
        return jnp.concatenate(
            [jnp.broadcast_to(_mod_vec(mod_ref, s, idx), (seq, D_MODEL)) for s in streams], axis=0)

    gate1, shift2, scale2, gate2 = mod_rows(2), mod_rows(3), mod_rows(4), mod_rows(5)
    x1 = x_ref[...] + gate1 * attn
    h2 = (x1 * _rms_scale(x1) * (g2_ref[...] * (1.0 + scale2)) + shift2).astype(BF16)

    assert D_FF % MXU_DIM == 0
    if before_ffn_up is not None:
        before_ffn_up()
    for lo in range(0, D_FF, MXU_DIM):
        hi = lo + MXU_DIM
        gg = jnp.dot(h2, wg_ref[:, lo:hi], preferred_element_type=F32)
        up = jnp.dot(h2, wu_ref[:, lo:hi], preferred_element_type=F32)
        act_ref[:, lo:hi] = (gg * jax.nn.sigmoid(gg) * up).astype(BF16)
    if before_ffn_down is not None:
        before_ffn_down()
    acc = jnp.dot(act_ref[...], wd_ref[...], preferred_element_type=F32)
    x2 = x1 + gate2 * acc
    y_ref[...] = x2 * _rms_scale(x2) * gf_ref[...]


def _main_sets_kernel(*refs, sets, n_shared):
    *scratch, stage_ref, wg_ref, wu_ref, wd_ref, sems = refs[-9:]
    step = pl.program_id(0)

    late = refs[n_shared - 3:n_shared]
    copies = [pltpu.make_async_copy(src, dst, sems.at[i])
              for i, (src, dst) in enumerate(zip(late, (wg_ref, wu_ref, wd_ref)))]
    shared = (*refs[:n_shared - 3], wg_ref, wu_ref, wd_ref)

    @pl.when(step == 0)
    def _():
        for c in copies:
            c.start()

    def wait_for(which):
        def wait():
            @pl.when(step == 0)
            def _():
                for c in which:
                    c.wait()
        return wait

    pos = n_shared
    out_pos = n_shared + 3 * len(sets)
    for st in sets:
        ins = refs[pos:pos + 3]
        pos += 3
        n_out = 2 if st["want_v"] else 1
        outs = refs[out_pos:out_pos + n_out]
        out_pos += n_out
        tm = st["tm"]

        @pl.when(jnp.logical_and(step >= st["start"], step < st["start"] + st["n_tiles"]))
        def _(ins=ins, outs=outs, st=st, tm=tm):
            hooks = dict(before_ffn_up=wait_for(copies[:2]), before_ffn_down=wait_for(copies[2:])) \
                if st["start"] == 0 else {}
            _main_kernel(*ins, *shared, *outs, *[r.at[:tm] for r in scratch],
                         stage_ref.at[:ins[1].shape[0]],
                         tm=tm, cl=st["cl"], seq=st["seq"], streams=st["streams"], **hooks)


def _main_call(stream_sets, mod_all, norm2_g, final_g, ln_g, ln_b, gm_w, gm_bt, w_out_b, w_gate_b, w_up_b,
               w_down_b):
    shared = (mod_all, norm2_g, final_g, ln_g, ln_b, gm_w, gm_bt, w_out_b, w_gate_b, w_up_b, w_down_b)
    set_args, set_in_specs, out_shape, out_specs, statics = [], [], [], [], []
    start = 0
    for st in stream_sets:
        n_tok = st["x2d"].shape[0]
        n_oct, n_lb, half_t, _, _ = st["m"].shape
        t = 2 * half_t
        tm = OCT * t
        seq, streams = st["seq"], st["streams"]
        assert n_oct * tm == n_tok
        assert len(streams) == 1 or (n_oct == 1 and len(streams) * seq == tm)

        def tile_index(i, start=start, n_oct=n_oct):
            return (jnp.clip(i - start, 0, n_oct - 1), 0)

        tok = lambda width: pl.BlockSpec((tm, width), tile_index)
        oct_rows = n_lb * t * OCT
        set_args += [st["x2d"], st["m"].reshape(n_oct * oct_rows, LANES), st["zuv"]]
        set_in_specs += [tok(D_MODEL), pl.BlockSpec((oct_rows, LANES), tile_index), tok(2 * GM_WIDTH)]
        out_shape.append(jax.ShapeDtypeStruct((n_tok, D_MODEL), F32))
        out_specs.append(tok(D_MODEL))
        if st["want_v"]:
            out_shape.append(jax.ShapeDtypeStruct((n_tok, GM_WIDTH), F32))
            out_specs.append(tok(GM_WIDTH))
        statics.append(dict(start=start, n_tiles=n_oct, tm=tm, cl=min(GM_CHUNK, seq), seq=seq,
                            streams=streams, want_v=st["want_v"]))
        start += n_oct
    tm_max = max(s["tm"] for s in statics)
    outs = pl.pallas_call(
        functools.partial(_main_sets_kernel, sets=statics, n_shared=len(shared)),
        out_shape=tuple(out_shape),
        grid=(start,),
        in_specs=[_const_spec(a.shape) for a in shared[:-3]]
        + [pl.BlockSpec(memory_space=pl.ANY)] * 3 + set_in_specs,
        out_specs=tuple(out_specs),
        scratch_shapes=[pltpu.VMEM((tm_max, GM_WIDTH), BF16), pltpu.VMEM((tm_max, GM_WIDTH), BF16),
                        pltpu.VMEM((tm_max, D_MODEL), F32), pltpu.VMEM((tm_max, D_FF), BF16),
                        pltpu.VMEM((tm_max * (S5_WIDTH // LANES), LANES), F32),
                        pltpu.VMEM(w_gate_b.shape, BF16), pltpu.VMEM(w_up_b.shape, BF16),
                        pltpu.VMEM(w_down_b.shape, BF16), pltpu.SemaphoreType.DMA((3,))],
        compiler_params=_cparams(1),
        name="main",
    )(*shared, *set_args)
    results, pos = [], 0
    for s in statics:
        n_out = 2 if s["want_v"] else 1
        results.append(tuple(outs[pos:pos + n_out]))
        pos += n_out
    return results


def _mixer_front(x, mod_all, streams, s0, prm, *, t, n_ph, row_blk, want_v, f32_weights=None,
                 glu_blocks=None):
    b, seq, _ = x.shape
    n_chunks = seq // t
    rows = b * n_chunks
    scan = s0 is None
    assert (b == 1) if scan else (n_chunks == 1)
    n_sub = t // SUB

    zt4, zuv = _inproj_call(
        x.reshape(rows, t, D_MODEL), mod_all, streams, prm["norm1_g"], prm["w_in"],
        n_ph=n_ph, row_blk=row_blk)
    lane_rows = zt4.shape[-1]

    assert not scan or rows == lane_rows
    pending = dict(f32_weights or {})
    on_glu = {k: pending.pop(k) for k in ("w_up",) if k in pending}
    yt, fin, *cast = _s5_call(zt4.reshape(S5_GROUPS, t * S5_GROUP, lane_rows),
                              prm["wk"], prm["qm"], prm["tabr"], prm["tabi"], s0,
                              n_streams=b, n_sub=n_sub, scan=scan, cast_riders=tuple(pending.values()))
    prm = {**prm, **dict(zip(pending.keys(), cast))}
    glu_n_ph, glu_row_blk = glu_blocks or (n_ph, row_blk)
    m, *cast = _glu_call(yt.reshape(S5_GROUPS, t, S5_GROUP, lane_rows), prm["w_glu"], prm["b_glu"],
                         rows=rows, n_ph=glu_n_ph, row_blk=glu_row_blk, cast_riders=tuple(on_glu.values()))
    prm = {**prm, **dict(zip(on_glu.keys(), cast))}

    n_tok = b * seq
    stream_set = dict(x2d=x.reshape(n_tok, D_MODEL), m=m, zuv=zuv.reshape(n_tok, 2 * GM_WIDTH),
                      streams=streams, seq=seq, want_v=want_v)

    return stream_set, fin[..., :S5_STATE], fin[..., S5_STATE:], prm


def kernel(x_prompt, x_sample, state_s5_re, state_s5_im, c_prompt, c_sample, norm1_g, norm2_g, w_ada, b_ada, w_in, s5_lambda_re, s5_lambda_im, s5_log_step, s5_b_re, s5_b_im, s5_c_re, s5_c_im, s5_d, s5_w_glu, s5_b_glu, gm_ln_g, gm_ln_b, gm_w_s, gm_b_s, w_out, ffn_w_gu, ffn_w_down, final_g):
    depth = w_in.shape[0]
    assert depth == 1
    l = 0
    n_p = c_prompt.shape[0]
    n_s = c_sample.shape[0]

    c_all = jnp.concatenate([c_prompt, c_sample], axis=0)
    c_pad = jnp.pad(c_all, ((0, -c_all.shape[0] % SUBLANES), (0, 0)))
    streams_p = tuple(range(n_p))
    streams_s = tuple(range(n_p, n_p + n_s))

    wk, qm, tabr, tabi, mod_all, w_in_b, w_out_b, w_down_b = _s5_prep_call(
        c_pad, w_ada[l], b_ada[l][None, :], s5_lambda_re[l], s5_lambda_im[l], s5_log_step[l][None, :],
        jnp.concatenate([jnp.swapaxes(s5_b_re[l], 1, 2), jnp.swapaxes(s5_b_im[l], 1, 2)], axis=-1),
        s5_c_re[l], s5_c_im[l], s5_d[l], cast_riders=(w_in[l], w_out[l], ffn_w_down[l]))

    prm = dict(
        norm1_g=norm1_g[l][None, :], norm2_g=norm2_g[l][None, :], final_g=final_g[None, :],
        w_in=w_in_b, w_out=w_out_b, w_down=w_down_b, ln_g=gm_ln_g[l][None, :], ln_b=gm_ln_b[l][None, :],
        wk=wk, qm=qm, tabr=tabr, tabi=tabi, b_glu=s5_b_glu[l][None, :],
        gm_w=gm_w_s[l], gm_bt=jnp.transpose(gm_b_s[l]),
    )
    later_weights = dict(w_glu=s5_w_glu[l], w_gate=(ffn_w_gu[l], 0, 2), w_up=(ffn_w_gu[l], 1, 2))

    set_p, pre, pim, prm = _mixer_front(
        x_prompt, mod_all, streams_p, None, prm, f32_weights=later_weights,
        t=S5_LONG_SUBS * SUB, n_ph=SUBLANES, row_blk=LANES, want_v=False,
        glu_blocks=(SUBLANES // 2, 2 * LANES))
    s0 = jnp.concatenate([state_s5_re[l], state_s5_im[l]], axis=-1)
    n_b, seq_s, _ = x_sample.shape
    set_s, sre, sim, _ = _mixer_front(
        x_sample, mod_all, streams_s, s0, prm, t=seq_s, n_ph=seq_s, row_blk=n_b, want_v=True)

    (yp,), (ys, vs) = _main_call(
        [set_p, set_s], mod_all, prm["norm2_g"], prm["final_g"], prm["ln_g"], prm["ln_b"],
        prm["gm_w"], prm["gm_bt"], prm["w_out"], prm["w_gate"], prm["w_up"], prm["w_down"])
    return (yp.reshape(x_prompt.shape), ys.reshape(x_sample.shape), pre[None], pim[None],
            sre[None], sim[None], vs.reshape(n_b, seq_s, GM_WIDTH)[None])
```
